```python
import jax, jax.numpy as jnp
from jax import lax
import numpy as np

D_MODEL = 1024
BATCH = 16
SEQ = 2048
DEPTH = 2
DEC_BATCH = 32
DEC_SEQ = 2048
PAST_LEN = 128

N_HEADS = 16
Q_LORA_RANK = 384
KV_LORA_RANK = 256
QK_NOPE_DIM = 64
QK_ROPE_DIM = 32
V_HEAD_DIM = 64
ROPE_THETA = 10000.0
Q_BLOCK = 128
POOL_WINDOWS = (2, 4, 8, 16)
N_POOL_GROUPS = len(POOL_WINDOWS)
POOL_GROUP_DIM = D_MODEL // N_POOL_GROUPS
N_EXPERTS = 32
TOP_K = 4
D_FF = D_MODEL
SWIGLU_LIMIT = 7.0
SWIGLU_ALPHA = 1.702
EXPERT_BLOCK = 128
N_MLA_LAYERS = (DEPTH + 1) // 2
N_POOL_LAYERS = DEPTH // 2
EPS = 1e-6

kernel_name = 'hybrid_mla_pool_moe_adaln_encoder'


def rms_norm(x, gain=None):
    xf = x.astype(jnp.float32)
    y = xf * lax.rsqrt(jnp.mean(xf * xf, axis=-1, keepdims=True) + EPS)
    if gain is not None:
        y = y * gain.astype(jnp.float32)
    return y.astype(x.dtype)


def rope(x, cos, sin):
    half = x.shape[-1] // 2
    x1, x2 = x[..., :half], x[..., half:]
    return jnp.concatenate([x1 * cos - x2 * sin, x2 * cos + x1 * sin], axis=-1)


def mla_mixer(h, w_in, q_norm, kv_norm, w_uq, w_ukv, w_o):
    b, s, _ = h.shape
    a = h @ w_in
    cq = rms_norm(a[..., :Q_LORA_RANK], q_norm)
    ckv = rms_norm(a[..., Q_LORA_RANK:Q_LORA_RANK + KV_LORA_RANK], kv_norm)
    k_pe = a[..., Q_LORA_RANK + KV_LORA_RANK:]
    q = (cq @ w_uq).reshape(b, s, N_HEADS, QK_NOPE_DIM + QK_ROPE_DIM)
    kv = (ckv @ w_ukv).reshape(b, s, N_HEADS, QK_NOPE_DIM + V_HEAD_DIM)
    q_nope, q_pe = q[..., :QK_NOPE_DIM], q[..., QK_NOPE_DIM:]
    k_nope, v = kv[..., :QK_NOPE_DIM], kv[..., QK_NOPE_DIM:]
    inv_freq = 1.0 / (ROPE_THETA ** (jnp.arange(0, QK_ROPE_DIM, 2, dtype=jnp.float32) / QK_ROPE_DIM))
    ang = jnp.arange(s, dtype=jnp.float32)[:, None] * inv_freq[None, :]
    cos, sin = jnp.cos(ang).astype(h.dtype), jnp.sin(ang).astype(h.dtype)
    q_pe = rope(q_pe, cos[:, None, :], sin[:, None, :])
    k_pe = rope(k_pe, cos, sin)
    scale = (QK_NOPE_DIM + QK_ROPE_DIM) ** -0.5
    nq = s // Q_BLOCK

    def blocks(t):
        return jnp.moveaxis(t.reshape(b, nq, Q_BLOCK, *t.shape[2:]), 1, 0)

    def attend(args):
        qn, qp = args
        sc = (jnp.einsum('bqhd,bkhd->bhqk', qn, k_nope)
              + jnp.einsum('bqhr,bkr->bhqk', qp, k_pe))
        p = jax.nn.softmax(sc.astype(jnp.float32) * scale, axis=-1).astype(v.dtype)
        return jnp.einsum('bhqk,bkhd->bqhd', p, v)

    o = lax.map(attend, (blocks(q_nope), blocks(q_pe)))
    o = jnp.moveaxis(o, 0, 1).reshape(b, s, N_HEADS * V_HEAD_DIM)
    return o @ w_o


def pool_mixer(h, w_group, ch_scale):
    b, s, d = h.shape
    hf = h.astype(jnp.float32)
    csum = jnp.concatenate([jnp.zeros((b, 1, d), jnp.float32), jnp.cumsum(hf, axis=1)], axis=1)
    t = jnp.arange(s)
    outs = []
    for g, w in enumerate(POOL_WINDOWS):
        left = w // 2
        right = w - 1 - left
        lo = jnp.clip(t - left, 0, s)
        hi = jnp.clip(t + right + 1, 0, s)
        cg = csum[..., g * POOL_GROUP_DIM:(g + 1) * POOL_GROUP_DIM]
        win_sum = jnp.take(cg, hi, axis=1) - jnp.take(cg, lo, axis=1)
        count = (hi - lo).astype(jnp.float32)[None, :, None]
        outs.append(win_sum / count)
    pooled = jnp.stack(outs, axis=2)
    diff = (pooled - hf.reshape(b, s, N_POOL_GROUPS, POOL_GROUP_DIM)).astype(h.dtype)
    y = jnp.einsum('bsgc,gcd->bsgd', diff, w_group).reshape(b, s, d)
    return y * ch_scale


def moe(h, router_w, router_b, w1, b1, w2, b2):
    b, s, d = h.shape
    n_tok = b * s
    x = h.reshape(n_tok, d)
    logits = (x @ router_w + router_b).astype(jnp.float32)
    top_val, top_idx = lax.top_k(logits, TOP_K)
    top_w = jax.nn.softmax(top_val, axis=-1)
    n_asg = n_tok * TOP_K
    flat_e = top_idx.reshape(-1).astype(jnp.int32)
    flat_tok = jnp.arange(n_asg, dtype=jnp.int32) // TOP_K
    flat_w = top_w.reshape(-1)
    order = jnp.argsort(flat_e)
    sorted_e = flat_e[order]
    counts = jnp.bincount(flat_e, length=N_EXPERTS).astype(jnp.int32)
    padded = (counts + EXPERT_BLOCK - 1) // EXPERT_BLOCK * EXPERT_BLOCK
    start = jnp.cumsum(counts) - counts
    padded_end = jnp.cumsum(padded)
    padded_start = padded_end - padded
    dest = padded_start[sorted_e] + jnp.arange(n_asg, dtype=jnp.int32) - start[sorted_e]
    n_pad = (-(-n_asg // EXPERT_BLOCK)) * EXPERT_BLOCK + N_EXPERTS * EXPERT_BLOCK
    n_blk = n_pad // EXPERT_BLOCK
    tok_buf = jnp.zeros((n_pad,), jnp.int32).at[dest].set(flat_tok[order])
    w_buf = jnp.zeros((n_pad,), jnp.float32).at[dest].set(flat_w[order])
    blk_start = jnp.arange(n_blk, dtype=jnp.int32) * EXPERT_BLOCK
    blk_e = jnp.minimum(jnp.searchsorted(padded_end, blk_start, side='right'), N_EXPERTS - 1)

    def expert_block(args):
        tok, wt, e = args
        xb = x[tok]
        gu = xb @ w1[e] + b1[e]
        gate = jnp.minimum(gu[:, :D_FF], SWIGLU_LIMIT)
        up = jnp.clip(gu[:, D_FF:], -SWIGLU_LIMIT, SWIGLU_LIMIT)
        act = (up + 1) * (gate * jax.nn.sigmoid(SWIGLU_ALPHA * gate))
        y = act @ w2[e] + b2[e]
        return y.astype(jnp.float32) * wt[:, None]

    y = lax.map(expert_block, (tok_buf.reshape(n_blk, EXPERT_BLOCK),
                               w_buf.reshape(n_blk, EXPERT_BLOCK), blk_e))
    out = jnp.zeros((n_tok, d), jnp.float32).at[tok_buf].add(y.reshape(n_pad, d))
    return out.astype(h.dtype).reshape(b, s, d)


def encoder_trunk(x, c, ada_w, ada_b, mla_w_in, mla_q_norm, mla_kv_norm, mla_w_uq,
                  mla_w_ukv, mla_w_o, pool_w, pool_scale, router_w, router_b,
                  moe_w1, moe_b1, moe_w2, moe_b2, final_norm):
    c_act = jax.nn.silu(c)
    for i in range(DEPTH):
        mod = (c_act @ ada_w[i] + ada_b[i])[:, None, :]
        sh1, sc1, g1, sh2, sc2, g2 = jnp.split(mod, 6, axis=-1)
        h = rms_norm(x) * (1 + sc1) + sh1
        j = i // 2
        if i % 2 == 0:
            mix = mla_mixer(h, mla_w_in[j], mla_q_norm[j], mla_kv_norm[j],
                            mla_w_uq[j], mla_w_ukv[j], mla_w_o[j])
        else:
            mix = pool_mixer(h, pool_w[j], pool_scale[j])
        x = x + g1 * mix
        h = rms_norm(x) * (1 + sc2) + sh2
        x = x + g2 * moe(h, router_w[i], router_b[i], moe_w1[i], moe_b1[i], moe_w2[i], moe_b2[i])
    return rms_norm(x, final_norm)


def setup_inputs(seed: int = 0) -> dict:
    key = jax.random.key(seed)
    ks = jax.random.split(key, 21)

    def nrm(k, shape, scale):
        return jax.random.normal(k, shape, jnp.float32) * scale

    qk = QK_NOPE_DIM + QK_ROPE_DIM
    return {
        'x_prompt': nrm(ks[0], (BATCH, SEQ, D_MODEL), 1.0),
        'x_sample': nrm(ks[1], (DEC_BATCH, DEC_SEQ, D_MODEL), 1.0),
        'c_prompt': nrm(ks[2], (BATCH, D_MODEL), 1.0),
        'c_sample': nrm(ks[3], (DEC_BATCH, D_MODEL), 1.0),
        'ada_w': nrm(ks[4], (DEPTH, D_MODEL, 6 * D_MODEL), 0.5 * D_MODEL ** -0.5),
        'ada_b': nrm(ks[5], (DEPTH, 6 * D_MODEL), 0.02),
        'mla_w_in': nrm(ks[6], (N_MLA_LAYERS, D_MODEL, Q_LORA_RANK + KV_LORA_RANK + QK_ROPE_DIM), D_MODEL ** -0.5),
        'mla_q_norm': 1 + nrm(ks[7], (N_MLA_LAYERS, Q_LORA_RANK), 0.1),
        'mla_kv_norm': 1 + nrm(ks[8], (N_MLA_LAYERS, KV_LORA_RANK), 0.1),
        'mla_w_uq': nrm(ks[9], (N_MLA_LAYERS, Q_LORA_RANK, N_HEADS * qk), Q_LORA_RANK ** -0.5),
        'mla_w_ukv': nrm(ks[10], (N_MLA_LAYERS, KV_LORA_RANK, N_HEADS * (QK_NOPE_DIM + V_HEAD_DIM)), KV_LORA_RANK ** -0.5),
        'mla_w_o': nrm(ks[11], (N_MLA_LAYERS, N_HEADS * V_HEAD_DIM, D_MODEL), (N_HEADS * V_HEAD_DIM) ** -0.5),
        'pool_w': nrm(ks[12], (N_POOL_LAYERS, N_POOL_GROUPS, POOL_GROUP_DIM, POOL_GROUP_DIM), POOL_GROUP_DIM ** -0.5),
        'pool_scale': 1 + nrm(ks[13], (N_POOL_LAYERS, D_MODEL), 0.1),
        'router_w': nrm(ks[14], (DEPTH, D_MODEL, N_EXPERTS), D_MODEL ** -0.5),
        'router_b': nrm(ks[15], (DEPTH, N_EXPERTS), 0.01),
        'moe_w1': nrm(ks[16], (DEPTH, N_EXPERTS, D_MODEL, 2 * D_FF), D_MODEL ** -0.5),
        'moe_b1': nrm(ks[17], (DEPTH, N_EXPERTS, 2 * D_FF), 0.01),
        'moe_w2': nrm(ks[18], (DEPTH, N_EXPERTS, D_FF, D_MODEL), D_FF ** -0.5),
        'moe_b2': nrm(ks[19], (DEPTH, N_EXPERTS, D_MODEL), 0.01),
        'final_norm': 1 + nrm(ks[20], (D_MODEL,), 0.1),
    }


def reference(x_prompt, x_sample, c_prompt, c_sample, ada_w, ada_b, mla_w_in, mla_q_norm,
              mla_kv_norm, mla_w_uq, mla_w_ukv, mla_w_o, pool_w, pool_scale, router_w,
              router_b, moe_w1, moe_b1, moe_w2, moe_b2, final_norm):
    y_prompt = encoder_trunk(x_prompt, c_prompt, ada_w, ada_b, mla_w_in, mla_q_norm, mla_kv_norm,
                             mla_w_uq, mla_w_ukv, mla_w_o, pool_w, pool_scale, router_w, router_b,
                             moe_w1, moe_b1, moe_w2, moe_b2, final_norm)
    y_sample = encoder_trunk(x_sample, c_sample, ada_w, ada_b, mla_w_in, mla_q_norm, mla_kv_norm,
                             mla_w_uq, mla_w_ukv, mla_w_o, pool_w, pool_scale, router_w, router_b,
                             moe_w1, moe_b1, moe_w2, moe_b2, final_norm)
    return (y_prompt, y_sample)
```

```python
import functools
import math

import jax
import jax.numpy as jnp
from jax import lax
from jax.experimental import pallas as pl
from jax.experimental.pallas import tpu as pltpu

F32 = jnp.float32
BF16 = jnp.bfloat16
I32 = jnp.int32

D_MODEL = 1024
N_HEADS = 16
Q_RANK = 384
KV_RANK = 256
D_NOPE = 64
D_ROPE = 32
D_V = 64
ROPE_THETA = 10000.0
POOL_WINDOWS = (2, 4, 8, 16)
POOL_GROUP = D_MODEL // len(POOL_WINDOWS)
N_EXPERTS = 32
TOP_K = 4
D_FF = D_MODEL
SWIGLU_LIMIT = 7.0
SWIGLU_ALPHA = 1.702
EPS = 1e-6

LANES = 128
SUBLANES = 8
HEAD_PAD = 128
ROPE_LO = D_NOPE
ROPE_HALF = D_ROPE // 2
VMEM_LIMIT = 56 * 1024 * 1024

TOKEN_TILE = 256
EXPERT_TILE = 256
POOL_HALO = 8
PACKED = D_MODEL // 2


def _rms(x):
    return x * lax.rsqrt(jnp.mean(x * x, axis=-1, keepdims=True) + EPS)


def _pack_rows(y):
    lo = lax.bitcast_convert_type(y[:, :PACKED].astype(BF16).astype(F32), I32)
    hi = lax.bitcast_convert_type(y[:, PACKED:].astype(BF16).astype(F32), I32)
    return lax.shift_right_logical(lo, 16) | (hi & jnp.int32(-65536))


def _unpack_rows(w):
    lo = lax.bitcast_convert_type(lax.shift_left(w, 16), F32)
    hi = lax.bitcast_convert_type(w & jnp.int32(-65536), F32)
    return jnp.concatenate([lo, hi], axis=1)


def _ada_kernel(c_ref, w_ref, b_ref, o_ref):
    c = c_ref[...]
    act = (c * jax.nn.sigmoid(c)).astype(BF16)
    o_ref[0] = jnp.dot(act, w_ref[0].astype(BF16), preferred_element_type=F32) + b_ref[0]


def _ada_mod(c, ada_w, ada_b):
    depth, _, n_out = ada_w.shape
    n_seq = c.shape[0]
    tn = 1536
    return pl.pallas_call(
        _ada_kernel,
        out_shape=jax.ShapeDtypeStruct((depth, n_seq, n_out), F32),
        grid=(depth, n_out // tn),
        in_specs=[
            pl.BlockSpec((n_seq, D_MODEL), lambda l, j: (0, 0)),
            pl.BlockSpec((1, D_MODEL, tn), lambda l, j: (l, 0, j)),
            pl.BlockSpec((1, 1, tn), lambda l, j: (l, 0, j)),
        ],
        out_specs=pl.BlockSpec((1, n_seq, tn), lambda l, j: (l, 0, j)),
        compiler_params=pltpu.CompilerParams(
            dimension_semantics=("arbitrary", "arbitrary"), vmem_limit_bytes=VMEM_LIMIT),
        name="ada_mod",
    )(c, ada_w, ada_b.reshape(depth, 1, n_out))


def _mla_pre_kernel(x_ref, mod_ref, win_ref, qn_ref, kvn_ref, wuqt_ref, wuk_ref, wuvt_ref,
                    cost_ref, sint_ref, ck_ref, s1k_ref, s2k_ref,
                    qt_ref, k_ref, vt_ref):
    tm = x_ref.shape[0]
    mod = mod_ref[0]
    sh1 = mod[:, 0:D_MODEL]
    sc1 = mod[:, D_MODEL:2 * D_MODEL]
    h = (_rms(x_ref[...]) * (1.0 + sc1) + sh1).astype(BF16)
    a = jnp.dot(h, win_ref[...], preferred_element_type=F32)
    cq = (_rms(a[:, :Q_RANK]) * qn_ref[...]).astype(BF16)
    ckv = (_rms(a[:, Q_RANK:Q_RANK + KV_RANK]) * kvn_ref[...]).astype(BF16)

    kpe = a[:, Q_RANK + KV_RANK:]
    kpe = (kpe * ck_ref[...]
           + pltpu.roll(kpe, LANES - ROPE_HALF, 1) * s1k_ref[...]
           + pltpu.roll(kpe, ROPE_HALF, 1) * s2k_ref[...])

    qt = lax.dot_general(wuqt_ref[...], cq, (((1,), (1,)), ((), ())),
                         preferred_element_type=F32)
    q3 = qt.reshape(N_HEADS, HEAD_PAD, tm)
    x1 = q3[:, ROPE_LO:ROPE_LO + ROPE_HALF, :]
    x2 = q3[:, ROPE_LO + ROPE_HALF:ROPE_LO + D_ROPE, :]
    cos = cost_ref[...][None]
    sin = sint_ref[...][None]
    q3 = jnp.concatenate(
        [q3[:, :ROPE_LO, :], x1 * cos - x2 * sin, x2 * cos + x1 * sin,
         q3[:, ROPE_LO + D_ROPE:, :]], axis=1)
    qt_ref[:, 0] = q3.astype(BF16)

    kn = jnp.dot(ckv, wuk_ref[...], preferred_element_type=F32)
    for hd in range(N_HEADS):
        sl = slice(hd * HEAD_PAD, (hd + 1) * HEAD_PAD)
        k_ref[:, sl] = (kn[:, sl] + kpe).astype(BF16)

    vt = lax.dot_general(wuvt_ref[...], ckv, (((1,), (1,)), ((), ())),
                         preferred_element_type=F32)
    vt_ref[:, 0] = vt.reshape(N_HEADS, D_V, tm).astype(BF16)


def _mla_pre(x, mod, w, seq_len):
    n_tok = x.shape[0]
    tm = TOKEN_TILE
    n_tiles = n_tok // tm
    per_seq = seq_len // tm
    const = lambda i: (0, 0)
    return pl.pallas_call(
        _mla_pre_kernel,
        out_shape=(
            jax.ShapeDtypeStruct((N_HEADS, n_tiles, HEAD_PAD, tm), BF16),
            jax.ShapeDtypeStruct((n_tok, N_HEADS * HEAD_PAD), BF16),
            jax.ShapeDtypeStruct((N_HEADS, n_tok // seq_len, D_V, seq_len), BF16),
        ),
        grid=(n_tiles,),
        in_specs=[
            pl.BlockSpec((tm, D_MODEL), lambda i: (i, 0)),
            pl.BlockSpec((1, 1, 6 * D_MODEL), lambda i: (i // per_seq, 0, 0)),
            pl.BlockSpec(w["w_in"].shape, const),
            pl.BlockSpec(w["q_norm"].shape, const),
            pl.BlockSpec(w["kv_norm"].shape, const),
            pl.BlockSpec(w["w_uq_t"].shape, const),
            pl.BlockSpec(w["w_uk"].shape, const),
            pl.BlockSpec(w["w_uv_t"].shape, const),
            pl.BlockSpec((ROPE_HALF, tm), lambda i: (0, i % per_seq)),
            pl.BlockSpec((ROPE_HALF, tm), lambda i: (0, i % per_seq)),
            pl.BlockSpec((tm, LANES), lambda i: (i % per_seq, 0)),
            pl.BlockSpec((tm, LANES), lambda i: (i % per_seq, 0)),
            pl.BlockSpec((tm, LANES), lambda i: (i % per_seq, 0)),
        ],
        out_specs=(
            pl.BlockSpec((N_HEADS, 1, HEAD_PAD, tm), lambda i: (0, i, 0, 0)),
            pl.BlockSpec((tm, N_HEADS * HEAD_PAD), lambda i: (i, 0)),
            pl.BlockSpec((N_HEADS, 1, D_V, tm), lambda i: (0, i // per_seq, 0, i % per_seq)),
        ),
        compiler_params=pltpu.CompilerParams(
            dimension_semantics=("arbitrary",), vmem_limit_bytes=VMEM_LIMIT),
        name="mla_pre",
    )(x, mod, w["w_in"], w["q_norm"], w["kv_norm"], w["w_uq_t"], w["w_uk"], w["w_uv_t"],
      w["cos_t"], w["sin_t"], w["rope_c"], w["rope_s1"], w["rope_s2"])


def _attention_kernel(qt_ref, k_ref, vt_ref, ot_ref):
    n_q = qt_ref.shape[1]

    def body(j, carry):
        q = qt_ref[0, j]
        s = jnp.dot(k_ref[...], q, preferred_element_type=F32)
        m = jnp.max(s, axis=0, keepdims=True)
        p = jnp.exp2(s - m)
        denom = jnp.sum(p, axis=0, keepdims=True)
        o = jnp.dot(vt_ref[0, 0], p.astype(BF16), preferred_element_type=F32)
        ot_ref[0, j] = (o * (1.0 / denom)).astype(BF16)
        return carry

    lax.fori_loop(0, n_q, body, 0)


def _attention(qt, k, vt, seq_len):
    n_heads, n_tiles, _, tq = qt.shape
    per_seq = seq_len // tq
    n_seq = n_tiles // per_seq
    return pl.pallas_call(
        _attention_kernel,
        out_shape=jax.ShapeDtypeStruct((n_heads, n_tiles, D_V, tq), BF16),
        grid=(n_seq, n_heads),
        in_specs=[
            pl.BlockSpec((1, per_seq, HEAD_PAD, tq), lambda b, h: (h, b, 0, 0)),
            pl.BlockSpec((seq_len, HEAD_PAD), lambda b, h: (b, h)),
            pl.BlockSpec((1, 1, D_V, seq_len), lambda b, h: (h, b, 0, 0)),
        ],
        out_specs=pl.BlockSpec((1, per_seq, D_V, tq), lambda b, h: (h, b, 0, 0)),
        compiler_params=pltpu.CompilerParams(
            dimension_semantics=("arbitrary", "arbitrary"), vmem_limit_bytes=VMEM_LIMIT),
        name="attention",
    )(qt, k, vt)


def _route_tail(x1, mod, rwt_ref, rb_ref, x1_ref, hp_ref, topi_ref, rank_ref, wtm_ref, cnt_ref):
    tm = x1.shape[0]
    sh2 = mod[:, 3 * D_MODEL:4 * D_MODEL]
    sc2 = mod[:, 4 * D_MODEL:5 * D_MODEL]
    x1_ref[...] = x1
    h2 = _rms(x1) * (1.0 + sc2) + sh2
    hp_ref[...] = _pack_rows(h2)
    logits = lax.dot_general(rwt_ref[...], h2.astype(BF16), (((1,), (1,)), ((), ())),
                             preferred_element_type=F32) + rb_ref[...]
    e_iota = lax.broadcasted_iota(I32, (N_EXPERTS, tm), 0)
    vals, idxs = [], []
    work = logits
    for _ in range(TOP_K):
        m = jnp.max(work, axis=0, keepdims=True)
        idx = jnp.min(jnp.where(work == m, e_iota, N_EXPERTS), axis=0, keepdims=True)
        vals.append(m)
        idxs.append(idx)
        work = jnp.where(e_iota == idx, -jnp.inf, work)
    ex = [jnp.exp(v - vals[0]) for v in vals]
    inv = 1.0 / (ex[0] + ex[1] + ex[2] + ex[3])
    topw = jnp.concatenate([e * inv for e in ex], axis=0)
    topi_ref[0] = jnp.concatenate(idxs, axis=0)

    @pl.when(pl.program_id(0) == 0)
    def _():
        cnt_ref[...] = jnp.zeros_like(cnt_ref)

    row = lax.broadcasted_iota(I32, (tm, tm), 0)
    col = lax.broadcasted_iota(I32, (tm, tm), 1)
    earlier = (row < col).astype(BF16)
    running = cnt_ref[...][:, 0:1]
    ranks = []
    for kk in range(TOP_K):
        onehot = (e_iota == idxs[kk]).astype(F32)
        before = jnp.dot(onehot.astype(BF16), earlier, preferred_element_type=F32)
        ranks.append(jnp.sum(onehot * (running + before), axis=0, keepdims=True))
        running = running + jnp.sum(onehot, axis=1, keepdims=True)
    rank_ref[0] = jnp.concatenate(ranks, axis=0).astype(I32)
    cnt_ref[...] = jnp.broadcast_to(running, cnt_ref.shape)

    wpad = jnp.concatenate([topw, jnp.zeros((LANES - TOP_K, tm), F32)], axis=0)
    wtm_ref[...] = wpad.T


def _route_out_shapes(n_tok, tm):
    n_tiles = n_tok // tm
    return (
        jax.ShapeDtypeStruct((n_tok, D_MODEL), F32),
        jax.ShapeDtypeStruct((n_tok, PACKED), I32),
        jax.ShapeDtypeStruct((n_tiles, TOP_K, tm), I32),
        jax.ShapeDtypeStruct((n_tiles, TOP_K, tm), I32),
        jax.ShapeDtypeStruct((n_tok, LANES), F32),
        jax.ShapeDtypeStruct((N_EXPERTS, LANES), F32),
    )


def _route_out_specs(tm):
    return (
        pl.BlockSpec((tm, D_MODEL), lambda i: (i, 0)),
        pl.BlockSpec((tm, PACKED), lambda i: (i, 0)),
        pl.BlockSpec((1, TOP_K, tm), lambda i: (i, 0, 0)),
        pl.BlockSpec((1, TOP_K, tm), lambda i: (i, 0, 0)),
        pl.BlockSpec((tm, LANES), lambda i: (i, 0)),
        pl.BlockSpec((N_EXPERTS, LANES), lambda i: (0, 0)),
    )


def _post_mix_kernel(ot_ref, x_ref, mod_ref, wo_ref, rwt_ref, rb_ref, *out_refs):
    tm = x_ref.shape[0]
    mod = mod_ref[0]
    g1 = mod[:, 2 * D_MODEL:3 * D_MODEL]
    ot = ot_ref[:, 0].reshape(N_HEADS * D_V, tm)
    mix = lax.dot_general(ot, wo_ref[...], (((0,), (0,)), ((), ())),
                          preferred_element_type=F32)
    _route_tail(x_ref[...] + g1 * mix, mod, rwt_ref, rb_ref, *out_refs)


def _post_mix(ot, x, mod, w_o, rw_t, rb, seq_len):
    n_tok = x.shape[0]
    tm = ot.shape[3]
    per_seq = seq_len // tm
    const = lambda i: (0, 0)
    return pl.pallas_call(
        _post_mix_kernel,
        out_shape=_route_out_shapes(n_tok, tm),
        grid=(n_tok // tm,),
        in_specs=[
            pl.BlockSpec((N_HEADS, 1, D_V, tm), lambda i: (0, i, 0, 0)),
            pl.BlockSpec((tm, D_MODEL), lambda i: (i, 0)),
            pl.BlockSpec((1, 1, 6 * D_MODEL), lambda i: (i // per_seq, 0, 0)),
            pl.BlockSpec(w_o.shape, const),
            pl.BlockSpec(rw_t.shape, const),
            pl.BlockSpec(rb.shape, const),
        ],
        out_specs=_route_out_specs(tm),
        compiler_params=pltpu.CompilerParams(
            dimension_semantics=("arbitrary",), vmem_limit_bytes=VMEM_LIMIT),
        name="post_mix",
    )(ot, x, mod, w_o, rw_t, rb)


def _pool_mix_kernel(x_ref, xp_ref, xn_ref, mod_ref, pw_ref, ps_ref, rwt_ref, rb_ref,
                     *out_refs, seq_len):
    tm = x_ref.shape[0]
    per_seq = seq_len // tm
    mod = mod_ref[0]
    sh1 = mod[:, 0:D_MODEL]
    sc1 = mod[:, D_MODEL:2 * D_MODEL]
    g1 = mod[:, 2 * D_MODEL:3 * D_MODEL]
    pos0 = (pl.program_id(0) % per_seq) * tm

    def normed(v):
        return _rms(v) * (1.0 + sc1) + sh1

    x = x_ref[...]
    h = normed(x)
    h_prev = jnp.where(pos0 > 0, normed(xp_ref[...]), 0.0)
    h_next = jnp.where(pos0 + tm < seq_len, normed(xn_ref[...]), 0.0)
    hext = jnp.concatenate([h_prev, h, h_next], axis=0)
    n_ext = tm + 2 * POOL_HALO
    pos = pos0 + lax.broadcasted_iota(I32, (tm, 1), 0)

    outs = []
    for g, win in enumerate(POOL_WINDOWS):
        left = win // 2
        right = win - 1 - left
        cols = slice(g * POOL_GROUP, (g + 1) * POOL_GROUP)
        s = hext[:, cols]
        span = 1
        while span < win:
            s = s + pltpu.roll(s, span, 0)
            span *= 2
        if right:
            s = pltpu.roll(s, n_ext - right, 0)
        num = s[POOL_HALO:POOL_HALO + tm, :]
        count = (jnp.minimum(pos + right + 1, seq_len) - jnp.maximum(pos - left, 0)).astype(F32)
        diff = (num / count - h[:, cols]).astype(BF16)
        outs.append(jnp.dot(diff, pw_ref[g], preferred_element_type=F32))
    mix = jnp.concatenate(outs, axis=1) * ps_ref[...]
    _route_tail(x + g1 * mix, mod, rwt_ref, rb_ref, *out_refs)


def _pool_mix(x, mod, pool_w, pool_scale, rw_t, rb, seq_len):
    n_tok = x.shape[0]
    tm = TOKEN_TILE
    per_seq = seq_len // tm
    halo_per_tile = tm // POOL_HALO
    n_halo = n_tok // POOL_HALO
    const = lambda i: (0, 0)
    return pl.pallas_call(
        functools.partial(_pool_mix_kernel, seq_len=seq_len),
        out_shape=_route_out_shapes(n_tok, tm),
        grid=(n_tok // tm,),
        in_specs=[
            pl.BlockSpec((tm, D_MODEL), lambda i: (i, 0)),
            pl.BlockSpec((POOL_HALO, D_MODEL),
                         lambda i: (jnp.maximum(i * halo_per_tile - 1, 0), 0)),
            pl.BlockSpec((POOL_HALO, D_MODEL),
                         lambda i: (jnp.minimum((i + 1) * halo_per_tile, n_halo - 1), 0)),
            pl.BlockSpec((1, 1, 6 * D_MODEL), lambda i: (i // per_seq, 0, 0)),
            pl.BlockSpec(pool_w.shape, lambda i: (0, 0, 0)),
            pl.BlockSpec(pool_scale.shape, const),
            pl.BlockSpec(rw_t.shape, const),
            pl.BlockSpec(rb.shape, const),
        ],
        out_specs=_route_out_specs(tm),
        compiler_params=pltpu.CompilerParams(
            dimension_semantics=("arbitrary",), vmem_limit_bytes=VMEM_LIMIT),
        name="pool_mix",
    )(x, x, x, mod, pool_w, pool_scale, rw_t, rb)


def _slot(pstart_ref, topi_ref, rank_ref, kk, t):
    return pstart_ref[topi_ref[0, kk, t]] + rank_ref[0, kk, t]


def _dispatch_kernel(pstart_ref, topi_ref, rank_ref, hp_ref, zero_ref, xs_ref, sem):
    del zero_ref
    tm = hp_ref.shape[0]

    def issue(t, carry):
        for kk in range(TOP_K):
            dst = _slot(pstart_ref, topi_ref, rank_ref, kk, t)
            pltpu.make_async_copy(hp_ref.at[pl.ds(t, 1)], xs_ref.at[pl.ds(dst, 1)], sem).start()
        return carry

    lax.fori_loop(0, tm, issue, 0)
    for _ in range(TOP_K):
        pltpu.make_async_copy(hp_ref, xs_ref.at[pl.ds(0, tm)], sem).wait()


def _dispatch(pstart, topi, rank, hp, n_slots):
    n_tok = hp.shape[0]
    tm = topi.shape[2]
    zeros = jnp.zeros((n_slots, PACKED), I32)
    return pl.pallas_call(
        _dispatch_kernel,
        out_shape=jax.ShapeDtypeStruct((n_slots, PACKED), I32),
        grid_spec=pltpu.PrefetchScalarGridSpec(
            num_scalar_prefetch=1,
            grid=(n_tok // tm,),
            in_specs=[
                pl.BlockSpec((1, TOP_K, tm), lambda i, ps: (i, 0, 0), memory_space=pltpu.SMEM),
                pl.BlockSpec((1, TOP_K, tm), lambda i, ps: (i, 0, 0), memory_space=pltpu.SMEM),
                pl.BlockSpec((tm, PACKED), lambda i, ps: (i, 0)),
                pl.BlockSpec(memory_space=pl.ANY),
            ],
            out_specs=pl.BlockSpec(memory_space=pl.ANY),
            scratch_shapes=[pltpu.SemaphoreType.DMA],
        ),
        input_output_aliases={4: 0},
        compiler_params=pltpu.CompilerParams(
            dimension_semantics=("arbitrary",), vmem_limit_bytes=VMEM_LIMIT),
        name="moe_dispatch",
    )(pstart, topi, rank, hp, zeros)


def _expert_kernel(blk_e_ref, xs_ref, w1_ref, b1_ref, w2_ref, b2_ref, ys_ref):
    del blk_e_ref
    xb = _unpack_rows(xs_ref[...]).astype(BF16)
    gu = jnp.dot(xb, w1_ref[0], preferred_element_type=F32) + b1_ref[0]
    gate = jnp.minimum(gu[:, :D_FF], SWIGLU_LIMIT)
    up = jnp.clip(gu[:, D_FF:], -SWIGLU_LIMIT, SWIGLU_LIMIT)
    act = (up + 1.0) * (gate * jax.nn.sigmoid(SWIGLU_ALPHA * gate))
    y = jnp.dot(act.astype(BF16), w2_ref[0], preferred_element_type=F32) + b2_ref[0]
    ys_ref[...] = _pack_rows(y)


def _expert_ffn(blk_e, xs, w1, b1, w2, b2):
    n_slots = xs.shape[0]
    bm = EXPERT_TILE
    return pl.pallas_call(
        _expert_kernel,
        out_shape=jax.ShapeDtypeStruct((n_slots, PACKED), I32),
        grid_spec=pltpu.PrefetchScalarGridSpec(
            num_scalar_prefetch=1,
            grid=(n_slots // bm,),
            in_specs=[
                pl.BlockSpec((bm, PACKED), lambda i, be: (i, 0)),
                pl.BlockSpec((1, D_MODEL, 2 * D_FF), lambda i, be: (be[i], 0, 0)),
                pl.BlockSpec((1, 1, 2 * D_FF), lambda i, be: (be[i], 0, 0)),
                pl.BlockSpec((1, D_FF, D_MODEL), lambda i, be: (be[i], 0, 0)),
                pl.BlockSpec((1, 1, D_MODEL), lambda i, be: (be[i], 0, 0)),
            ],
            out_specs=pl.BlockSpec((bm, PACKED), lambda i, be: (i, 0)),
        ),
        compiler_params=pltpu.CompilerParams(
            dimension_semantics=("arbitrary",), vmem_limit_bytes=VMEM_LIMIT),
        name="expert_ffn",
    )(blk_e, xs, w1, b1, w2, b2)


def _combine_kernel(pstart_ref, topi_ref, rank_ref, ys_ref, wtm_ref, x1_ref, mod_ref, fn_ref,
                    o_ref, buf, sem, *, final):
    tm = x1_ref.shape[0]

    def issue(t, carry):
        for kk in range(TOP_K):
            src = _slot(pstart_ref, topi_ref, rank_ref, kk, t)
            pltpu.make_async_copy(ys_ref.at[pl.ds(src, 1)], buf.at[kk, pl.ds(t, 1)], sem).start()
        return carry

    lax.fori_loop(0, tm, issue, 0)
    for kk in range(TOP_K):
        pltpu.make_async_copy(ys_ref.at[pl.ds(0, tm)], buf.at[kk], sem).wait()

    wtm = wtm_ref[...]
    moe = jnp.zeros((tm, D_MODEL), F32)
    for kk in range(TOP_K):
        moe = moe + wtm[:, kk:kk + 1] * _unpack_rows(buf[kk])
    g2 = mod_ref[0][:, 5 * D_MODEL:6 * D_MODEL]
    out = x1_ref[...] + g2 * moe
    if final:
        out = _rms(out) * fn_ref[...]
    o_ref[...] = out


def _combine(pstart, topi, rank, ys, wtm, x1, mod, final_norm, seq_len, final):
    n_tok = x1.shape[0]
    tm = topi.shape[2]
    per_seq = seq_len // tm
    return pl.pallas_call(
        functools.partial(_combine_kernel, final=final),
        out_shape=jax.ShapeDtypeStruct((n_tok, D_MODEL), F32),
        grid_spec=pltpu.PrefetchScalarGridSpec(
            num_scalar_prefetch=1,
            grid=(n_tok // tm,),
            in_specs=[
                pl.BlockSpec((1, TOP_K, tm), lambda i, ps: (i, 0, 0), memory_space=pltpu.SMEM),
                pl.BlockSpec((1, TOP_K, tm), lambda i, ps: (i, 0, 0), memory_space=pltpu.SMEM),
                pl.BlockSpec(memory_space=pl.ANY),
                pl.BlockSpec((tm, LANES), lambda i, ps: (i, 0)),
                pl.BlockSpec((tm, D_MODEL), lambda i, ps: (i, 0)),
                pl.BlockSpec((1, 1, 6 * D_MODEL), lambda i, ps: (i // per_seq, 0, 0)),
                pl.BlockSpec((1, D_MODEL), lambda i, ps: (0, 0)),
            ],
            out_specs=pl.BlockSpec((tm, D_MODEL), lambda i, ps: (i, 0)),
            scratch_shapes=[pltpu.VMEM((TOP_K, tm, PACKED), I32), pltpu.SemaphoreType.DMA],
        ),
        compiler_params=pltpu.CompilerParams(
            dimension_semantics=("arbitrary",), vmem_limit_bytes=VMEM_LIMIT),
        name="moe_combine",
    )(pstart, topi, rank, ys, wtm, x1, mod, final_norm)


def _moe_layer(route, mod, w, final_norm, seq_len, final):
    x1, hp, topi, rank, wtm, counts = route
    n_tok = x1.shape[0]
    bm = EXPERT_TILE
    n_slots = n_tok * TOP_K + N_EXPERTS * bm
    n_blk = n_slots // bm
    cnt = counts[:, 0].astype(I32)
    padded = (cnt + bm - 1) // bm * bm
    pend = jnp.cumsum(padded)
    pstart = (pend - padded).astype(I32)
    blk_start = jnp.arange(n_blk, dtype=I32) * bm
    blk_e = jnp.minimum(jnp.sum(blk_start[:, None] >= pend[None, :], axis=1), N_EXPERTS - 1)
    xs = _dispatch(pstart, topi, rank, hp, n_slots)
    ys = _expert_ffn(blk_e.astype(I32), xs, w["w1"], w["b1"], w["w2"], w["b2"])
    return _combine(pstart, topi, rank, ys, wtm, x1, mod, final_norm, seq_len, final)


def _rope_tables(seq_len):
    inv_freq = 1.0 / (ROPE_THETA ** (jnp.arange(0, D_ROPE, 2, dtype=F32) / D_ROPE))
    ang = jnp.arange(seq_len, dtype=F32)[:, None] * inv_freq[None, :]
    cos, sin = jnp.cos(ang), jnp.sin(ang)
    ones = jnp.ones((seq_len, ROPE_LO), F32)
    zeros_lo = jnp.zeros((seq_len, ROPE_LO), F32)
    zeros_hi = jnp.zeros((seq_len, HEAD_PAD - ROPE_LO - D_ROPE), F32)
    zeros_h = jnp.zeros((seq_len, ROPE_HALF), F32)
    rope_c = jnp.concatenate([ones, cos, cos, zeros_hi], axis=1)
    rope_s1 = jnp.concatenate([zeros_lo, -sin, zeros_h, zeros_hi], axis=1)
    rope_s2 = jnp.concatenate([zeros_lo, zeros_h, sin, zeros_hi], axis=1)
    return cos.T, sin.T, rope_c, rope_s1, rope_s2


def _mla_weights(w_in, q_norm, kv_norm, w_uq, w_ukv, w_o, seq_len):
    d_qk = D_NOPE + D_ROPE
    q_scale = d_qk ** -0.5 * math.log2(math.e)
    pad_pe = jnp.zeros((D_MODEL, HEAD_PAD), F32).at[:, ROPE_LO:ROPE_LO + D_ROPE].set(
        w_in[:, Q_RANK + KV_RANK:])
    w_in_p = jnp.concatenate([w_in[:, :Q_RANK + KV_RANK], pad_pe], axis=1)
    w_uq_p = jnp.pad(w_uq.reshape(Q_RANK, N_HEADS, d_qk) * q_scale,
                     ((0, 0), (0, 0), (0, HEAD_PAD - d_qk)))
    w_kv = w_ukv.reshape(KV_RANK, N_HEADS, D_NOPE + D_V)
    w_uk_p = jnp.pad(w_kv[:, :, :D_NOPE], ((0, 0), (0, 0), (0, HEAD_PAD - D_NOPE)))
    cos_t, sin_t, rope_c, rope_s1, rope_s2 = _rope_tables(seq_len)
    return {
        "w_in": w_in_p.astype(BF16),
        "q_norm": q_norm.reshape(1, Q_RANK),
        "kv_norm": kv_norm.reshape(1, KV_RANK),
        "w_uq_t": w_uq_p.reshape(Q_RANK, N_HEADS * HEAD_PAD).T.astype(BF16),
        "w_uk": w_uk_p.reshape(KV_RANK, N_HEADS * HEAD_PAD).astype(BF16),
        "w_uv_t": w_kv[:, :, D_NOPE:].reshape(KV_RANK, N_HEADS * D_V).T.astype(BF16),
        "w_o": w_o.astype(BF16),
        "cos_t": cos_t, "sin_t": sin_t, "rope_c": rope_c, "rope_s1": rope_s1, "rope_s2": rope_s2,
    }


def _moe_weights(router_w, router_b, w1, b1, w2, b2):
    return {
        "rw_t": router_w.T.astype(BF16),
        "rb": router_b.reshape(N_EXPERTS, 1),
        "w1": w1.astype(BF16), "b1": b1.reshape(N_EXPERTS, 1, 2 * D_FF),
        "w2": w2.astype(BF16), "b2": b2.reshape(N_EXPERTS, 1, D_MODEL),
    }


def kernel(x_prompt, x_sample, c_prompt, c_sample, ada_w, ada_b, mla_w_in, mla_q_norm,
           mla_kv_norm, mla_w_uq, mla_w_ukv, mla_w_o, pool_w, pool_scale, router_w, router_b,
           moe_w1, moe_b1, moe_w2, moe_b2, final_norm):
    n_prompt, seq_len, _ = x_prompt.shape
    assert x_sample.shape[1] == seq_len and seq_len % TOKEN_TILE == 0
    depth = ada_w.shape[0]
    x = jnp.concatenate([x_prompt, x_sample], axis=0).reshape(-1, D_MODEL)
    n_seq = x.shape[0] // seq_len
    mods = _ada_mod(jnp.concatenate([c_prompt, c_sample], axis=0), ada_w, ada_b)
    fnorm = final_norm.reshape(1, D_MODEL)

    for i in range(depth):
        mod = mods[i].reshape(n_seq, 1, 6 * D_MODEL)
        moe_w = _moe_weights(router_w[i], router_b[i], moe_w1[i], moe_b1[i], moe_w2[i], moe_b2[i])
        j = i // 2
        if i % 2 == 0:
            w = _mla_weights(mla_w_in[j], mla_q_norm[j], mla_kv_norm[j], mla_w_uq[j],
                             mla_w_ukv[j], mla_w_o[j], seq_len)
            qt, k, vt = _mla_pre(x, mod, w, seq_len)
            ot = _attention(qt, k, vt, seq_len)
            route = _post_mix(ot, x, mod, w["w_o"], moe_w["rw_t"], moe_w["rb"], seq_len)
        else:
            route = _pool_mix(x, mod, pool_w[j].astype(BF16), pool_scale[j].reshape(1, D_MODEL),
                              moe_w["rw_t"], moe_w["rb"], seq_len)
        x = _moe_layer(route, mod, moe_w, fnorm, seq_len, final=(i == depth - 1))

    y = x.reshape(n_seq, seq_len, D_MODEL)
    return (y[:n_prompt], y[n_prompt:])
```

```python
import functools
import math

import jax
import jax.numpy as jnp
from jax import lax
from jax.experimental import pallas as pl
from jax.experimental.pallas import tpu as pltpu
from jax.experimental.pallas import tpu_sc as plsc

F32 = jnp.float32
BF16 = jnp.bfloat16
I32 = jnp.int32

D_MODEL = 1024
N_HEADS = 16
Q_RANK = 384
KV_RANK = 256
D_NOPE = 64
D_ROPE = 32
D_V = 64
ROPE_THETA = 10000.0
POOL_WINDOWS = (2, 4, 8, 16)
POOL_GROUP = D_MODEL // len(POOL_WINDOWS)
N_EXPERTS = 32
TOP_K = 4
D_FF = D_MODEL
SWIGLU_LIMIT = 7.0
SWIGLU_ALPHA = 1.702
EPS = 1e-6

LANES = 128
SUBLANES = 8
HEAD_PAD = 128
ROPE_LO = D_NOPE
ROPE_HALF = D_ROPE // 2
VMEM_LIMIT = 56 * 1024 * 1024

TOKEN_TILE = 256
EXPERT_TILE = 256
POOL_HALO = 8
PACKED = D_MODEL // 2
ROW_CHUNKS = PACKED // LANES
ROW_SHAPE = (ROW_CHUNKS, LANES)

SC_CORES = 2
SC_SUBCORES = 16
SC_WORKERS = SC_CORES * SC_SUBCORES
SC_ROWS = 64


def _rms(x):
    return x * lax.rsqrt(jnp.mean(x * x, axis=-1, keepdims=True) + EPS)


def _pack_rows(y, out_ref):
    lo = lax.bitcast_convert_type(y[:, :PACKED].astype(BF16).astype(F32), I32)
    hi = lax.bitcast_convert_type(y[:, PACKED:].astype(BF16).astype(F32), I32)
    words = lax.shift_right_logical(lo, 16) | (hi & jnp.int32(-65536))
    for j in range(ROW_CHUNKS):
        out_ref[:, j, :] = words[:, j * LANES:(j + 1) * LANES]


def _unpack_rows(in_ref):
    words = jnp.concatenate([in_ref[:, j, :] for j in range(ROW_CHUNKS)], axis=1)
    lo = lax.bitcast_convert_type(lax.shift_left(words, 16), F32)
    hi = lax.bitcast_convert_type(words & jnp.int32(-65536), F32)
    return jnp.concatenate([lo, hi], axis=1)


def _ada_kernel(c_ref, w_ref, b_ref, o_ref):
    c = c_ref[...]
    act = (c * jax.nn.sigmoid(c)).astype(BF16)
    o_ref[0] = jnp.dot(act, w_ref[0].astype(BF16), preferred_element_type=F32) + b_ref[0]


def _ada_mod(c, ada_w, ada_b):
    depth, _, n_out = ada_w.shape
    n_seq = c.shape[0]
    tn = 1536
    return pl.pallas_call(
        _ada_kernel,
        out_shape=jax.ShapeDtypeStruct((depth, n_seq, n_out), F32),
        grid=(depth, n_out // tn),
        in_specs=[
            pl.BlockSpec((n_seq, D_MODEL), lambda l, j: (0, 0)),
            pl.BlockSpec((1, D_MODEL, tn), lambda l, j: (l, 0, j)),
            pl.BlockSpec((1, 1, tn), lambda l, j: (l, 0, j)),
        ],
        out_specs=pl.BlockSpec((1, n_seq, tn), lambda l, j: (l, 0, j)),
        compiler_params=pltpu.CompilerParams(
            dimension_semantics=("arbitrary", "arbitrary"), vmem_limit_bytes=VMEM_LIMIT),
        name="ada_mod",
    )(c, ada_w, ada_b.reshape(depth, 1, n_out))


def _mla_pre_kernel(x_ref, mod_ref, win_ref, qn_ref, kvn_ref, wuqt_ref, wuk_ref, wuvt_ref,
                    cost_ref, sint_ref, ck_ref, s1k_ref, s2k_ref,
                    qt_ref, k_ref, vt_ref):
    tm = x_ref.shape[0]
    mod = mod_ref[0]
    sh1 = mod[:, 0:D_MODEL]
    sc1 = mod[:, D_MODEL:2 * D_MODEL]
    h = (_rms(x_ref[...]) * (1.0 + sc1) + sh1).astype(BF16)
    a = jnp.dot(h, win_ref[...], preferred_element_type=F32)
    cq = (_rms(a[:, :Q_RANK]) * qn_ref[...]).astype(BF16)
    ckv = (_rms(a[:, Q_RANK:Q_RANK + KV_RANK]) * kvn_ref[...]).astype(BF16)

    kpe = a[:, Q_RANK + KV_RANK:]
    kpe = (kpe * ck_ref[...]
           + pltpu.roll(kpe, LANES - ROPE_HALF, 1) * s1k_ref[...]
           + pltpu.roll(kpe, ROPE_HALF, 1) * s2k_ref[...])

    qt = lax.dot_general(wuqt_ref[...], cq, (((1,), (1,)), ((), ())),
                         preferred_element_type=F32)
    q3 = qt.reshape(N_HEADS, HEAD_PAD, tm)
    x1 = q3[:, ROPE_LO:ROPE_LO + ROPE_HALF, :]
    x2 = q3[:, ROPE_LO + ROPE_HALF:ROPE_LO + D_ROPE, :]
    cos = cost_ref[...][None]
    sin = sint_ref[...][None]
    q3 = jnp.concatenate(
        [q3[:, :ROPE_LO, :], x1 * cos - x2 * sin, x2 * cos + x1 * sin,
         q3[:, ROPE_LO + D_ROPE:, :]], axis=1)
    qt_ref[:, 0] = q3.astype(BF16)

    kn = jnp.dot(ckv, wuk_ref[...], preferred_element_type=F32)
    for hd in range(N_HEADS):
        sl = slice(hd * HEAD_PAD, (hd + 1) * HEAD_PAD)
        k_ref[:, sl] = (kn[:, sl] + kpe).astype(BF16)

    vt = lax.dot_general(wuvt_ref[...], ckv, (((1,), (1,)), ((), ())),
                         preferred_element_type=F32)
    vt_ref[:, 0] = vt.reshape(N_HEADS, D_V, tm).astype(BF16)


def _mla_pre(x, mod, w, seq_len):
    n_tok = x.shape[0]
    tm = TOKEN_TILE
    n_tiles = n_tok // tm
    per_seq = seq_len // tm
    const = lambda i: (0, 0)
    return pl.pallas_call(
        _mla_pre_kernel,
        out_shape=(
            jax.ShapeDtypeStruct((N_HEADS, n_tiles, HEAD_PAD, tm), BF16),
            jax.ShapeDtypeStruct((n_tok, N_HEADS * HEAD_PAD), BF16),
            jax.ShapeDtypeStruct((N_HEADS, n_tok // seq_len, D_V, seq_len), BF16),
        ),
        grid=(n_tiles,),
        in_specs=[
            pl.BlockSpec((tm, D_MODEL), lambda i: (i, 0)),
            pl.BlockSpec((1, 1, 6 * D_MODEL), lambda i: (i // per_seq, 0, 0)),
            pl.BlockSpec(w["w_in"].shape, const),
            pl.BlockSpec(w["q_norm"].shape, const),
            pl.BlockSpec(w["kv_norm"].shape, const),
            pl.BlockSpec(w["w_uq_t"].shape, const),
            pl.BlockSpec(w["w_uk"].shape, const),
            pl.BlockSpec(w["w_uv_t"].shape, const),
            pl.BlockSpec((ROPE_HALF, tm), lambda i: (0, i % per_seq)),
            pl.BlockSpec((ROPE_HALF, tm), lambda i: (0, i % per_seq)),
            pl.BlockSpec((tm, LANES), lambda i: (i % per_seq, 0)),
            pl.BlockSpec((tm, LANES), lambda i: (i % per_seq, 0)),
            pl.BlockSpec((tm, LANES), lambda i: (i % per_seq, 0)),
        ],
        out_specs=(
            pl.BlockSpec((N_HEADS, 1, HEAD_PAD, tm), lambda i: (0, i, 0, 0)),
            pl.BlockSpec((tm, N_HEADS * HEAD_PAD), lambda i: (i, 0)),
            pl.BlockSpec((N_HEADS, 1, D_V, tm), lambda i: (0, i // per_seq, 0, i % per_seq)),
        ),
        compiler_params=pltpu.CompilerParams(
            dimension_semantics=("arbitrary",), vmem_limit_bytes=VMEM_LIMIT),
        name="mla_pre",
    )(x, mod, w["w_in"], w["q_norm"], w["kv_norm"], w["w_uq_t"], w["w_uk"], w["w_uv_t"],
      w["cos_t"], w["sin_t"], w["rope_c"], w["rope_s1"], w["rope_s2"])


def _attention_kernel(qt_ref, k_ref, vt_ref, ot_ref):
    n_q = qt_ref.shape[1]

    def body(j, carry):
        q = qt_ref[0, j]
        s = jnp.dot(k_ref[...], q, preferred_element_type=F32)
        m = jnp.max(s, axis=0, keepdims=True)
        p = jnp.exp2(s - m)
        denom = jnp.sum(p, axis=0, keepdims=True)
        o = jnp.dot(vt_ref[0, 0], p.astype(BF16), preferred_element_type=F32)
        ot_ref[0, j] = (o * (1.0 / denom)).astype(BF16)
        return carry

    lax.fori_loop(0, n_q, body, 0)


def _attention(qt, k, vt, seq_len):
    n_heads, n_tiles, _, tq = qt.shape
    per_seq = seq_len // tq
    n_seq = n_tiles // per_seq
    return pl.pallas_call(
        _attention_kernel,
        out_shape=jax.ShapeDtypeStruct((n_heads, n_tiles, D_V, tq), BF16),
        grid=(n_seq, n_heads),
        in_specs=[
            pl.BlockSpec((1, per_seq, HEAD_PAD, tq), lambda b, h: (h, b, 0, 0)),
            pl.BlockSpec((seq_len, HEAD_PAD), lambda b, h: (b, h)),
            pl.BlockSpec((1, 1, D_V, seq_len), lambda b, h: (h, b, 0, 0)),
        ],
        out_specs=pl.BlockSpec((1, per_seq, D_V, tq), lambda b, h: (h, b, 0, 0)),
        compiler_params=pltpu.CompilerParams(
            dimension_semantics=("arbitrary", "arbitrary"), vmem_limit_bytes=VMEM_LIMIT),
        name="attention",
    )(qt, k, vt)


def _route_tail(x1, mod, rwt_ref, rb_ref, x1_ref, hp_ref, topi_ref, rank_ref, wtm_ref, cnt_ref):
    tm = x1.shape[0]
    sh2 = mod[:, 3 * D_MODEL:4 * D_MODEL]
    sc2 = mod[:, 4 * D_MODEL:5 * D_MODEL]
    x1_ref[...] = x1
    h2 = _rms(x1) * (1.0 + sc2) + sh2
    _pack_rows(h2, hp_ref)
    logits = lax.dot_general(rwt_ref[...], h2.astype(BF16), (((1,), (1,)), ((), ())),
                             preferred_element_type=F32) + rb_ref[...]
    e_iota = lax.broadcasted_iota(I32, (N_EXPERTS, tm), 0)
    vals, idxs = [], []
    work = logits
    for _ in range(TOP_K):
        m = jnp.max(work, axis=0, keepdims=True)
        idx = jnp.min(jnp.where(work == m, e_iota, N_EXPERTS), axis=0, keepdims=True)
        vals.append(m)
        idxs.append(idx)
        work = jnp.where(e_iota == idx, -jnp.inf, work)
    ex = [jnp.exp(v - vals[0]) for v in vals]
    inv = 1.0 / (ex[0] + ex[1] + ex[2] + ex[3])
    topw = jnp.concatenate([e * inv for e in ex], axis=0)
    topi_ref[0] = jnp.concatenate(idxs, axis=0)

    @pl.when(pl.program_id(0) == 0)
    def _():
        cnt_ref[...] = jnp.zeros_like(cnt_ref)

    row = lax.broadcasted_iota(I32, (tm, tm), 0)
    col = lax.broadcasted_iota(I32, (tm, tm), 1)
    earlier = (row < col).astype(BF16)
    running = cnt_ref[...][:, 0:1]
    ranks = []
    for kk in range(TOP_K):
        onehot = (e_iota == idxs[kk]).astype(F32)
        before = jnp.dot(onehot.astype(BF16), earlier, preferred_element_type=F32)
        ranks.append(jnp.sum(onehot * (running + before), axis=0, keepdims=True))
        running = running + jnp.sum(onehot, axis=1, keepdims=True)
    rank_ref[0] = jnp.concatenate(ranks, axis=0).astype(I32)
    cnt_ref[...] = jnp.broadcast_to(running, cnt_ref.shape)

    wpad = jnp.concatenate([topw, jnp.zeros((LANES - TOP_K, tm), F32)], axis=0)
    wtm_ref[...] = wpad.T


def _route_out_shapes(n_tok, tm):
    n_tiles = n_tok // tm
    return (
        jax.ShapeDtypeStruct((n_tok, D_MODEL), F32),
        jax.ShapeDtypeStruct((n_tok,) + ROW_SHAPE, I32),
        jax.ShapeDtypeStruct((n_tiles, TOP_K, tm), I32),
        jax.ShapeDtypeStruct((n_tiles, TOP_K, tm), I32),
        jax.ShapeDtypeStruct((n_tok, LANES), F32),
        jax.ShapeDtypeStruct((N_EXPERTS, LANES), F32),
    )


def _route_out_specs(tm):
    return (
        pl.BlockSpec((tm, D_MODEL), lambda i: (i, 0)),
        pl.BlockSpec((tm,) + ROW_SHAPE, lambda i: (i, 0, 0)),
        pl.BlockSpec((1, TOP_K, tm), lambda i: (i, 0, 0)),
        pl.BlockSpec((1, TOP_K, tm), lambda i: (i, 0, 0)),
        pl.BlockSpec((tm, LANES), lambda i: (i, 0)),
        pl.BlockSpec((N_EXPERTS, LANES), lambda i: (0, 0)),
    )


def _post_mix_kernel(ot_ref, x_ref, mod_ref, wo_ref, rwt_ref, rb_ref, *out_refs):
    tm = x_ref.shape[0]
    mod = mod_ref[0]
    g1 = mod[:, 2 * D_MODEL:3 * D_MODEL]
    ot = ot_ref[:, 0].reshape(N_HEADS * D_V, tm)
    mix = lax.dot_general(ot, wo_ref[...], (((0,), (0,)), ((), ())),
                          preferred_element_type=F32)
    _route_tail(x_ref[...] + g1 * mix, mod, rwt_ref, rb_ref, *out_refs)


def _post_mix(ot, x, mod, w_o, rw_t, rb, seq_len):
    n_tok = x.shape[0]
    tm = ot.shape[3]
    per_seq = seq_len // tm
    const = lambda i: (0, 0)
    return pl.pallas_call(
        _post_mix_kernel,
        out_shape=_route_out_shapes(n_tok, tm),
        grid=(n_tok // tm,),
        in_specs=[
            pl.BlockSpec((N_HEADS, 1, D_V, tm), lambda i: (0, i, 0, 0)),
            pl.BlockSpec((tm, D_MODEL), lambda i: (i, 0)),
            pl.BlockSpec((1, 1, 6 * D_MODEL), lambda i: (i // per_seq, 0, 0)),
            pl.BlockSpec(w_o.shape, const),
            pl.BlockSpec(rw_t.shape, const),
            pl.BlockSpec(rb.shape, const),
        ],
        out_specs=_route_out_specs(tm),
        compiler_params=pltpu.CompilerParams(
            dimension_semantics=("arbitrary",), vmem_limit_bytes=VMEM_LIMIT),
        name="post_mix",
    )(ot, x, mod, w_o, rw_t, rb)


def _pool_mix_kernel(x_ref, xp_ref, xn_ref, mod_ref, pw_ref, ps_ref, rwt_ref, rb_ref,
                     *out_refs, seq_len):
    tm = x_ref.shape[0]
    per_seq = seq_len // tm
    mod = mod_ref[0]
    sh1 = mod[:, 0:D_MODEL]
    sc1 = mod[:, D_MODEL:2 * D_MODEL]
    g1 = mod[:, 2 * D_MODEL:3 * D_MODEL]
    pos0 = (pl.program_id(0) % per_seq) * tm

    def normed(v):
        return _rms(v) * (1.0 + sc1) + sh1

    x = x_ref[...]
    h = normed(x)
    h_prev = jnp.where(pos0 > 0, normed(xp_ref[...]), 0.0)
    h_next = jnp.where(pos0 + tm < seq_len, normed(xn_ref[...]), 0.0)
    hext = jnp.concatenate([h_prev, h, h_next], axis=0)
    n_ext = tm + 2 * POOL_HALO
    pos = pos0 + lax.broadcasted_iota(I32, (tm, 1), 0)

    outs = []
    for g, win in enumerate(POOL_WINDOWS):
        left = win // 2
        right = win - 1 - left
        cols = slice(g * POOL_GROUP, (g + 1) * POOL_GROUP)
        s = hext[:, cols]
        span = 1
        while span < win:
            s = s + pltpu.roll(s, span, 0)
            span *= 2
        if right:
            s = pltpu.roll(s, n_ext - right, 0)
        num = s[POOL_HALO:POOL_HALO + tm, :]
        count = (jnp.minimum(pos + right + 1, seq_len) - jnp.maximum(pos - left, 0)).astype(F32)
        diff = (num / count - h[:, cols]).astype(BF16)
        outs.append(jnp.dot(diff, pw_ref[g], preferred_element_type=F32))
    mix = jnp.concatenate(outs, axis=1) * ps_ref[...]
    _route_tail(x + g1 * mix, mod, rwt_ref, rb_ref, *out_refs)


def _pool_mix(x, mod, pool_w, pool_scale, rw_t, rb, seq_len):
    n_tok = x.shape[0]
    tm = TOKEN_TILE
    per_seq = seq_len // tm
    halo_per_tile = tm // POOL_HALO
    n_halo = n_tok // POOL_HALO
    const = lambda i: (0, 0)
    return pl.pallas_call(
        functools.partial(_pool_mix_kernel, seq_len=seq_len),
        out_shape=_route_out_shapes(n_tok, tm),
        grid=(n_tok // tm,),
        in_specs=[
            pl.BlockSpec((tm, D_MODEL), lambda i: (i, 0)),
            pl.BlockSpec((POOL_HALO, D_MODEL),
                         lambda i: (jnp.maximum(i * halo_per_tile - 1, 0), 0)),
            pl.BlockSpec((POOL_HALO, D_MODEL),
                         lambda i: (jnp.minimum((i + 1) * halo_per_tile, n_halo - 1), 0)),
            pl.BlockSpec((1, 1, 6 * D_MODEL), lambda i: (i // per_seq, 0, 0)),
            pl.BlockSpec(pool_w.shape, lambda i: (0, 0, 0)),
            pl.BlockSpec(pool_scale.shape, const),
            pl.BlockSpec(rw_t.shape, const),
            pl.BlockSpec(rb.shape, const),
        ],
        out_specs=_route_out_specs(tm),
        compiler_params=pltpu.CompilerParams(
            dimension_semantics=("arbitrary",), vmem_limit_bytes=VMEM_LIMIT),
        name="pool_mix",
    )(x, x, x, mod, pool_w, pool_scale, rw_t, rb)


def _slots_kernel(pstart_ref, topi_ref, rank_ref, dest_ref):
    topi = topi_ref[...]
    start = jnp.zeros_like(topi)
    for e in range(N_EXPERTS):
        start = jnp.where(topi == e, pstart_ref[e], start)
    dest_ref[...] = start + rank_ref[...]


def _slots(pstart, topi, rank):
    n_tiles, _, tm = topi.shape
    tb = math.gcd(n_tiles, 32)
    spec = pl.BlockSpec((tb, TOP_K, tm), lambda i, ps: (i, 0, 0))
    return pl.pallas_call(
        _slots_kernel,
        out_shape=jax.ShapeDtypeStruct(topi.shape, I32),
        grid_spec=pltpu.PrefetchScalarGridSpec(
            num_scalar_prefetch=1, grid=(n_tiles // tb,), in_specs=[spec, spec], out_specs=spec),
        compiler_params=pltpu.CompilerParams(dimension_semantics=("arbitrary",)),
        name="moe_slots",
    )(pstart, topi, rank)


def _sc_chunk_rows(c, tm):
    per_tile = tm // SC_ROWS
    tile = c // per_tile
    part = c % per_tile
    return [(tile * TOP_K + kk) * per_tile + part for kk in range(TOP_K)]


def _sc_dispatch(hp, dest, n_slots, tm):
    n_tok = hp.shape[0]
    rows_per_w = dest.shape[0] // SC_WORKERS
    chunks_per_w = n_tok // SC_WORKERS // SC_ROWS
    mesh = plsc.VectorSubcoreMesh(core_axis_name="c", subcore_axis_name="s")

    @functools.partial(
        pl.kernel, mesh=mesh,
        out_type=jax.ShapeDtypeStruct((n_slots,) + ROW_SHAPE, I32),
        scratch_types=[
            pltpu.VMEM((rows_per_w, SC_ROWS), I32),
            pltpu.VMEM((SC_ROWS,) + ROW_SHAPE, I32),
            pltpu.SemaphoreType.DMA,
        ],
        name="sc_dispatch",
    )
    def run(hp_hbm, dest_hbm, xs_hbm, dest_v, rows_v, sem):
        wid = lax.axis_index("s") * SC_CORES + lax.axis_index("c")
        pltpu.sync_copy(dest_hbm.at[pl.ds(wid * rows_per_w, rows_per_w)], dest_v)

        @pl.loop(0, chunks_per_w)
        def _(c):
            tok0 = (wid * chunks_per_w + c) * SC_ROWS
            pltpu.sync_copy(hp_hbm.at[pl.ds(tok0, SC_ROWS)], rows_v)
            copies = [pltpu.async_copy(rows_v, xs_hbm.at[dest_v.at[row]], sem)
                      for row in _sc_chunk_rows(c, tm)]
            for cp in copies:
                cp.wait()

    return run(hp, dest)


def _sc_gather(ys, dest, tm):
    n_tok = dest.shape[0] * SC_ROWS // TOP_K
    rows_per_w = dest.shape[0] // SC_WORKERS
    chunks_per_w = n_tok // SC_WORKERS // SC_ROWS
    mesh = plsc.VectorSubcoreMesh(core_axis_name="c", subcore_axis_name="s")

    @functools.partial(
        pl.kernel, mesh=mesh,
        out_type=jax.ShapeDtypeStruct((TOP_K, n_tok) + ROW_SHAPE, I32),
        scratch_types=[
            pltpu.VMEM((rows_per_w, SC_ROWS), I32),
            pltpu.VMEM((SC_ROWS,) + ROW_SHAPE, I32),
            pltpu.SemaphoreType.DMA,
        ],
        name="sc_gather",
    )
    def run(ys_hbm, dest_hbm, yg_hbm, dest_v, rows_v, sem):
        wid = lax.axis_index("s") * SC_CORES + lax.axis_index("c")
        pltpu.sync_copy(dest_hbm.at[pl.ds(wid * rows_per_w, rows_per_w)], dest_v)

        @pl.loop(0, chunks_per_w)
        def _(c):
            tok0 = (wid * chunks_per_w + c) * SC_ROWS
            for kk, row in enumerate(_sc_chunk_rows(c, tm)):
                pltpu.async_copy(ys_hbm.at[dest_v.at[row]], rows_v, sem).wait()
                pltpu.sync_copy(rows_v, yg_hbm.at[kk, pl.ds(tok0, SC_ROWS)])

    return run(ys, dest)


def _expert_kernel(blk_e_ref, xs_ref, w1_ref, b1_ref, w2_ref, b2_ref, ys_ref):
    del blk_e_ref
    xb = _unpack_rows(xs_ref).astype(BF16)
    gu = jnp.dot(xb, w1_ref[0], preferred_element_type=F32) + b1_ref[0]
    gate = jnp.minimum(gu[:, :D_FF], SWIGLU_LIMIT)
    up = jnp.clip(gu[:, D_FF:], -SWIGLU_LIMIT, SWIGLU_LIMIT)
    act = (up + 1.0) * (gate * jax.nn.sigmoid(SWIGLU_ALPHA * gate))
    y = jnp.dot(act.astype(BF16), w2_ref[0], preferred_element_type=F32) + b2_ref[0]
    _pack_rows(y, ys_ref)


def _expert_ffn(blk_e, xs, w1, b1, w2, b2):
    n_slots = xs.shape[0]
    bm = EXPERT_TILE
    return pl.pallas_call(
        _expert_kernel,
        out_shape=jax.ShapeDtypeStruct((n_slots,) + ROW_SHAPE, I32),
        grid_spec=pltpu.PrefetchScalarGridSpec(
            num_scalar_prefetch=1,
            grid=(n_slots // bm,),
            in_specs=[
                pl.BlockSpec((bm,) + ROW_SHAPE, lambda i, be: (i, 0, 0)),
                pl.BlockSpec((1, D_MODEL, 2 * D_FF), lambda i, be: (be[i], 0, 0)),
                pl.BlockSpec((1, 1, 2 * D_FF), lambda i, be: (be[i], 0, 0)),
                pl.BlockSpec((1, D_FF, D_MODEL), lambda i, be: (be[i], 0, 0)),
                pl.BlockSpec((1, 1, D_MODEL), lambda i, be: (be[i], 0, 0)),
            ],
            out_specs=pl.BlockSpec((bm,) + ROW_SHAPE, lambda i, be: (i, 0, 0)),
        ),
        compiler_params=pltpu.CompilerParams(
            dimension_semantics=("arbitrary",), vmem_limit_bytes=VMEM_LIMIT),
        name="expert_ffn",
    )(blk_e, xs, w1, b1, w2, b2)


def _combine_kernel(yg_ref, wtm_ref, x1_ref, mod_ref, fn_ref, o_ref, *, final):
    tm = x1_ref.shape[0]
    wtm = wtm_ref[...]
    moe = jnp.zeros((tm, D_MODEL), F32)
    for kk in range(TOP_K):
        moe = moe + wtm[:, kk:kk + 1] * _unpack_rows(yg_ref.at[kk])
    g2 = mod_ref[0][:, 5 * D_MODEL:6 * D_MODEL]
    out = x1_ref[...] + g2 * moe
    if final:
        out = _rms(out) * fn_ref[...]
    o_ref[...] = out


def _combine(yg, wtm, x1, mod, final_norm, seq_len, final):
    n_tok = x1.shape[0]
    tm = TOKEN_TILE
    per_seq = seq_len // tm
    return pl.pallas_call(
        functools.partial(_combine_kernel, final=final),
        out_shape=jax.ShapeDtypeStruct((n_tok, D_MODEL), F32),
        grid=(n_tok // tm,),
        in_specs=[
            pl.BlockSpec((TOP_K, tm) + ROW_SHAPE, lambda i: (0, i, 0, 0)),
            pl.BlockSpec((tm, LANES), lambda i: (i, 0)),
            pl.BlockSpec((tm, D_MODEL), lambda i: (i, 0)),
            pl.BlockSpec((1, 1, 6 * D_MODEL), lambda i: (i // per_seq, 0, 0)),
            pl.BlockSpec((1, D_MODEL), lambda i: (0, 0)),
        ],
        out_specs=pl.BlockSpec((tm, D_MODEL), lambda i: (i, 0)),
        compiler_params=pltpu.CompilerParams(
            dimension_semantics=("arbitrary",), vmem_limit_bytes=VMEM_LIMIT),
        name="moe_combine",
    )(yg, wtm, x1, mod, final_norm)


def _moe_layer(route, mod, w, final_norm, seq_len, final):
    x1, hp, topi, rank, wtm, counts = route
    n_tok = x1.shape[0]
    bm = EXPERT_TILE
    n_slots = n_tok * TOP_K + N_EXPERTS * bm
    n_blk = n_slots // bm
    cnt = counts[:, 0].astype(I32)
    padded = (cnt + bm - 1) // bm * bm
    pend = jnp.cumsum(padded)
    pstart = (pend - padded).astype(I32)
    blk_start = jnp.arange(n_blk, dtype=I32) * bm
    blk_e = jnp.minimum(jnp.sum(blk_start[:, None] >= pend[None, :], axis=1), N_EXPERTS - 1)
    tm = topi.shape[2]
    dest = _slots(pstart, topi, rank).reshape(-1, SC_ROWS)
    xs = _sc_dispatch(hp, dest, n_slots, tm)
    ys = _expert_ffn(blk_e.astype(I32), xs, w["w1"], w["b1"], w["w2"], w["b2"])
    yg = _sc_gather(ys, dest, tm)
    return _combine(yg, wtm, x1, mod, final_norm, seq_len, final)


def _rope_tables(seq_len):
    inv_freq = 1.0 / (ROPE_THETA ** (jnp.arange(0, D_ROPE, 2, dtype=F32) / D_ROPE))
    ang = jnp.arange(seq_len, dtype=F32)[:, None] * inv_freq[None, :]
    cos, sin = jnp.cos(ang), jnp.sin(ang)
    ones = jnp.ones((seq_len, ROPE_LO), F32)
    zeros_lo = jnp.zeros((seq_len, ROPE_LO), F32)
    zeros_hi = jnp.zeros((seq_len, HEAD_PAD - ROPE_LO - D_ROPE), F32)
    zeros_h = jnp.zeros((seq_len, ROPE_HALF), F32)
    rope_c = jnp.concatenate([ones, cos, cos, zeros_hi], axis=1)
    rope_s1 = jnp.concatenate([zeros_lo, -sin, zeros_h, zeros_hi], axis=1)
    rope_s2 = jnp.concatenate([zeros_lo, zeros_h, sin, zeros_hi], axis=1)
    return cos.T, sin.T, rope_c, rope_s1, rope_s2


def _mla_weights(w_in, q_norm, kv_norm, w_uq, w_ukv, w_o, seq_len):
    d_qk = D_NOPE + D_ROPE
    q_scale = d_qk ** -0.5 * math.log2(math.e)
    pad_pe = jnp.zeros((D_MODEL, HEAD_PAD), F32).at[:, ROPE_LO:ROPE_LO + D_ROPE].set(
        w_in[:, Q_RANK + KV_RANK:])
    w_in_p = jnp.concatenate([w_in[:, :Q_RANK + KV_RANK], pad_pe], axis=1)
    w_uq_p = jnp.pad(w_uq.reshape(Q_RANK, N_HEADS, d_qk) * q_scale,
                     ((0, 0), (0, 0), (0, HEAD_PAD - d_qk)))
    w_kv = w_ukv.reshape(KV_RANK, N_HEADS, D_NOPE + D_V)
    w_uk_p = jnp.pad(w_kv[:, :, :D_NOPE], ((0, 0), (0, 0), (0, HEAD_PAD - D_NOPE)))
    cos_t, sin_t, rope_c, rope_s1, rope_s2 = _rope_tables(seq_len)
    return {
        "w_in": w_in_p.astype(BF16),
        "q_norm": q_norm.reshape(1, Q_RANK),
        "kv_norm": kv_norm.reshape(1, KV_RANK),
        "w_uq_t": w_uq_p.reshape(Q_RANK, N_HEADS * HEAD_PAD).T.astype(BF16),
        "w_uk": w_uk_p.reshape(KV_RANK, N_HEADS * HEAD_PAD).astype(BF16),
        "w_uv_t": w_kv[:, :, D_NOPE:].reshape(KV_RANK, N_HEADS * D_V).T.astype(BF16),
        "w_o": w_o.astype(BF16),
        "cos_t": cos_t, "sin_t": sin_t, "rope_c": rope_c, "rope_s1": rope_s1, "rope_s2": rope_s2,
    }


def _moe_weights(router_w, router_b, w1, b1, w2, b2):
    return {
        "rw_t": router_w.T.astype(BF16),
        "rb": router_b.reshape(N_EXPERTS, 1),
        "w1": w1.astype(BF16), "b1": b1.reshape(N_EXPERTS, 1, 2 * D_FF),
        "w2": w2.astype(BF16), "b2": b2.reshape(N_EXPERTS, 1, D_MODEL),
    }


def kernel(x_prompt, x_sample, c_prompt, c_sample, ada_w, ada_b, mla_w_in, mla_q_norm,
           mla_kv_norm, mla_w_uq, mla_w_ukv, mla_w_o, pool_w, pool_scale, router_w, router_b,
           moe_w1, moe_b1, moe_w2, moe_b2, final_norm):
    n_prompt, seq_len, _ = x_prompt.shape
    assert x_sample.shape[1] == seq_len and seq_len % TOKEN_TILE == 0
    depth = ada_w.shape[0]
    x = jnp.concatenate([x_prompt, x_sample], axis=0).reshape(-1, D_MODEL)
    n_seq = x.shape[0] // seq_len
    mods = _ada_mod(jnp.concatenate([c_prompt, c_sample], axis=0), ada_w, ada_b)
    fnorm = final_norm.reshape(1, D_MODEL)

    for i in range(depth):
        mod = mods[i].reshape(n_seq, 1, 6 * D_MODEL)
        moe_w = _moe_weights(router_w[i], router_b[i], moe_w1[i], moe_b1[i], moe_w2[i], moe_b2[i])
        j = i // 2
        if i % 2 == 0:
            w = _mla_weights(mla_w_in[j], mla_q_norm[j], mla_kv_norm[j], mla_w_uq[j],
                             mla_w_ukv[j], mla_w_o[j], seq_len)
            qt, k, vt = _mla_pre(x, mod, w, seq_len)
            ot = _attention(qt, k, vt, seq_len)
            route = _post_mix(ot, x, mod, w["w_o"], moe_w["rw_t"], moe_w["rb"], seq_len)
        else:
            route = _pool_mix(x, mod, pool_w[j].astype(BF16), pool_scale[j].reshape(1, D_MODEL),
                              moe_w["rw_t"], moe_w["rb"], seq_len)
        x = _moe_layer(route, mod, moe_w, fnorm, seq_len, final=(i == depth - 1))

    y = x.reshape(n_seq, seq_len, D_MODEL)
    return (y[:n_prompt], y[n_prompt:])
```

```python
import functools
import math

import jax
import jax.numpy as jnp
from jax import lax
from jax.experimental import pallas as pl
from jax.experimental.pallas import tpu as pltpu
from jax.experimental.pallas import tpu_sc as plsc

F32 = jnp.float32
BF16 = jnp.bfloat16
I32 = jnp.int32

D_MODEL = 1024
N_HEADS = 16
Q_RANK = 384
KV_RANK = 256
D_NOPE = 64
D_ROPE = 32
D_V = 64
ROPE_THETA = 10000.0
POOL_WINDOWS = (2, 4, 8, 16)
POOL_GROUP = D_MODEL // len(POOL_WINDOWS)
N_EXPERTS = 32
TOP_K = 4
D_FF = D_MODEL
SWIGLU_LIMIT = 7.0
SWIGLU_ALPHA = 1.702
EPS = 1e-6

LANES = 128
SUBLANES = 8
HEAD_PAD = 128
ROPE_LO = D_NOPE
ROPE_HALF = D_ROPE // 2
VMEM_LIMIT = 56 * 1024 * 1024

TOKEN_TILE = 256
EXPERT_TILE = 512
EXPERT_SUB = 256
POOL_HALO = 8
PACKED = D_MODEL // 2
ROW_CHUNKS = PACKED // LANES
ROW_SHAPE = (ROW_CHUNKS, LANES)

SC_CORES = 2
SC_SUBCORES = 16
SC_WORKERS = SC_CORES * SC_SUBCORES
SC_ROWS = 64


def _rms(x):
    return x * lax.rsqrt(jnp.mean(x * x, axis=-1, keepdims=True) + EPS)


def _pack_rows(y, out_ref, row0=0):
    n = y.shape[0]
    lo = lax.bitcast_convert_type(y[:, :PACKED].astype(BF16).astype(F32), I32)
    hi = lax.bitcast_convert_type(y[:, PACKED:].astype(BF16).astype(F32), I32)
    words = lax.shift_right_logical(lo, 16) | (hi & jnp.int32(-65536))
    for j in range(ROW_CHUNKS):
        out_ref[pl.ds(row0 * ROW_CHUNKS + j, n, stride=ROW_CHUNKS), :] = (
            words[:, j * LANES:(j + 1) * LANES])


def _unpack_rows(in_ref, row0, n):
    words = jnp.concatenate(
        [in_ref[pl.ds(row0 * ROW_CHUNKS + j, n, stride=ROW_CHUNKS), :] for j in range(ROW_CHUNKS)],
        axis=1)
    lo = lax.bitcast_convert_type(lax.shift_left(words, 16), F32)
    hi = lax.bitcast_convert_type(words & jnp.int32(-65536), F32)
    return jnp.concatenate([lo, hi], axis=1)


def _part_specs(parts, tm):
    if len(parts) == 1:
        return [pl.BlockSpec((tm, D_MODEL), lambda i: (i, 0))]
    n0 = parts[0].shape[0] // tm
    return [pl.BlockSpec((tm, D_MODEL), lambda i: (jnp.minimum(i, n0 - 1), 0)),
            pl.BlockSpec((tm, D_MODEL), lambda i: (jnp.maximum(i - n0, 0), 0))]


def _part_load(refs, n0):
    if len(refs) == 1:
        return refs[0][...]
    return jnp.where(pl.program_id(0) < n0, refs[0][...], refs[1][...])


def _ada_kernel(c_ref, w_ref, b_ref, o_ref):
    c = c_ref[...]
    act = (c * jax.nn.sigmoid(c)).astype(BF16)
    o_ref[0] = jnp.dot(act, w_ref[0].astype(BF16), preferred_element_type=F32) + b_ref[0]


def _ada_mod(c, ada_w, ada_b):
    depth, _, n_out = ada_w.shape
    n_seq = c.shape[0]
    tn = 1536
    return pl.pallas_call(
        _ada_kernel,
        out_shape=jax.ShapeDtypeStruct((depth, n_seq, n_out), F32),
        grid=(depth, n_out // tn),
        in_specs=[
            pl.BlockSpec((n_seq, D_MODEL), lambda l, j: (0, 0)),
            pl.BlockSpec((1, D_MODEL, tn), lambda l, j: (l, 0, j)),
            pl.BlockSpec((1, 1, tn), lambda l, j: (l, 0, j)),
        ],
        out_specs=pl.BlockSpec((1, n_seq, tn), lambda l, j: (l, 0, j)),
        compiler_params=pltpu.CompilerParams(
            dimension_semantics=("arbitrary", "arbitrary"), vmem_limit_bytes=VMEM_LIMIT),
        name="ada_mod",
    )(c, ada_w, ada_b.reshape(depth, 1, n_out))


def _mla_pre_kernel(*refs, n_x, n0):
    x_refs = refs[:n_x]
    (mod_ref, win_ref, qn_ref, kvn_ref, wuqt_ref, wuk_ref, wuvt_ref,
     cost_ref, sint_ref, ck_ref, s1k_ref, s2k_ref, qt_ref, k_ref, vt_ref) = refs[n_x:]
    tm = x_refs[0].shape[0]
    mod = mod_ref[0]
    sh1 = mod[:, 0:D_MODEL]
    sc1 = mod[:, D_MODEL:2 * D_MODEL]
    h = (_rms(_part_load(x_refs, n0)) * (1.0 + sc1) + sh1).astype(BF16)
    a = jnp.dot(h, win_ref[...], preferred_element_type=F32)
    cq = (_rms(a[:, :Q_RANK]) * qn_ref[...]).astype(BF16)
    ckv = (_rms(a[:, Q_RANK:Q_RANK + KV_RANK]) * kvn_ref[...]).astype(BF16)

    kpe = a[:, Q_RANK + KV_RANK:]
    kpe = (kpe * ck_ref[...]
           + pltpu.roll(kpe, LANES - ROPE_HALF, 1) * s1k_ref[...]
           + pltpu.roll(kpe, ROPE_HALF, 1) * s2k_ref[...])

    qt = lax.dot_general(wuqt_ref[...], cq, (((1,), (1,)), ((), ())),
                         preferred_element_type=F32)
    q3 = qt.reshape(N_HEADS, HEAD_PAD, tm)
    x1 = q3[:, ROPE_LO:ROPE_LO + ROPE_HALF, :]
    x2 = q3[:, ROPE_LO + ROPE_HALF:ROPE_LO + D_ROPE, :]
    cos = cost_ref[...][None]
    sin = sint_ref[...][None]
    q3 = jnp.concatenate(
        [q3[:, :ROPE_LO, :], x1 * cos - x2 * sin, x2 * cos + x1 * sin,
         q3[:, ROPE_LO + D_ROPE:, :]], axis=1)
    qt_ref[:, 0] = q3.astype(BF16)

    kn = jnp.dot(ckv, wuk_ref[...], preferred_element_type=F32)
    for hd in range(N_HEADS):
        sl = slice(hd * HEAD_PAD, (hd + 1) * HEAD_PAD)
        k_ref[:, sl] = (kn[:, sl] + kpe).astype(BF16)

    vt = lax.dot_general(wuvt_ref[...], ckv, (((1,), (1,)), ((), ())),
                         preferred_element_type=F32)
    vt_ref[:, 0] = vt.reshape(N_HEADS, D_V, tm).astype(BF16)


def _mla_pre(x_parts, mod, w, seq_len):
    n_tok = sum(p.shape[0] for p in x_parts)
    tm = TOKEN_TILE
    n_tiles = n_tok // tm
    per_seq = seq_len // tm
    const = lambda i: (0, 0)
    return pl.pallas_call(
        functools.partial(_mla_pre_kernel, n_x=len(x_parts), n0=x_parts[0].shape[0] // tm),
        out_shape=(
            jax.ShapeDtypeStruct((N_HEADS, n_tiles, HEAD_PAD, tm), BF16),
            jax.ShapeDtypeStruct((n_tok, N_HEADS * HEAD_PAD), BF16),
            jax.ShapeDtypeStruct((N_HEADS, n_tok // seq_len, D_V, seq_len), BF16),
        ),
        grid=(n_tiles,),
        in_specs=_part_specs(x_parts, tm) + [
            pl.BlockSpec((1, 1, 6 * D_MODEL), lambda i: (i // per_seq, 0, 0)),
            pl.BlockSpec(w["w_in"].shape, const),
            pl.BlockSpec(w["q_norm"].shape, const),
            pl.BlockSpec(w["kv_norm"].shape, const),
            pl.BlockSpec(w["w_uq_t"].shape, const),
            pl.BlockSpec(w["w_uk"].shape, const),
            pl.BlockSpec(w["w_uv_t"].shape, const),
            pl.BlockSpec((ROPE_HALF, tm), lambda i: (0, i % per_seq)),
            pl.BlockSpec((ROPE_HALF, tm), lambda i: (0, i % per_seq)),
            pl.BlockSpec((tm, LANES), lambda i: (i % per_seq, 0)),
            pl.BlockSpec((tm, LANES), lambda i: (i % per_seq, 0)),
            pl.BlockSpec((tm, LANES), lambda i: (i % per_seq, 0)),
        ],
        out_specs=(
            pl.BlockSpec((N_HEADS, 1, HEAD_PAD, tm), lambda i: (0, i, 0, 0)),
            pl.BlockSpec((tm, N_HEADS * HEAD_PAD), lambda i: (i, 0)),
            pl.BlockSpec((N_HEADS, 1, D_V, tm), lambda i: (0, i // per_seq, 0, i % per_seq)),
        ),
        compiler_params=pltpu.CompilerParams(
            dimension_semantics=("arbitrary",), vmem_limit_bytes=VMEM_LIMIT),
        name="mla_pre",
    )(*x_parts, mod, w["w_in"], w["q_norm"], w["kv_norm"], w["w_uq_t"], w["w_uk"], w["w_uv_t"],
      w["cos_t"], w["sin_t"], w["rope_c"], w["rope_s1"], w["rope_s2"])


def _attention_kernel(qt_ref, k_ref, vt_ref, ot_ref, s0_ref, s1_ref):
    n_q = qt_ref.shape[1]

    def scores(j, s_ref):
        s = jnp.dot(k_ref[...], qt_ref[0, j], preferred_element_type=F32)
        s_ref[...] = s
        return jnp.max(s, axis=0, keepdims=True)

    def finish(j, s_ref, m):
        p = jnp.exp2(s_ref[...] - m)
        denom = jnp.sum(p, axis=0, keepdims=True)
        o = jnp.dot(vt_ref[0, 0], p.astype(BF16), preferred_element_type=F32)
        ot_ref[0, j] = (o * (1.0 / denom)).astype(BF16)

    def body(i, m0):
        j = 2 * i
        m1 = scores(j + 1, s1_ref)
        finish(j, s0_ref, m0)
        m0 = scores(j + 2, s0_ref)
        finish(j + 1, s1_ref, m1)
        return m0

    m0 = lax.fori_loop(0, n_q // 2 - 1, body, scores(0, s0_ref))
    m1 = scores(n_q - 1, s1_ref)
    finish(n_q - 2, s0_ref, m0)
    finish(n_q - 1, s1_ref, m1)


def _attention(qt, k, vt, seq_len):
    n_heads, n_tiles, _, tq = qt.shape
    per_seq = seq_len // tq
    n_seq = n_tiles // per_seq
    return pl.pallas_call(
        _attention_kernel,
        out_shape=jax.ShapeDtypeStruct((n_heads, n_tiles, D_V, tq), BF16),
        grid=(n_seq, n_heads),
        in_specs=[
            pl.BlockSpec((1, per_seq, HEAD_PAD, tq), lambda b, h: (h, b, 0, 0)),
            pl.BlockSpec((seq_len, HEAD_PAD), lambda b, h: (b, h)),
            pl.BlockSpec((1, 1, D_V, seq_len), lambda b, h: (h, b, 0, 0)),
        ],
        out_specs=pl.BlockSpec((1, per_seq, D_V, tq), lambda b, h: (h, b, 0, 0)),
        scratch_shapes=[pltpu.VMEM((seq_len, tq), F32), pltpu.VMEM((seq_len, tq), F32)],
        compiler_params=pltpu.CompilerParams(
            dimension_semantics=("arbitrary", "arbitrary"), vmem_limit_bytes=VMEM_LIMIT),
        name="attention",
    )(qt, k, vt)


def _route_tail(x1, mod, rwt_ref, rb_ref, x1_ref, hp_ref, topi_ref, rank_ref, wtm_ref, cnt_ref):
    tm = x1.shape[0]
    sh2 = mod[:, 3 * D_MODEL:4 * D_MODEL]
    sc2 = mod[:, 4 * D_MODEL:5 * D_MODEL]
    x1_ref[...] = x1
    h2 = _rms(x1) * (1.0 + sc2) + sh2
    _pack_rows(h2, hp_ref)
    logits = lax.dot_general(rwt_ref[...], h2.astype(BF16), (((1,), (1,)), ((), ())),
                             preferred_element_type=F32) + rb_ref[...]
    e_iota = lax.broadcasted_iota(I32, (N_EXPERTS, tm), 0)
    vals, idxs = [], []
    work = logits
    for _ in range(TOP_K):
        m = jnp.max(work, axis=0, keepdims=True)
        idx = jnp.min(jnp.where(work == m, e_iota, N_EXPERTS), axis=0, keepdims=True)
        vals.append(m)
        idxs.append(idx)
        work = jnp.where(e_iota == idx, -jnp.inf, work)
    ex = [jnp.exp(v - vals[0]) for v in vals]
    inv = 1.0 / (ex[0] + ex[1] + ex[2] + ex[3])
    topw = jnp.concatenate([e * inv for e in ex], axis=0)
    topi_ref[0] = jnp.concatenate(idxs, axis=0)

    @pl.when(pl.program_id(0) == 0)
    def _():
        cnt_ref[...] = jnp.zeros_like(cnt_ref)

    row = lax.broadcasted_iota(I32, (tm, tm), 0)
    col = lax.broadcasted_iota(I32, (tm, tm), 1)
    earlier = (row < col).astype(BF16)
    running = cnt_ref[...][:, 0:1]
    ranks = []
    for kk in range(TOP_K):
        onehot = (e_iota == idxs[kk]).astype(F32)
        before = jnp.dot(onehot.astype(BF16), earlier, preferred_element_type=F32)
        ranks.append(jnp.sum(onehot * (running + before), axis=0, keepdims=True))
        running = running + jnp.sum(onehot, axis=1, keepdims=True)
    rank_ref[0] = jnp.concatenate(ranks, axis=0).astype(I32)
    cnt_ref[...] = jnp.broadcast_to(running, cnt_ref.shape)

    wpad = jnp.concatenate([topw, jnp.zeros((LANES - TOP_K, tm), F32)], axis=0)
    wtm_ref[...] = wpad.T


def _route_out_shapes(n_tok, tm):
    n_tiles = n_tok // tm
    return (
        jax.ShapeDtypeStruct((n_tok, D_MODEL), F32),
        jax.ShapeDtypeStruct((n_tok * ROW_CHUNKS, LANES), I32),
        jax.ShapeDtypeStruct((n_tiles, TOP_K, tm), I32),
        jax.ShapeDtypeStruct((n_tiles, TOP_K, tm), I32),
        jax.ShapeDtypeStruct((n_tok, LANES), F32),
        jax.ShapeDtypeStruct((N_EXPERTS, LANES), F32),
    )


def _route_out_specs(tm):
    return (
        pl.BlockSpec((tm, D_MODEL), lambda i: (i, 0)),
        pl.BlockSpec((tm * ROW_CHUNKS, LANES), lambda i: (i, 0)),
        pl.BlockSpec((1, TOP_K, tm), lambda i: (i, 0, 0)),
        pl.BlockSpec((1, TOP_K, tm), lambda i: (i, 0, 0)),
        pl.BlockSpec((tm, LANES), lambda i: (i, 0)),
        pl.BlockSpec((N_EXPERTS, LANES), lambda i: (0, 0)),
    )


def _post_mix_kernel(ot_ref, *refs, n_x, n0):
    x_refs = refs[:n_x]
    mod_ref, wo_ref, rwt_ref, rb_ref = refs[n_x:n_x + 4]
    out_refs = refs[n_x + 4:]
    tm = x_refs[0].shape[0]
    mod = mod_ref[0]
    g1 = mod[:, 2 * D_MODEL:3 * D_MODEL]
    ot = ot_ref[:, 0].reshape(N_HEADS * D_V, tm)
    mix = lax.dot_general(ot, wo_ref[...], (((0,), (0,)), ((), ())),
                          preferred_element_type=F32)
    _route_tail(_part_load(x_refs, n0) + g1 * mix, mod, rwt_ref, rb_ref, *out_refs)


def _post_mix(ot, x_parts, mod, w_o, rw_t, rb, seq_len):
    n_tok = sum(p.shape[0] for p in x_parts)
    tm = ot.shape[3]
    per_seq = seq_len // tm
    const = lambda i: (0, 0)
    return pl.pallas_call(
        functools.partial(_post_mix_kernel, n_x=len(x_parts), n0=x_parts[0].shape[0] // tm),
        out_shape=_route_out_shapes(n_tok, tm),
        grid=(n_tok // tm,),
        in_specs=[pl.BlockSpec((N_HEADS, 1, D_V, tm), lambda i: (0, i, 0, 0))]
        + _part_specs(x_parts, tm) + [
            pl.BlockSpec((1, 1, 6 * D_MODEL), lambda i: (i // per_seq, 0, 0)),
            pl.BlockSpec(w_o.shape, const),
            pl.BlockSpec(rw_t.shape, const),
            pl.BlockSpec(rb.shape, const),
        ],
        out_specs=_route_out_specs(tm),
        compiler_params=pltpu.CompilerParams(
            dimension_semantics=("arbitrary",), vmem_limit_bytes=VMEM_LIMIT),
        name="post_mix",
    )(ot, *x_parts, mod, w_o, rw_t, rb)


def _pool_mix_kernel(x_ref, xp_ref, xn_ref, mod_ref, pw_ref, ps_ref, rwt_ref, rb_ref,
                     *out_refs, seq_len):
    tm = x_ref.shape[0]
    per_seq = seq_len // tm
    mod = mod_ref[0]
    sh1 = mod[:, 0:D_MODEL]
    sc1 = mod[:, D_MODEL:2 * D_MODEL]
    g1 = mod[:, 2 * D_MODEL:3 * D_MODEL]
    pos0 = (pl.program_id(0) % per_seq) * tm

    def normed(v):
        return _rms(v) * (1.0 + sc1) + sh1

    x = x_ref[...]
    h = normed(x)
    h_prev = jnp.where(pos0 > 0, normed(xp_ref[...]), 0.0)
    h_next = jnp.where(pos0 + tm < seq_len, normed(xn_ref[...]), 0.0)
    hext = jnp.concatenate([h_prev, h, h_next], axis=0)
    n_ext = tm + 2 * POOL_HALO
    pos = pos0 + lax.broadcasted_iota(I32, (tm, 1), 0)

    outs = []
    for g, win in enumerate(POOL_WINDOWS):
        left = win // 2
        right = win - 1 - left
        cols = slice(g * POOL_GROUP, (g + 1) * POOL_GROUP)
        s = hext[:, cols]
        span = 1
        while span < win:
            s = s + pltpu.roll(s, span, 0)
            span *= 2
        if right:
            s = pltpu.roll(s, n_ext - right, 0)
        num = s[POOL_HALO:POOL_HALO + tm, :]
        count = (jnp.minimum(pos + right + 1, seq_len) - jnp.maximum(pos - left, 0)).astype(F32)
        diff = (num / count - h[:, cols]).astype(BF16)
        outs.append(jnp.dot(diff, pw_ref[g], preferred_element_type=F32))
    mix = jnp.concatenate(outs, axis=1) * ps_ref[...]
    _route_tail(x + g1 * mix, mod, rwt_ref, rb_ref, *out_refs)


def _pool_mix(x, mod, pool_w, pool_scale, rw_t, rb, seq_len):
    n_tok = x.shape[0]
    tm = TOKEN_TILE
    per_seq = seq_len // tm
    halo_per_tile = tm // POOL_HALO
    n_halo = n_tok // POOL_HALO
    const = lambda i: (0, 0)
    return pl.pallas_call(
        functools.partial(_pool_mix_kernel, seq_len=seq_len),
        out_shape=_route_out_shapes(n_tok, tm),
        grid=(n_tok // tm,),
        in_specs=[
            pl.BlockSpec((tm, D_MODEL), lambda i: (i, 0)),
            pl.BlockSpec((POOL_HALO, D_MODEL),
                         lambda i: (jnp.maximum(i * halo_per_tile - 1, 0), 0)),
            pl.BlockSpec((POOL_HALO, D_MODEL),
                         lambda i: (jnp.minimum((i + 1) * halo_per_tile, n_halo - 1), 0)),
            pl.BlockSpec((1, 1, 6 * D_MODEL), lambda i: (i // per_seq, 0, 0)),
            pl.BlockSpec(pool_w.shape, lambda i: (0, 0, 0)),
            pl.BlockSpec(pool_scale.shape, const),
            pl.BlockSpec(rw_t.shape, const),
            pl.BlockSpec(rb.shape, const),
        ],
        out_specs=_route_out_specs(tm),
        compiler_params=pltpu.CompilerParams(
            dimension_semantics=("arbitrary",), vmem_limit_bytes=VMEM_LIMIT),
        name="pool_mix",
    )(x, x, x, mod, pool_w, pool_scale, rw_t, rb)


def _slots_kernel(pstart_ref, topi_ref, rank_ref, dest_ref):
    topi = topi_ref[...]
    start = jnp.zeros_like(topi)
    for e in range(N_EXPERTS):
        start = jnp.where(topi == e, pstart_ref[e], start)
    dest_ref[...] = start + rank_ref[...]


def _slots(pstart, topi, rank):
    n_tiles, _, tm = topi.shape
    tb = math.gcd(n_tiles, 32)
    spec = pl.BlockSpec((tb, TOP_K, tm), lambda i, ps: (i, 0, 0))
    return pl.pallas_call(
        _slots_kernel,
        out_shape=jax.ShapeDtypeStruct(topi.shape, I32),
        grid_spec=pltpu.PrefetchScalarGridSpec(
            num_scalar_prefetch=1, grid=(n_tiles // tb,), in_specs=[spec, spec], out_specs=spec),
        compiler_params=pltpu.CompilerParams(dimension_semantics=("arbitrary",)),
        name="moe_slots",
    )(pstart, topi, rank)


def _sc_chunk_rows(c, tm):
    per_tile = tm // SC_ROWS
    tile = c // per_tile
    part = c % per_tile
    return [(tile * TOP_K + kk) * per_tile + part for kk in range(TOP_K)]


def _sc_dispatch(hp, dest, n_slots, tm):
    hp = hp.reshape((-1,) + ROW_SHAPE)
    n_tok = hp.shape[0]
    rows_per_w = dest.shape[0] // SC_WORKERS
    chunks_per_w = n_tok // SC_WORKERS // SC_ROWS
    mesh = plsc.VectorSubcoreMesh(core_axis_name="c", subcore_axis_name="s")

    @functools.partial(
        pl.kernel, mesh=mesh,
        out_type=jax.ShapeDtypeStruct((n_slots,) + ROW_SHAPE, I32),
        scratch_types=[
            pltpu.VMEM((rows_per_w, SC_ROWS), I32),
            pltpu.VMEM((SC_ROWS,) + ROW_SHAPE, I32),
            pltpu.SemaphoreType.DMA,
        ],
        name="sc_dispatch",
    )
    def run(hp_hbm, dest_hbm, xs_hbm, dest_v, rows_v, sem):
        wid = lax.axis_index("s") * SC_CORES + lax.axis_index("c")
        pltpu.sync_copy(dest_hbm.at[pl.ds(wid * rows_per_w, rows_per_w)], dest_v)

        @pl.loop(0, chunks_per_w)
        def _(c):
            tok0 = (wid * chunks_per_w + c) * SC_ROWS
            pltpu.sync_copy(hp_hbm.at[pl.ds(tok0, SC_ROWS)], rows_v)
            copies = [pltpu.async_copy(rows_v, xs_hbm.at[dest_v.at[row]], sem)
                      for row in _sc_chunk_rows(c, tm)]
            for cp in copies:
                cp.wait()

    return run(hp, dest).reshape(n_slots * ROW_CHUNKS, LANES)


def _sc_gather(ys, dest, tm):
    ys = ys.reshape((-1,) + ROW_SHAPE)
    n_tok = dest.shape[0] * SC_ROWS // TOP_K
    rows_per_w = dest.shape[0] // SC_WORKERS
    chunks_per_w = n_tok // SC_WORKERS // SC_ROWS
    mesh = plsc.VectorSubcoreMesh(core_axis_name="c", subcore_axis_name="s")

    @functools.partial(
        pl.kernel, mesh=mesh,
        out_type=jax.ShapeDtypeStruct((TOP_K, n_tok) + ROW_SHAPE, I32),
        scratch_types=[
            pltpu.VMEM((rows_per_w, SC_ROWS), I32),
            pltpu.VMEM((SC_ROWS,) + ROW_SHAPE, I32),
            pltpu.SemaphoreType.DMA,
        ],
        name="sc_gather",
    )
    def run(ys_hbm, dest_hbm, yg_hbm, dest_v, rows_v, sem):
        wid = lax.axis_index("s") * SC_CORES + lax.axis_index("c")
        pltpu.sync_copy(dest_hbm.at[pl.ds(wid * rows_per_w, rows_per_w)], dest_v)

        @pl.loop(0, chunks_per_w)
        def _(c):
            tok0 = (wid * chunks_per_w + c) * SC_ROWS
            for kk, row in enumerate(_sc_chunk_rows(c, tm)):
                pltpu.async_copy(ys_hbm.at[dest_v.at[row]], rows_v, sem).wait()
                pltpu.sync_copy(rows_v, yg_hbm.at[kk, pl.ds(tok0, SC_ROWS)])

    return run(ys, dest).reshape(TOP_K, n_tok * ROW_CHUNKS, LANES)


def _expert_kernel(blk_e_ref, xs_ref, w1_ref, b1_ref, w2_ref, b2_ref, ys_ref, w1b_ref, w2b_ref):
    i = pl.program_id(0)
    changed = blk_e_ref[i] != blk_e_ref[jnp.maximum(i - 1, 0)]

    @pl.when((i == 0) | changed)
    def _():
        w1b_ref[...] = w1_ref[0, 0].astype(BF16)
        w2b_ref[...] = w2_ref[0, 0].astype(BF16)

    for r in range(0, xs_ref.shape[0] // ROW_CHUNKS, EXPERT_SUB):
        xb = _unpack_rows(xs_ref, r, EXPERT_SUB).astype(BF16)
        gu = jnp.dot(xb, w1b_ref[...], preferred_element_type=F32) + b1_ref[0, 0]
        gate = jnp.minimum(gu[:, :D_FF], SWIGLU_LIMIT)
        up = jnp.clip(gu[:, D_FF:], -SWIGLU_LIMIT, SWIGLU_LIMIT)
        act = (up + 1.0) * (gate * jax.nn.sigmoid(SWIGLU_ALPHA * gate))
        y = jnp.dot(act.astype(BF16), w2b_ref[...], preferred_element_type=F32) + b2_ref[0, 0]
        _pack_rows(y, ys_ref, r)


def _expert_ffn(blk_e, xs, layer, w1, b1, w2, b2):
    n_slots = xs.shape[0] // ROW_CHUNKS
    bm = EXPERT_TILE
    return pl.pallas_call(
        _expert_kernel,
        out_shape=jax.ShapeDtypeStruct(xs.shape, I32),
        grid_spec=pltpu.PrefetchScalarGridSpec(
            num_scalar_prefetch=1,
            grid=(n_slots // bm,),
            in_specs=[
                pl.BlockSpec((bm * ROW_CHUNKS, LANES), lambda i, be: (i, 0)),
                pl.BlockSpec((1, 1, D_MODEL, 2 * D_FF), lambda i, be: (layer, be[i], 0, 0)),
                pl.BlockSpec((1, 1, 1, 2 * D_FF), lambda i, be: (layer, be[i], 0, 0)),
                pl.BlockSpec((1, 1, D_FF, D_MODEL), lambda i, be: (layer, be[i], 0, 0)),
                pl.BlockSpec((1, 1, 1, D_MODEL), lambda i, be: (layer, be[i], 0, 0)),
            ],
            out_specs=pl.BlockSpec((bm * ROW_CHUNKS, LANES), lambda i, be: (i, 0)),
            scratch_shapes=[pltpu.VMEM((D_MODEL, 2 * D_FF), BF16), pltpu.VMEM((D_FF, D_MODEL), BF16)],
        ),
        compiler_params=pltpu.CompilerParams(
            dimension_semantics=("arbitrary",), vmem_limit_bytes=VMEM_LIMIT),
        name="expert_ffn",
    )(blk_e, xs, w1, b1, w2, b2)


def _combine_kernel(yg_ref, wtm_ref, x1_ref, mod_ref, fn_ref, *o_refs, final, n0):
    tm = x1_ref.shape[0]
    wtm = wtm_ref[...]
    moe = jnp.zeros((tm, D_MODEL), F32)
    for kk in range(TOP_K):
        moe = moe + wtm[:, kk:kk + 1] * _unpack_rows(yg_ref.at[kk], 0, tm)
    g2 = mod_ref[0][:, 5 * D_MODEL:6 * D_MODEL]
    out = x1_ref[...] + g2 * moe
    if not final:
        o_refs[0][...] = out
        return
    out = _rms(out) * fn_ref[...]

    @pl.when(pl.program_id(0) < n0)
    def _():
        o_refs[0][...] = out

    @pl.when(pl.program_id(0) >= n0)
    def _():
        o_refs[1][...] = out


def _combine(yg, wtm, x1, mod, final_norm, seq_len, final, n_first):
    n_tok = x1.shape[0]
    tm = TOKEN_TILE
    per_seq = seq_len // tm
    n0 = n_first // tm
    if final:
        out_shape = (jax.ShapeDtypeStruct((n_first, D_MODEL), F32),
                     jax.ShapeDtypeStruct((n_tok - n_first, D_MODEL), F32))
        out_specs = (pl.BlockSpec((tm, D_MODEL), lambda i: (jnp.minimum(i, n0 - 1), 0)),
                     pl.BlockSpec((tm, D_MODEL), lambda i: (jnp.maximum(i - n0, 0), 0)))
    else:
        out_shape = jax.ShapeDtypeStruct((n_tok, D_MODEL), F32)
        out_specs = pl.BlockSpec((tm, D_MODEL), lambda i: (i, 0))
    return pl.pallas_call(
        functools.partial(_combine_kernel, final=final, n0=n0),
        out_shape=out_shape,
        grid=(n_tok // tm,),
        in_specs=[
            pl.BlockSpec((TOP_K, tm * ROW_CHUNKS, LANES), lambda i: (0, i, 0)),
            pl.BlockSpec((tm, LANES), lambda i: (i, 0)),
            pl.BlockSpec((tm, D_MODEL), lambda i: (i, 0)),
            pl.BlockSpec((1, 1, 6 * D_MODEL), lambda i: (i // per_seq, 0, 0)),
            pl.BlockSpec((1, D_MODEL), lambda i: (0, 0)),
        ],
        out_specs=out_specs,
        compiler_params=pltpu.CompilerParams(
            dimension_semantics=("arbitrary",), vmem_limit_bytes=VMEM_LIMIT),
        name="moe_combine",
    )(yg, wtm, x1, mod, final_norm)


def _moe_layer(route, mod, layer, ffn_w, final_norm, seq_len, final, n_first):
    x1, hp, topi, rank, wtm, counts = route
    n_tok = x1.shape[0]
    bm = EXPERT_TILE
    n_slots = n_tok * TOP_K + N_EXPERTS * bm
    n_blk = n_slots // bm
    cnt = counts[:, 0].astype(I32)
    padded = (cnt + bm - 1) // bm * bm
    pend = jnp.cumsum(padded)
    pstart = (pend - padded).astype(I32)
    blk_start = jnp.arange(n_blk, dtype=I32) * bm
    blk_e = jnp.minimum(jnp.sum(blk_start[:, None] >= pend[None, :], axis=1), N_EXPERTS - 1)
    tm = topi.shape[2]
    dest = _slots(pstart, topi, rank).reshape(-1, SC_ROWS)
    xs = _sc_dispatch(hp, dest, n_slots, tm)
    ys = _expert_ffn(blk_e.astype(I32), xs, layer, *ffn_w)
    yg = _sc_gather(ys, dest, tm)
    return _combine(yg, wtm, x1, mod, final_norm, seq_len, final, n_first)


def _rope_tables(seq_len):
    inv_freq = 1.0 / (ROPE_THETA ** (jnp.arange(0, D_ROPE, 2, dtype=F32) / D_ROPE))
    ang = jnp.arange(seq_len, dtype=F32)[:, None] * inv_freq[None, :]
    cos, sin = jnp.cos(ang), jnp.sin(ang)
    ones = jnp.ones((seq_len, ROPE_LO), F32)
    zeros_lo = jnp.zeros((seq_len, ROPE_LO), F32)
    zeros_hi = jnp.zeros((seq_len, HEAD_PAD - ROPE_LO - D_ROPE), F32)
    zeros_h = jnp.zeros((seq_len, ROPE_HALF), F32)
    rope_c = jnp.concatenate([ones, cos, cos, zeros_hi], axis=1)
    rope_s1 = jnp.concatenate([zeros_lo, -sin, zeros_h, zeros_hi], axis=1)
    rope_s2 = jnp.concatenate([zeros_lo, zeros_h, sin, zeros_hi], axis=1)
    return cos.T, sin.T, rope_c, rope_s1, rope_s2


def _mla_weights(w_in, q_norm, kv_norm, w_uq, w_ukv, w_o, seq_len):
    d_qk = D_NOPE + D_ROPE
    q_scale = d_qk ** -0.5 * math.log2(math.e)
    pad_pe = jnp.zeros((D_MODEL, HEAD_PAD), F32).at[:, ROPE_LO:ROPE_LO + D_ROPE].set(
        w_in[:, Q_RANK + KV_RANK:])
    w_in_p = jnp.concatenate([w_in[:, :Q_RANK + KV_RANK], pad_pe], axis=1)
    w_uq_p = jnp.pad(w_uq.reshape(Q_RANK, N_HEADS, d_qk) * q_scale,
                     ((0, 0), (0, 0), (0, HEAD_PAD - d_qk)))
    w_kv = w_ukv.reshape(KV_RANK, N_HEADS, D_NOPE + D_V)
    w_uk_p = jnp.pad(w_kv[:, :, :D_NOPE], ((0, 0), (0, 0), (0, HEAD_PAD - D_NOPE)))
    cos_t, sin_t, rope_c, rope_s1, rope_s2 = _rope_tables(seq_len)
    return {
        "w_in": w_in_p.astype(BF16),
        "q_norm": q_norm.reshape(1, Q_RANK),
        "kv_norm": kv_norm.reshape(1, KV_RANK),
        "w_uq_t": w_uq_p.reshape(Q_RANK, N_HEADS * HEAD_PAD).T.astype(BF16),
        "w_uk": w_uk_p.reshape(KV_RANK, N_HEADS * HEAD_PAD).astype(BF16),
        "w_uv_t": w_kv[:, :, D_NOPE:].reshape(KV_RANK, N_HEADS * D_V).T.astype(BF16),
        "w_o": w_o.astype(BF16),
        "cos_t": cos_t, "sin_t": sin_t, "rope_c": rope_c, "rope_s1": rope_s1, "rope_s2": rope_s2,
    }


def _router_weights(router_w, router_b):
    return router_w.T.astype(BF16), router_b.reshape(N_EXPERTS, 1)


def kernel(x_prompt, x_sample, c_prompt, c_sample, ada_w, ada_b, mla_w_in, mla_q_norm,
           mla_kv_norm, mla_w_uq, mla_w_ukv, mla_w_o, pool_w, pool_scale, router_w, router_b,
           moe_w1, moe_b1, moe_w2, moe_b2, final_norm):
    n_prompt, seq_len, _ = x_prompt.shape
    assert x_sample.shape[1] == seq_len and seq_len % TOKEN_TILE == 0
    depth = ada_w.shape[0]
    n_sample = x_sample.shape[0]
    n_seq = n_prompt + n_sample
    n_first = n_prompt * seq_len
    x_parts = [x_prompt.reshape(-1, D_MODEL), x_sample.reshape(-1, D_MODEL)]
    mods = _ada_mod(jnp.concatenate([c_prompt, c_sample], axis=0), ada_w, ada_b)
    fnorm = final_norm.reshape(1, D_MODEL)
    ffn_w = (moe_w1, moe_b1.reshape(depth, N_EXPERTS, 1, 2 * D_FF),
             moe_w2, moe_b2.reshape(depth, N_EXPERTS, 1, D_MODEL))

    for i in range(depth):
        mod = mods[i].reshape(n_seq, 1, 6 * D_MODEL)
        rw_t, rb = _router_weights(router_w[i], router_b[i])
        j = i // 2
        if i % 2 == 0:
            w = _mla_weights(mla_w_in[j], mla_q_norm[j], mla_kv_norm[j], mla_w_uq[j],
                             mla_w_ukv[j], mla_w_o[j], seq_len)
            qt, k, vt = _mla_pre(x_parts, mod, w, seq_len)
            ot = _attention(qt, k, vt, seq_len)
            route = _post_mix(ot, x_parts, mod, w["w_o"], rw_t, rb, seq_len)
        else:
            x = x_parts[0] if len(x_parts) == 1 else jnp.concatenate(x_parts, axis=0)
            route = _pool_mix(x, mod, pool_w[j].astype(BF16), pool_scale[j].reshape(1, D_MODEL),
                              rw_t, rb, seq_len)
        out = _moe_layer(route, mod, i, ffn_w, fnorm, seq_len, i == depth - 1, n_first)
        x_parts = list(out) if isinstance(out, (tuple, list)) else [out]

    y_prompt, y_sample = x_parts
    return (y_prompt.reshape(n_prompt, seq_len, D_MODEL),
            y_sample.reshape(n_sample, seq_len, D_MODEL))
```

```python
import functools
import math

import jax
import jax.numpy as jnp
from jax import lax
from jax.experimental import pallas as pl
from jax.experimental.pallas import tpu as pltpu
from jax.experimental.pallas import tpu_sc as plsc

F32 = jnp.float32
BF16 = jnp.bfloat16
I32 = jnp.int32

D_MODEL = 1024
N_HEADS = 16
Q_RANK = 384
KV_RANK = 256
D_NOPE = 64
D_ROPE = 32
D_V = 64
V_ROWS = D_V + 16
ROPE_THETA = 10000.0
POOL_WINDOWS = (2, 4, 8, 16)
POOL_GROUP = D_MODEL // len(POOL_WINDOWS)
N_EXPERTS = 32
TOP_K = 4
D_FF = D_MODEL
SWIGLU_LIMIT = 7.0
SWIGLU_ALPHA = 1.702
EPS = 1e-6

LANES = 128
SUBLANES = 8
HEAD_PAD = 128
ROPE_LO = D_NOPE
ROPE_HALF = D_ROPE // 2
VMEM_LIMIT = 56 * 1024 * 1024

TOKEN_TILE = 256
EXPERT_TILE = 512
EXPERT_SUB = 256
ATTN_KEY_CHUNKS = 4
ATTN_HEADS_PER_STEP = 4
POOL_HALO = 8
PACKED = D_MODEL // 2
ROW_CHUNKS = PACKED // LANES
ROW_SHAPE = (ROW_CHUNKS, LANES)

SC_CORES = 2
SC_SUBCORES = 16
SC_WORKERS = SC_CORES * SC_SUBCORES
SC_ROWS = 64


def _rms(x):
    return x * lax.rsqrt(jnp.mean(x * x, axis=-1, keepdims=True) + EPS)


def _pack_rows(y, out_ref, row0=0):
    n = y.shape[0]
    lo = lax.bitcast_convert_type(y[:, :PACKED].astype(BF16).astype(F32), I32)
    hi = lax.bitcast_convert_type(y[:, PACKED:].astype(BF16).astype(F32), I32)
    words = lax.shift_right_logical(lo, 16) | (hi & jnp.int32(-65536))
    for j in range(ROW_CHUNKS):
        out_ref[pl.ds(row0 * ROW_CHUNKS + j, n, stride=ROW_CHUNKS), :] = (
            words[:, j * LANES:(j + 1) * LANES])


def _unpack_rows(in_ref, row0, n):
    words = jnp.concatenate(
        [in_ref[pl.ds(row0 * ROW_CHUNKS + j, n, stride=ROW_CHUNKS), :] for j in range(ROW_CHUNKS)],
        axis=1)
    lo = lax.bitcast_convert_type(lax.shift_left(words, 16), F32)
    hi = lax.bitcast_convert_type(words & jnp.int32(-65536), F32)
    return jnp.concatenate([lo, hi], axis=1)


def _part_specs(parts, tm):
    if len(parts) == 1:
        return [pl.BlockSpec((tm, D_MODEL), lambda i: (i, 0))]
    n0 = parts[0].shape[0] // tm
    return [pl.BlockSpec((tm, D_MODEL), lambda i: (jnp.minimum(i, n0 - 1), 0)),
            pl.BlockSpec((tm, D_MODEL), lambda i: (jnp.maximum(i - n0, 0), 0))]


def _part_load(refs, n0):
    if len(refs) == 1:
        return refs[0][...]
    return jnp.where(pl.program_id(0) < n0, refs[0][...], refs[1][...])


def _ada_kernel(c_ref, w_ref, b_ref, o_ref):
    c = c_ref[...]
    act = (c * jax.nn.sigmoid(c)).astype(BF16)
    o_ref[0] = jnp.dot(act, w_ref[0].astype(BF16), preferred_element_type=F32) + b_ref[0]


def _ada_mod(c, ada_w, ada_b):
    depth, _, n_out = ada_w.shape
    n_seq = c.shape[0]
    tn = 1536
    return pl.pallas_call(
        _ada_kernel,
        out_shape=jax.ShapeDtypeStruct((depth, n_seq, n_out), F32),
        grid=(depth, n_out // tn),
        in_specs=[
            pl.BlockSpec((n_seq, D_MODEL), lambda l, j: (0, 0)),
            pl.BlockSpec((1, D_MODEL, tn), lambda l, j: (l, 0, j)),
            pl.BlockSpec((1, 1, tn), lambda l, j: (l, 0, j)),
        ],
        out_specs=pl.BlockSpec((1, n_seq, tn), lambda l, j: (l, 0, j)),
        compiler_params=pltpu.CompilerParams(
            dimension_semantics=("arbitrary", "arbitrary"), vmem_limit_bytes=VMEM_LIMIT),
        name="ada_mod",
    )(c, ada_w, ada_b.reshape(depth, 1, n_out))


def _mla_pre_kernel(*refs, n_x, n0):
    x_refs = refs[:n_x]
    (mod_ref, win_ref, qn_ref, kvn_ref, wuqt_ref, wuk_ref, wuvt_ref,
     cost_ref, sint_ref, ck_ref, s1k_ref, s2k_ref, qt_ref, k_ref, vt_ref) = refs[n_x:]
    tm = x_refs[0].shape[0]
    mod = mod_ref[0]
    sh1 = mod[:, 0:D_MODEL]
    sc1 = mod[:, D_MODEL:2 * D_MODEL]
    h = (_rms(_part_load(x_refs, n0)) * (1.0 + sc1) + sh1).astype(BF16)
    a = jnp.dot(h, win_ref[...], preferred_element_type=F32)
    cq = (_rms(a[:, :Q_RANK]) * qn_ref[...]).astype(BF16)
    ckv = (_rms(a[:, Q_RANK:Q_RANK + KV_RANK]) * kvn_ref[...]).astype(BF16)

    kpe = a[:, Q_RANK + KV_RANK:]
    kpe = (kpe * ck_ref[...]
           + pltpu.roll(kpe, LANES - ROPE_HALF, 1) * s1k_ref[...]
           + pltpu.roll(kpe, ROPE_HALF, 1) * s2k_ref[...])

    qt = lax.dot_general(wuqt_ref[...], cq, (((1,), (1,)), ((), ())),
                         preferred_element_type=F32)
    q3 = qt.reshape(N_HEADS, HEAD_PAD, tm)
    x1 = q3[:, ROPE_LO:ROPE_LO + ROPE_HALF, :]
    x2 = q3[:, ROPE_LO + ROPE_HALF:ROPE_LO + D_ROPE, :]
    cos = cost_ref[...][None]
    sin = sint_ref[...][None]
    q3 = jnp.concatenate(
        [q3[:, :ROPE_LO, :], x1 * cos - x2 * sin, x2 * cos + x1 * sin,
         q3[:, ROPE_LO + D_ROPE:, :]], axis=1)
    qt_ref[:, 0] = q3.astype(BF16)

    kn = jnp.dot(ckv, wuk_ref[...], preferred_element_type=F32)
    for hd in range(N_HEADS):
        k_ref[hd] = (kn[:, hd * HEAD_PAD:(hd + 1) * HEAD_PAD] + kpe).astype(BF16)

    vt = lax.dot_general(wuvt_ref[...], ckv, (((1,), (1,)), ((), ())),
                         preferred_element_type=F32)
    vt_ref[:, 0, :D_V, :] = vt.reshape(N_HEADS, D_V, tm).astype(BF16)
    vt_ref[:, 0, D_V:, :] = jnp.ones((N_HEADS, V_ROWS - D_V, tm), BF16)


def _mla_pre(x_parts, mod, w, seq_len):
    n_tok = sum(p.shape[0] for p in x_parts)
    tm = TOKEN_TILE
    n_tiles = n_tok // tm
    per_seq = seq_len // tm
    const = lambda i: (0, 0)
    return pl.pallas_call(
        functools.partial(_mla_pre_kernel, n_x=len(x_parts), n0=x_parts[0].shape[0] // tm),
        out_shape=(
            jax.ShapeDtypeStruct((N_HEADS, n_tiles, HEAD_PAD, tm), BF16),
            jax.ShapeDtypeStruct((N_HEADS, n_tok, HEAD_PAD), BF16),
            jax.ShapeDtypeStruct((N_HEADS, n_tok // seq_len, V_ROWS, seq_len), BF16),
        ),
        grid=(n_tiles,),
        in_specs=_part_specs(x_parts, tm) + [
            pl.BlockSpec((1, 1, 6 * D_MODEL), lambda i: (i // per_seq, 0, 0)),
            pl.BlockSpec(w["w_in"].shape, const),
            pl.BlockSpec(w["q_norm"].shape, const),
            pl.BlockSpec(w["kv_norm"].shape, const),
            pl.BlockSpec(w["w_uq_t"].shape, const),
            pl.BlockSpec(w["w_uk"].shape, const),
            pl.BlockSpec(w["w_uv_t"].shape, const),
            pl.BlockSpec((ROPE_HALF, tm), lambda i: (0, i % per_seq)),
            pl.BlockSpec((ROPE_HALF, tm), lambda i: (0, i % per_seq)),
            pl.BlockSpec((tm, LANES), lambda i: (i % per_seq, 0)),
            pl.BlockSpec((tm, LANES), lambda i: (i % per_seq, 0)),
            pl.BlockSpec((tm, LANES), lambda i: (i % per_seq, 0)),
        ],
        out_specs=(
            pl.BlockSpec((N_HEADS, 1, HEAD_PAD, tm), lambda i: (0, i, 0, 0)),
            pl.BlockSpec((N_HEADS, tm, HEAD_PAD), lambda i: (0, i, 0)),
            pl.BlockSpec((N_HEADS, 1, V_ROWS, tm), lambda i: (0, i // per_seq, 0, i % per_seq)),
        ),
        compiler_params=pltpu.CompilerParams(
            dimension_semantics=("arbitrary",), vmem_limit_bytes=VMEM_LIMIT),
        name="mla_pre",
    )(*x_parts, mod, w["w_in"], w["q_norm"], w["kv_norm"], w["w_uq_t"], w["w_uk"], w["w_uv_t"],
      w["cos_t"], w["sin_t"], w["rope_c"], w["rope_s1"], w["rope_s2"])


def _attention_kernel(qt_ref, k_ref, vt_ref, ot_ref, s0_ref, s1_ref):
    n_heads, n_q = qt_ref.shape[:2]
    n_tiles = n_heads * n_q
    n_keys = k_ref.shape[1]
    kc = n_keys // ATTN_KEY_CHUNKS

    def stage(t_next, s_next_ref, t_cur, s_cur_ref, m_cur):
        m_next, o = None, None
        if t_next is not None:
            h_next, j_next = t_next // n_q, t_next % n_q
        if t_cur is not None:
            h_cur, j_cur = t_cur // n_q, t_cur % n_q
        for c in range(ATTN_KEY_CHUNKS):
            rows = slice(c * kc, (c + 1) * kc)
            if t_next is not None:
                s = jnp.dot(k_ref[h_next, rows, :], qt_ref[h_next, j_next],
                            preferred_element_type=F32)
                s_next_ref[rows, :] = s
                cm = jnp.max(s, axis=0, keepdims=True)
                m_next = cm if m_next is None else jnp.maximum(m_next, cm)
            if t_cur is not None:
                p = jnp.exp2(s_cur_ref[rows, :] - m_cur).astype(BF16)
                part = jnp.dot(vt_ref[h_cur, 0, :, rows], p,
                               preferred_element_type=F32)
                o = part if o is None else o + part
        if t_cur is not None:
            denom = o[D_V:D_V + 1, :]
            ot_ref[h_cur, j_cur] = (o[:D_V, :] * (1.0 / denom)).astype(BF16)
        return m_next

    def body(i, m0):
        t = 2 * i
        m1 = stage(t + 1, s1_ref, t, s0_ref, m0)
        return stage(t + 2, s0_ref, t + 1, s1_ref, m1)

    m0 = lax.fori_loop(0, n_tiles // 2 - 1, body, stage(0, s0_ref, None, None, None))
    m1 = stage(n_tiles - 1, s1_ref, n_tiles - 2, s0_ref, m0)
    stage(None, None, n_tiles - 1, s1_ref, m1)


def _attention(qt, k, vt, seq_len):
    n_heads, n_tiles, _, tq = qt.shape
    per_seq = seq_len // tq
    n_seq = n_tiles // per_seq
    hb = ATTN_HEADS_PER_STEP
    return pl.pallas_call(
        _attention_kernel,
        out_shape=jax.ShapeDtypeStruct((n_heads, n_tiles, D_V, tq), BF16),
        grid=(n_seq, n_heads // hb),
        in_specs=[
            pl.BlockSpec((hb, per_seq, HEAD_PAD, tq), lambda b, h: (h, b, 0, 0)),
            pl.BlockSpec((hb, seq_len, HEAD_PAD), lambda b, h: (h, b, 0)),
            pl.BlockSpec((hb, 1, V_ROWS, seq_len), lambda b, h: (h, b, 0, 0)),
        ],
        out_specs=pl.BlockSpec((hb, per_seq, D_V, tq), lambda b, h: (h, b, 0, 0)),
        scratch_shapes=[pltpu.VMEM((seq_len, tq), F32), pltpu.VMEM((seq_len, tq), F32)],
        compiler_params=pltpu.CompilerParams(
            dimension_semantics=("arbitrary", "arbitrary"), vmem_limit_bytes=VMEM_LIMIT),
        name="attention",
    )(qt, k, vt)


def _route_tail(x1, mod, rwt_ref, rb_ref, x1_ref, hp_ref, topi_ref, rank_ref, wtm_ref, cnt_ref):
    tm = x1.shape[0]
    sh2 = mod[:, 3 * D_MODEL:4 * D_MODEL]
    sc2 = mod[:, 4 * D_MODEL:5 * D_MODEL]
    x1_ref[...] = x1
    h2 = _rms(x1) * (1.0 + sc2) + sh2
    _pack_rows(h2, hp_ref)
    logits = lax.dot_general(rwt_ref[...], h2.astype(BF16), (((1,), (1,)), ((), ())),
                             preferred_element_type=F32) + rb_ref[...]
    e_iota = lax.broadcasted_iota(I32, (N_EXPERTS, tm), 0)
    vals, idxs = [], []
    work = logits
    for _ in range(TOP_K):
        m = jnp.max(work, axis=0, keepdims=True)
        idx = jnp.min(jnp.where(work == m, e_iota, N_EXPERTS), axis=0, keepdims=True)
        vals.append(m)
        idxs.append(idx)
        work = jnp.where(e_iota == idx, -jnp.inf, work)
    ex = [jnp.exp(v - vals[0]) for v in vals]
    inv = 1.0 / (ex[0] + ex[1] + ex[2] + ex[3])
    topw = jnp.concatenate([e * inv for e in ex], axis=0)
    topi_ref[0] = jnp.concatenate(idxs, axis=0)

    @pl.when(pl.program_id(0) == 0)
    def _():
        cnt_ref[...] = jnp.zeros_like(cnt_ref)

    row = lax.broadcasted_iota(I32, (tm, tm), 0)
    col = lax.broadcasted_iota(I32, (tm, tm), 1)
    earlier = (row < col).astype(BF16)
    running = cnt_ref[...][:, 0:1]
    ranks = []
    for kk in range(TOP_K):
        onehot = (e_iota == idxs[kk]).astype(F32)
        before = jnp.dot(onehot.astype(BF16), earlier, preferred_element_type=F32)
        ranks.append(jnp.sum(onehot * (running + before), axis=0, keepdims=True))
        running = running + jnp.sum(onehot, axis=1, keepdims=True)
    rank_ref[0] = jnp.concatenate(ranks, axis=0).astype(I32)
    cnt_ref[...] = jnp.broadcast_to(running, cnt_ref.shape)

    wpad = jnp.concatenate([topw, jnp.zeros((LANES - TOP_K, tm), F32)], axis=0)
    wtm_ref[...] = wpad.T


def _route_out_shapes(n_tok, tm):
    n_tiles = n_tok // tm
    return (
        jax.ShapeDtypeStruct((n_tok, D_MODEL), F32),
        jax.ShapeDtypeStruct((n_tok * ROW_CHUNKS, LANES), I32),
        jax.ShapeDtypeStruct((n_tiles, TOP_K, tm), I32),
        jax.ShapeDtypeStruct((n_tiles, TOP_K, tm), I32),
        jax.ShapeDtypeStruct((n_tok, LANES), F32),
        jax.ShapeDtypeStruct((N_EXPERTS, LANES), F32),
    )


def _route_out_specs(tm):
    return (
        pl.BlockSpec((tm, D_MODEL), lambda i: (i, 0)),
        pl.BlockSpec((tm * ROW_CHUNKS, LANES), lambda i: (i, 0)),
        pl.BlockSpec((1, TOP_K, tm), lambda i: (i, 0, 0)),
        pl.BlockSpec((1, TOP_K, tm), lambda i: (i, 0, 0)),
        pl.BlockSpec((tm, LANES), lambda i: (i, 0)),
        pl.BlockSpec((N_EXPERTS, LANES), lambda i: (0, 0)),
    )


def _post_mix_kernel(ot_ref, *refs, n_x, n0):
    x_refs = refs[:n_x]
    mod_ref, wo_ref, rwt_ref, rb_ref = refs[n_x:n_x + 4]
    out_refs = refs[n_x + 4:]
    tm = x_refs[0].shape[0]
    mod = mod_ref[0]
    g1 = mod[:, 2 * D_MODEL:3 * D_MODEL]
    ot = ot_ref[:, 0].reshape(N_HEADS * D_V, tm)
    mix = lax.dot_general(ot, wo_ref[...], (((0,), (0,)), ((), ())),
                          preferred_element_type=F32)
    _route_tail(_part_load(x_refs, n0) + g1 * mix, mod, rwt_ref, rb_ref, *out_refs)


def _post_mix(ot, x_parts, mod, w_o, rw_t, rb, seq_len):
    n_tok = sum(p.shape[0] for p in x_parts)
    tm = ot.shape[3]
    per_seq = seq_len // tm
    const = lambda i: (0, 0)
    return pl.pallas_call(
        functools.partial(_post_mix_kernel, n_x=len(x_parts), n0=x_parts[0].shape[0] // tm),
        out_shape=_route_out_shapes(n_tok, tm),
        grid=(n_tok // tm,),
        in_specs=[pl.BlockSpec((N_HEADS, 1, D_V, tm), lambda i: (0, i, 0, 0))]
        + _part_specs(x_parts, tm) + [
            pl.BlockSpec((1, 1, 6 * D_MODEL), lambda i: (i // per_seq, 0, 0)),
            pl.BlockSpec(w_o.shape, const),
            pl.BlockSpec(rw_t.shape, const),
            pl.BlockSpec(rb.shape, const),
        ],
        out_specs=_route_out_specs(tm),
        compiler_params=pltpu.CompilerParams(
            dimension_semantics=("arbitrary",), vmem_limit_bytes=VMEM_LIMIT),
        name="post_mix",
    )(ot, *x_parts, mod, w_o, rw_t, rb)


def _pool_mix_kernel(x_ref, xp_ref, xn_ref, mod_ref, pw_ref, ps_ref, rwt_ref, rb_ref,
                     *out_refs, seq_len):
    tm = x_ref.shape[0]
    per_seq = seq_len // tm
    mod = mod_ref[0]
    sh1 = mod[:, 0:D_MODEL]
    sc1 = mod[:, D_MODEL:2 * D_MODEL]
    g1 = mod[:, 2 * D_MODEL:3 * D_MODEL]
    pos0 = (pl.program_id(0) % per_seq) * tm

    def normed(v):
        return _rms(v) * (1.0 + sc1) + sh1

    x = x_ref[...]
    h = normed(x)
    h_prev = jnp.where(pos0 > 0, normed(xp_ref[...]), 0.0)
    h_next = jnp.where(pos0 + tm < seq_len, normed(xn_ref[...]), 0.0)
    hext = jnp.concatenate([h_prev, h, h_next], axis=0)
    n_ext = tm + 2 * POOL_HALO
    pos = pos0 + lax.broadcasted_iota(I32, (tm, 1), 0)

    outs = []
    for g, win in enumerate(POOL_WINDOWS):
        left = win // 2
        right = win - 1 - left
        cols = slice(g * POOL_GROUP, (g + 1) * POOL_GROUP)
        s = hext[:, cols]
        span = 1
        while span < win:
            s = s + pltpu.roll(s, span, 0)
            span *= 2
        if right:
            s = pltpu.roll(s, n_ext - right, 0)
        num = s[POOL_HALO:POOL_HALO + tm, :]
        count = (jnp.minimum(pos + right + 1, seq_len) - jnp.maximum(pos - left, 0)).astype(F32)
        diff = (num / count - h[:, cols]).astype(BF16)
        outs.append(jnp.dot(diff, pw_ref[g], preferred_element_type=F32))
    mix = jnp.concatenate(outs, axis=1) * ps_ref[...]
    _route_tail(x + g1 * mix, mod, rwt_ref, rb_ref, *out_refs)


def _pool_mix(x, mod, pool_w, pool_scale, rw_t, rb, seq_len):
    n_tok = x.shape[0]
    tm = TOKEN_TILE
    per_seq = seq_len // tm
    halo_per_tile = tm // POOL_HALO
    n_halo = n_tok // POOL_HALO
    const = lambda i: (0, 0)
    return pl.pallas_call(
        functools.partial(_pool_mix_kernel, seq_len=seq_len),
        out_shape=_route_out_shapes(n_tok, tm),
        grid=(n_tok // tm,),
        in_specs=[
            pl.BlockSpec((tm, D_MODEL), lambda i: (i, 0)),
            pl.BlockSpec((POOL_HALO, D_MODEL),
                         lambda i: (jnp.maximum(i * halo_per_tile - 1, 0), 0)),
            pl.BlockSpec((POOL_HALO, D_MODEL),
                         lambda i: (jnp.minimum((i + 1) * halo_per_tile, n_halo - 1), 0)),
            pl.BlockSpec((1, 1, 6 * D_MODEL), lambda i: (i // per_seq, 0, 0)),
            pl.BlockSpec(pool_w.shape, lambda i: (0, 0, 0)),
            pl.BlockSpec(pool_scale.shape, const),
            pl.BlockSpec(rw_t.shape, const),
            pl.BlockSpec(rb.shape, const),
        ],
        out_specs=_route_out_specs(tm),
        compiler_params=pltpu.CompilerParams(
            dimension_semantics=("arbitrary",), vmem_limit_bytes=VMEM_LIMIT),
        name="pool_mix",
    )(x, x, x, mod, pool_w, pool_scale, rw_t, rb)


def _slots_kernel(pstart_ref, topi_ref, rank_ref, dest_ref):
    topi = topi_ref[...]
    start = jnp.zeros_like(topi)
    for e in range(N_EXPERTS):
        start = jnp.where(topi == e, pstart_ref[e], start)
    dest_ref[...] = start + rank_ref[...]


def _slots(pstart, topi, rank):
    n_tiles, _, tm = topi.shape
    tb = math.gcd(n_tiles, 32)
    spec = pl.BlockSpec((tb, TOP_K, tm), lambda i, ps: (i, 0, 0))
    return pl.pallas_call(
        _slots_kernel,
        out_shape=jax.ShapeDtypeStruct(topi.shape, I32),
        grid_spec=pltpu.PrefetchScalarGridSpec(
            num_scalar_prefetch=1, grid=(n_tiles // tb,), in_specs=[spec, spec], out_specs=spec),
        compiler_params=pltpu.CompilerParams(dimension_semantics=("arbitrary",)),
        name="moe_slots",
    )(pstart, topi, rank)


def _sc_chunk_rows(c, tm):
    per_tile = tm // SC_ROWS
    tile = c // per_tile
    part = c % per_tile
    return [(tile * TOP_K + kk) * per_tile + part for kk in range(TOP_K)]


def _sc_dispatch(hp, dest, n_slots, tm):
    hp = hp.reshape((-1,) + ROW_SHAPE)
    n_tok = hp.shape[0]
    rows_per_w = dest.shape[0] // SC_WORKERS
    chunks_per_w = n_tok // SC_WORKERS // SC_ROWS
    mesh = plsc.VectorSubcoreMesh(core_axis_name="c", subcore_axis_name="s")

    @functools.partial(
        pl.kernel, mesh=mesh,
        out_type=jax.ShapeDtypeStruct((n_slots,) + ROW_SHAPE, I32),
        scratch_types=[
            pltpu.VMEM((rows_per_w, SC_ROWS), I32),
            pltpu.VMEM((SC_ROWS,) + ROW_SHAPE, I32),
            pltpu.SemaphoreType.DMA,
        ],
        name="sc_dispatch",
    )
    def run(hp_hbm, dest_hbm, xs_hbm, dest_v, rows_v, sem):
        wid = lax.axis_index("s") * SC_CORES + lax.axis_index("c")
        pltpu.sync_copy(dest_hbm.at[pl.ds(wid * rows_per_w, rows_per_w)], dest_v)

        @pl.loop(0, chunks_per_w)
        def _(c):
            tok0 = (wid * chunks_per_w + c) * SC_ROWS
            pltpu.sync_copy(hp_hbm.at[pl.ds(tok0, SC_ROWS)], rows_v)
            copies = [pltpu.async_copy(rows_v, xs_hbm.at[dest_v.at[row]], sem)
                      for row in _sc_chunk_rows(c, tm)]
            for cp in copies:
                cp.wait()

    return run(hp, dest).reshape(n_slots * ROW_CHUNKS, LANES)


def _sc_gather(ys, dest, tm):
    ys = ys.reshape((-1,) + ROW_SHAPE)
    n_tok = dest.shape[0] * SC_ROWS // TOP_K
    rows_per_w = dest.shape[0] // SC_WORKERS
    chunks_per_w = n_tok // SC_WORKERS // SC_ROWS
    mesh = plsc.VectorSubcoreMesh(core_axis_name="c", subcore_axis_name="s")

    @functools.partial(
        pl.kernel, mesh=mesh,
        out_type=jax.ShapeDtypeStruct((TOP_K, n_tok) + ROW_SHAPE, I32),
        scratch_types=[
            pltpu.VMEM((rows_per_w, SC_ROWS), I32),
            pltpu.VMEM((SC_ROWS,) + ROW_SHAPE, I32),
            pltpu.SemaphoreType.DMA,
        ],
        name="sc_gather",
    )
    def run(ys_hbm, dest_hbm, yg_hbm, dest_v, rows_v, sem):
        wid = lax.axis_index("s") * SC_CORES + lax.axis_index("c")
        pltpu.sync_copy(dest_hbm.at[pl.ds(wid * rows_per_w, rows_per_w)], dest_v)

        @pl.loop(0, chunks_per_w)
        def _(c):
            tok0 = (wid * chunks_per_w + c) * SC_ROWS
            for kk, row in enumerate(_sc_chunk_rows(c, tm)):
                pltpu.async_copy(ys_hbm.at[dest_v.at[row]], rows_v, sem).wait()
                pltpu.sync_copy(rows_v, yg_hbm.at[kk, pl.ds(tok0, SC_ROWS)])

    return run(ys, dest).reshape(TOP_K, n_tok * ROW_CHUNKS, LANES)


def _expert_kernel(blk_e_ref, xs_ref, w1_ref, b1_ref, w2_ref, b2_ref, ys_ref, w1b_ref, w2b_ref):
    i = pl.program_id(0)
    changed = blk_e_ref[i] != blk_e_ref[jnp.maximum(i - 1, 0)]

    @pl.when((i == 0) | changed)
    def _():
        w1b_ref[...] = w1_ref[0, 0].astype(BF16)
        w2b_ref[...] = w2_ref[0, 0].astype(BF16)

    for r in range(0, xs_ref.shape[0] // ROW_CHUNKS, EXPERT_SUB):
        xb = _unpack_rows(xs_ref, r, EXPERT_SUB).astype(BF16)
        gu = jnp.dot(xb, w1b_ref[...], preferred_element_type=F32) + b1_ref[0, 0]
        gate = jnp.minimum(gu[:, :D_FF], SWIGLU_LIMIT)
        up = jnp.clip(gu[:, D_FF:], -SWIGLU_LIMIT, SWIGLU_LIMIT)
        act = (up + 1.0) * (gate * jax.nn.sigmoid(SWIGLU_ALPHA * gate))
        y = jnp.dot(act.astype(BF16), w2b_ref[...], preferred_element_type=F32) + b2_ref[0, 0]
        _pack_rows(y, ys_ref, r)


def _expert_ffn(blk_e, xs, layer, w1, b1, w2, b2):
    n_slots = xs.shape[0] // ROW_CHUNKS
    bm = EXPERT_TILE
    return pl.pallas_call(
        _expert_kernel,
        out_shape=jax.ShapeDtypeStruct(xs.shape, I32),
        grid_spec=pltpu.PrefetchScalarGridSpec(
            num_scalar_prefetch=1,
            grid=(n_slots // bm,),
            in_specs=[
                pl.BlockSpec((bm * ROW_CHUNKS, LANES), lambda i, be: (i, 0)),
                pl.BlockSpec((1, 1, D_MODEL, 2 * D_FF), lambda i, be: (layer, be[i], 0, 0)),
                pl.BlockSpec((1, 1, 1, 2 * D_FF), lambda i, be: (layer, be[i], 0, 0)),
                pl.BlockSpec((1, 1, D_FF, D_MODEL), lambda i, be: (layer, be[i], 0, 0)),
                pl.BlockSpec((1, 1, 1, D_MODEL), lambda i, be: (layer, be[i], 0, 0)),
            ],
            out_specs=pl.BlockSpec((bm * ROW_CHUNKS, LANES), lambda i, be: (i, 0)),
            scratch_shapes=[pltpu.VMEM((D_MODEL, 2 * D_FF), BF16), pltpu.VMEM((D_FF, D_MODEL), BF16)],
        ),
        compiler_params=pltpu.CompilerParams(
            dimension_semantics=("arbitrary",), vmem_limit_bytes=VMEM_LIMIT),
        name="expert_ffn",
    )(blk_e, xs, w1, b1, w2, b2)


def _combine_kernel(yg_ref, wtm_ref, x1_ref, mod_ref, fn_ref, *o_refs, final, n0):
    tm = x1_ref.shape[0]
    wtm = wtm_ref[...]
    moe = jnp.zeros((tm, D_MODEL), F32)
    for kk in range(TOP_K):
        moe = moe + wtm[:, kk:kk + 1] * _unpack_rows(yg_ref.at[kk], 0, tm)
    g2 = mod_ref[0][:, 5 * D_MODEL:6 * D_MODEL]
    out = x1_ref[...] + g2 * moe
    if not final:
        o_refs[0][...] = out
        return
    out = _rms(out) * fn_ref[...]

    @pl.when(pl.program_id(0) < n0)
    def _():
        o_refs[0][...] = out

    @pl.when(pl.program_id(0) >= n0)
    def _():
        o_refs[1][...] = out


def _combine(yg, wtm, x1, mod, final_norm, seq_len, final, n_first):
    n_tok = x1.shape[0]
    tm = TOKEN_TILE
    per_seq = seq_len // tm
    n0 = n_first // tm
    if final:
        out_shape = (jax.ShapeDtypeStruct((n_first, D_MODEL), F32),
                     jax.ShapeDtypeStruct((n_tok - n_first, D_MODEL), F32))
        out_specs = (pl.BlockSpec((tm, D_MODEL), lambda i: (jnp.minimum(i, n0 - 1), 0)),
                     pl.BlockSpec((tm, D_MODEL), lambda i: (jnp.maximum(i - n0, 0), 0)))
    else:
        out_shape = jax.ShapeDtypeStruct((n_tok, D_MODEL), F32)
        out_specs = pl.BlockSpec((tm, D_MODEL), lambda i: (i, 0))
    return pl.pallas_call(
        functools.partial(_combine_kernel, final=final, n0=n0),
        out_shape=out_shape,
        grid=(n_tok // tm,),
        in_specs=[
            pl.BlockSpec((TOP_K, tm * ROW_CHUNKS, LANES), lambda i: (0, i, 0)),
            pl.BlockSpec((tm, LANES), lambda i: (i, 0)),
            pl.BlockSpec((tm, D_MODEL), lambda i: (i, 0)),
            pl.BlockSpec((1, 1, 6 * D_MODEL), lambda i: (i // per_seq, 0, 0)),
            pl.BlockSpec((1, D_MODEL), lambda i: (0, 0)),
        ],
        out_specs=out_specs,
        compiler_params=pltpu.CompilerParams(
            dimension_semantics=("arbitrary",), vmem_limit_bytes=VMEM_LIMIT),
        name="moe_combine",
    )(yg, wtm, x1, mod, final_norm)


def _moe_layer(route, mod, layer, ffn_w, final_norm, seq_len, final, n_first):
    x1, hp, topi, rank, wtm, counts = route
    n_tok = x1.shape[0]
    bm = EXPERT_TILE
    n_slots = n_tok * TOP_K + N_EXPERTS * bm
    n_blk = n_slots // bm
    cnt = counts[:, 0].astype(I32)
    padded = (cnt + bm - 1) // bm * bm
    pend = jnp.cumsum(padded)
    pstart = (pend - padded).astype(I32)
    blk_start = jnp.arange(n_blk, dtype=I32) * bm
    blk_e = jnp.minimum(jnp.sum(blk_start[:, None] >= pend[None, :], axis=1), N_EXPERTS - 1)
    tm = topi.shape[2]
    dest = _slots(pstart, topi, rank).reshape(-1, SC_ROWS)
    xs = _sc_dispatch(hp, dest, n_slots, tm)
    ys = _expert_ffn(blk_e.astype(I32), xs, layer, *ffn_w)
    yg = _sc_gather(ys, dest, tm)
    return _combine(yg, wtm, x1, mod, final_norm, seq_len, final, n_first)


def _rope_tables(seq_len):
    inv_freq = 1.0 / (ROPE_THETA ** (jnp.arange(0, D_ROPE, 2, dtype=F32) / D_ROPE))
    ang = jnp.arange(seq_len, dtype=F32)[:, None] * inv_freq[None, :]
    cos, sin = jnp.cos(ang), jnp.sin(ang)
    ones = jnp.ones((seq_len, ROPE_LO), F32)
    zeros_lo = jnp.zeros((seq_len, ROPE_LO), F32)
    zeros_hi = jnp.zeros((seq_len, HEAD_PAD - ROPE_LO - D_ROPE), F32)
    zeros_h = jnp.zeros((seq_len, ROPE_HALF), F32)
    rope_c = jnp.concatenate([ones, cos, cos, zeros_hi], axis=1)
    rope_s1 = jnp.concatenate([zeros_lo, -sin, zeros_h, zeros_hi], axis=1)
    rope_s2 = jnp.concatenate([zeros_lo, zeros_h, sin, zeros_hi], axis=1)
    return cos.T, sin.T, rope_c, rope_s1, rope_s2


def _mla_weights(w_in, q_norm, kv_norm, w_uq, w_ukv, w_o, seq_len):
    d_qk = D_NOPE + D_ROPE
    q_scale = d_qk ** -0.5 * math.log2(math.e)
    pad_pe = jnp.zeros((D_MODEL, HEAD_PAD), F32).at[:, ROPE_LO:ROPE_LO + D_ROPE].set(
        w_in[:, Q_RANK + KV_RANK:])
    w_in_p = jnp.concatenate([w_in[:, :Q_RANK + KV_RANK], pad_pe], axis=1)
    w_uq_p = jnp.pad(w_uq.reshape(Q_RANK, N_HEADS, d_qk) * q_scale,
                     ((0, 0), (0, 0), (0, HEAD_PAD - d_qk)))
    w_kv = w_ukv.reshape(KV_RANK, N_HEADS, D_NOPE + D_V)
    w_uk_p = jnp.pad(w_kv[:, :, :D_NOPE], ((0, 0), (0, 0), (0, HEAD_PAD - D_NOPE)))
    cos_t, sin_t, rope_c, rope_s1, rope_s2 = _rope_tables(seq_len)
    return {
        "w_in": w_in_p.astype(BF16),
        "q_norm": q_norm.reshape(1, Q_RANK),
        "kv_norm": kv_norm.reshape(1, KV_RANK),
        "w_uq_t": w_uq_p.reshape(Q_RANK, N_HEADS * HEAD_PAD).T.astype(BF16),
        "w_uk": w_uk_p.reshape(KV_RANK, N_HEADS * HEAD_PAD).astype(BF16),
        "w_uv_t": w_kv[:, :, D_NOPE:].reshape(KV_RANK, N_HEADS * D_V).T.astype(BF16),
        "w_o": w_o.astype(BF16),
        "cos_t": cos_t, "sin_t": sin_t, "rope_c": rope_c, "rope_s1": rope_s1, "rope_s2": rope_s2,
    }


def _router_weights(router_w, router_b):
    return router_w.T.astype(BF16), router_b.reshape(N_EXPERTS, 1)


def kernel(x_prompt, x_sample, c_prompt, c_sample, ada_w, ada_b, mla_w_in, mla_q_norm,
           mla_kv_norm, mla_w_uq, mla_w_ukv, mla_w_o, pool_w, pool_scale, router_w, router_b,
           moe_w1, moe_b1, moe_w2, moe_b2, final_norm):
    n_prompt, seq_len, _ = x_prompt.shape
    assert x_sample.shape[1] == seq_len and seq_len % TOKEN_TILE == 0
    depth = ada_w.shape[0]
    n_sample = x_sample.shape[0]
    n_seq = n_prompt + n_sample
    n_first = n_prompt * seq_len
    x_parts = [x_prompt.reshape(-1, D_MODEL), x_sample.reshape(-1, D_MODEL)]
    mods = _ada_mod(jnp.concatenate([c_prompt, c_sample], axis=0), ada_w, ada_b)
    fnorm = final_norm.reshape(1, D_MODEL)
    ffn_w = (moe_w1, moe_b1.reshape(depth, N_EXPERTS, 1, 2 * D_FF),
             moe_w2, moe_b2.reshape(depth, N_EXPERTS, 1, D_MODEL))

    for i in range(depth):
        mod = mods[i].reshape(n_seq, 1, 6 * D_MODEL)
        rw_t, rb = _router_weights(router_w[i], router_b[i])
        j = i // 2
        if i % 2 == 0:
            w = _mla_weights(mla_w_in[j], mla_q_norm[j], mla_kv_norm[j], mla_w_uq[j],
                             mla_w_ukv[j], mla_w_o[j], seq_len)
            qt, k, vt = _mla_pre(x_parts, mod, w, seq_len)
            ot = _attention(qt, k, vt, seq_len)
            route = _post_mix(ot, x_parts, mod, w["w_o"], rw_t, rb, seq_len)
        else:
            x = x_parts[0] if len(x_parts) == 1 else jnp.concatenate(x_parts, axis=0)
            route = _pool_mix(x, mod, pool_w[j].astype(BF16), pool_scale[j].reshape(1, D_MODEL),
                              rw_t, rb, seq_len)
        out = _moe_layer(route, mod, i, ffn_w, fnorm, seq_len, i == depth - 1, n_first)
        x_parts = list(out) if isinstance(out, (tuple, list)) else [out]

    y_prompt, y_sample = x_parts
    return (y_prompt.reshape(n_prompt, seq_len, D_MODEL),
            y_sample.reshape(n_sample, seq_len, D_MODEL))
```

```python
import functools
import math

import jax
import jax.numpy as jnp
from jax import lax
from jax.experimental import pallas as pl
from jax.experimental.pallas import tpu as pltpu
from jax.experimental.pallas import tpu_sc as plsc

F32 = jnp.float32
BF16 = jnp.bfloat16
I32 = jnp.int32

D_MODEL = 1024
N_HEADS = 16
Q_RANK = 384
KV_RANK = 256
D_NOPE = 64
D_ROPE = 32
D_V = 64
V_ROWS = D_V + 16
ROPE_THETA = 10000.0
POOL_WINDOWS = (2, 4, 8, 16)
POOL_GROUP = D_MODEL // len(POOL_WINDOWS)
N_EXPERTS = 32
TOP_K = 4
D_FF = D_MODEL
SWIGLU_LIMIT = 7.0
SWIGLU_ALPHA = 1.702
EPS = 1e-6

LANES = 128
SUBLANES = 8
HEAD_PAD = 128
ROPE_LO = D_NOPE
ROPE_HALF = D_ROPE // 2
VMEM_LIMIT = 56 * 1024 * 1024

TOKEN_TILE = 256
EXPERT_TILE = 512
EXPERT_SUB = 256
ATTN_KEY_CHUNKS = 4
ATTN_HEADS_PER_STEP = 4
POOL_HALO = 8
PACKED = D_MODEL // 2
ROW_CHUNKS = PACKED // LANES
ROW_SHAPE = (ROW_CHUNKS, LANES)

SC_CORES = 2
SC_SUBCORES = 16
SC_WORKERS = SC_CORES * SC_SUBCORES
SC_ROWS = 64


def _rms(x):
    return x * lax.rsqrt(jnp.mean(x * x, axis=-1, keepdims=True) + EPS)


def _pack_rows(y, out_ref, row0=0):
    n = y.shape[0]
    lo = lax.bitcast_convert_type(y[:, :PACKED].astype(BF16).astype(F32), I32)
    hi = lax.bitcast_convert_type(y[:, PACKED:].astype(BF16).astype(F32), I32)
    words = lax.shift_right_logical(lo, 16) | (hi & jnp.int32(-65536))
    for j in range(ROW_CHUNKS):
        out_ref[pl.ds(row0 * ROW_CHUNKS + j, n, stride=ROW_CHUNKS), :] = (
            words[:, j * LANES:(j + 1) * LANES])


def _unpack_rows(in_ref, row0, n):
    words = jnp.concatenate(
        [in_ref[pl.ds(row0 * ROW_CHUNKS + j, n, stride=ROW_CHUNKS), :] for j in range(ROW_CHUNKS)],
        axis=1)
    lo = lax.bitcast_convert_type(lax.shift_left(words, 16), F32)
    hi = lax.bitcast_convert_type(words & jnp.int32(-65536), F32)
    return jnp.concatenate([lo, hi], axis=1)


def _part_specs(parts, tm):
    if len(parts) == 1:
        return [pl.BlockSpec((tm, D_MODEL), lambda i: (i, 0))]
    n0 = parts[0].shape[0] // tm
    return [pl.BlockSpec((tm, D_MODEL), lambda i: (jnp.minimum(i, n0 - 1), 0)),
            pl.BlockSpec((tm, D_MODEL), lambda i: (jnp.maximum(i - n0, 0), 0))]


def _part_load(refs, n0):
    if len(refs) == 1:
        return refs[0][...]
    return jnp.where(pl.program_id(0) < n0, refs[0][...], refs[1][...])


def _ada_kernel(c_ref, w_ref, b_ref, o_ref):
    c = c_ref[...]
    act = (c * jax.nn.sigmoid(c)).astype(BF16)
    o_ref[0] = jnp.dot(act, w_ref[0].astype(BF16), preferred_element_type=F32) + b_ref[0]


def _ada_mod(c, ada_w, ada_b):
    depth, _, n_out = ada_w.shape
    n_seq = c.shape[0]
    tn = 1536
    return pl.pallas_call(
        _ada_kernel,
        out_shape=jax.ShapeDtypeStruct((depth, n_seq, n_out), F32),
        grid=(depth, n_out // tn),
        in_specs=[
            pl.BlockSpec((n_seq, D_MODEL), lambda l, j: (0, 0)),
            pl.BlockSpec((1, D_MODEL, tn), lambda l, j: (l, 0, j)),
            pl.BlockSpec((1, 1, tn), lambda l, j: (l, 0, j)),
        ],
        out_specs=pl.BlockSpec((1, n_seq, tn), lambda l, j: (l, 0, j)),
        compiler_params=pltpu.CompilerParams(
            dimension_semantics=("arbitrary", "arbitrary"), vmem_limit_bytes=VMEM_LIMIT),
        name="ada_mod",
    )(c, ada_w, ada_b.reshape(depth, 1, n_out))


def _mla_pre_kernel(*refs, n_x, n0):
    x_refs = refs[:n_x]
    (mod_ref, win_ref, qn_ref, kvn_ref, wuqt_ref, wuk_ref, wuvt_ref,
     cost_ref, sint_ref, ck_ref, s1k_ref, s2k_ref, qt_ref, k_ref, vt_ref) = refs[n_x:]
    tm = x_refs[0].shape[0]
    mod = mod_ref[0]
    sh1 = mod[:, 0:D_MODEL]
    sc1 = mod[:, D_MODEL:2 * D_MODEL]
    h = (_rms(_part_load(x_refs, n0)) * (1.0 + sc1) + sh1).astype(BF16)
    a = jnp.dot(h, win_ref[...], preferred_element_type=F32)
    cq = (_rms(a[:, :Q_RANK]) * qn_ref[...]).astype(BF16)
    ckv = (_rms(a[:, Q_RANK:Q_RANK + KV_RANK]) * kvn_ref[...]).astype(BF16)

    kpe = a[:, Q_RANK + KV_RANK:]
    kpe = (kpe * ck_ref[...]
           + pltpu.roll(kpe, LANES - ROPE_HALF, 1) * s1k_ref[...]
           + pltpu.roll(kpe, ROPE_HALF, 1) * s2k_ref[...])

    qt = lax.dot_general(wuqt_ref[...], cq, (((1,), (1,)), ((), ())),
                         preferred_element_type=F32)
    q3 = qt.reshape(N_HEADS, HEAD_PAD, tm)
    x1 = q3[:, ROPE_LO:ROPE_LO + ROPE_HALF, :]
    x2 = q3[:, ROPE_LO + ROPE_HALF:ROPE_LO + D_ROPE, :]
    cos = cost_ref[...][None]
    sin = sint_ref[...][None]
    q3 = jnp.concatenate(
        [q3[:, :ROPE_LO, :], x1 * cos - x2 * sin, x2 * cos + x1 * sin,
         q3[:, ROPE_LO + D_ROPE:, :]], axis=1)
    qt_ref[:, 0] = q3.astype(BF16)

    kn = jnp.dot(ckv, wuk_ref[...], preferred_element_type=F32)
    for hd in range(N_HEADS):
        k_ref[hd] = (kn[:, hd * HEAD_PAD:(hd + 1) * HEAD_PAD] + kpe).astype(BF16)

    vt = lax.dot_general(wuvt_ref[...], ckv, (((1,), (1,)), ((), ())),
                         preferred_element_type=F32)
    vt_ref[:, 0, :D_V, :] = vt.reshape(N_HEADS, D_V, tm).astype(BF16)
    vt_ref[:, 0, D_V:, :] = jnp.ones((N_HEADS, V_ROWS - D_V, tm), BF16)


def _mla_pre(x_parts, mod, w, seq_len):
    n_tok = sum(p.shape[0] for p in x_parts)
    tm = TOKEN_TILE
    n_tiles = n_tok // tm
    per_seq = seq_len // tm
    const = lambda i: (0, 0)
    return pl.pallas_call(
        functools.partial(_mla_pre_kernel, n_x=len(x_parts), n0=x_parts[0].shape[0] // tm),
        out_shape=(
            jax.ShapeDtypeStruct((N_HEADS, n_tiles, HEAD_PAD, tm), BF16),
            jax.ShapeDtypeStruct((N_HEADS, n_tok, HEAD_PAD), BF16),
            jax.ShapeDtypeStruct((N_HEADS, n_tok // seq_len, V_ROWS, seq_len), BF16),
        ),
        grid=(n_tiles,),
        in_specs=_part_specs(x_parts, tm) + [
            pl.BlockSpec((1, 1, 6 * D_MODEL), lambda i: (i // per_seq, 0, 0)),
            pl.BlockSpec(w["w_in"].shape, const),
            pl.BlockSpec(w["q_norm"].shape, const),
            pl.BlockSpec(w["kv_norm"].shape, const),
            pl.BlockSpec(w["w_uq_t"].shape, const),
            pl.BlockSpec(w["w_uk"].shape, const),
            pl.BlockSpec(w["w_uv_t"].shape, const),
            pl.BlockSpec((ROPE_HALF, tm), lambda i: (0, i % per_seq)),
            pl.BlockSpec((ROPE_HALF, tm), lambda i: (0, i % per_seq)),
            pl.BlockSpec((tm, LANES), lambda i: (i % per_seq, 0)),
            pl.BlockSpec((tm, LANES), lambda i: (i % per_seq, 0)),
            pl.BlockSpec((tm, LANES), lambda i: (i % per_seq, 0)),
        ],
        out_specs=(
            pl.BlockSpec((N_HEADS, 1, HEAD_PAD, tm), lambda i: (0, i, 0, 0)),
            pl.BlockSpec((N_HEADS, tm, HEAD_PAD), lambda i: (0, i, 0)),
            pl.BlockSpec((N_HEADS, 1, V_ROWS, tm), lambda i: (0, i // per_seq, 0, i % per_seq)),
        ),
        compiler_params=pltpu.CompilerParams(
            dimension_semantics=("arbitrary",), vmem_limit_bytes=VMEM_LIMIT),
        name="mla_pre",
    )(*x_parts, mod, w["w_in"], w["q_norm"], w["kv_norm"], w["w_uq_t"], w["w_uk"], w["w_uv_t"],
      w["cos_t"], w["sin_t"], w["rope_c"], w["rope_s1"], w["rope_s2"])


def _attention_kernel(qt_ref, k_ref, vt_ref, ot_ref, s0_ref, s1_ref):
    n_heads, n_q = qt_ref.shape[:2]
    n_tiles = n_heads * n_q
    n_keys = k_ref.shape[1]
    kc = n_keys // ATTN_KEY_CHUNKS

    def stage(t_next, s_next_ref, t_cur, s_cur_ref, m_cur):
        m_next, o = None, None
        if t_next is not None:
            h_next, j_next = t_next // n_q, t_next % n_q
        if t_cur is not None:
            h_cur, j_cur = t_cur // n_q, t_cur % n_q
        for c in range(ATTN_KEY_CHUNKS):
            rows = slice(c * kc, (c + 1) * kc)
            if t_next is not None:
                s = jnp.dot(k_ref[h_next, rows, :], qt_ref[h_next, j_next],
                            preferred_element_type=F32)
                s_next_ref[rows, :] = s
                cm = jnp.max(s, axis=0, keepdims=True)
                m_next = cm if m_next is None else jnp.maximum(m_next, cm)
            if t_cur is not None:
                p = jnp.exp2(s_cur_ref[rows, :] - m_cur).astype(BF16)
                part = jnp.dot(vt_ref[h_cur, 0, :, rows], p,
                               preferred_element_type=F32)
                o = part if o is None else o + part
        if t_cur is not None:
            denom = o[D_V:D_V + 1, :]
            ot_ref[h_cur, j_cur] = (o[:D_V, :] * (1.0 / denom)).astype(BF16)
        return m_next

    def body(i, m0):
        t = 2 * i
        m1 = stage(t + 1, s1_ref, t, s0_ref, m0)
        return stage(t + 2, s0_ref, t + 1, s1_ref, m1)

    m0 = lax.fori_loop(0, n_tiles // 2 - 1, body, stage(0, s0_ref, None, None, None))
    m1 = stage(n_tiles - 1, s1_ref, n_tiles - 2, s0_ref, m0)
    stage(None, None, n_tiles - 1, s1_ref, m1)


def _attention(qt, k, vt, seq_len):
    n_heads, n_tiles, _, tq = qt.shape
    per_seq = seq_len // tq
    n_seq = n_tiles // per_seq
    hb = ATTN_HEADS_PER_STEP
    return pl.pallas_call(
        _attention_kernel,
        out_shape=jax.ShapeDtypeStruct((n_heads, n_tiles, D_V, tq), BF16),
        grid=(n_seq, n_heads // hb),
        in_specs=[
            pl.BlockSpec((hb, per_seq, HEAD_PAD, tq), lambda b, h: (h, b, 0, 0)),
            pl.BlockSpec((hb, seq_len, HEAD_PAD), lambda b, h: (h, b, 0)),
            pl.BlockSpec((hb, 1, V_ROWS, seq_len), lambda b, h: (h, b, 0, 0)),
        ],
        out_specs=pl.BlockSpec((hb, per_seq, D_V, tq), lambda b, h: (h, b, 0, 0)),
        scratch_shapes=[pltpu.VMEM((seq_len, tq), F32), pltpu.VMEM((seq_len, tq), F32)],
        compiler_params=pltpu.CompilerParams(
            dimension_semantics=("arbitrary", "arbitrary"), vmem_limit_bytes=VMEM_LIMIT),
        name="attention",
    )(qt, k, vt)


def _route_tail(x1, mod, rwt_ref, rb_ref, x1_ref, hp_ref, topi_ref, rank_ref, wtm_ref, cnt_ref):
    tm = x1.shape[0]
    sh2 = mod[:, 3 * D_MODEL:4 * D_MODEL]
    sc2 = mod[:, 4 * D_MODEL:5 * D_MODEL]
    x1_ref[...] = x1
    h2 = _rms(x1) * (1.0 + sc2) + sh2
    _pack_rows(h2, hp_ref)
    logits = lax.dot_general(rwt_ref[...], h2.astype(BF16), (((1,), (1,)), ((), ())),
                             preferred_element_type=F32) + rb_ref[...]
    e_iota = lax.broadcasted_iota(I32, (N_EXPERTS, tm), 0)
    vals, idxs = [], []
    work = logits
    for _ in range(TOP_K):
        m = jnp.max(work, axis=0, keepdims=True)
        idx = jnp.min(jnp.where(work == m, e_iota, N_EXPERTS), axis=0, keepdims=True)
        vals.append(m)
        idxs.append(idx)
        work = jnp.where(e_iota == idx, -jnp.inf, work)
    ex = [jnp.exp(v - vals[0]) for v in vals]
    inv = 1.0 / (ex[0] + ex[1] + ex[2] + ex[3])
    topw = jnp.concatenate([e * inv for e in ex], axis=0)
    topi_ref[0] = jnp.concatenate(idxs, axis=0)

    @pl.when(pl.program_id(0) == 0)
    def _():
        cnt_ref[...] = jnp.zeros_like(cnt_ref)

    row = lax.broadcasted_iota(I32, (tm, tm), 0)
    col = lax.broadcasted_iota(I32, (tm, tm), 1)
    earlier = (row < col).astype(BF16)
    running = cnt_ref[...][:, 0:1]
    ranks = []
    for kk in range(TOP_K):
        onehot = (e_iota == idxs[kk]).astype(F32)
        before = jnp.dot(onehot.astype(BF16), earlier, preferred_element_type=F32)
        ranks.append(jnp.sum(onehot * (running + before), axis=0, keepdims=True))
        running = running + jnp.sum(onehot, axis=1, keepdims=True)
    rank_ref[0] = jnp.concatenate(ranks, axis=0).astype(I32)
    cnt_ref[...] = jnp.broadcast_to(running, cnt_ref.shape)

    wpad = jnp.concatenate([topw, jnp.zeros((LANES - TOP_K, tm), F32)], axis=0)
    wtm_ref[...] = wpad.T


def _route_out_shapes(n_tok, tm):
    n_tiles = n_tok // tm
    return (
        jax.ShapeDtypeStruct((n_tok, D_MODEL), F32),
        jax.ShapeDtypeStruct((n_tok * ROW_CHUNKS, LANES), I32),
        jax.ShapeDtypeStruct((n_tiles, TOP_K, tm), I32),
        jax.ShapeDtypeStruct((n_tiles, TOP_K, tm), I32),
        jax.ShapeDtypeStruct((n_tok, LANES), F32),
        jax.ShapeDtypeStruct((N_EXPERTS, LANES), F32),
    )


def _route_out_specs(tm):
    return (
        pl.BlockSpec((tm, D_MODEL), lambda i: (i, 0)),
        pl.BlockSpec((tm * ROW_CHUNKS, LANES), lambda i: (i, 0)),
        pl.BlockSpec((1, TOP_K, tm), lambda i: (i, 0, 0)),
        pl.BlockSpec((1, TOP_K, tm), lambda i: (i, 0, 0)),
        pl.BlockSpec((tm, LANES), lambda i: (i, 0)),
        pl.BlockSpec((N_EXPERTS, LANES), lambda i: (0, 0)),
    )


def _post_mix_kernel(ot_ref, *refs, n_x, n0):
    x_refs = refs[:n_x]
    mod_ref, wo_ref, rwt_ref, rb_ref = refs[n_x:n_x + 4]
    out_refs = refs[n_x + 4:]
    tm = x_refs[0].shape[0]
    mod = mod_ref[0]
    g1 = mod[:, 2 * D_MODEL:3 * D_MODEL]
    ot = ot_ref[:, 0].reshape(N_HEADS * D_V, tm)
    mix = lax.dot_general(ot, wo_ref[...], (((0,), (0,)), ((), ())),
                          preferred_element_type=F32)
    _route_tail(_part_load(x_refs, n0) + g1 * mix, mod, rwt_ref, rb_ref, *out_refs)


def _post_mix(ot, x_parts, mod, w_o, rw_t, rb, seq_len):
    n_tok = sum(p.shape[0] for p in x_parts)
    tm = ot.shape[3]
    per_seq = seq_len // tm
    const = lambda i: (0, 0)
    return pl.pallas_call(
        functools.partial(_post_mix_kernel, n_x=len(x_parts), n0=x_parts[0].shape[0] // tm),
        out_shape=_route_out_shapes(n_tok, tm),
        grid=(n_tok // tm,),
        in_specs=[pl.BlockSpec((N_HEADS, 1, D_V, tm), lambda i: (0, i, 0, 0))]
        + _part_specs(x_parts, tm) + [
            pl.BlockSpec((1, 1, 6 * D_MODEL), lambda i: (i // per_seq, 0, 0)),
            pl.BlockSpec(w_o.shape, const),
            pl.BlockSpec(rw_t.shape, const),
            pl.BlockSpec(rb.shape, const),
        ],
        out_specs=_route_out_specs(tm),
        compiler_params=pltpu.CompilerParams(
            dimension_semantics=("arbitrary",), vmem_limit_bytes=VMEM_LIMIT),
        name="post_mix",
    )(ot, *x_parts, mod, w_o, rw_t, rb)


def _pool_mix_kernel(x_ref, xp_ref, xn_ref, mod_ref, pw_ref, ps_ref, rwt_ref, rb_ref,
                     *out_refs, seq_len):
    tm = x_ref.shape[0]
    per_seq = seq_len // tm
    mod = mod_ref[0]
    sh1 = mod[:, 0:D_MODEL]
    sc1 = mod[:, D_MODEL:2 * D_MODEL]
    g1 = mod[:, 2 * D_MODEL:3 * D_MODEL]
    pos0 = (pl.program_id(0) % per_seq) * tm

    def normed(v):
        return _rms(v) * (1.0 + sc1) + sh1

    x = x_ref[...]
    h = normed(x)
    h_prev = jnp.where(pos0 > 0, normed(xp_ref[...]), 0.0)
    h_next = jnp.where(pos0 + tm < seq_len, normed(xn_ref[...]), 0.0)
    hext = jnp.concatenate([h_prev, h, h_next], axis=0)
    n_ext = tm + 2 * POOL_HALO
    pos = pos0 + lax.broadcasted_iota(I32, (tm, 1), 0)

    outs = []
    for g, win in enumerate(POOL_WINDOWS):
        left = win // 2
        right = win - 1 - left
        cols = slice(g * POOL_GROUP, (g + 1) * POOL_GROUP)
        s = hext[:, cols]
        span = 1
        while span < win:
            s = s + pltpu.roll(s, span, 0)
            span *= 2
        if right:
            s = pltpu.roll(s, n_ext - right, 0)
        num = s[POOL_HALO:POOL_HALO + tm, :]
        count = (jnp.minimum(pos + right + 1, seq_len) - jnp.maximum(pos - left, 0)).astype(F32)
        diff = (num / count - h[:, cols]).astype(BF16)
        outs.append(jnp.dot(diff, pw_ref[g], preferred_element_type=F32))
    mix = jnp.concatenate(outs, axis=1) * ps_ref[...]
    _route_tail(x + g1 * mix, mod, rwt_ref, rb_ref, *out_refs)


def _pool_mix(x, mod, pool_w, pool_scale, rw_t, rb, seq_len):
    n_tok = x.shape[0]
    tm = TOKEN_TILE
    per_seq = seq_len // tm
    halo_per_tile = tm // POOL_HALO
    n_halo = n_tok // POOL_HALO
    const = lambda i: (0, 0)
    return pl.pallas_call(
        functools.partial(_pool_mix_kernel, seq_len=seq_len),
        out_shape=_route_out_shapes(n_tok, tm),
        grid=(n_tok // tm,),
        in_specs=[
            pl.BlockSpec((tm, D_MODEL), lambda i: (i, 0)),
            pl.BlockSpec((POOL_HALO, D_MODEL),
                         lambda i: (jnp.maximum(i * halo_per_tile - 1, 0), 0)),
            pl.BlockSpec((POOL_HALO, D_MODEL),
                         lambda i: (jnp.minimum((i + 1) * halo_per_tile, n_halo - 1), 0)),
            pl.BlockSpec((1, 1, 6 * D_MODEL), lambda i: (i // per_seq, 0, 0)),
            pl.BlockSpec(pool_w.shape, lambda i: (0, 0, 0)),
            pl.BlockSpec(pool_scale.shape, const),
            pl.BlockSpec(rw_t.shape, const),
            pl.BlockSpec(rb.shape, const),
        ],
        out_specs=_route_out_specs(tm),
        compiler_params=pltpu.CompilerParams(
            dimension_semantics=("arbitrary",), vmem_limit_bytes=VMEM_LIMIT),
        name="pool_mix",
    )(x, x, x, mod, pool_w, pool_scale, rw_t, rb)


def _slots_kernel(pstart_ref, topi_ref, rank_ref, dest_ref):
    topi = topi_ref[...]
    start = jnp.zeros_like(topi)
    for e in range(N_EXPERTS):
        start = jnp.where(topi == e, pstart_ref[e], start)
    dest_ref[...] = start + rank_ref[...]


def _slots(pstart, topi, rank):
    n_tiles, _, tm = topi.shape
    tb = math.gcd(n_tiles, 32)
    spec = pl.BlockSpec((tb, TOP_K, tm), lambda i, ps: (i, 0, 0))
    return pl.pallas_call(
        _slots_kernel,
        out_shape=jax.ShapeDtypeStruct(topi.shape, I32),
        grid_spec=pltpu.PrefetchScalarGridSpec(
            num_scalar_prefetch=1, grid=(n_tiles // tb,), in_specs=[spec, spec], out_specs=spec),
        compiler_params=pltpu.CompilerParams(dimension_semantics=("arbitrary",)),
        name="moe_slots",
    )(pstart, topi, rank)


def _sc_chunk_rows(c, tm):
    per_tile = tm // SC_ROWS
    tile = c // per_tile
    part = c % per_tile
    return [(tile * TOP_K + kk) * per_tile + part for kk in range(TOP_K)]


def _sc_dispatch(hp, dest, n_slots, tm):
    hp = hp.reshape((-1,) + ROW_SHAPE)
    n_tok = hp.shape[0]
    rows_per_w = dest.shape[0] // SC_WORKERS
    chunks_per_w = n_tok // SC_WORKERS // SC_ROWS
    mesh = plsc.VectorSubcoreMesh(core_axis_name="c", subcore_axis_name="s")

    @functools.partial(
        pl.kernel, mesh=mesh,
        out_type=jax.ShapeDtypeStruct((n_slots,) + ROW_SHAPE, I32),
        scratch_types=[
            pltpu.VMEM((rows_per_w, SC_ROWS), I32),
            pltpu.VMEM((SC_ROWS,) + ROW_SHAPE, I32),
            pltpu.SemaphoreType.DMA,
        ],
        name="sc_dispatch",
    )
    def run(hp_hbm, dest_hbm, xs_hbm, dest_v, rows_v, sem):
        wid = lax.axis_index("s") * SC_CORES + lax.axis_index("c")
        pltpu.sync_copy(dest_hbm.at[pl.ds(wid * rows_per_w, rows_per_w)], dest_v)

        @pl.loop(0, chunks_per_w)
        def _(c):
            tok0 = (wid * chunks_per_w + c) * SC_ROWS
            pltpu.sync_copy(hp_hbm.at[pl.ds(tok0, SC_ROWS)], rows_v)
            copies = [pltpu.async_copy(rows_v, xs_hbm.at[dest_v.at[row]], sem)
                      for row in _sc_chunk_rows(c, tm)]
            for cp in copies:
                cp.wait()

    return run(hp, dest).reshape(n_slots * ROW_CHUNKS, LANES)


def _sc_gather(ys, dest, tm):
    ys = ys.reshape((-1,) + ROW_SHAPE)
    n_tok = dest.shape[0] * SC_ROWS // TOP_K
    rows_per_w = dest.shape[0] // SC_WORKERS
    chunks_per_w = n_tok // SC_WORKERS // SC_ROWS
    mesh = plsc.VectorSubcoreMesh(core_axis_name="c", subcore_axis_name="s")

    @functools.partial(
        pl.kernel, mesh=mesh,
        out_type=jax.ShapeDtypeStruct((TOP_K, n_tok) + ROW_SHAPE, I32),
        scratch_types=[
            pltpu.VMEM((rows_per_w, SC_ROWS), I32),
            pltpu.VMEM((SC_ROWS,) + ROW_SHAPE, I32),
            pltpu.SemaphoreType.DMA,
        ],
        name="sc_gather",
    )
    def run(ys_hbm, dest_hbm, yg_hbm, dest_v, rows_v, sem):
        wid = lax.axis_index("s") * SC_CORES + lax.axis_index("c")
        pltpu.sync_copy(dest_hbm.at[pl.ds(wid * rows_per_w, rows_per_w)], dest_v)

        @pl.loop(0, chunks_per_w)
        def _(c):
            tok0 = (wid * chunks_per_w + c) * SC_ROWS
            for kk, row in enumerate(_sc_chunk_rows(c, tm)):
                pltpu.async_copy(ys_hbm.at[dest_v.at[row]], rows_v, sem).wait()
                pltpu.sync_copy(rows_v, yg_hbm.at[kk, pl.ds(tok0, SC_ROWS)])

    return run(ys, dest).reshape(TOP_K, n_tok * ROW_CHUNKS, LANES)


def _expert_kernel(blk_e_ref, xs_ref, w1_ref, b1_ref, w2_ref, b2_ref, ys_ref, w1b_ref, w2b_ref):
    i = pl.program_id(0)
    changed = blk_e_ref[i] != blk_e_ref[jnp.maximum(i - 1, 0)]

    @pl.when((i == 0) | changed)
    def _():
        w1b_ref[...] = w1_ref[0, 0].astype(BF16)
        w2b_ref[...] = w2_ref[0, 0].astype(BF16)

    for r in range(0, xs_ref.shape[0] // ROW_CHUNKS, EXPERT_SUB):
        xb = _unpack_rows(xs_ref, r, EXPERT_SUB).astype(BF16)
        gu = jnp.dot(xb, w1b_ref[...], preferred_element_type=F32) + b1_ref[0, 0]
        gate = jnp.minimum(gu[:, :D_FF], SWIGLU_LIMIT)
        up = jnp.clip(gu[:, D_FF:], -SWIGLU_LIMIT, SWIGLU_LIMIT)
        act = (up + 1.0) * (gate * jax.nn.sigmoid(SWIGLU_ALPHA * gate))
        y = jnp.dot(act.astype(BF16), w2b_ref[...], preferred_element_type=F32) + b2_ref[0, 0]
        _pack_rows(y, ys_ref, r)


def _expert_ffn(blk_e, xs, layer, w1, b1, w2, b2):
    n_slots = xs.shape[0] // ROW_CHUNKS
    bm = EXPERT_TILE
    return pl.pallas_call(
        _expert_kernel,
        out_shape=jax.ShapeDtypeStruct(xs.shape, I32),
        grid_spec=pltpu.PrefetchScalarGridSpec(
            num_scalar_prefetch=1,
            grid=(n_slots // bm,),
            in_specs=[
                pl.BlockSpec((bm * ROW_CHUNKS, LANES), lambda i, be: (i, 0)),
                pl.BlockSpec((1, 1, D_MODEL, 2 * D_FF), lambda i, be: (layer, be[i], 0, 0)),
                pl.BlockSpec((1, 1, 1, 2 * D_FF), lambda i, be: (layer, be[i], 0, 0)),
                pl.BlockSpec((1, 1, D_FF, D_MODEL), lambda i, be: (layer, be[i], 0, 0)),
                pl.BlockSpec((1, 1, 1, D_MODEL), lambda i, be: (layer, be[i], 0, 0)),
            ],
            out_specs=pl.BlockSpec((bm * ROW_CHUNKS, LANES), lambda i, be: (i, 0)),
            scratch_shapes=[pltpu.VMEM((D_MODEL, 2 * D_FF), BF16), pltpu.VMEM((D_FF, D_MODEL), BF16)],
        ),
        compiler_params=pltpu.CompilerParams(
            dimension_semantics=("arbitrary",), vmem_limit_bytes=VMEM_LIMIT),
        name="expert_ffn",
    )(blk_e, xs, w1, b1, w2, b2)


def _combine_kernel(yg_ref, wtm_ref, x1_ref, mod_ref, fn_ref, *o_refs, final, n0):
    tm = x1_ref.shape[0]
    wtm = wtm_ref[...]
    moe = jnp.zeros((tm, D_MODEL), F32)
    for kk in range(TOP_K):
        moe = moe + wtm[:, kk:kk + 1] * _unpack_rows(yg_ref.at[kk], 0, tm)
    g2 = mod_ref[0][:, 5 * D_MODEL:6 * D_MODEL]
    out = x1_ref[...] + g2 * moe
    if final:
        out = _rms(out) * fn_ref[...]
    if len(o_refs) == 1:
        o_refs[0][...] = out
        return

    @pl.when(pl.program_id(0) < n0)
    def _():
        o_refs[0][...] = out

    @pl.when(pl.program_id(0) >= n0)
    def _():
        o_refs[1][...] = out


def _combine(yg, wtm, x1, mod, final_norm, seq_len, final, n_first):
    n_tok = x1.shape[0]
    tm = TOKEN_TILE
    per_seq = seq_len // tm
    n0 = n_first // tm
    if final and n_first < n_tok:
        out_shape = (jax.ShapeDtypeStruct((n_first, D_MODEL), F32),
                     jax.ShapeDtypeStruct((n_tok - n_first, D_MODEL), F32))
        out_specs = (pl.BlockSpec((tm, D_MODEL), lambda i: (jnp.minimum(i, n0 - 1), 0)),
                     pl.BlockSpec((tm, D_MODEL), lambda i: (jnp.maximum(i - n0, 0), 0)))
    else:
        out_shape = jax.ShapeDtypeStruct((n_tok, D_MODEL), F32)
        out_specs = pl.BlockSpec((tm, D_MODEL), lambda i: (i, 0))
    return pl.pallas_call(
        functools.partial(_combine_kernel, final=final, n0=n0),
        out_shape=out_shape,
        grid=(n_tok // tm,),
        in_specs=[
            pl.BlockSpec((TOP_K, tm * ROW_CHUNKS, LANES), lambda i: (0, i, 0)),
            pl.BlockSpec((tm, LANES), lambda i: (i, 0)),
            pl.BlockSpec((tm, D_MODEL), lambda i: (i, 0)),
            pl.BlockSpec((1, 1, 6 * D_MODEL), lambda i: (i // per_seq, 0, 0)),
            pl.BlockSpec((1, D_MODEL), lambda i: (0, 0)),
        ],
        out_specs=out_specs,
        compiler_params=pltpu.CompilerParams(
            dimension_semantics=("arbitrary",), vmem_limit_bytes=VMEM_LIMIT),
        name="moe_combine",
    )(yg, wtm, x1, mod, final_norm)


def _moe_layers(routes, mods, layer, ffn_w, final_norm, seq_len, final):
    bm = EXPERT_TILE
    plans = []
    for x1, hp, topi, rank, wtm, counts in routes:
        n_slots = x1.shape[0] * TOP_K + N_EXPERTS * bm
        n_blk = n_slots // bm
        cnt = counts[:, 0].astype(I32)
        padded = (cnt + bm - 1) // bm * bm
        pend = jnp.cumsum(padded)
        pstart = (pend - padded).astype(I32)
        blk_start = jnp.arange(n_blk, dtype=I32) * bm
        blk_e = jnp.minimum(jnp.sum(blk_start[:, None] >= pend[None, :], axis=1), N_EXPERTS - 1)
        dest = _slots(pstart, topi, rank).reshape(-1, SC_ROWS)
        plans.append((n_slots, blk_e.astype(I32), dest, topi.shape[2]))
    xs = [_sc_dispatch(r[1], dest, n_slots, tm)
          for r, (n_slots, _, dest, tm) in zip(routes, plans)]
    ys = [_expert_ffn(blk_e, x, layer, *ffn_w) for x, (_, blk_e, _, _) in zip(xs, plans)]
    yg = [_sc_gather(y, dest, tm) for y, (_, _, dest, tm) in zip(ys, plans)]
    return [_combine(g, r[4], r[0], mod, final_norm, seq_len, final, r[0].shape[0])
            for g, r, mod in zip(yg, routes, mods)]


def _rope_tables(seq_len):
    inv_freq = 1.0 / (ROPE_THETA ** (jnp.arange(0, D_ROPE, 2, dtype=F32) / D_ROPE))
    ang = jnp.arange(seq_len, dtype=F32)[:, None] * inv_freq[None, :]
    cos, sin = jnp.cos(ang), jnp.sin(ang)
    ones = jnp.ones((seq_len, ROPE_LO), F32)
    zeros_lo = jnp.zeros((seq_len, ROPE_LO), F32)
    zeros_hi = jnp.zeros((seq_len, HEAD_PAD - ROPE_LO - D_ROPE), F32)
    zeros_h = jnp.zeros((seq_len, ROPE_HALF), F32)
    rope_c = jnp.concatenate([ones, cos, cos, zeros_hi], axis=1)
    rope_s1 = jnp.concatenate([zeros_lo, -sin, zeros_h, zeros_hi], axis=1)
    rope_s2 = jnp.concatenate([zeros_lo, zeros_h, sin, zeros_hi], axis=1)
    return cos.T, sin.T, rope_c, rope_s1, rope_s2


def _mla_weights(w_in, q_norm, kv_norm, w_uq, w_ukv, w_o, seq_len):
    d_qk = D_NOPE + D_ROPE
    q_scale = d_qk ** -0.5 * math.log2(math.e)
    pad_pe = jnp.zeros((D_MODEL, HEAD_PAD), F32).at[:, ROPE_LO:ROPE_LO + D_ROPE].set(
        w_in[:, Q_RANK + KV_RANK:])
    w_in_p = jnp.concatenate([w_in[:, :Q_RANK + KV_RANK], pad_pe], axis=1)
    w_uq_p = jnp.pad(w_uq.reshape(Q_RANK, N_HEADS, d_qk) * q_scale,
                     ((0, 0), (0, 0), (0, HEAD_PAD - d_qk)))
    w_kv = w_ukv.reshape(KV_RANK, N_HEADS, D_NOPE + D_V)
    w_uk_p = jnp.pad(w_kv[:, :, :D_NOPE], ((0, 0), (0, 0), (0, HEAD_PAD - D_NOPE)))
    cos_t, sin_t, rope_c, rope_s1, rope_s2 = _rope_tables(seq_len)
    return {
        "w_in": w_in_p.astype(BF16),
        "q_norm": q_norm.reshape(1, Q_RANK),
        "kv_norm": kv_norm.reshape(1, KV_RANK),
        "w_uq_t": w_uq_p.reshape(Q_RANK, N_HEADS * HEAD_PAD).T.astype(BF16),
        "w_uk": w_uk_p.reshape(KV_RANK, N_HEADS * HEAD_PAD).astype(BF16),
        "w_uv_t": w_kv[:, :, D_NOPE:].reshape(KV_RANK, N_HEADS * D_V).T.astype(BF16),
        "w_o": w_o.astype(BF16),
        "cos_t": cos_t, "sin_t": sin_t, "rope_c": rope_c, "rope_s1": rope_s1, "rope_s2": rope_s2,
    }


def _router_weights(router_w, router_b):
    return router_w.T.astype(BF16), router_b.reshape(N_EXPERTS, 1)


def kernel(x_prompt, x_sample, c_prompt, c_sample, ada_w, ada_b, mla_w_in, mla_q_norm,
           mla_kv_norm, mla_w_uq, mla_w_ukv, mla_w_o, pool_w, pool_scale, router_w, router_b,
           moe_w1, moe_b1, moe_w2, moe_b2, final_norm):
    n_prompt, seq_len, _ = x_prompt.shape
    assert x_sample.shape[1] == seq_len and seq_len % TOKEN_TILE == 0
    depth = ada_w.shape[0]
    n_sample = x_sample.shape[0]
    xs = [x_prompt.reshape(-1, D_MODEL), x_sample.reshape(-1, D_MODEL)]
    n_seqs = [n_prompt, n_sample]
    mods = _ada_mod(jnp.concatenate([c_prompt, c_sample], axis=0), ada_w, ada_b)
    fnorm = final_norm.reshape(1, D_MODEL)
    ffn_w = (moe_w1, moe_b1.reshape(depth, N_EXPERTS, 1, 2 * D_FF),
             moe_w2, moe_b2.reshape(depth, N_EXPERTS, 1, D_MODEL))

    for i in range(depth):
        rw_t, rb = _router_weights(router_w[i], router_b[i])
        j = i // 2
        group_mods = [mods[i, :n_prompt].reshape(n_prompt, 1, 6 * D_MODEL),
                      mods[i, n_prompt:].reshape(n_sample, 1, 6 * D_MODEL)]
        if i % 2 == 0:
            w = _mla_weights(mla_w_in[j], mla_q_norm[j], mla_kv_norm[j], mla_w_uq[j],
                             mla_w_ukv[j], mla_w_o[j], seq_len)
        routes = []
        for x, mod in zip(xs, group_mods):
            if i % 2 == 0:
                qt, k, vt = _mla_pre([x], mod, w, seq_len)
                ot = _attention(qt, k, vt, seq_len)
                routes.append(_post_mix(ot, [x], mod, w["w_o"], rw_t, rb, seq_len))
            else:
                routes.append(_pool_mix(x, mod, pool_w[j].astype(BF16),
                                        pool_scale[j].reshape(1, D_MODEL), rw_t, rb, seq_len))
        xs = _moe_layers(routes, group_mods, i, ffn_w, fnorm, seq_len, i == depth - 1)

    return (xs[0].reshape(n_prompt, seq_len, D_MODEL), xs[1].reshape(n_sample, seq_len, D_MODEL))
```

```python
import functools
import math

import jax
import jax.numpy as jnp
from jax import lax
from jax.experimental import pallas as pl
from jax.experimental.pallas import tpu as pltpu
from jax.experimental.pallas import tpu_sc as plsc

F32 = jnp.float32
BF16 = jnp.bfloat16
I32 = jnp.int32

D_MODEL = 1024
N_HEADS = 16
Q_RANK = 384
KV_RANK = 256
D_NOPE = 64
D_ROPE = 32
D_V = 64
V_ROWS = D_V + 16
ROPE_THETA = 10000.0
POOL_WINDOWS = (2, 4, 8, 16)
POOL_GROUP = D_MODEL // len(POOL_WINDOWS)
N_EXPERTS = 32
TOP_K = 4
D_FF = D_MODEL
SWIGLU_LIMIT = 7.0
SWIGLU_ALPHA = 1.702
EPS = 1e-6

LANES = 128
SUBLANES = 8
HEAD_PAD = 128
ROPE_LO = D_NOPE
ROPE_HALF = D_ROPE // 2
VMEM_LIMIT = 56 * 1024 * 1024

TOKEN_TILE = 256
EXPERT_TILE = 512
EXPERT_SUB = 256
ATTN_KEY_CHUNKS = 4
ATTN_HEADS_PER_STEP = 4
POOL_HALO = 8
PACKED = D_MODEL // 2
ROW_CHUNKS = PACKED // LANES
ROW_SHAPE = (ROW_CHUNKS, LANES)

SC_CORES = 2
SC_SUBCORES = 16
SC_WORKERS = SC_CORES * SC_SUBCORES
SC_ROWS = 64


def _rms(x):
    return x * lax.rsqrt(jnp.mean(x * x, axis=-1, keepdims=True) + EPS)


def _pack_rows(y, out_ref, row0=0):
    n = y.shape[0]
    lo = lax.bitcast_convert_type(y[:, :PACKED].astype(BF16).astype(F32), I32)
    hi = lax.bitcast_convert_type(y[:, PACKED:].astype(BF16).astype(F32), I32)
    words = lax.shift_right_logical(lo, 16) | (hi & jnp.int32(-65536))
    for j in range(ROW_CHUNKS):
        out_ref[pl.ds(row0 * ROW_CHUNKS + j, n, stride=ROW_CHUNKS), :] = (
            words[:, j * LANES:(j + 1) * LANES])


def _unpack_rows(in_ref, row0, n):
    words = jnp.concatenate(
        [in_ref[pl.ds(row0 * ROW_CHUNKS + j, n, stride=ROW_CHUNKS), :] for j in range(ROW_CHUNKS)],
        axis=1)
    lo = lax.bitcast_convert_type(lax.shift_left(words, 16), F32)
    hi = lax.bitcast_convert_type(words & jnp.int32(-65536), F32)
    return jnp.concatenate([lo, hi], axis=1)


def _part_specs(parts, tm):
    if len(parts) == 1:
        return [pl.BlockSpec((tm, D_MODEL), lambda i: (i, 0))]
    n0 = parts[0].shape[0] // tm
    return [pl.BlockSpec((tm, D_MODEL), lambda i: (jnp.minimum(i, n0 - 1), 0)),
            pl.BlockSpec((tm, D_MODEL), lambda i: (jnp.maximum(i - n0, 0), 0))]


def _part_load(refs, n0):
    if len(refs) == 1:
        return refs[0][...]
    return jnp.where(pl.program_id(0) < n0, refs[0][...], refs[1][...])


def _ada_kernel(c_ref, w_ref, b_ref, o_ref):
    c = c_ref[...]
    act = (c * jax.nn.sigmoid(c)).astype(BF16)
    o_ref[0] = jnp.dot(act, w_ref[0].astype(BF16), preferred_element_type=F32) + b_ref[0]


def _ada_mod(c, ada_w, ada_b):
    depth, _, n_out = ada_w.shape
    n_seq = c.shape[0]
    tn = 1536
    return pl.pallas_call(
        _ada_kernel,
        out_shape=jax.ShapeDtypeStruct((depth, n_seq, n_out), F32),
        grid=(depth, n_out // tn),
        in_specs=[
            pl.BlockSpec((n_seq, D_MODEL), lambda l, j: (0, 0)),
            pl.BlockSpec((1, D_MODEL, tn), lambda l, j: (l, 0, j)),
            pl.BlockSpec((1, 1, tn), lambda l, j: (l, 0, j)),
        ],
        out_specs=pl.BlockSpec((1, n_seq, tn), lambda l, j: (l, 0, j)),
        compiler_params=pltpu.CompilerParams(
            dimension_semantics=("arbitrary", "arbitrary"), vmem_limit_bytes=VMEM_LIMIT),
        name="ada_mod",
    )(c, ada_w, ada_b.reshape(depth, 1, n_out))


def _mla_pre_kernel(*refs, n_x, n0):
    x_refs = refs[:n_x]
    (mod_ref, win_ref, qn_ref, kvn_ref, wuqt_ref, wuk_ref, wuvt_ref,
     cost_ref, sint_ref, ck_ref, s1k_ref, s2k_ref, qt_ref, k_ref, vt_ref) = refs[n_x:]
    tm = x_refs[0].shape[0]
    mod = mod_ref[0]
    sh1 = mod[:, 0:D_MODEL]
    sc1 = mod[:, D_MODEL:2 * D_MODEL]
    h = (_rms(_part_load(x_refs, n0)) * (1.0 + sc1) + sh1).astype(BF16)
    a = jnp.dot(h, win_ref[...], preferred_element_type=F32)
    cq = (_rms(a[:, :Q_RANK]) * qn_ref[...]).astype(BF16)
    ckv = (_rms(a[:, Q_RANK:Q_RANK + KV_RANK]) * kvn_ref[...]).astype(BF16)

    kpe = a[:, Q_RANK + KV_RANK:]
    kpe = (kpe * ck_ref[...]
           + pltpu.roll(kpe, LANES - ROPE_HALF, 1) * s1k_ref[...]
           + pltpu.roll(kpe, ROPE_HALF, 1) * s2k_ref[...])

    qt = lax.dot_general(wuqt_ref[...], cq, (((1,), (1,)), ((), ())),
                         preferred_element_type=F32)
    q3 = qt.reshape(N_HEADS, HEAD_PAD, tm)
    x1 = q3[:, ROPE_LO:ROPE_LO + ROPE_HALF, :]
    x2 = q3[:, ROPE_LO + ROPE_HALF:ROPE_LO + D_ROPE, :]
    cos = cost_ref[...][None]
    sin = sint_ref[...][None]
    q3 = jnp.concatenate(
        [q3[:, :ROPE_LO, :], x1 * cos - x2 * sin, x2 * cos + x1 * sin,
         q3[:, ROPE_LO + D_ROPE:, :]], axis=1)
    qt_ref[:, 0] = q3.astype(BF16)

    kn = jnp.dot(ckv, wuk_ref[...], preferred_element_type=F32)
    for hd in range(N_HEADS):
        k_ref[hd] = (kn[:, hd * HEAD_PAD:(hd + 1) * HEAD_PAD] + kpe).astype(BF16)

    vt = lax.dot_general(wuvt_ref[...], ckv, (((1,), (1,)), ((), ())),
                         preferred_element_type=F32)
    vt_ref[:, 0, :D_V, :] = vt.reshape(N_HEADS, D_V, tm).astype(BF16)
    vt_ref[:, 0, D_V:, :] = jnp.ones((N_HEADS, V_ROWS - D_V, tm), BF16)


def _mla_pre(x_parts, mod, w, seq_len):
    n_tok = sum(p.shape[0] for p in x_parts)
    tm = TOKEN_TILE
    n_tiles = n_tok // tm
    per_seq = seq_len // tm
    const = lambda i: (0, 0)
    return pl.pallas_call(
        functools.partial(_mla_pre_kernel, n_x=len(x_parts), n0=x_parts[0].shape[0] // tm),
        out_shape=(
            jax.ShapeDtypeStruct((N_HEADS, n_tiles, HEAD_PAD, tm), BF16),
            jax.ShapeDtypeStruct((N_HEADS, n_tok, HEAD_PAD), BF16),
            jax.ShapeDtypeStruct((N_HEADS, n_tok // seq_len, V_ROWS, seq_len), BF16),
        ),
        grid=(n_tiles,),
        in_specs=_part_specs(x_parts, tm) + [
            pl.BlockSpec((1, 1, 6 * D_MODEL), lambda i: (i // per_seq, 0, 0)),
            pl.BlockSpec(w["w_in"].shape, const),
            pl.BlockSpec(w["q_norm"].shape, const),
            pl.BlockSpec(w["kv_norm"].shape, const),
            pl.BlockSpec(w["w_uq_t"].shape, const),
            pl.BlockSpec(w["w_uk"].shape, const),
            pl.BlockSpec(w["w_uv_t"].shape, const),
            pl.BlockSpec((ROPE_HALF, tm), lambda i: (0, i % per_seq)),
            pl.BlockSpec((ROPE_HALF, tm), lambda i: (0, i % per_seq)),
            pl.BlockSpec((tm, LANES), lambda i: (i % per_seq, 0)),
            pl.BlockSpec((tm, LANES), lambda i: (i % per_seq, 0)),
            pl.BlockSpec((tm, LANES), lambda i: (i % per_seq, 0)),
        ],
        out_specs=(
            pl.BlockSpec((N_HEADS, 1, HEAD_PAD, tm), lambda i: (0, i, 0, 0)),
            pl.BlockSpec((N_HEADS, tm, HEAD_PAD), lambda i: (0, i, 0)),
            pl.BlockSpec((N_HEADS, 1, V_ROWS, tm), lambda i: (0, i // per_seq, 0, i % per_seq)),
        ),
        compiler_params=pltpu.CompilerParams(
            dimension_semantics=("arbitrary",), vmem_limit_bytes=VMEM_LIMIT),
        name="mla_pre",
    )(*x_parts, mod, w["w_in"], w["q_norm"], w["kv_norm"], w["w_uq_t"], w["w_uk"], w["w_uv_t"],
      w["cos_t"], w["sin_t"], w["rope_c"], w["rope_s1"], w["rope_s2"])


def _attention_kernel(qt_ref, k_ref, vt_ref, ot_ref, s0_ref, s1_ref):
    n_heads, n_q = qt_ref.shape[:2]
    n_tiles = n_heads * n_q
    n_keys = k_ref.shape[1]
    kc = n_keys // ATTN_KEY_CHUNKS

    def stage(t_next, s_next_ref, t_cur, s_cur_ref, m_cur):
        m_next, o = None, None
        if t_next is not None:
            h_next, j_next = t_next // n_q, t_next % n_q
        if t_cur is not None:
            h_cur, j_cur = t_cur // n_q, t_cur % n_q
        for c in range(ATTN_KEY_CHUNKS):
            rows = slice(c * kc, (c + 1) * kc)
            if t_next is not None:
                s = jnp.dot(k_ref[h_next, rows, :], qt_ref[h_next, j_next],
                            preferred_element_type=F32)
                s_next_ref[rows, :] = s
                cm = jnp.max(s, axis=0, keepdims=True)
                m_next = cm if m_next is None else jnp.maximum(m_next, cm)
            if t_cur is not None:
                p = jnp.exp2(s_cur_ref[rows, :] - m_cur).astype(BF16)
                part = jnp.dot(vt_ref[h_cur, 0, :, rows], p,
                               preferred_element_type=F32)
                o = part if o is None else o + part
        if t_cur is not None:
            denom = o[D_V:D_V + 1, :]
            ot_ref[h_cur, j_cur] = (o[:D_V, :] * (1.0 / denom)).astype(BF16)
        return m_next

    def body(i, m0):
        t = 2 * i
        m1 = stage(t + 1, s1_ref, t, s0_ref, m0)
        return stage(t + 2, s0_ref, t + 1, s1_ref, m1)

    m0 = lax.fori_loop(0, n_tiles // 2 - 1, body, stage(0, s0_ref, None, None, None))
    m1 = stage(n_tiles - 1, s1_ref, n_tiles - 2, s0_ref, m0)
    stage(None, None, n_tiles - 1, s1_ref, m1)


def _attention(qt, k, vt, seq_len):
    n_heads, n_tiles, _, tq = qt.shape
    per_seq = seq_len // tq
    n_seq = n_tiles // per_seq
    hb = ATTN_HEADS_PER_STEP
    return pl.pallas_call(
        _attention_kernel,
        out_shape=jax.ShapeDtypeStruct((n_heads, n_tiles, D_V, tq), BF16),
        grid=(n_seq, n_heads // hb),
        in_specs=[
            pl.BlockSpec((hb, per_seq, HEAD_PAD, tq), lambda b, h: (h, b, 0, 0)),
            pl.BlockSpec((hb, seq_len, HEAD_PAD), lambda b, h: (h, b, 0)),
            pl.BlockSpec((hb, 1, V_ROWS, seq_len), lambda b, h: (h, b, 0, 0)),
        ],
        out_specs=pl.BlockSpec((hb, per_seq, D_V, tq), lambda b, h: (h, b, 0, 0)),
        scratch_shapes=[pltpu.VMEM((seq_len, tq), F32), pltpu.VMEM((seq_len, tq), F32)],
        compiler_params=pltpu.CompilerParams(
            dimension_semantics=("arbitrary", "arbitrary"), vmem_limit_bytes=VMEM_LIMIT),
        name="attention",
    )(qt, k, vt)


def _route_tail(x1, mod, rwt_ref, rb_ref, x1_ref, hp_ref, topi_ref, rank_ref, wtm_ref, cnt_ref):
    tm = x1.shape[0]
    sh2 = mod[:, 3 * D_MODEL:4 * D_MODEL]
    sc2 = mod[:, 4 * D_MODEL:5 * D_MODEL]
    x1_ref[...] = x1
    h2 = _rms(x1) * (1.0 + sc2) + sh2
    _pack_rows(h2, hp_ref)
    logits = lax.dot_general(rwt_ref[...], h2.astype(BF16), (((1,), (1,)), ((), ())),
                             preferred_element_type=F32) + rb_ref[...]
    e_iota = lax.broadcasted_iota(I32, (N_EXPERTS, tm), 0)
    vals, idxs = [], []
    work = logits
    for _ in range(TOP_K):
        m = jnp.max(work, axis=0, keepdims=True)
        idx = jnp.min(jnp.where(work == m, e_iota, N_EXPERTS), axis=0, keepdims=True)
        vals.append(m)
        idxs.append(idx)
        work = jnp.where(e_iota == idx, -jnp.inf, work)
    ex = [jnp.exp(v - vals[0]) for v in vals]
    inv = 1.0 / (ex[0] + ex[1] + ex[2] + ex[3])
    topw = jnp.concatenate([e * inv for e in ex], axis=0)
    topi_ref[0] = jnp.concatenate(idxs, axis=0)

    @pl.when(pl.program_id(0) == 0)
    def _():
        cnt_ref[...] = jnp.zeros_like(cnt_ref)

    row = lax.broadcasted_iota(I32, (tm, tm), 0)
    col = lax.broadcasted_iota(I32, (tm, tm), 1)
    earlier = (row < col).astype(BF16)
    running = cnt_ref[...][:, 0:1]
    ranks = []
    for kk in range(TOP_K):
        onehot = (e_iota == idxs[kk]).astype(F32)
        before = jnp.dot(onehot.astype(BF16), earlier, preferred_element_type=F32)
        ranks.append(jnp.sum(onehot * (running + before), axis=0, keepdims=True))
        running = running + jnp.sum(onehot, axis=1, keepdims=True)
    rank_ref[0] = jnp.concatenate(ranks, axis=0).astype(I32)
    cnt_ref[...] = jnp.broadcast_to(running, cnt_ref.shape)

    wpad = jnp.concatenate([topw, jnp.zeros((LANES - TOP_K, tm), F32)], axis=0)
    wtm_ref[...] = wpad.T


def _route_out_shapes(n_tok, tm):
    n_tiles = n_tok // tm
    return (
        jax.ShapeDtypeStruct((n_tok, D_MODEL), F32),
        jax.ShapeDtypeStruct((n_tok * ROW_CHUNKS, LANES), I32),
        jax.ShapeDtypeStruct((n_tiles, TOP_K, tm), I32),
        jax.ShapeDtypeStruct((n_tiles, TOP_K, tm), I32),
        jax.ShapeDtypeStruct((n_tok, LANES), F32),
        jax.ShapeDtypeStruct((N_EXPERTS, LANES), F32),
    )


def _route_out_specs(tm):
    return (
        pl.BlockSpec((tm, D_MODEL), lambda i: (i, 0)),
        pl.BlockSpec((tm * ROW_CHUNKS, LANES), lambda i: (i, 0)),
        pl.BlockSpec((1, TOP_K, tm), lambda i: (i, 0, 0)),
        pl.BlockSpec((1, TOP_K, tm), lambda i: (i, 0, 0)),
        pl.BlockSpec((tm, LANES), lambda i: (i, 0)),
        pl.BlockSpec((N_EXPERTS, LANES), lambda i: (0, 0)),
    )


def _post_mix_kernel(ot_ref, *refs, n_x, n0):
    x_refs = refs[:n_x]
    mod_ref, wo_ref, rwt_ref, rb_ref = refs[n_x:n_x + 4]
    out_refs = refs[n_x + 4:]
    tm = x_refs[0].shape[0]
    mod = mod_ref[0]
    g1 = mod[:, 2 * D_MODEL:3 * D_MODEL]
    ot = ot_ref[:, 0].reshape(N_HEADS * D_V, tm)
    mix = lax.dot_general(ot, wo_ref[...], (((0,), (0,)), ((), ())),
                          preferred_element_type=F32)
    _route_tail(_part_load(x_refs, n0) + g1 * mix, mod, rwt_ref, rb_ref, *out_refs)


def _post_mix(ot, x_parts, mod, w_o, rw_t, rb, seq_len):
    n_tok = sum(p.shape[0] for p in x_parts)
    tm = ot.shape[3]
    per_seq = seq_len // tm
    const = lambda i: (0, 0)
    return pl.pallas_call(
        functools.partial(_post_mix_kernel, n_x=len(x_parts), n0=x_parts[0].shape[0] // tm),
        out_shape=_route_out_shapes(n_tok, tm),
        grid=(n_tok // tm,),
        in_specs=[pl.BlockSpec((N_HEADS, 1, D_V, tm), lambda i: (0, i, 0, 0))]
        + _part_specs(x_parts, tm) + [
            pl.BlockSpec((1, 1, 6 * D_MODEL), lambda i: (i // per_seq, 0, 0)),
            pl.BlockSpec(w_o.shape, const),
            pl.BlockSpec(rw_t.shape, const),
            pl.BlockSpec(rb.shape, const),
        ],
        out_specs=_route_out_specs(tm),
        compiler_params=pltpu.CompilerParams(
            dimension_semantics=("arbitrary",), vmem_limit_bytes=VMEM_LIMIT),
        name="post_mix",
    )(ot, *x_parts, mod, w_o, rw_t, rb)


def _pool_mix_kernel(x_ref, xp_ref, xn_ref, mod_ref, pw_ref, ps_ref, rwt_ref, rb_ref,
                     *out_refs, seq_len):
    tm = x_ref.shape[0]
    per_seq = seq_len // tm
    mod = mod_ref[0]
    sh1 = mod[:, 0:D_MODEL]
    sc1 = mod[:, D_MODEL:2 * D_MODEL]
    g1 = mod[:, 2 * D_MODEL:3 * D_MODEL]
    pos0 = (pl.program_id(0) % per_seq) * tm

    def normed(v):
        return _rms(v) * (1.0 + sc1) + sh1

    x = x_ref[...]
    h = normed(x)
    h_prev = jnp.where(pos0 > 0, normed(xp_ref[...]), 0.0)
    h_next = jnp.where(pos0 + tm < seq_len, normed(xn_ref[...]), 0.0)
    hext = jnp.concatenate([h_prev, h, h_next], axis=0)
    n_ext = tm + 2 * POOL_HALO
    pos = pos0 + lax.broadcasted_iota(I32, (tm, 1), 0)

    outs = []
    for g, win in enumerate(POOL_WINDOWS):
        left = win // 2
        right = win - 1 - left
        cols = slice(g * POOL_GROUP, (g + 1) * POOL_GROUP)
        s = hext[:, cols]
        span = 1
        while span < win:
            s = s + pltpu.roll(s, span, 0)
            span *= 2
        if right:
            s = pltpu.roll(s, n_ext - right, 0)
        num = s[POOL_HALO:POOL_HALO + tm, :]
        count = (jnp.minimum(pos + right + 1, seq_len) - jnp.maximum(pos - left, 0)).astype(F32)
        diff = (num / count - h[:, cols]).astype(BF16)
        outs.append(jnp.dot(diff, pw_ref[g], preferred_element_type=F32))
    mix = jnp.concatenate(outs, axis=1) * ps_ref[...]
    _route_tail(x + g1 * mix, mod, rwt_ref, rb_ref, *out_refs)


def _pool_mix(x, mod, pool_w, pool_scale, rw_t, rb, seq_len):
    n_tok = x.shape[0]
    tm = TOKEN_TILE
    per_seq = seq_len // tm
    halo_per_tile = tm // POOL_HALO
    n_halo = n_tok // POOL_HALO
    const = lambda i: (0, 0)
    return pl.pallas_call(
        functools.partial(_pool_mix_kernel, seq_len=seq_len),
        out_shape=_route_out_shapes(n_tok, tm),
        grid=(n_tok // tm,),
        in_specs=[
            pl.BlockSpec((tm, D_MODEL), lambda i: (i, 0)),
            pl.BlockSpec((POOL_HALO, D_MODEL),
                         lambda i: (jnp.maximum(i * halo_per_tile - 1, 0), 0)),
            pl.BlockSpec((POOL_HALO, D_MODEL),
                         lambda i: (jnp.minimum((i + 1) * halo_per_tile, n_halo - 1), 0)),
            pl.BlockSpec((1, 1, 6 * D_MODEL), lambda i: (i // per_seq, 0, 0)),
            pl.BlockSpec(pool_w.shape, lambda i: (0, 0, 0)),
            pl.BlockSpec(pool_scale.shape, const),
            pl.BlockSpec(rw_t.shape, const),
            pl.BlockSpec(rb.shape, const),
        ],
        out_specs=_route_out_specs(tm),
        compiler_params=pltpu.CompilerParams(
            dimension_semantics=("arbitrary",), vmem_limit_bytes=VMEM_LIMIT),
        name="pool_mix",
    )(x, x, x, mod, pool_w, pool_scale, rw_t, rb)


def _slots_kernel(pstart_ref, topi_ref, rank_ref, dest_ref):
    topi = topi_ref[...]
    start = jnp.zeros_like(topi)
    for e in range(N_EXPERTS):
        start = jnp.where(topi == e, pstart_ref[e], start)
    dest_ref[...] = start + rank_ref[...]


def _slots(pstart, topi, rank):
    n_tiles, _, tm = topi.shape
    tb = math.gcd(n_tiles, 32)
    spec = pl.BlockSpec((tb, TOP_K, tm), lambda i, ps: (i, 0, 0))
    return pl.pallas_call(
        _slots_kernel,
        out_shape=jax.ShapeDtypeStruct(topi.shape, I32),
        grid_spec=pltpu.PrefetchScalarGridSpec(
            num_scalar_prefetch=1, grid=(n_tiles // tb,), in_specs=[spec, spec], out_specs=spec),
        compiler_params=pltpu.CompilerParams(dimension_semantics=("arbitrary",)),
        name="moe_slots",
    )(pstart, topi, rank)


def _sc_chunk_rows(c, tm):
    per_tile = tm // SC_ROWS
    tile = c // per_tile
    part = c % per_tile
    return [(tile * TOP_K + kk) * per_tile + part for kk in range(TOP_K)]


def _sc_dispatch(hp, dest, n_slots, tm):
    hp = hp.reshape((-1,) + ROW_SHAPE)
    n_tok = hp.shape[0]
    rows_per_w = dest.shape[0] // SC_WORKERS
    chunks_per_w = n_tok // SC_WORKERS // SC_ROWS
    mesh = plsc.VectorSubcoreMesh(core_axis_name="c", subcore_axis_name="s")

    @functools.partial(
        pl.kernel, mesh=mesh,
        out_type=jax.ShapeDtypeStruct((n_slots,) + ROW_SHAPE, I32),
        scratch_types=[
            pltpu.VMEM((rows_per_w, SC_ROWS), I32),
            pltpu.VMEM((SC_ROWS,) + ROW_SHAPE, I32),
            pltpu.SemaphoreType.DMA,
        ],
        name="sc_dispatch",
    )
    def run(hp_hbm, dest_hbm, xs_hbm, dest_v, rows_v, sem):
        wid = lax.axis_index("s") * SC_CORES + lax.axis_index("c")
        pltpu.sync_copy(dest_hbm.at[pl.ds(wid * rows_per_w, rows_per_w)], dest_v)

        @pl.loop(0, chunks_per_w)
        def _(c):
            tok0 = (wid * chunks_per_w + c) * SC_ROWS
            pltpu.sync_copy(hp_hbm.at[pl.ds(tok0, SC_ROWS)], rows_v)
            copies = [pltpu.async_copy(rows_v, xs_hbm.at[dest_v.at[row]], sem)
                      for row in _sc_chunk_rows(c, tm)]
            for cp in copies:
                cp.wait()

    return run(hp, dest).reshape(n_slots * ROW_CHUNKS, LANES)


def _sc_gather(ys, dest, tm):
    ys = ys.reshape((-1,) + ROW_SHAPE)
    n_tok = dest.shape[0] * SC_ROWS // TOP_K
    rows_per_w = dest.shape[0] // SC_WORKERS
    chunks_per_w = n_tok // SC_WORKERS // SC_ROWS
    mesh = plsc.VectorSubcoreMesh(core_axis_name="c", subcore_axis_name="s")

    @functools.partial(
        pl.kernel, mesh=mesh,
        out_type=jax.ShapeDtypeStruct((TOP_K, n_tok) + ROW_SHAPE, I32),
        scratch_types=[
            pltpu.VMEM((rows_per_w, SC_ROWS), I32),
            pltpu.VMEM((SC_ROWS,) + ROW_SHAPE, I32),
            pltpu.SemaphoreType.DMA,
        ],
        name="sc_gather",
    )
    def run(ys_hbm, dest_hbm, yg_hbm, dest_v, rows_v, sem):
        wid = lax.axis_index("s") * SC_CORES + lax.axis_index("c")
        pltpu.sync_copy(dest_hbm.at[pl.ds(wid * rows_per_w, rows_per_w)], dest_v)

        @pl.loop(0, chunks_per_w)
        def _(c):
            tok0 = (wid * chunks_per_w + c) * SC_ROWS
            for kk, row in enumerate(_sc_chunk_rows(c, tm)):
                pltpu.async_copy(ys_hbm.at[dest_v.at[row]], rows_v, sem).wait()
                pltpu.sync_copy(rows_v, yg_hbm.at[kk, pl.ds(tok0, SC_ROWS)])

    return run(ys, dest).reshape(TOP_K, n_tok * ROW_CHUNKS, LANES)


def _expert_kernel(e_ref, first_ref, slot_ref, next_ref, nused_ref,
                   xs_ref, w1_hbm, b1_ref, w2_hbm, b2_ref, ys_ref,
                   w1f_ref, w2f_ref, w1b_ref, w2b_ref, sem, *, layer):
    i = pl.program_id(0)

    def weight_copies(expert, slot):
        return (pltpu.make_async_copy(w1_hbm.at[layer, expert], w1f_ref.at[slot], sem.at[slot, 0]),
                pltpu.make_async_copy(w2_hbm.at[layer, expert], w2f_ref.at[slot], sem.at[slot, 1]))

    @pl.when(first_ref[i] == 1)
    def _():
        slot = slot_ref[i]

        @pl.when(i == 0)
        def _():
            for cp in weight_copies(e_ref[i], slot):
                cp.start()

        for cp in weight_copies(e_ref[i], slot):
            cp.wait()
        w1b_ref[...] = w1f_ref[slot].astype(BF16)
        w2b_ref[...] = w2f_ref[slot].astype(BF16)

        @pl.when(next_ref[i] >= 0)
        def _():
            for cp in weight_copies(next_ref[i], 1 - slot):
                cp.start()

    @pl.when(i < nused_ref[0])
    def _():
        for r in range(0, xs_ref.shape[0] // ROW_CHUNKS, EXPERT_SUB):
            xb = _unpack_rows(xs_ref, r, EXPERT_SUB).astype(BF16)
            gu = jnp.dot(xb, w1b_ref[...], preferred_element_type=F32) + b1_ref[0, 0]
            gate = jnp.minimum(gu[:, :D_FF], SWIGLU_LIMIT)
            up = jnp.clip(gu[:, D_FF:], -SWIGLU_LIMIT, SWIGLU_LIMIT)
            act = (up + 1.0) * (gate * jax.nn.sigmoid(SWIGLU_ALPHA * gate))
            y = jnp.dot(act.astype(BF16), w2b_ref[...], preferred_element_type=F32) + b2_ref[0, 0]
            _pack_rows(y, ys_ref, r)


def _expert_plan(pend, n_blk, bm):
    blk = jnp.arange(n_blk, dtype=I32)
    n_used = (pend[-1] // bm).astype(I32)
    expert = jnp.minimum(jnp.sum(blk[:, None] * bm >= pend[None, :], axis=1), N_EXPERTS - 1)
    expert = expert.astype(I32)
    prev = jnp.concatenate([jnp.full((1,), -1, I32), expert[:-1]])
    first = (blk < n_used) & (expert != prev)
    slot = (jnp.cumsum(first.astype(I32)) - 1) % 2
    later_first = first[None, :] & (blk[None, :] > blk[:, None])
    nxt = jnp.where(jnp.any(later_first, axis=1), expert[jnp.argmax(later_first, axis=1)], -1)
    return (expert, first.astype(I32), slot.astype(I32), nxt.astype(I32), n_used.reshape(1))


def _expert_ffn(plan, xs, layer, w1, b1, w2, b2):
    n_slots = xs.shape[0] // ROW_CHUNKS
    bm = EXPERT_TILE
    used_block = lambda i, e, f, s, nx, nu: (jnp.minimum(i, nu[0] - 1), 0)
    bias_block = lambda i, e, f, s, nx, nu: (layer, e[i], 0, 0)
    return pl.pallas_call(
        functools.partial(_expert_kernel, layer=layer),
        out_shape=jax.ShapeDtypeStruct(xs.shape, I32),
        grid_spec=pltpu.PrefetchScalarGridSpec(
            num_scalar_prefetch=5,
            grid=(n_slots // bm,),
            in_specs=[
                pl.BlockSpec((bm * ROW_CHUNKS, LANES), used_block),
                pl.BlockSpec(memory_space=pl.ANY),
                pl.BlockSpec((1, 1, 1, 2 * D_FF), bias_block),
                pl.BlockSpec(memory_space=pl.ANY),
                pl.BlockSpec((1, 1, 1, D_MODEL), bias_block),
            ],
            out_specs=pl.BlockSpec((bm * ROW_CHUNKS, LANES), used_block),
            scratch_shapes=[
                pltpu.VMEM((2, D_MODEL, 2 * D_FF), F32), pltpu.VMEM((2, D_FF, D_MODEL), F32),
                pltpu.VMEM((D_MODEL, 2 * D_FF), BF16), pltpu.VMEM((D_FF, D_MODEL), BF16),
                pltpu.SemaphoreType.DMA((2, 2)),
            ],
        ),
        compiler_params=pltpu.CompilerParams(
            dimension_semantics=("arbitrary",), vmem_limit_bytes=VMEM_LIMIT),
        name="expert_ffn",
    )(*plan, xs, w1, b1, w2, b2)


def _combine_kernel(yg_ref, wtm_ref, x1_ref, mod_ref, fn_ref, *o_refs, final, n0):
    tm = x1_ref.shape[0]
    wtm = wtm_ref[...]
    moe = jnp.zeros((tm, D_MODEL), F32)
    for kk in range(TOP_K):
        moe = moe + wtm[:, kk:kk + 1] * _unpack_rows(yg_ref.at[kk], 0, tm)
    g2 = mod_ref[0][:, 5 * D_MODEL:6 * D_MODEL]
    out = x1_ref[...] + g2 * moe
    if final:
        out = _rms(out) * fn_ref[...]
    if len(o_refs) == 1:
        o_refs[0][...] = out
        return

    @pl.when(pl.program_id(0) < n0)
    def _():
        o_refs[0][...] = out

    @pl.when(pl.program_id(0) >= n0)
    def _():
        o_refs[1][...] = out


def _combine(yg, wtm, x1, mod, final_norm, seq_len, final, n_first):
    n_tok = x1.shape[0]
    tm = TOKEN_TILE
    per_seq = seq_len // tm
    n0 = n_first // tm
    if final and n_first < n_tok:
        out_shape = (jax.ShapeDtypeStruct((n_first, D_MODEL), F32),
                     jax.ShapeDtypeStruct((n_tok - n_first, D_MODEL), F32))
        out_specs = (pl.BlockSpec((tm, D_MODEL), lambda i: (jnp.minimum(i, n0 - 1), 0)),
                     pl.BlockSpec((tm, D_MODEL), lambda i: (jnp.maximum(i - n0, 0), 0)))
    else:
        out_shape = jax.ShapeDtypeStruct((n_tok, D_MODEL), F32)
        out_specs = pl.BlockSpec((tm, D_MODEL), lambda i: (i, 0))
    return pl.pallas_call(
        functools.partial(_combine_kernel, final=final, n0=n0),
        out_shape=out_shape,
        grid=(n_tok // tm,),
        in_specs=[
            pl.BlockSpec((TOP_K, tm * ROW_CHUNKS, LANES), lambda i: (0, i, 0)),
            pl.BlockSpec((tm, LANES), lambda i: (i, 0)),
            pl.BlockSpec((tm, D_MODEL), lambda i: (i, 0)),
            pl.BlockSpec((1, 1, 6 * D_MODEL), lambda i: (i // per_seq, 0, 0)),
            pl.BlockSpec((1, D_MODEL), lambda i: (0, 0)),
        ],
        out_specs=out_specs,
        compiler_params=pltpu.CompilerParams(
            dimension_semantics=("arbitrary",), vmem_limit_bytes=VMEM_LIMIT),
        name="moe_combine",
    )(yg, wtm, x1, mod, final_norm)


def _moe_layers(routes, mods, layer, ffn_w, final_norm, seq_len, final):
    bm = EXPERT_TILE
    plans = []
    for x1, hp, topi, rank, wtm, counts in routes:
        n_slots = x1.shape[0] * TOP_K + N_EXPERTS * bm
        n_blk = n_slots // bm
        cnt = counts[:, 0].astype(I32)
        padded = (cnt + bm - 1) // bm * bm
        pend = jnp.cumsum(padded)
        pstart = (pend - padded).astype(I32)
        dest = _slots(pstart, topi, rank).reshape(-1, SC_ROWS)
        plans.append((n_slots, _expert_plan(pend, n_blk, bm), dest, topi.shape[2]))
    xs = [_sc_dispatch(r[1], dest, n_slots, tm)
          for r, (n_slots, _, dest, tm) in zip(routes, plans)]
    ys = [_expert_ffn(plan, x, layer, *ffn_w) for x, (_, plan, _, _) in zip(xs, plans)]
    yg = [_sc_gather(y, dest, tm) for y, (_, _, dest, tm) in zip(ys, plans)]
    return [_combine(g, r[4], r[0], mod, final_norm, seq_len, final, r[0].shape[0])
            for g, r, mod in zip(yg, routes, mods)]


def _rope_tables(seq_len):
    inv_freq = 1.0 / (ROPE_THETA ** (jnp.arange(0, D_ROPE, 2, dtype=F32) / D_ROPE))
    ang = jnp.arange(seq_len, dtype=F32)[:, None] * inv_freq[None, :]
    cos, sin = jnp.cos(ang), jnp.sin(ang)
    ones = jnp.ones((seq_len, ROPE_LO), F32)
    zeros_lo = jnp.zeros((seq_len, ROPE_LO), F32)
    zeros_hi = jnp.zeros((seq_len, HEAD_PAD - ROPE_LO - D_ROPE), F32)
    zeros_h = jnp.zeros((seq_len, ROPE_HALF), F32)
    rope_c = jnp.concatenate([ones, cos, cos, zeros_hi], axis=1)
    rope_s1 = jnp.concatenate([zeros_lo, -sin, zeros_h, zeros_hi], axis=1)
    rope_s2 = jnp.concatenate([zeros_lo, zeros_h, sin, zeros_hi], axis=1)
    return cos.T, sin.T, rope_c, rope_s1, rope_s2


def _mla_weights(w_in, q_norm, kv_norm, w_uq, w_ukv, w_o, seq_len):
    d_qk = D_NOPE + D_ROPE
    q_scale = d_qk ** -0.5 * math.log2(math.e)
    pad_pe = jnp.zeros((D_MODEL, HEAD_PAD), F32).at[:, ROPE_LO:ROPE_LO + D_ROPE].set(
        w_in[:, Q_RANK + KV_RANK:])
    w_in_p = jnp.concatenate([w_in[:, :Q_RANK + KV_RANK], pad_pe], axis=1)
    w_uq_p = jnp.pad(w_uq.reshape(Q_RANK, N_HEADS, d_qk) * q_scale,
                     ((0, 0), (0, 0), (0, HEAD_PAD - d_qk)))
    w_kv = w_ukv.reshape(KV_RANK, N_HEADS, D_NOPE + D_V)
    w_uk_p = jnp.pad(w_kv[:, :, :D_NOPE], ((0, 0), (0, 0), (0, HEAD_PAD - D_NOPE)))
    cos_t, sin_t, rope_c, rope_s1, rope_s2 = _rope_tables(seq_len)
    return {
        "w_in": w_in_p.astype(BF16),
        "q_norm": q_norm.reshape(1, Q_RANK),
        "kv_norm": kv_norm.reshape(1, KV_RANK),
        "w_uq_t": w_uq_p.reshape(Q_RANK, N_HEADS * HEAD_PAD).T.astype(BF16),
        "w_uk": w_uk_p.reshape(KV_RANK, N_HEADS * HEAD_PAD).astype(BF16),
        "w_uv_t": w_kv[:, :, D_NOPE:].reshape(KV_RANK, N_HEADS * D_V).T.astype(BF16),
        "w_o": w_o.astype(BF16),
        "cos_t": cos_t, "sin_t": sin_t, "rope_c": rope_c, "rope_s1": rope_s1, "rope_s2": rope_s2,
    }


def _router_weights(router_w, router_b):
    return router_w.T.astype(BF16), router_b.reshape(N_EXPERTS, 1)


def kernel(x_prompt, x_sample, c_prompt, c_sample, ada_w, ada_b, mla_w_in, mla_q_norm,
           mla_kv_norm, mla_w_uq, mla_w_ukv, mla_w_o, pool_w, pool_scale, router_w, router_b,
           moe_w1, moe_b1, moe_w2, moe_b2, final_norm):
    n_prompt, seq_len, _ = x_prompt.shape
    assert x_sample.shape[1] == seq_len and seq_len % TOKEN_TILE == 0
    depth = ada_w.shape[0]
    n_sample = x_sample.shape[0]
    xs = [x_prompt.reshape(-1, D_MODEL), x_sample.reshape(-1, D_MODEL)]
    n_seqs = [n_prompt, n_sample]
    mods = _ada_mod(jnp.concatenate([c_prompt, c_sample], axis=0), ada_w, ada_b)
    fnorm = final_norm.reshape(1, D_MODEL)
    ffn_w = (moe_w1, moe_b1.reshape(depth, N_EXPERTS, 1, 2 * D_FF),
             moe_w2, moe_b2.reshape(depth, N_EXPERTS, 1, D_MODEL))

    for i in range(depth):
        rw_t, rb = _router_weights(router_w[i], router_b[i])
        j = i // 2
        group_mods = [mods[i, :n_prompt].reshape(n_prompt, 1, 6 * D_MODEL),
                      mods[i, n_prompt:].reshape(n_sample, 1, 6 * D_MODEL)]
        if i % 2 == 0:
            w = _mla_weights(mla_w_in[j], mla_q_norm[j], mla_kv_norm[j], mla_w_uq[j],
                             mla_w_ukv[j], mla_w_o[j], seq_len)
        routes = []
        for x, mod in zip(xs, group_mods):
            if i % 2 == 0:
                qt, k, vt = _mla_pre([x], mod, w, seq_len)
                ot = _attention(qt, k, vt, seq_len)
                routes.append(_post_mix(ot, [x], mod, w["w_o"], rw_t, rb, seq_len))
            else:
                routes.append(_pool_mix(x, mod, pool_w[j].astype(BF16),
                                        pool_scale[j].reshape(1, D_MODEL), rw_t, rb, seq_len))
        xs = _moe_layers(routes, group_mods, i, ffn_w, fnorm, seq_len, i == depth - 1)

    return (xs[0].reshape(n_prompt, seq_len, D_MODEL), xs[1].reshape(n_sample, seq_len, D_MODEL))
```

```python
import functools
import math

import jax
import jax.numpy as jnp
from jax import lax
from jax.experimental import pallas as pl
from jax.experimental.pallas import tpu as pltpu
from jax.experimental.pallas import tpu_sc as plsc

F32 = jnp.float32
BF16 = jnp.bfloat16
I32 = jnp.int32

D_MODEL = 1024
N_HEADS = 16
Q_RANK = 384
KV_RANK = 256
D_NOPE = 64
D_ROPE = 32
D_V = 64
V_ROWS = D_V + 16
ROPE_THETA = 10000.0
POOL_WINDOWS = (2, 4, 8, 16)
POOL_GROUP = D_MODEL // len(POOL_WINDOWS)
N_EXPERTS = 32
TOP_K = 4
D_FF = D_MODEL
SWIGLU_LIMIT = 7.0
SWIGLU_ALPHA = 1.702
EPS = 1e-6

LANES = 128
SUBLANES = 8
HEAD_PAD = 128
ROPE_LO = D_NOPE
ROPE_HALF = D_ROPE // 2
VMEM_LIMIT = 56 * 1024 * 1024

TOKEN_TILE = 256
COMBINE_TILE = 512
EXPERT_TILE = 512
EXPERT_SUB = 256
ATTN_KEY_CHUNKS = 4
ATTN_HEADS_PER_STEP = 4
POOL_HALO = 8
PACKED = D_MODEL // 2
ROW_CHUNKS = PACKED // LANES
ROW_SHAPE = (ROW_CHUNKS, LANES)

SC_CORES = 2
SC_SUBCORES = 16
SC_WORKERS = SC_CORES * SC_SUBCORES
SC_LANES = 16
SC_ROWS = 64
SC_SUM_ROWS = 32


def _rms(x):
    return x * lax.rsqrt(jnp.mean(x * x, axis=-1, keepdims=True) + EPS)


def _pack_rows(y, out_ref, row0=0):
    n = y.shape[0]
    lo = lax.bitcast_convert_type(y[:, :PACKED].astype(BF16).astype(F32), I32)
    hi = lax.bitcast_convert_type(y[:, PACKED:].astype(BF16).astype(F32), I32)
    words = lax.shift_right_logical(lo, 16) | (hi & jnp.int32(-65536))
    for j in range(ROW_CHUNKS):
        out_ref[pl.ds(row0 * ROW_CHUNKS + j, n, stride=ROW_CHUNKS), :] = (
            words[:, j * LANES:(j + 1) * LANES])


def _unpack_rows(in_ref, row0, n):
    words = jnp.concatenate(
        [in_ref[pl.ds(row0 * ROW_CHUNKS + j, n, stride=ROW_CHUNKS), :] for j in range(ROW_CHUNKS)],
        axis=1)
    lo = lax.bitcast_convert_type(lax.shift_left(words, 16), F32)
    hi = lax.bitcast_convert_type(words & jnp.int32(-65536), F32)
    return jnp.concatenate([lo, hi], axis=1)


def _part_specs(parts, tm):
    if len(parts) == 1:
        return [pl.BlockSpec((tm, D_MODEL), lambda i: (i, 0))]
    n0 = parts[0].shape[0] // tm
    return [pl.BlockSpec((tm, D_MODEL), lambda i: (jnp.minimum(i, n0 - 1), 0)),
            pl.BlockSpec((tm, D_MODEL), lambda i: (jnp.maximum(i - n0, 0), 0))]


def _part_load(refs, n0):
    if len(refs) == 1:
        return refs[0][...]
    return jnp.where(pl.program_id(0) < n0, refs[0][...], refs[1][...])


def _ada_kernel(c_ref, w_ref, b_ref, o_ref):
    c = c_ref[...]
    act = (c * jax.nn.sigmoid(c)).astype(BF16)
    o_ref[0] = jnp.dot(act, w_ref[0].astype(BF16), preferred_element_type=F32) + b_ref[0]


def _ada_mod(c, ada_w, ada_b):
    depth, _, n_out = ada_w.shape
    n_seq = c.shape[0]
    tn = 1536
    return pl.pallas_call(
        _ada_kernel,
        out_shape=jax.ShapeDtypeStruct((depth, n_seq, n_out), F32),
        grid=(depth, n_out // tn),
        in_specs=[
            pl.BlockSpec((n_seq, D_MODEL), lambda l, j: (0, 0)),
            pl.BlockSpec((1, D_MODEL, tn), lambda l, j: (l, 0, j)),
            pl.BlockSpec((1, 1, tn), lambda l, j: (l, 0, j)),
        ],
        out_specs=pl.BlockSpec((1, n_seq, tn), lambda l, j: (l, 0, j)),
        compiler_params=pltpu.CompilerParams(
            dimension_semantics=("arbitrary", "arbitrary"), vmem_limit_bytes=VMEM_LIMIT),
        name="ada_mod",
    )(c, ada_w, ada_b.reshape(depth, 1, n_out))


def _mla_pre_kernel(*refs, n_x, n0):
    x_refs = refs[:n_x]
    (mod_ref, win_ref, qn_ref, kvn_ref, wuqt_ref, wuk_ref, wuvt_ref,
     cost_ref, sint_ref, ck_ref, s1k_ref, s2k_ref, qt_ref, k_ref, vt_ref) = refs[n_x:]
    tm = x_refs[0].shape[0]
    mod = mod_ref[0]
    sh1 = mod[:, 0:D_MODEL]
    sc1 = mod[:, D_MODEL:2 * D_MODEL]
    h = (_rms(_part_load(x_refs, n0)) * (1.0 + sc1) + sh1).astype(BF16)
    a = jnp.dot(h, win_ref[...], preferred_element_type=F32)
    cq = (_rms(a[:, :Q_RANK]) * qn_ref[...]).astype(BF16)
    ckv = (_rms(a[:, Q_RANK:Q_RANK + KV_RANK]) * kvn_ref[...]).astype(BF16)

    kpe = a[:, Q_RANK + KV_RANK:]
    kpe = (kpe * ck_ref[...]
           + pltpu.roll(kpe, LANES - ROPE_HALF, 1) * s1k_ref[...]
           + pltpu.roll(kpe, ROPE_HALF, 1) * s2k_ref[...])

    qt = lax.dot_general(wuqt_ref[...], cq, (((1,), (1,)), ((), ())),
                         preferred_element_type=F32)
    q3 = qt.reshape(N_HEADS, HEAD_PAD, tm)
    x1 = q3[:, ROPE_LO:ROPE_LO + ROPE_HALF, :]
    x2 = q3[:, ROPE_LO + ROPE_HALF:ROPE_LO + D_ROPE, :]
    cos = cost_ref[...][None]
    sin = sint_ref[...][None]
    q3 = jnp.concatenate(
        [q3[:, :ROPE_LO, :], x1 * cos - x2 * sin, x2 * cos + x1 * sin,
         q3[:, ROPE_LO + D_ROPE:, :]], axis=1)
    qt_ref[:, 0] = q3.astype(BF16)

    kn = jnp.dot(ckv, wuk_ref[...], preferred_element_type=F32)
    for hd in range(N_HEADS):
        k_ref[hd] = (kn[:, hd * HEAD_PAD:(hd + 1) * HEAD_PAD] + kpe).astype(BF16)

    vt = lax.dot_general(wuvt_ref[...], ckv, (((1,), (1,)), ((), ())),
                         preferred_element_type=F32)
    vt_ref[:, 0, :D_V, :] = vt.reshape(N_HEADS, D_V, tm).astype(BF16)
    vt_ref[:, 0, D_V:, :] = jnp.ones((N_HEADS, V_ROWS - D_V, tm), BF16)


def _mla_pre(x_parts, mod, w, seq_len):
    n_tok = sum(p.shape[0] for p in x_parts)
    tm = TOKEN_TILE
    n_tiles = n_tok // tm
    per_seq = seq_len // tm
    const = lambda i: (0, 0)
    return pl.pallas_call(
        functools.partial(_mla_pre_kernel, n_x=len(x_parts), n0=x_parts[0].shape[0] // tm),
        out_shape=(
            jax.ShapeDtypeStruct((N_HEADS, n_tiles, HEAD_PAD, tm), BF16),
            jax.ShapeDtypeStruct((N_HEADS, n_tok, HEAD_PAD), BF16),
            jax.ShapeDtypeStruct((N_HEADS, n_tok // seq_len, V_ROWS, seq_len), BF16),
        ),
        grid=(n_tiles,),
        in_specs=_part_specs(x_parts, tm) + [
            pl.BlockSpec((1, 1, 6 * D_MODEL), lambda i: (i // per_seq, 0, 0)),
            pl.BlockSpec(w["w_in"].shape, const),
            pl.BlockSpec(w["q_norm"].shape, const),
            pl.BlockSpec(w["kv_norm"].shape, const),
            pl.BlockSpec(w["w_uq_t"].shape, const),
            pl.BlockSpec(w["w_uk"].shape, const),
            pl.BlockSpec(w["w_uv_t"].shape, const),
            pl.BlockSpec((ROPE_HALF, tm), lambda i: (0, i % per_seq)),
            pl.BlockSpec((ROPE_HALF, tm), lambda i: (0, i % per_seq)),
            pl.BlockSpec((tm, LANES), lambda i: (i % per_seq, 0)),
            pl.BlockSpec((tm, LANES), lambda i: (i % per_seq, 0)),
            pl.BlockSpec((tm, LANES), lambda i: (i % per_seq, 0)),
        ],
        out_specs=(
            pl.BlockSpec((N_HEADS, 1, HEAD_PAD, tm), lambda i: (0, i, 0, 0)),
            pl.BlockSpec((N_HEADS, tm, HEAD_PAD), lambda i: (0, i, 0)),
            pl.BlockSpec((N_HEADS, 1, V_ROWS, tm), lambda i: (0, i // per_seq, 0, i % per_seq)),
        ),
        compiler_params=pltpu.CompilerParams(
            dimension_semantics=("arbitrary",), vmem_limit_bytes=VMEM_LIMIT),
        name="mla_pre",
    )(*x_parts, mod, w["w_in"], w["q_norm"], w["kv_norm"], w["w_uq_t"], w["w_uk"], w["w_uv_t"],
      w["cos_t"], w["sin_t"], w["rope_c"], w["rope_s1"], w["rope_s2"])


def _attention_kernel(qt_ref, k_ref, vt_ref, ot_ref, s0_ref, s1_ref):
    n_heads, n_q = qt_ref.shape[:2]
    n_tiles = n_heads * n_q
    n_keys = k_ref.shape[1]
    kc = n_keys // ATTN_KEY_CHUNKS

    def stage(t_next, s_next_ref, t_cur, s_cur_ref, m_cur):
        m_next, o = None, None
        if t_next is not None:
            h_next, j_next = t_next // n_q, t_next % n_q
        if t_cur is not None:
            h_cur, j_cur = t_cur // n_q, t_cur % n_q
        for c in range(ATTN_KEY_CHUNKS):
            rows = slice(c * kc, (c + 1) * kc)
            if t_next is not None:
                s = jnp.dot(k_ref[h_next, rows, :], qt_ref[h_next, j_next],
                            preferred_element_type=F32)
                s_next_ref[rows, :] = s
                cm = jnp.max(s, axis=0, keepdims=True)
                m_next = cm if m_next is None else jnp.maximum(m_next, cm)
            if t_cur is not None:
                p = jnp.exp2(s_cur_ref[rows, :] - m_cur).astype(BF16)
                part = jnp.dot(vt_ref[h_cur, 0, :, rows], p,
                               preferred_element_type=F32)
                o = part if o is None else o + part
        if t_cur is not None:
            denom = o[D_V:D_V + 1, :]
            ot_ref[h_cur, j_cur] = (o[:D_V, :] * (1.0 / denom)).astype(BF16)
        return m_next

    def body(i, m0):
        t = 2 * i
        m1 = stage(t + 1, s1_ref, t, s0_ref, m0)
        return stage(t + 2, s0_ref, t + 1, s1_ref, m1)

    m0 = lax.fori_loop(0, n_tiles // 2 - 1, body, stage(0, s0_ref, None, None, None))
    m1 = stage(n_tiles - 1, s1_ref, n_tiles - 2, s0_ref, m0)
    stage(None, None, n_tiles - 1, s1_ref, m1)


def _attention(qt, k, vt, seq_len):
    n_heads, n_tiles, _, tq = qt.shape
    per_seq = seq_len // tq
    n_seq = n_tiles // per_seq
    hb = ATTN_HEADS_PER_STEP
    return pl.pallas_call(
        _attention_kernel,
        out_shape=jax.ShapeDtypeStruct((n_heads, n_tiles, D_V, tq), BF16),
        grid=(n_seq, n_heads // hb),
        in_specs=[
            pl.BlockSpec((hb, per_seq, HEAD_PAD, tq), lambda b, h: (h, b, 0, 0)),
            pl.BlockSpec((hb, seq_len, HEAD_PAD), lambda b, h: (h, b, 0)),
            pl.BlockSpec((hb, 1, V_ROWS, seq_len), lambda b, h: (h, b, 0, 0)),
        ],
        out_specs=pl.BlockSpec((hb, per_seq, D_V, tq), lambda b, h: (h, b, 0, 0)),
        scratch_shapes=[pltpu.VMEM((seq_len, tq), F32), pltpu.VMEM((seq_len, tq), F32)],
        compiler_params=pltpu.CompilerParams(
            dimension_semantics=("arbitrary", "arbitrary"), vmem_limit_bytes=VMEM_LIMIT),
        name="attention",
    )(qt, k, vt)


def _route_tail(x1, mod, rwt_ref, rb_ref, x1_ref, hp_ref, topi_ref, rank_ref, wtm_ref, cnt_ref):
    tm = x1.shape[0]
    sh2 = mod[:, 3 * D_MODEL:4 * D_MODEL]
    sc2 = mod[:, 4 * D_MODEL:5 * D_MODEL]
    x1_ref[...] = x1
    h2 = _rms(x1) * (1.0 + sc2) + sh2
    _pack_rows(h2, hp_ref)
    logits = lax.dot_general(rwt_ref[...], h2.astype(BF16), (((1,), (1,)), ((), ())),
                             preferred_element_type=F32) + rb_ref[...]
    e_iota = lax.broadcasted_iota(I32, (N_EXPERTS, tm), 0)
    vals, idxs = [], []
    work = logits
    for _ in range(TOP_K):
        m = jnp.max(work, axis=0, keepdims=True)
        idx = jnp.min(jnp.where(work == m, e_iota, N_EXPERTS), axis=0, keepdims=True)
        vals.append(m)
        idxs.append(idx)
        work = jnp.where(e_iota == idx, -jnp.inf, work)
    ex = [jnp.exp(v - vals[0]) for v in vals]
    inv = 1.0 / (ex[0] + ex[1] + ex[2] + ex[3])
    topw = jnp.concatenate([e * inv for e in ex], axis=0)
    topi_ref[0] = jnp.concatenate(idxs, axis=0)

    @pl.when(pl.program_id(0) == 0)
    def _():
        cnt_ref[...] = jnp.zeros_like(cnt_ref)

    row = lax.broadcasted_iota(I32, (tm, tm), 0)
    col = lax.broadcasted_iota(I32, (tm, tm), 1)
    earlier = (row < col).astype(BF16)
    running = cnt_ref[...][:, 0:1]
    ranks = []
    for kk in range(TOP_K):
        onehot = (e_iota == idxs[kk]).astype(F32)
        before = jnp.dot(onehot.astype(BF16), earlier, preferred_element_type=F32)
        ranks.append(jnp.sum(onehot * (running + before), axis=0, keepdims=True))
        running = running + jnp.sum(onehot, axis=1, keepdims=True)
    rank_ref[0] = jnp.concatenate(ranks, axis=0).astype(I32)
    cnt_ref[...] = jnp.broadcast_to(running, cnt_ref.shape)

    wpad = jnp.concatenate(
        [jnp.broadcast_to(topw[kk:kk + 1], (SC_LANES, tm)) for kk in range(TOP_K)]
        + [jnp.zeros((LANES - TOP_K * SC_LANES, tm), F32)], axis=0)
    wtm_ref[...] = wpad.T


def _route_out_shapes(n_tok, tm):
    n_tiles = n_tok // tm
    return (
        jax.ShapeDtypeStruct((n_tok, D_MODEL), F32),
        jax.ShapeDtypeStruct((n_tok * ROW_CHUNKS, LANES), I32),
        jax.ShapeDtypeStruct((n_tiles, TOP_K, tm), I32),
        jax.ShapeDtypeStruct((n_tiles, TOP_K, tm), I32),
        jax.ShapeDtypeStruct((n_tok, LANES), F32),
        jax.ShapeDtypeStruct((N_EXPERTS, LANES), F32),
    )


def _route_out_specs(tm):
    return (
        pl.BlockSpec((tm, D_MODEL), lambda i: (i, 0)),
        pl.BlockSpec((tm * ROW_CHUNKS, LANES), lambda i: (i, 0)),
        pl.BlockSpec((1, TOP_K, tm), lambda i: (i, 0, 0)),
        pl.BlockSpec((1, TOP_K, tm), lambda i: (i, 0, 0)),
        pl.BlockSpec((tm, LANES), lambda i: (i, 0)),
        pl.BlockSpec((N_EXPERTS, LANES), lambda i: (0, 0)),
    )


def _post_mix_kernel(ot_ref, *refs, n_x, n0):
    x_refs = refs[:n_x]
    mod_ref, wo_ref, rwt_ref, rb_ref = refs[n_x:n_x + 4]
    out_refs = refs[n_x + 4:]
    tm = x_refs[0].shape[0]
    mod = mod_ref[0]
    g1 = mod[:, 2 * D_MODEL:3 * D_MODEL]
    ot = ot_ref[:, 0].reshape(N_HEADS * D_V, tm)
    mix = lax.dot_general(ot, wo_ref[...], (((0,), (0,)), ((), ())),
                          preferred_element_type=F32)
    _route_tail(_part_load(x_refs, n0) + g1 * mix, mod, rwt_ref, rb_ref, *out_refs)


def _post_mix(ot, x_parts, mod, w_o, rw_t, rb, seq_len):
    n_tok = sum(p.shape[0] for p in x_parts)
    tm = ot.shape[3]
    per_seq = seq_len // tm
    const = lambda i: (0, 0)
    return pl.pallas_call(
        functools.partial(_post_mix_kernel, n_x=len(x_parts), n0=x_parts[0].shape[0] // tm),
        out_shape=_route_out_shapes(n_tok, tm),
        grid=(n_tok // tm,),
        in_specs=[pl.BlockSpec((N_HEADS, 1, D_V, tm), lambda i: (0, i, 0, 0))]
        + _part_specs(x_parts, tm) + [
            pl.BlockSpec((1, 1, 6 * D_MODEL), lambda i: (i // per_seq, 0, 0)),
            pl.BlockSpec(w_o.shape, const),
            pl.BlockSpec(rw_t.shape, const),
            pl.BlockSpec(rb.shape, const),
        ],
        out_specs=_route_out_specs(tm),
        compiler_params=pltpu.CompilerParams(
            dimension_semantics=("arbitrary",), vmem_limit_bytes=VMEM_LIMIT),
        name="post_mix",
    )(ot, *x_parts, mod, w_o, rw_t, rb)


def _pool_mix_kernel(x_ref, xp_ref, xn_ref, mod_ref, pw_ref, ps_ref, rwt_ref, rb_ref,
                     *out_refs, seq_len):
    tm = x_ref.shape[0]
    per_seq = seq_len // tm
    mod = mod_ref[0]
    sh1 = mod[:, 0:D_MODEL]
    sc1 = mod[:, D_MODEL:2 * D_MODEL]
    g1 = mod[:, 2 * D_MODEL:3 * D_MODEL]
    pos0 = (pl.program_id(0) % per_seq) * tm

    def normed(v):
        return _rms(v) * (1.0 + sc1) + sh1

    x = x_ref[...]
    h = normed(x)
    h_prev = jnp.where(pos0 > 0, normed(xp_ref[...]), 0.0)
    h_next = jnp.where(pos0 + tm < seq_len, normed(xn_ref[...]), 0.0)
    hext = jnp.concatenate([h_prev, h, h_next], axis=0)
    n_ext = tm + 2 * POOL_HALO
    pos = pos0 + lax.broadcasted_iota(I32, (tm, 1), 0)

    outs = []
    for g, win in enumerate(POOL_WINDOWS):
        left = win // 2
        right = win - 1 - left
        cols = slice(g * POOL_GROUP, (g + 1) * POOL_GROUP)
        s = hext[:, cols]
        span = 1
        while span < win:
            s = s + pltpu.roll(s, span, 0)
            span *= 2
        if right:
            s = pltpu.roll(s, n_ext - right, 0)
        num = s[POOL_HALO:POOL_HALO + tm, :]
        count = (jnp.minimum(pos + right + 1, seq_len) - jnp.maximum(pos - left, 0)).astype(F32)
        diff = (num / count - h[:, cols]).astype(BF16)
        outs.append(jnp.dot(diff, pw_ref[g], preferred_element_type=F32))
    mix = jnp.concatenate(outs, axis=1) * ps_ref[...]
    _route_tail(x + g1 * mix, mod, rwt_ref, rb_ref, *out_refs)


def _pool_mix(x, mod, pool_w, pool_scale, rw_t, rb, seq_len):
    n_tok = x.shape[0]
    tm = TOKEN_TILE
    per_seq = seq_len // tm
    halo_per_tile = tm // POOL_HALO
    n_halo = n_tok // POOL_HALO
    const = lambda i: (0, 0)
    return pl.pallas_call(
        functools.partial(_pool_mix_kernel, seq_len=seq_len),
        out_shape=_route_out_shapes(n_tok, tm),
        grid=(n_tok // tm,),
        in_specs=[
            pl.BlockSpec((tm, D_MODEL), lambda i: (i, 0)),
            pl.BlockSpec((POOL_HALO, D_MODEL),
                         lambda i: (jnp.maximum(i * halo_per_tile - 1, 0), 0)),
            pl.BlockSpec((POOL_HALO, D_MODEL),
                         lambda i: (jnp.minimum((i + 1) * halo_per_tile, n_halo - 1), 0)),
            pl.BlockSpec((1, 1, 6 * D_MODEL), lambda i: (i // per_seq, 0, 0)),
            pl.BlockSpec(pool_w.shape, lambda i: (0, 0, 0)),
            pl.BlockSpec(pool_scale.shape, const),
            pl.BlockSpec(rw_t.shape, const),
            pl.BlockSpec(rb.shape, const),
        ],
        out_specs=_route_out_specs(tm),
        compiler_params=pltpu.CompilerParams(
            dimension_semantics=("arbitrary",), vmem_limit_bytes=VMEM_LIMIT),
        name="pool_mix",
    )(x, x, x, mod, pool_w, pool_scale, rw_t, rb)


def _slots_kernel(pstart_ref, topi_ref, rank_ref, dest_ref):
    topi = topi_ref[...]
    start = jnp.zeros_like(topi)
    for e in range(N_EXPERTS):
        start = jnp.where(topi == e, pstart_ref[e], start)
    dest_ref[...] = start + rank_ref[...]


def _slots(pstart, topi, rank):
    n_tiles, _, tm = topi.shape
    tb = math.gcd(n_tiles, 32)
    spec = pl.BlockSpec((tb, TOP_K, tm), lambda i, ps: (i, 0, 0))
    return pl.pallas_call(
        _slots_kernel,
        out_shape=jax.ShapeDtypeStruct(topi.shape, I32),
        grid_spec=pltpu.PrefetchScalarGridSpec(
            num_scalar_prefetch=1, grid=(n_tiles // tb,), in_specs=[spec, spec], out_specs=spec),
        compiler_params=pltpu.CompilerParams(dimension_semantics=("arbitrary",)),
        name="moe_slots",
    )(pstart, topi, rank)


def _sc_chunk_rows(c, tm, width=SC_ROWS):
    per_tile = tm // width
    tile = c // per_tile
    part = c % per_tile
    return [(tile * TOP_K + kk) * per_tile + part for kk in range(TOP_K)]


def _sc_dispatch(hp, dest, n_slots, tm):
    hp = hp.reshape((-1,) + ROW_SHAPE)
    n_tok = hp.shape[0]
    rows_per_w = dest.shape[0] // SC_WORKERS
    chunks_per_w = n_tok // SC_WORKERS // SC_ROWS
    mesh = plsc.VectorSubcoreMesh(core_axis_name="c", subcore_axis_name="s")

    @functools.partial(
        pl.kernel, mesh=mesh,
        out_type=jax.ShapeDtypeStruct((n_slots,) + ROW_SHAPE, I32),
        scratch_types=[
            pltpu.VMEM((rows_per_w, SC_ROWS), I32),
            pltpu.VMEM((SC_ROWS,) + ROW_SHAPE, I32),
            pltpu.SemaphoreType.DMA,
        ],
        name="sc_dispatch",
    )
    def run(hp_hbm, dest_hbm, xs_hbm, dest_v, rows_v, sem):
        wid = lax.axis_index("s") * SC_CORES + lax.axis_index("c")
        pltpu.sync_copy(dest_hbm.at[pl.ds(wid * rows_per_w, rows_per_w)], dest_v)

        @pl.loop(0, chunks_per_w)
        def _(c):
            tok0 = (wid * chunks_per_w + c) * SC_ROWS
            pltpu.sync_copy(hp_hbm.at[pl.ds(tok0, SC_ROWS)], rows_v)
            copies = [pltpu.async_copy(rows_v, xs_hbm.at[dest_v.at[row]], sem)
                      for row in _sc_chunk_rows(c, tm)]
            for cp in copies:
                cp.wait()

    return run(hp, dest).reshape(n_slots * ROW_CHUNKS, LANES)


def _sc_gather_sum(ys, dest, wtm, tm):
    ys = ys.reshape((-1,) + ROW_SHAPE)
    n_tok = wtm.shape[0]
    g = SC_SUM_ROWS
    rows_per_w = dest.shape[0] // SC_WORKERS
    chunks_per_w = n_tok // SC_WORKERS // g
    mesh = plsc.VectorSubcoreMesh(core_axis_name="c", subcore_axis_name="s")

    @functools.partial(
        pl.kernel, mesh=mesh,
        out_type=jax.ShapeDtypeStruct((n_tok,) + ROW_SHAPE, I32),
        scratch_types=[
            pltpu.VMEM((rows_per_w, g), I32),
            pltpu.VMEM((TOP_K, g) + ROW_SHAPE, I32),
            pltpu.VMEM((g, LANES), F32),
            pltpu.VMEM((g,) + ROW_SHAPE, I32),
            pltpu.SemaphoreType.DMA,
        ],
        compiler_params=pltpu.CompilerParams(needs_layout_passes=False),
        name="sc_gather_sum",
    )
    def run(ys_hbm, dest_hbm, w_hbm, out_hbm, dest_v, rows_v, w_v, out_v, sem):
        wid = lax.axis_index("s") * SC_CORES + lax.axis_index("c")
        pltpu.sync_copy(dest_hbm.at[pl.ds(wid * rows_per_w, rows_per_w)], dest_v)

        @pl.loop(0, chunks_per_w)
        def _(c):
            tok0 = (wid * chunks_per_w + c) * g
            copies = [pltpu.async_copy(ys_hbm.at[dest_v.at[row]], rows_v.at[kk], sem)
                      for kk, row in enumerate(_sc_chunk_rows(c, tm, g))]
            pltpu.sync_copy(w_hbm.at[pl.ds(tok0, g)], w_v)
            for cp in copies:
                cp.wait()

            @pl.loop(0, g)
            def _(t):
                wk = [w_v[t, pl.ds(SC_LANES * kk, SC_LANES)] for kk in range(TOP_K)]
                for j in range(ROW_CHUNKS):
                    for i in range(LANES // SC_LANES):
                        lanes = pl.ds(SC_LANES * i, SC_LANES)
                        lo = hi = None
                        for kk in range(TOP_K):
                            word = rows_v[kk, t, j, lanes]
                            a = lax.bitcast_convert_type(lax.shift_left(word, 16), F32) * wk[kk]
                            b = lax.bitcast_convert_type(word & jnp.int32(-65536), F32) * wk[kk]
                            lo = a if lo is None else lo + a
                            hi = b if hi is None else hi + b
                        packed = plsc.pack(lo, hi, format=plsc.PackFormat.INTERLEAVED)
                        out_v[t, j, lanes] = plsc.bitcast(packed, I32)

            pltpu.sync_copy(out_v, out_hbm.at[pl.ds(tok0, g)])

    return run(ys, dest, wtm).reshape(n_tok * ROW_CHUNKS, LANES)


def _expert_kernel(e_ref, first_ref, slot_ref, next_ref, nused_ref,
                   xs_ref, w1_hbm, b1_ref, w2_hbm, b2_ref, ys_ref,
                   w1f_ref, w2f_ref, w1b_ref, w2b_ref, sem, *, layer):
    i = pl.program_id(0)

    def weight_copies(expert, slot):
        return (pltpu.make_async_copy(w1_hbm.at[layer, expert], w1f_ref.at[slot], sem.at[slot, 0]),
                pltpu.make_async_copy(w2_hbm.at[layer, expert], w2f_ref.at[slot], sem.at[slot, 1]))

    @pl.when(first_ref[i] == 1)
    def _():
        slot = slot_ref[i]

        @pl.when(i == 0)
        def _():
            for cp in weight_copies(e_ref[i], slot):
                cp.start()

        for cp in weight_copies(e_ref[i], slot):
            cp.wait()
        w1b_ref[...] = w1f_ref[slot].astype(BF16)
        w2b_ref[...] = w2f_ref[slot].astype(BF16)

        @pl.when(next_ref[i] >= 0)
        def _():
            for cp in weight_copies(next_ref[i], 1 - slot):
                cp.start()

    @pl.when(i < nused_ref[0])
    def _():
        for r in range(0, xs_ref.shape[0] // ROW_CHUNKS, EXPERT_SUB):
            xb = _unpack_rows(xs_ref, r, EXPERT_SUB).astype(BF16)
            gu = jnp.dot(xb, w1b_ref[...], preferred_element_type=F32) + b1_ref[0, 0]
            gate = jnp.minimum(gu[:, :D_FF], SWIGLU_LIMIT)
            up = jnp.clip(gu[:, D_FF:], -SWIGLU_LIMIT, SWIGLU_LIMIT)
            act = (up + 1.0) * (gate * jax.nn.sigmoid(SWIGLU_ALPHA * gate))
            y = jnp.dot(act.astype(BF16), w2b_ref[...], preferred_element_type=F32) + b2_ref[0, 0]
            _pack_rows(y, ys_ref, r)


def _expert_plan(pend, n_blk, bm):
    blk = jnp.arange(n_blk, dtype=I32)
    n_used = (pend[-1] // bm).astype(I32)
    expert = jnp.minimum(jnp.sum(blk[:, None] * bm >= pend[None, :], axis=1), N_EXPERTS - 1)
    expert = expert.astype(I32)
    prev = jnp.concatenate([jnp.full((1,), -1, I32), expert[:-1]])
    first = (blk < n_used) & (expert != prev)
    slot = (jnp.cumsum(first.astype(I32)) - 1) % 2
    later_first = first[None, :] & (blk[None, :] > blk[:, None])
    nxt = jnp.where(jnp.any(later_first, axis=1), expert[jnp.argmax(later_first, axis=1)], -1)
    return (expert, first.astype(I32), slot.astype(I32), nxt.astype(I32), n_used.reshape(1))


def _expert_ffn(plan, xs, layer, w1, b1, w2, b2):
    n_slots = xs.shape[0] // ROW_CHUNKS
    bm = EXPERT_TILE
    used_block = lambda i, e, f, s, nx, nu: (jnp.minimum(i, nu[0] - 1), 0)
    bias_block = lambda i, e, f, s, nx, nu: (layer, e[i], 0, 0)
    return pl.pallas_call(
        functools.partial(_expert_kernel, layer=layer),
        out_shape=jax.ShapeDtypeStruct(xs.shape, I32),
        grid_spec=pltpu.PrefetchScalarGridSpec(
            num_scalar_prefetch=5,
            grid=(n_slots // bm,),
            in_specs=[
                pl.BlockSpec((bm * ROW_CHUNKS, LANES), used_block),
                pl.BlockSpec(memory_space=pl.ANY),
                pl.BlockSpec((1, 1, 1, 2 * D_FF), bias_block),
                pl.BlockSpec(memory_space=pl.ANY),
                pl.BlockSpec((1, 1, 1, D_MODEL), bias_block),
            ],
            out_specs=pl.BlockSpec((bm * ROW_CHUNKS, LANES), used_block),
            scratch_shapes=[
                pltpu.VMEM((2, D_MODEL, 2 * D_FF), F32), pltpu.VMEM((2, D_FF, D_MODEL), F32),
                pltpu.VMEM((D_MODEL, 2 * D_FF), BF16), pltpu.VMEM((D_FF, D_MODEL), BF16),
                pltpu.SemaphoreType.DMA((2, 2)),
            ],
        ),
        compiler_params=pltpu.CompilerParams(
            dimension_semantics=("arbitrary",), vmem_limit_bytes=VMEM_LIMIT),
        name="expert_ffn",
    )(*plan, xs, w1, b1, w2, b2)


def _combine_kernel(moe_ref, x1_ref, mod_ref, fn_ref, o_ref, *, final):
    tm = x1_ref.shape[0]
    g2 = mod_ref[0][:, 5 * D_MODEL:6 * D_MODEL]
    out = x1_ref[...] + g2 * _unpack_rows(moe_ref, 0, tm)
    if final:
        out = _rms(out) * fn_ref[...]
    o_ref[...] = out


def _combine(moe, x1, mod, final_norm, seq_len, final):
    n_tok = x1.shape[0]
    tm = COMBINE_TILE
    per_seq = seq_len // tm
    return pl.pallas_call(
        functools.partial(_combine_kernel, final=final),
        out_shape=jax.ShapeDtypeStruct((n_tok, D_MODEL), F32),
        grid=(n_tok // tm,),
        in_specs=[
            pl.BlockSpec((tm * ROW_CHUNKS, LANES), lambda i: (i, 0)),
            pl.BlockSpec((tm, D_MODEL), lambda i: (i, 0)),
            pl.BlockSpec((1, 1, 6 * D_MODEL), lambda i: (i // per_seq, 0, 0)),
            pl.BlockSpec((1, D_MODEL), lambda i: (0, 0)),
        ],
        out_specs=pl.BlockSpec((tm, D_MODEL), lambda i: (i, 0)),
        compiler_params=pltpu.CompilerParams(
            dimension_semantics=("arbitrary",), vmem_limit_bytes=VMEM_LIMIT),
        name="moe_combine",
    )(moe, x1, mod, final_norm)


def _moe_layers(routes, mods, layer, ffn_w, final_norm, seq_len, final):
    bm = EXPERT_TILE
    plans = []
    for x1, hp, topi, rank, wtm, counts in routes:
        n_slots = x1.shape[0] * TOP_K + N_EXPERTS * bm
        n_blk = n_slots // bm
        cnt = counts[:, 0].astype(I32)
        padded = (cnt + bm - 1) // bm * bm
        pend = jnp.cumsum(padded)
        pstart = (pend - padded).astype(I32)
        dest = _slots(pstart, topi, rank).reshape(-1, SC_ROWS)
        plans.append((n_slots, _expert_plan(pend, n_blk, bm), dest, topi.shape[2]))
    xs = [_sc_dispatch(r[1], dest, n_slots, tm)
          for r, (n_slots, _, dest, tm) in zip(routes, plans)]
    ys = [_expert_ffn(plan, x, layer, *ffn_w) for x, (_, plan, _, _) in zip(xs, plans)]
    moe = [_sc_gather_sum(y, dest.reshape(-1, SC_SUM_ROWS), r[4], tm)
           for y, r, (_, _, dest, tm) in zip(ys, routes, plans)]
    return [_combine(m, r[0], mod, final_norm, seq_len, final)
            for m, r, mod in zip(moe, routes, mods)]


def _rope_tables(seq_len):
    inv_freq = 1.0 / (ROPE_THETA ** (jnp.arange(0, D_ROPE, 2, dtype=F32) / D_ROPE))
    ang = jnp.arange(seq_len, dtype=F32)[:, None] * inv_freq[None, :]
    cos, sin = jnp.cos(ang), jnp.sin(ang)
    ones = jnp.ones((seq_len, ROPE_LO), F32)
    zeros_lo = jnp.zeros((seq_len, ROPE_LO), F32)
    zeros_hi = jnp.zeros((seq_len, HEAD_PAD - ROPE_LO - D_ROPE), F32)
    zeros_h = jnp.zeros((seq_len, ROPE_HALF), F32)
    rope_c = jnp.concatenate([ones, cos, cos, zeros_hi], axis=1)
    rope_s1 = jnp.concatenate([zeros_lo, -sin, zeros_h, zeros_hi], axis=1)
    rope_s2 = jnp.concatenate([zeros_lo, zeros_h, sin, zeros_hi], axis=1)
    return cos.T, sin.T, rope_c, rope_s1, rope_s2


def _mla_weights(w_in, q_norm, kv_norm, w_uq, w_ukv, w_o, seq_len):
    d_qk = D_NOPE + D_ROPE
    q_scale = d_qk ** -0.5 * math.log2(math.e)
    pad_pe = jnp.zeros((D_MODEL, HEAD_PAD), F32).at[:, ROPE_LO:ROPE_LO + D_ROPE].set(
        w_in[:, Q_RANK + KV_RANK:])
    w_in_p = jnp.concatenate([w_in[:, :Q_RANK + KV_RANK], pad_pe], axis=1)
    w_uq_p = jnp.pad(w_uq.reshape(Q_RANK, N_HEADS, d_qk) * q_scale,
                     ((0, 0), (0, 0), (0, HEAD_PAD - d_qk)))
    w_kv = w_ukv.reshape(KV_RANK, N_HEADS, D_NOPE + D_V)
    w_uk_p = jnp.pad(w_kv[:, :, :D_NOPE], ((0, 0), (0, 0), (0, HEAD_PAD - D_NOPE)))
    cos_t, sin_t, rope_c, rope_s1, rope_s2 = _rope_tables(seq_len)
    return {
        "w_in": w_in_p.astype(BF16),
        "q_norm": q_norm.reshape(1, Q_RANK),
        "kv_norm": kv_norm.reshape(1, KV_RANK),
        "w_uq_t": w_uq_p.reshape(Q_RANK, N_HEADS * HEAD_PAD).T.astype(BF16),
        "w_uk": w_uk_p.reshape(KV_RANK, N_HEADS * HEAD_PAD).astype(BF16),
        "w_uv_t": w_kv[:, :, D_NOPE:].reshape(KV_RANK, N_HEADS * D_V).T.astype(BF16),
        "w_o": w_o.astype(BF16),
        "cos_t": cos_t, "sin_t": sin_t, "rope_c": rope_c, "rope_s1": rope_s1, "rope_s2": rope_s2,
    }


def _router_weights(router_w, router_b):
    return router_w.T.astype(BF16), router_b.reshape(N_EXPERTS, 1)


def kernel(x_prompt, x_sample, c_prompt, c_sample, ada_w, ada_b, mla_w_in, mla_q_norm,
           mla_kv_norm, mla_w_uq, mla_w_ukv, mla_w_o, pool_w, pool_scale, router_w, router_b,
           moe_w1, moe_b1, moe_w2, moe_b2, final_norm):
    n_prompt, seq_len, _ = x_prompt.shape
    assert x_sample.shape[1] == seq_len and seq_len % TOKEN_TILE == 0
    depth = ada_w.shape[0]
    n_sample = x_sample.shape[0]
    xs = [x_prompt.reshape(-1, D_MODEL), x_sample.reshape(-1, D_MODEL)]
    n_seqs = [n_prompt, n_sample]
    mods = _ada_mod(jnp.concatenate([c_prompt, c_sample], axis=0), ada_w, ada_b)
    fnorm = final_norm.reshape(1, D_MODEL)
    ffn_w = (moe_w1, moe_b1.reshape(depth, N_EXPERTS, 1, 2 * D_FF),
             moe_w2, moe_b2.reshape(depth, N_EXPERTS, 1, D_MODEL))

    for i in range(depth):
        rw_t, rb = _router_weights(router_w[i], router_b[i])
        j = i // 2
        group_mods = [mods[i, :n_prompt].reshape(n_prompt, 1, 6 * D_MODEL),
                      mods[i, n_prompt:].reshape(n_sample, 1, 6 * D_MODEL)]
        if i % 2 == 0:
            w = _mla_weights(mla_w_in[j], mla_q_norm[j], mla_kv_norm[j], mla_w_uq[j],
                             mla_w_ukv[j], mla_w_o[j], seq_len)
        routes = []
        for x, mod in zip(xs, group_mods):
            if i % 2 == 0:
                qt, k, vt = _mla_pre([x], mod, w, seq_len)
                ot = _attention(qt, k, vt, seq_len)
                routes.append(_post_mix(ot, [x], mod, w["w_o"], rw_t, rb, seq_len))
            else:
                routes.append(_pool_mix(x, mod, pool_w[j].astype(BF16),
                                        pool_scale[j].reshape(1, D_MODEL), rw_t, rb, seq_len))
        xs = _moe_layers(routes, group_mods, i, ffn_w, fnorm, seq_len, i == depth - 1)

    return (xs[0].reshape(n_prompt, seq_len, D_MODEL), xs[1].reshape(n_sample, seq_len, D_MODEL))
```

```python
import functools
import math

import jax
import jax.numpy as jnp
from jax import lax
from jax.experimental import pallas as pl
from jax.experimental.pallas import tpu as pltpu
from jax.experimental.pallas import tpu_sc as plsc

F32 = jnp.float32
BF16 = jnp.bfloat16
I32 = jnp.int32

D_MODEL = 1024
N_HEADS = 16
Q_RANK = 384
KV_RANK = 256
D_NOPE = 64
D_ROPE = 32
D_V = 64
V_ROWS = D_V + 16
ROPE_THETA = 10000.0
POOL_WINDOWS = (2, 4, 8, 16)
POOL_GROUP = D_MODEL // len(POOL_WINDOWS)
N_EXPERTS = 32
TOP_K = 4
D_FF = D_MODEL
SWIGLU_LIMIT = 7.0
SWIGLU_ALPHA = 1.702
EPS = 1e-6

LANES = 128
SUBLANES = 8
HEAD_PAD = 128
ROPE_LO = D_NOPE
ROPE_HALF = D_ROPE // 2
VMEM_LIMIT = 56 * 1024 * 1024

TOKEN_TILE = 256
COMBINE_TILE = 512
EXPERT_TILE = 512
EXPERT_SUB = 256
ATTN_KEY_CHUNKS = 4
ATTN_HEADS_PER_STEP = 4
POOL_HALO = 8
PACKED = D_MODEL // 2
ROW_CHUNKS = PACKED // LANES
ROW_SHAPE = (ROW_CHUNKS, LANES)

SC_CORES = 2
SC_SUBCORES = 16
SC_WORKERS = SC_CORES * SC_SUBCORES
SC_LANES = 16
SC_ROWS = 64
SC_SUM_ROWS = 32


def _rms(x):
    return x * lax.rsqrt(jnp.mean(x * x, axis=-1, keepdims=True) + EPS)


def _pack_rows(y, out_ref, row0=0):
    n = y.shape[0]
    lo = lax.bitcast_convert_type(y[:, :PACKED].astype(BF16).astype(F32), I32)
    hi = lax.bitcast_convert_type(y[:, PACKED:].astype(BF16).astype(F32), I32)
    words = lax.shift_right_logical(lo, 16) | (hi & jnp.int32(-65536))
    for j in range(ROW_CHUNKS):
        out_ref[pl.ds(row0 * ROW_CHUNKS + j, n, stride=ROW_CHUNKS), :] = (
            words[:, j * LANES:(j + 1) * LANES])


def _unpack_rows(in_ref, row0, n):
    words = jnp.concatenate(
        [in_ref[pl.ds(row0 * ROW_CHUNKS + j, n, stride=ROW_CHUNKS), :] for j in range(ROW_CHUNKS)],
        axis=1)
    lo = lax.bitcast_convert_type(lax.shift_left(words, 16), F32)
    hi = lax.bitcast_convert_type(words & jnp.int32(-65536), F32)
    return jnp.concatenate([lo, hi], axis=1)


def _part_specs(parts, tm):
    if len(parts) == 1:
        return [pl.BlockSpec((tm, D_MODEL), lambda i: (i, 0))]
    n0 = parts[0].shape[0] // tm
    return [pl.BlockSpec((tm, D_MODEL), lambda i: (jnp.minimum(i, n0 - 1), 0)),
            pl.BlockSpec((tm, D_MODEL), lambda i: (jnp.maximum(i - n0, 0), 0))]


def _part_load(refs, n0):
    if len(refs) == 1:
        return refs[0][...]
    return jnp.where(pl.program_id(0) < n0, refs[0][...], refs[1][...])


def _ada_kernel(c_ref, w_ref, b_ref, o_ref):
    c = c_ref[...]
    act = (c * jax.nn.sigmoid(c)).astype(BF16)
    o_ref[0] = jnp.dot(act, w_ref[0].astype(BF16), preferred_element_type=F32) + b_ref[0]


def _ada_mod(c, ada_w, ada_b):
    depth, _, n_out = ada_w.shape
    n_seq = c.shape[0]
    tn = 1536
    return pl.pallas_call(
        _ada_kernel,
        out_shape=jax.ShapeDtypeStruct((depth, n_seq, n_out), F32),
        grid=(depth, n_out // tn),
        in_specs=[
            pl.BlockSpec((n_seq, D_MODEL), lambda l, j: (0, 0)),
            pl.BlockSpec((1, D_MODEL, tn), lambda l, j: (l, 0, j)),
            pl.BlockSpec((1, 1, tn), lambda l, j: (l, 0, j)),
        ],
        out_specs=pl.BlockSpec((1, n_seq, tn), lambda l, j: (l, 0, j)),
        compiler_params=pltpu.CompilerParams(
            dimension_semantics=("arbitrary", "arbitrary"), vmem_limit_bytes=VMEM_LIMIT),
        name="ada_mod",
    )(c, ada_w, ada_b.reshape(depth, 1, n_out))


def _mla_pre_kernel(*refs, n_x, n0):
    x_refs = refs[:n_x]
    (mod_ref, win_ref, qn_ref, kvn_ref, wuqt_ref, wuk_ref, wuvt_ref,
     cost_ref, sint_ref, ck_ref, s1k_ref, s2k_ref, qt_ref, k_ref, vt_ref) = refs[n_x:]
    tm = x_refs[0].shape[0]
    mod = mod_ref[0]
    sh1 = mod[:, 0:D_MODEL]
    sc1 = mod[:, D_MODEL:2 * D_MODEL]
    h = (_rms(_part_load(x_refs, n0)) * (1.0 + sc1) + sh1).astype(BF16)
    a = jnp.dot(h, win_ref[...], preferred_element_type=F32)
    cq = (_rms(a[:, :Q_RANK]) * qn_ref[...]).astype(BF16)
    ckv = (_rms(a[:, Q_RANK:Q_RANK + KV_RANK]) * kvn_ref[...]).astype(BF16)

    kpe = a[:, Q_RANK + KV_RANK:]
    kpe = (kpe * ck_ref[...]
           + pltpu.roll(kpe, LANES - ROPE_HALF, 1) * s1k_ref[...]
           + pltpu.roll(kpe, ROPE_HALF, 1) * s2k_ref[...])

    qt = lax.dot_general(wuqt_ref[...], cq, (((1,), (1,)), ((), ())),
                         preferred_element_type=F32)
    q3 = qt.reshape(N_HEADS, HEAD_PAD, tm)
    x1 = q3[:, ROPE_LO:ROPE_LO + ROPE_HALF, :]
    x2 = q3[:, ROPE_LO + ROPE_HALF:ROPE_LO + D_ROPE, :]
    cos = cost_ref[...][None]
    sin = sint_ref[...][None]
    q3 = jnp.concatenate(
        [q3[:, :ROPE_LO, :], x1 * cos - x2 * sin, x2 * cos + x1 * sin,
         q3[:, ROPE_LO + D_ROPE:, :]], axis=1)
    qt_ref[:, 0] = q3.astype(BF16)

    kn = jnp.dot(ckv, wuk_ref[...], preferred_element_type=F32)
    for hd in range(N_HEADS):
        k_ref[hd] = (kn[:, hd * HEAD_PAD:(hd + 1) * HEAD_PAD] + kpe).astype(BF16)

    vt = lax.dot_general(wuvt_ref[...], ckv, (((1,), (1,)), ((), ())),
                         preferred_element_type=F32)
    vt_ref[:, 0, :D_V, :] = vt.reshape(N_HEADS, D_V, tm).astype(BF16)
    vt_ref[:, 0, D_V:, :] = jnp.ones((N_HEADS, V_ROWS - D_V, tm), BF16)


def _mla_pre(x_parts, mod, w, seq_len):
    n_tok = sum(p.shape[0] for p in x_parts)
    tm = TOKEN_TILE
    n_tiles = n_tok // tm
    per_seq = seq_len // tm
    const = lambda i: (0, 0)
    return pl.pallas_call(
        functools.partial(_mla_pre_kernel, n_x=len(x_parts), n0=x_parts[0].shape[0] // tm),
        out_shape=(
            jax.ShapeDtypeStruct((N_HEADS, n_tiles, HEAD_PAD, tm), BF16),
            jax.ShapeDtypeStruct((N_HEADS, n_tok, HEAD_PAD), BF16),
            jax.ShapeDtypeStruct((N_HEADS, n_tok // seq_len, V_ROWS, seq_len), BF16),
        ),
        grid=(n_tiles,),
        in_specs=_part_specs(x_parts, tm) + [
            pl.BlockSpec((1, 1, 6 * D_MODEL), lambda i: (i // per_seq, 0, 0)),
            pl.BlockSpec(w["w_in"].shape, const),
            pl.BlockSpec(w["q_norm"].shape, const),
            pl.BlockSpec(w["kv_norm"].shape, const),
            pl.BlockSpec(w["w_uq_t"].shape, const),
            pl.BlockSpec(w["w_uk"].shape, const),
            pl.BlockSpec(w["w_uv_t"].shape, const),
            pl.BlockSpec((ROPE_HALF, tm), lambda i: (0, i % per_seq)),
            pl.BlockSpec((ROPE_HALF, tm), lambda i: (0, i % per_seq)),
            pl.BlockSpec((tm, LANES), lambda i: (i % per_seq, 0)),
            pl.BlockSpec((tm, LANES), lambda i: (i % per_seq, 0)),
            pl.BlockSpec((tm, LANES), lambda i: (i % per_seq, 0)),
        ],
        out_specs=(
            pl.BlockSpec((N_HEADS, 1, HEAD_PAD, tm), lambda i: (0, i, 0, 0)),
            pl.BlockSpec((N_HEADS, tm, HEAD_PAD), lambda i: (0, i, 0)),
            pl.BlockSpec((N_HEADS, 1, V_ROWS, tm), lambda i: (0, i // per_seq, 0, i % per_seq)),
        ),
        compiler_params=pltpu.CompilerParams(
            dimension_semantics=("arbitrary",), vmem_limit_bytes=VMEM_LIMIT),
        name="mla_pre",
    )(*x_parts, mod, w["w_in"], w["q_norm"], w["kv_norm"], w["w_uq_t"], w["w_uk"], w["w_uv_t"],
      w["cos_t"], w["sin_t"], w["rope_c"], w["rope_s1"], w["rope_s2"])


def _attention_kernel(qt_ref, k_ref, vt_ref, ot_ref, s0_ref, s1_ref):
    n_heads, n_q = qt_ref.shape[:2]
    n_tiles = n_heads * n_q
    n_keys = k_ref.shape[1]
    kc = n_keys // ATTN_KEY_CHUNKS

    def stage(t_next, s_next_ref, t_cur, s_cur_ref, m_cur):
        m_next, o = None, None
        if t_next is not None:
            h_next, j_next = t_next // n_q, t_next % n_q
        if t_cur is not None:
            h_cur, j_cur = t_cur // n_q, t_cur % n_q
        for c in range(ATTN_KEY_CHUNKS):
            rows = slice(c * kc, (c + 1) * kc)
            if t_next is not None:
                s = jnp.dot(k_ref[h_next, rows, :], qt_ref[h_next, j_next],
                            preferred_element_type=F32)
                s_next_ref[rows, :] = s
                cm = jnp.max(s, axis=0, keepdims=True)
                m_next = cm if m_next is None else jnp.maximum(m_next, cm)
            if t_cur is not None:
                p = jnp.exp2(s_cur_ref[rows, :] - m_cur).astype(BF16)
                part = jnp.dot(vt_ref[h_cur, 0, :, rows], p,
                               preferred_element_type=F32)
                o = part if o is None else o + part
        if t_cur is not None:
            denom = o[D_V:D_V + 1, :]
            ot_ref[h_cur, j_cur] = (o[:D_V, :] * (1.0 / denom)).astype(BF16)
        return m_next

    def body(i, m0):
        t = 2 * i
        m1 = stage(t + 1, s1_ref, t, s0_ref, m0)
        return stage(t + 2, s0_ref, t + 1, s1_ref, m1)

    m0 = lax.fori_loop(0, n_tiles // 2 - 1, body, stage(0, s0_ref, None, None, None))
    m1 = stage(n_tiles - 1, s1_ref, n_tiles - 2, s0_ref, m0)
    stage(None, None, n_tiles - 1, s1_ref, m1)


def _attention(qt, k, vt, seq_len):
    n_heads, n_tiles, _, tq = qt.shape
    per_seq = seq_len // tq
    n_seq = n_tiles // per_seq
    hb = ATTN_HEADS_PER_STEP
    return pl.pallas_call(
        _attention_kernel,
        out_shape=jax.ShapeDtypeStruct((n_heads, n_tiles, D_V, tq), BF16),
        grid=(n_seq, n_heads // hb),
        in_specs=[
            pl.BlockSpec((hb, per_seq, HEAD_PAD, tq), lambda b, h: (h, b, 0, 0)),
            pl.BlockSpec((hb, seq_len, HEAD_PAD), lambda b, h: (h, b, 0)),
            pl.BlockSpec((hb, 1, V_ROWS, seq_len), lambda b, h: (h, b, 0, 0)),
        ],
        out_specs=pl.BlockSpec((hb, per_seq, D_V, tq), lambda b, h: (h, b, 0, 0)),
        scratch_shapes=[pltpu.VMEM((seq_len, tq), F32), pltpu.VMEM((seq_len, tq), F32)],
        compiler_params=pltpu.CompilerParams(
            dimension_semantics=("arbitrary", "arbitrary"), vmem_limit_bytes=VMEM_LIMIT),
        name="attention",
    )(qt, k, vt)


def _route_tail(x1, mod, rwt_ref, rb_ref, x1_ref, hp_ref, topi_ref, rank_ref, wtm_ref, cnt_ref):
    tm = x1.shape[0]
    sh2 = mod[:, 3 * D_MODEL:4 * D_MODEL]
    sc2 = mod[:, 4 * D_MODEL:5 * D_MODEL]
    x1_ref[...] = x1
    h2 = _rms(x1) * (1.0 + sc2) + sh2
    _pack_rows(h2, hp_ref)
    logits = lax.dot_general(rwt_ref[...], h2.astype(BF16), (((1,), (1,)), ((), ())),
                             preferred_element_type=F32) + rb_ref[...]
    e_iota = lax.broadcasted_iota(I32, (N_EXPERTS, tm), 0)
    vals, idxs = [], []
    work = logits
    for _ in range(TOP_K):
        m = jnp.max(work, axis=0, keepdims=True)
        idx = jnp.min(jnp.where(work == m, e_iota, N_EXPERTS), axis=0, keepdims=True)
        vals.append(m)
        idxs.append(idx)
        work = jnp.where(e_iota == idx, -jnp.inf, work)
    ex = [jnp.exp(v - vals[0]) for v in vals]
    inv = 1.0 / (ex[0] + ex[1] + ex[2] + ex[3])
    topw = jnp.concatenate([e * inv for e in ex], axis=0)
    topi_ref[0] = jnp.concatenate(idxs, axis=0)

    @pl.when(pl.program_id(0) == 0)
    def _():
        cnt_ref[...] = jnp.zeros_like(cnt_ref)

    row = lax.broadcasted_iota(I32, (tm, tm), 0)
    col = lax.broadcasted_iota(I32, (tm, tm), 1)
    earlier = (row < col).astype(BF16)
    running = cnt_ref[...][:, 0:1]
    ranks = []
    for kk in range(TOP_K):
        onehot = (e_iota == idxs[kk]).astype(F32)
        before = jnp.dot(onehot.astype(BF16), earlier, preferred_element_type=F32)
        ranks.append(jnp.sum(onehot * (running + before), axis=0, keepdims=True))
        running = running + jnp.sum(onehot, axis=1, keepdims=True)
    rank_ref[0] = jnp.concatenate(ranks, axis=0).astype(I32)
    cnt_ref[...] = jnp.broadcast_to(running, cnt_ref.shape)

    wpad = jnp.concatenate(
        [jnp.broadcast_to(topw[kk:kk + 1], (SC_LANES, tm)) for kk in range(TOP_K)]
        + [jnp.zeros((LANES - TOP_K * SC_LANES, tm), F32)], axis=0)
    wtm_ref[...] = wpad.T


def _route_out_shapes(n_tok, tm):
    n_tiles = n_tok // tm
    return (
        jax.ShapeDtypeStruct((n_tok, D_MODEL), F32),
        jax.ShapeDtypeStruct((n_tok * ROW_CHUNKS, LANES), I32),
        jax.ShapeDtypeStruct((n_tiles, TOP_K, tm), I32),
        jax.ShapeDtypeStruct((n_tiles, TOP_K, tm), I32),
        jax.ShapeDtypeStruct((n_tok, LANES), F32),
        jax.ShapeDtypeStruct((N_EXPERTS, LANES), F32),
    )


def _route_out_specs(tm):
    return (
        pl.BlockSpec((tm, D_MODEL), lambda i: (i, 0)),
        pl.BlockSpec((tm * ROW_CHUNKS, LANES), lambda i: (i, 0)),
        pl.BlockSpec((1, TOP_K, tm), lambda i: (i, 0, 0)),
        pl.BlockSpec((1, TOP_K, tm), lambda i: (i, 0, 0)),
        pl.BlockSpec((tm, LANES), lambda i: (i, 0)),
        pl.BlockSpec((N_EXPERTS, LANES), lambda i: (0, 0)),
    )


def _post_mix_kernel(ot_ref, *refs, n_x, n0):
    x_refs = refs[:n_x]
    mod_ref, wo_ref, rwt_ref, rb_ref = refs[n_x:n_x + 4]
    out_refs = refs[n_x + 4:]
    tm = x_refs[0].shape[0]
    mod = mod_ref[0]
    g1 = mod[:, 2 * D_MODEL:3 * D_MODEL]
    ot = ot_ref[:, 0].reshape(N_HEADS * D_V, tm)
    mix = lax.dot_general(ot, wo_ref[...], (((0,), (0,)), ((), ())),
                          preferred_element_type=F32)
    _route_tail(_part_load(x_refs, n0) + g1 * mix, mod, rwt_ref, rb_ref, *out_refs)


def _post_mix(ot, x_parts, mod, w_o, rw_t, rb, seq_len):
    n_tok = sum(p.shape[0] for p in x_parts)
    tm = ot.shape[3]
    per_seq = seq_len // tm
    const = lambda i: (0, 0)
    return pl.pallas_call(
        functools.partial(_post_mix_kernel, n_x=len(x_parts), n0=x_parts[0].shape[0] // tm),
        out_shape=_route_out_shapes(n_tok, tm),
        grid=(n_tok // tm,),
        in_specs=[pl.BlockSpec((N_HEADS, 1, D_V, tm), lambda i: (0, i, 0, 0))]
        + _part_specs(x_parts, tm) + [
            pl.BlockSpec((1, 1, 6 * D_MODEL), lambda i: (i // per_seq, 0, 0)),
            pl.BlockSpec(w_o.shape, const),
            pl.BlockSpec(rw_t.shape, const),
            pl.BlockSpec(rb.shape, const),
        ],
        out_specs=_route_out_specs(tm),
        compiler_params=pltpu.CompilerParams(
            dimension_semantics=("arbitrary",), vmem_limit_bytes=VMEM_LIMIT),
        name="post_mix",
    )(ot, *x_parts, mod, w_o, rw_t, rb)


def _pool_mix_kernel(*refs, seq_len, pending):
    n_x = 7 if pending else 3
    x_refs = refs[:n_x]
    mod_ref, pw_ref, ps_ref, rwt_ref, rb_ref = refs[n_x:n_x + 5]
    out_refs = refs[n_x + 5:]
    tm = x_refs[0].shape[0]
    per_seq = seq_len // tm
    mod = mod_ref[0]
    sh1 = mod[:, 0:D_MODEL]
    sc1 = mod[:, D_MODEL:2 * D_MODEL]
    g1 = mod[:, 2 * D_MODEL:3 * D_MODEL]
    pos0 = (pl.program_id(0) % per_seq) * tm

    if pending:
        g2_prev = x_refs[6][0][:, 5 * D_MODEL:6 * D_MODEL]

        def layer_input(k):
            x1_ref, moe_ref = x_refs[k], x_refs[3 + k]
            return x1_ref[...] + g2_prev * _unpack_rows(moe_ref, 0, x1_ref.shape[0])
    else:
        def layer_input(k):
            return x_refs[k][...]

    def normed(v):
        return _rms(v) * (1.0 + sc1) + sh1

    x = layer_input(0)
    h = normed(x)
    h_prev = jnp.where(pos0 > 0, normed(layer_input(1)), 0.0)
    h_next = jnp.where(pos0 + tm < seq_len, normed(layer_input(2)), 0.0)
    hext = jnp.concatenate([h_prev, h, h_next], axis=0)
    n_ext = tm + 2 * POOL_HALO
    pos = pos0 + lax.broadcasted_iota(I32, (tm, 1), 0)

    outs = []
    for g, win in enumerate(POOL_WINDOWS):
        left = win // 2
        right = win - 1 - left
        cols = slice(g * POOL_GROUP, (g + 1) * POOL_GROUP)
        s = hext[:, cols]
        span = 1
        while span < win:
            s = s + pltpu.roll(s, span, 0)
            span *= 2
        if right:
            s = pltpu.roll(s, n_ext - right, 0)
        num = s[POOL_HALO:POOL_HALO + tm, :]
        count = (jnp.minimum(pos + right + 1, seq_len) - jnp.maximum(pos - left, 0)).astype(F32)
        diff = (num / count - h[:, cols]).astype(BF16)
        outs.append(jnp.dot(diff, pw_ref[g], preferred_element_type=F32))
    mix = jnp.concatenate(outs, axis=1) * ps_ref[...]
    _route_tail(x + g1 * mix, mod, rwt_ref, rb_ref, *out_refs)


def _pool_mix(x, mod, pool_w, pool_scale, rw_t, rb, seq_len):
    pending = isinstance(x, tuple)
    n_tok = (x[0] if pending else x).shape[0]
    tm = TOKEN_TILE
    per_seq = seq_len // tm
    halo_per_tile = tm // POOL_HALO
    n_halo = n_tok // POOL_HALO
    const = lambda i: (0, 0)
    seq_block = lambda i: (i // per_seq, 0, 0)
    halo_prev = lambda i: (jnp.maximum(i * halo_per_tile - 1, 0), 0)
    halo_next = lambda i: (jnp.minimum((i + 1) * halo_per_tile, n_halo - 1), 0)

    def tile_and_halos(lines_per_row, width):
        return [pl.BlockSpec((tm * lines_per_row, width), lambda i: (i, 0)),
                pl.BlockSpec((POOL_HALO * lines_per_row, width), halo_prev),
                pl.BlockSpec((POOL_HALO * lines_per_row, width), halo_next)]

    if pending:
        x1, moe, mod_prev = x
        x_args = [x1, x1, x1, moe, moe, moe, mod_prev]
        x_specs = (tile_and_halos(1, D_MODEL) + tile_and_halos(ROW_CHUNKS, LANES)
                   + [pl.BlockSpec((1, 1, 6 * D_MODEL), seq_block)])
    else:
        x_args = [x, x, x]
        x_specs = tile_and_halos(1, D_MODEL)
    return pl.pallas_call(
        functools.partial(_pool_mix_kernel, seq_len=seq_len, pending=pending),
        out_shape=_route_out_shapes(n_tok, tm),
        grid=(n_tok // tm,),
        in_specs=x_specs + [
            pl.BlockSpec((1, 1, 6 * D_MODEL), seq_block),
            pl.BlockSpec(pool_w.shape, lambda i: (0, 0, 0)),
            pl.BlockSpec(pool_scale.shape, const),
            pl.BlockSpec(rw_t.shape, const),
            pl.BlockSpec(rb.shape, const),
        ],
        out_specs=_route_out_specs(tm),
        compiler_params=pltpu.CompilerParams(
            dimension_semantics=("arbitrary",), vmem_limit_bytes=VMEM_LIMIT),
        name="pool_mix",
    )(*x_args, mod, pool_w, pool_scale, rw_t, rb)


def _slots_kernel(pstart_ref, topi_ref, rank_ref, dest_ref):
    topi = topi_ref[...]
    start = jnp.zeros_like(topi)
    for e in range(N_EXPERTS):
        start = jnp.where(topi == e, pstart_ref[e], start)
    dest_ref[...] = start + rank_ref[...]


def _slots(pstart, topi, rank):
    n_tiles, _, tm = topi.shape
    tb = math.gcd(n_tiles, 32)
    spec = pl.BlockSpec((tb, TOP_K, tm), lambda i, ps: (i, 0, 0))
    return pl.pallas_call(
        _slots_kernel,
        out_shape=jax.ShapeDtypeStruct(topi.shape, I32),
        grid_spec=pltpu.PrefetchScalarGridSpec(
            num_scalar_prefetch=1, grid=(n_tiles // tb,), in_specs=[spec, spec], out_specs=spec),
        compiler_params=pltpu.CompilerParams(dimension_semantics=("arbitrary",)),
        name="moe_slots",
    )(pstart, topi, rank)


def _sc_chunk_rows(c, tm, width=SC_ROWS):
    per_tile = tm // width
    tile = c // per_tile
    part = c % per_tile
    return [(tile * TOP_K + kk) * per_tile + part for kk in range(TOP_K)]


def _sc_dispatch(hp, dest, n_slots, tm):
    hp = hp.reshape((-1,) + ROW_SHAPE)
    n_tok = hp.shape[0]
    rows_per_w = dest.shape[0] // SC_WORKERS
    chunks_per_w = n_tok // SC_WORKERS // SC_ROWS
    mesh = plsc.VectorSubcoreMesh(core_axis_name="c", subcore_axis_name="s")

    @functools.partial(
        pl.kernel, mesh=mesh,
        out_type=jax.ShapeDtypeStruct((n_slots,) + ROW_SHAPE, I32),
        scratch_types=[
            pltpu.VMEM((rows_per_w, SC_ROWS), I32),
            pltpu.VMEM((SC_ROWS,) + ROW_SHAPE, I32),
            pltpu.SemaphoreType.DMA,
        ],
        name="sc_dispatch",
    )
    def run(hp_hbm, dest_hbm, xs_hbm, dest_v, rows_v, sem):
        wid = lax.axis_index("s") * SC_CORES + lax.axis_index("c")
        pltpu.sync_copy(dest_hbm.at[pl.ds(wid * rows_per_w, rows_per_w)], dest_v)

        @pl.loop(0, chunks_per_w)
        def _(c):
            tok0 = (wid * chunks_per_w + c) * SC_ROWS
            pltpu.sync_copy(hp_hbm.at[pl.ds(tok0, SC_ROWS)], rows_v)
            copies = [pltpu.async_copy(rows_v, xs_hbm.at[dest_v.at[row]], sem)
                      for row in _sc_chunk_rows(c, tm)]
            for cp in copies:
                cp.wait()

    return run(hp, dest).reshape(n_slots * ROW_CHUNKS, LANES)


def _sc_gather_sum(ys, dest, wtm, tm):
    ys = ys.reshape((-1,) + ROW_SHAPE)
    n_tok = wtm.shape[0]
    g = SC_SUM_ROWS
    rows_per_w = dest.shape[0] // SC_WORKERS
    chunks_per_w = n_tok // SC_WORKERS // g
    mesh = plsc.VectorSubcoreMesh(core_axis_name="c", subcore_axis_name="s")

    @functools.partial(
        pl.kernel, mesh=mesh,
        out_type=jax.ShapeDtypeStruct((n_tok,) + ROW_SHAPE, I32),
        scratch_types=[
            pltpu.VMEM((rows_per_w, g), I32),
            pltpu.VMEM((TOP_K, g) + ROW_SHAPE, I32),
            pltpu.VMEM((g, LANES), F32),
            pltpu.VMEM((g,) + ROW_SHAPE, I32),
            pltpu.SemaphoreType.DMA,
        ],
        compiler_params=pltpu.CompilerParams(needs_layout_passes=False),
        name="sc_gather_sum",
    )
    def run(ys_hbm, dest_hbm, w_hbm, out_hbm, dest_v, rows_v, w_v, out_v, sem):
        wid = lax.axis_index("s") * SC_CORES + lax.axis_index("c")
        pltpu.sync_copy(dest_hbm.at[pl.ds(wid * rows_per_w, rows_per_w)], dest_v)

        @pl.loop(0, chunks_per_w)
        def _(c):
            tok0 = (wid * chunks_per_w + c) * g
            copies = [pltpu.async_copy(ys_hbm.at[dest_v.at[row]], rows_v.at[kk], sem)
                      for kk, row in enumerate(_sc_chunk_rows(c, tm, g))]
            pltpu.sync_copy(w_hbm.at[pl.ds(tok0, g)], w_v)
            for cp in copies:
                cp.wait()

            @pl.loop(0, g)
            def _(t):
                wk = [w_v[t, pl.ds(SC_LANES * kk, SC_LANES)] for kk in range(TOP_K)]
                for j in range(ROW_CHUNKS):
                    for i in range(LANES // SC_LANES):
                        lanes = pl.ds(SC_LANES * i, SC_LANES)
                        lo = hi = None
                        for kk in range(TOP_K):
                            word = rows_v[kk, t, j, lanes]
                            a = lax.bitcast_convert_type(lax.shift_left(word, 16), F32) * wk[kk]
                            b = lax.bitcast_convert_type(word & jnp.int32(-65536), F32) * wk[kk]
                            lo = a if lo is None else lo + a
                            hi = b if hi is None else hi + b
                        packed = plsc.pack(lo, hi, format=plsc.PackFormat.INTERLEAVED)
                        out_v[t, j, lanes] = plsc.bitcast(packed, I32)

            pltpu.sync_copy(out_v, out_hbm.at[pl.ds(tok0, g)])

    return run(ys, dest, wtm).reshape(n_tok * ROW_CHUNKS, LANES)


def _expert_kernel(e_ref, first_ref, slot_ref, next_ref, nused_ref,
                   xs_ref, w1_hbm, b1_ref, w2_hbm, b2_ref, ys_ref,
                   w1f_ref, w2f_ref, w1b_ref, w2b_ref, sem, *, layer):
    i = pl.program_id(0)

    def weight_copies(expert, slot):
        return (pltpu.make_async_copy(w1_hbm.at[layer, expert], w1f_ref.at[slot], sem.at[slot, 0]),
                pltpu.make_async_copy(w2_hbm.at[layer, expert], w2f_ref.at[slot], sem.at[slot, 1]))

    @pl.when(first_ref[i] == 1)
    def _():
        slot = slot_ref[i]

        @pl.when(i == 0)
        def _():
            for cp in weight_copies(e_ref[i], slot):
                cp.start()

        for cp in weight_copies(e_ref[i], slot):
            cp.wait()
        w1b_ref[...] = w1f_ref[slot].astype(BF16)
        w2b_ref[...] = w2f_ref[slot].astype(BF16)

        @pl.when(next_ref[i] >= 0)
        def _():
            for cp in weight_copies(next_ref[i], 1 - slot):
                cp.start()

    @pl.when(i < nused_ref[0])
    def _():
        for r in range(0, xs_ref.shape[0] // ROW_CHUNKS, EXPERT_SUB):
            xb = _unpack_rows(xs_ref, r, EXPERT_SUB).astype(BF16)
            gu = jnp.dot(xb, w1b_ref[...], preferred_element_type=F32) + b1_ref[0, 0]
            gate = jnp.minimum(gu[:, :D_FF], SWIGLU_LIMIT)
            up = jnp.clip(gu[:, D_FF:], -SWIGLU_LIMIT, SWIGLU_LIMIT)
            act = (up + 1.0) * (gate * jax.nn.sigmoid(SWIGLU_ALPHA * gate))
            y = jnp.dot(act.astype(BF16), w2b_ref[...], preferred_element_type=F32) + b2_ref[0, 0]
            _pack_rows(y, ys_ref, r)


def _expert_plan(pend, n_blk, bm):
    blk = jnp.arange(n_blk, dtype=I32)
    n_used = (pend[-1] // bm).astype(I32)
    expert = jnp.minimum(jnp.sum(blk[:, None] * bm >= pend[None, :], axis=1), N_EXPERTS - 1)
    expert = expert.astype(I32)
    prev = jnp.concatenate([jnp.full((1,), -1, I32), expert[:-1]])
    first = (blk < n_used) & (expert != prev)
    slot = (jnp.cumsum(first.astype(I32)) - 1) % 2
    later_first = first[None, :] & (blk[None, :] > blk[:, None])
    nxt = jnp.where(jnp.any(later_first, axis=1), expert[jnp.argmax(later_first, axis=1)], -1)
    return (expert, first.astype(I32), slot.astype(I32), nxt.astype(I32), n_used.reshape(1))


def _expert_ffn(plan, xs, layer, w1, b1, w2, b2):
    n_slots = xs.shape[0] // ROW_CHUNKS
    bm = EXPERT_TILE
    used_block = lambda i, e, f, s, nx, nu: (jnp.minimum(i, nu[0] - 1), 0)
    bias_block = lambda i, e, f, s, nx, nu: (layer, e[i], 0, 0)
    return pl.pallas_call(
        functools.partial(_expert_kernel, layer=layer),
        out_shape=jax.ShapeDtypeStruct(xs.shape, I32),
        grid_spec=pltpu.PrefetchScalarGridSpec(
            num_scalar_prefetch=5,
            grid=(n_slots // bm,),
            in_specs=[
                pl.BlockSpec((bm * ROW_CHUNKS, LANES), used_block),
                pl.BlockSpec(memory_space=pl.ANY),
                pl.BlockSpec((1, 1, 1, 2 * D_FF), bias_block),
                pl.BlockSpec(memory_space=pl.ANY),
                pl.BlockSpec((1, 1, 1, D_MODEL), bias_block),
            ],
            out_specs=pl.BlockSpec((bm * ROW_CHUNKS, LANES), used_block),
            scratch_shapes=[
                pltpu.VMEM((2, D_MODEL, 2 * D_FF), F32), pltpu.VMEM((2, D_FF, D_MODEL), F32),
                pltpu.VMEM((D_MODEL, 2 * D_FF), BF16), pltpu.VMEM((D_FF, D_MODEL), BF16),
                pltpu.SemaphoreType.DMA((2, 2)),
            ],
        ),
        compiler_params=pltpu.CompilerParams(
            dimension_semantics=("arbitrary",), vmem_limit_bytes=VMEM_LIMIT),
        name="expert_ffn",
    )(*plan, xs, w1, b1, w2, b2)


def _combine_kernel(moe_ref, x1_ref, mod_ref, fn_ref, o_ref, *, final):
    tm = x1_ref.shape[0]
    g2 = mod_ref[0][:, 5 * D_MODEL:6 * D_MODEL]
    out = x1_ref[...] + g2 * _unpack_rows(moe_ref, 0, tm)
    if final:
        out = _rms(out) * fn_ref[...]
    o_ref[...] = out


def _combine(moe, x1, mod, final_norm, seq_len, final):
    n_tok = x1.shape[0]
    tm = COMBINE_TILE
    per_seq = seq_len // tm
    return pl.pallas_call(
        functools.partial(_combine_kernel, final=final),
        out_shape=jax.ShapeDtypeStruct((n_tok, D_MODEL), F32),
        grid=(n_tok // tm,),
        in_specs=[
            pl.BlockSpec((tm * ROW_CHUNKS, LANES), lambda i: (i, 0)),
            pl.BlockSpec((tm, D_MODEL), lambda i: (i, 0)),
            pl.BlockSpec((1, 1, 6 * D_MODEL), lambda i: (i // per_seq, 0, 0)),
            pl.BlockSpec((1, D_MODEL), lambda i: (0, 0)),
        ],
        out_specs=pl.BlockSpec((tm, D_MODEL), lambda i: (i, 0)),
        compiler_params=pltpu.CompilerParams(
            dimension_semantics=("arbitrary",), vmem_limit_bytes=VMEM_LIMIT),
        name="moe_combine",
    )(moe, x1, mod, final_norm)


def _moe_layers(routes, mods, layer, ffn_w, final_norm, seq_len, final, defer_combine):
    bm = EXPERT_TILE
    plans = []
    for x1, hp, topi, rank, wtm, counts in routes:
        n_slots = x1.shape[0] * TOP_K + N_EXPERTS * bm
        n_blk = n_slots // bm
        cnt = counts[:, 0].astype(I32)
        padded = (cnt + bm - 1) // bm * bm
        pend = jnp.cumsum(padded)
        pstart = (pend - padded).astype(I32)
        dest = _slots(pstart, topi, rank).reshape(-1, SC_ROWS)
        plans.append((n_slots, _expert_plan(pend, n_blk, bm), dest, topi.shape[2]))
    xs = [_sc_dispatch(r[1], dest, n_slots, tm)
          for r, (n_slots, _, dest, tm) in zip(routes, plans)]
    ys = [_expert_ffn(plan, x, layer, *ffn_w) for x, (_, plan, _, _) in zip(xs, plans)]
    moe = [_sc_gather_sum(y, dest.reshape(-1, SC_SUM_ROWS), r[4], tm)
           for y, r, (_, _, dest, tm) in zip(ys, routes, plans)]
    if defer_combine:
        return [(r[0], m, mod) for m, r, mod in zip(moe, routes, mods)]
    return [_combine(m, r[0], mod, final_norm, seq_len, final)
            for m, r, mod in zip(moe, routes, mods)]


def _rope_tables(seq_len):
    inv_freq = 1.0 / (ROPE_THETA ** (jnp.arange(0, D_ROPE, 2, dtype=F32) / D_ROPE))
    ang = jnp.arange(seq_len, dtype=F32)[:, None] * inv_freq[None, :]
    cos, sin = jnp.cos(ang), jnp.sin(ang)
    ones = jnp.ones((seq_len, ROPE_LO), F32)
    zeros_lo = jnp.zeros((seq_len, ROPE_LO), F32)
    zeros_hi = jnp.zeros((seq_len, HEAD_PAD - ROPE_LO - D_ROPE), F32)
    zeros_h = jnp.zeros((seq_len, ROPE_HALF), F32)
    rope_c = jnp.concatenate([ones, cos, cos, zeros_hi], axis=1)
    rope_s1 = jnp.concatenate([zeros_lo, -sin, zeros_h, zeros_hi], axis=1)
    rope_s2 = jnp.concatenate([zeros_lo, zeros_h, sin, zeros_hi], axis=1)
    return cos.T, sin.T, rope_c, rope_s1, rope_s2


def _mla_weights(w_in, q_norm, kv_norm, w_uq, w_ukv, w_o, seq_len):
    d_qk = D_NOPE + D_ROPE
    q_scale = d_qk ** -0.5 * math.log2(math.e)
    pad_pe = jnp.zeros((D_MODEL, HEAD_PAD), F32).at[:, ROPE_LO:ROPE_LO + D_ROPE].set(
        w_in[:, Q_RANK + KV_RANK:])
    w_in_p = jnp.concatenate([w_in[:, :Q_RANK + KV_RANK], pad_pe], axis=1)
    w_uq_p = jnp.pad(w_uq.reshape(Q_RANK, N_HEADS, d_qk) * q_scale,
                     ((0, 0), (0, 0), (0, HEAD_PAD - d_qk)))
    w_kv = w_ukv.reshape(KV_RANK, N_HEADS, D_NOPE + D_V)
    w_uk_p = jnp.pad(w_kv[:, :, :D_NOPE], ((0, 0), (0, 0), (0, HEAD_PAD - D_NOPE)))
    cos_t, sin_t, rope_c, rope_s1, rope_s2 = _rope_tables(seq_len)
    return {
        "w_in": w_in_p.astype(BF16),
        "q_norm": q_norm.reshape(1, Q_RANK),
        "kv_norm": kv_norm.reshape(1, KV_RANK),
        "w_uq_t": w_uq_p.reshape(Q_RANK, N_HEADS * HEAD_PAD).T.astype(BF16),
        "w_uk": w_uk_p.reshape(KV_RANK, N_HEADS * HEAD_PAD).astype(BF16),
        "w_uv_t": w_kv[:, :, D_NOPE:].reshape(KV_RANK, N_HEADS * D_V).T.astype(BF16),
        "w_o": w_o.astype(BF16),
        "cos_t": cos_t, "sin_t": sin_t, "rope_c": rope_c, "rope_s1": rope_s1, "rope_s2": rope_s2,
    }


def _router_weights(router_w, router_b):
    return router_w.T.astype(BF16), router_b.reshape(N_EXPERTS, 1)


def kernel(x_prompt, x_sample, c_prompt, c_sample, ada_w, ada_b, mla_w_in, mla_q_norm,
           mla_kv_norm, mla_w_uq, mla_w_ukv, mla_w_o, pool_w, pool_scale, router_w, router_b,
           moe_w1, moe_b1, moe_w2, moe_b2, final_norm):
    n_prompt, seq_len, _ = x_prompt.shape
    assert x_sample.shape[1] == seq_len and seq_len % TOKEN_TILE == 0
    depth = ada_w.shape[0]
    n_sample = x_sample.shape[0]
    xs = [x_prompt.reshape(-1, D_MODEL), x_sample.reshape(-1, D_MODEL)]
    n_seqs = [n_prompt, n_sample]
    mods = _ada_mod(jnp.concatenate([c_prompt, c_sample], axis=0), ada_w, ada_b)
    fnorm = final_norm.reshape(1, D_MODEL)
    ffn_w = (moe_w1, moe_b1.reshape(depth, N_EXPERTS, 1, 2 * D_FF),
             moe_w2, moe_b2.reshape(depth, N_EXPERTS, 1, D_MODEL))

    for i in range(depth):
        rw_t, rb = _router_weights(router_w[i], router_b[i])
        j = i // 2
        group_mods = [mods[i, :n_prompt].reshape(n_prompt, 1, 6 * D_MODEL),
                      mods[i, n_prompt:].reshape(n_sample, 1, 6 * D_MODEL)]
        if i % 2 == 0:
            w = _mla_weights(mla_w_in[j], mla_q_norm[j], mla_kv_norm[j], mla_w_uq[j],
                             mla_w_ukv[j], mla_w_o[j], seq_len)
        routes = []
        for x, mod in zip(xs, group_mods):
            if i % 2 == 0:
                qt, k, vt = _mla_pre([x], mod, w, seq_len)
                ot = _attention(qt, k, vt, seq_len)
                routes.append(_post_mix(ot, [x], mod, w["w_o"], rw_t, rb, seq_len))
            else:
                routes.append(_pool_mix(x, mod, pool_w[j].astype(BF16),
                                        pool_scale[j].reshape(1, D_MODEL), rw_t, rb, seq_len))
        last = i == depth - 1
        next_is_pool = not last and (i + 1) % 2 == 1
        xs = _moe_layers(routes, group_mods, i, ffn_w, fnorm, seq_len, last, next_is_pool)

    return (xs[0].reshape(n_prompt, seq_len, D_MODEL), xs[1].reshape(n_sample, seq_len, D_MODEL))
```

```python
import functools
import math

import jax
import jax.numpy as jnp
from jax import lax
from jax.experimental import pallas as pl
from jax.experimental.pallas import tpu as pltpu
from jax.experimental.pallas import tpu_sc as plsc

F32 = jnp.float32
BF16 = jnp.bfloat16
I32 = jnp.int32

D_MODEL = 1024
N_HEADS = 16
Q_RANK = 384
KV_RANK = 256
D_NOPE = 64
D_ROPE = 32
D_V = 64
V_ROWS = D_V + 16
ROPE_THETA = 10000.0
POOL_WINDOWS = (2, 4, 8, 16)
POOL_GROUP = D_MODEL // len(POOL_WINDOWS)
N_EXPERTS = 32
TOP_K = 4
D_FF = D_MODEL
SWIGLU_LIMIT = 7.0
SWIGLU_ALPHA = 1.702
EPS = 1e-6

LANES = 128
SUBLANES = 8
HEAD_PAD = 128
ROPE_LO = D_NOPE
ROPE_HALF = D_ROPE // 2
VMEM_LIMIT = 56 * 1024 * 1024

TOKEN_TILE = 256
COMBINE_TILE = 512
ROUTE_SUBTILES = 2
EXPERT_TILE = 512
EXPERT_SUB = 256
ATTN_KEY_CHUNKS = 4
ATTN_HEADS_PER_STEP = 8
POOL_HALO = 8
PACKED = D_MODEL // 2
ROW_CHUNKS = PACKED // LANES
ROW_SHAPE = (ROW_CHUNKS, LANES)

SC_CORES = 2
SC_SUBCORES = 16
SC_WORKERS = SC_CORES * SC_SUBCORES
SC_LANES = 16
SC_ROWS = 64
SC_SUM_ROWS = 32


def _rms(x):
    return x * lax.rsqrt(jnp.mean(x * x, axis=-1, keepdims=True) + EPS)


def _pack_rows(y, out_ref, row0=0):
    n = y.shape[0]
    lo = lax.bitcast_convert_type(y[:, :PACKED].astype(BF16).astype(F32), I32)
    hi = lax.bitcast_convert_type(y[:, PACKED:].astype(BF16).astype(F32), I32)
    words = lax.shift_right_logical(lo, 16) | (hi & jnp.int32(-65536))
    for j in range(ROW_CHUNKS):
        out_ref[pl.ds(row0 * ROW_CHUNKS + j, n, stride=ROW_CHUNKS), :] = (
            words[:, j * LANES:(j + 1) * LANES])


def _unpack_rows(in_ref, row0, n):
    words = jnp.concatenate(
        [in_ref[pl.ds(row0 * ROW_CHUNKS + j, n, stride=ROW_CHUNKS), :] for j in range(ROW_CHUNKS)],
        axis=1)
    lo = lax.bitcast_convert_type(lax.shift_left(words, 16), F32)
    hi = lax.bitcast_convert_type(words & jnp.int32(-65536), F32)
    return jnp.concatenate([lo, hi], axis=1)


def _part_specs(parts, tm):
    if len(parts) == 1:
        return [pl.BlockSpec((tm, D_MODEL), lambda i: (i, 0))]
    n0 = parts[0].shape[0] // tm
    return [pl.BlockSpec((tm, D_MODEL), lambda i: (jnp.minimum(i, n0 - 1), 0)),
            pl.BlockSpec((tm, D_MODEL), lambda i: (jnp.maximum(i - n0, 0), 0))]


def _part_load(refs, n0):
    if len(refs) == 1:
        return refs[0][...]
    return jnp.where(pl.program_id(0) < n0, refs[0][...], refs[1][...])


def _ada_kernel(c_ref, w_ref, b_ref, o_ref):
    c = c_ref[...]
    act = (c * jax.nn.sigmoid(c)).astype(BF16)
    o_ref[0] = jnp.dot(act, w_ref[0].astype(BF16), preferred_element_type=F32) + b_ref[0]


def _ada_mod(c, ada_w, ada_b):
    depth, _, n_out = ada_w.shape
    n_seq = c.shape[0]
    tn = 1536
    return pl.pallas_call(
        _ada_kernel,
        out_shape=jax.ShapeDtypeStruct((depth, n_seq, n_out), F32),
        grid=(depth, n_out // tn),
        in_specs=[
            pl.BlockSpec((n_seq, D_MODEL), lambda l, j: (0, 0)),
            pl.BlockSpec((1, D_MODEL, tn), lambda l, j: (l, 0, j)),
            pl.BlockSpec((1, 1, tn), lambda l, j: (l, 0, j)),
        ],
        out_specs=pl.BlockSpec((1, n_seq, tn), lambda l, j: (l, 0, j)),
        compiler_params=pltpu.CompilerParams(
            dimension_semantics=("arbitrary", "arbitrary"), vmem_limit_bytes=VMEM_LIMIT),
        name="ada_mod",
    )(c, ada_w, ada_b.reshape(depth, 1, n_out))


def _mla_pre_kernel(*refs, n_x, n0):
    x_refs = refs[:n_x]
    (mod_ref, win_ref, qn_ref, kvn_ref, wuqt_ref, wuk_ref, wuvt_ref,
     cost_ref, sint_ref, ck_ref, s1k_ref, s2k_ref, qt_ref, k_ref, vt_ref) = refs[n_x:]
    n_sub, ts = qt_ref.shape[1], qt_ref.shape[3]
    mod = mod_ref[0]
    sh1 = mod[:, 0:D_MODEL]
    sc1 = mod[:, D_MODEL:2 * D_MODEL]
    x = _part_load(x_refs, n0)
    d_qk = D_NOPE + D_ROPE

    def chain(sub):
        rows = slice(sub * ts, (sub + 1) * ts)
        h = (_rms(x[rows, :]) * (1.0 + sc1) + sh1).astype(BF16)
        a = jnp.dot(h, win_ref[...], preferred_element_type=F32)
        yield
        cq = (_rms(a[:, :Q_RANK]) * qn_ref[...]).astype(BF16)
        ckv = (_rms(a[:, Q_RANK:Q_RANK + KV_RANK]) * kvn_ref[...]).astype(BF16)
        kpe = a[:, Q_RANK + KV_RANK:]
        kpe = (kpe * ck_ref[rows, :]
               + pltpu.roll(kpe, LANES - ROPE_HALF, 1) * s1k_ref[rows, :]
               + pltpu.roll(kpe, ROPE_HALF, 1) * s2k_ref[rows, :])
        yield
        qt = lax.dot_general(wuqt_ref[...], cq, (((1,), (1,)), ((), ())),
                             preferred_element_type=F32)
        q3 = qt.reshape(N_HEADS, d_qk, ts)
        x1 = q3[:, ROPE_LO:ROPE_LO + ROPE_HALF, :]
        x2 = q3[:, ROPE_LO + ROPE_HALF:, :]
        cos = cost_ref[:, rows][None]
        sin = sint_ref[:, rows][None]
        q3 = jnp.concatenate(
            [q3[:, :ROPE_LO, :], x1 * cos - x2 * sin, x2 * cos + x1 * sin], axis=1)
        qt_ref[:, sub, :d_qk, :] = q3.astype(BF16)
        qt_ref[:, sub, d_qk:, :] = jnp.zeros((N_HEADS, HEAD_PAD - d_qk, ts), BF16)
        yield
        kn = jnp.dot(ckv, wuk_ref[...], preferred_element_type=F32)
        for hd in range(N_HEADS):
            k_ref[hd, rows, :] = (kn[:, hd * HEAD_PAD:(hd + 1) * HEAD_PAD] + kpe).astype(BF16)
        yield
        vt = lax.dot_general(wuvt_ref[...], ckv, (((1,), (1,)), ((), ())),
                             preferred_element_type=F32)
        vt_ref[:, 0, :D_V, rows] = vt.reshape(N_HEADS, D_V, ts).astype(BF16)
        vt_ref[:, 0, D_V:, rows] = jnp.ones((N_HEADS, V_ROWS - D_V, ts), BF16)

    _interleave(chain(sub) for sub in range(n_sub))


def _mla_pre(x_parts, mod, w, seq_len):
    n_tok = sum(p.shape[0] for p in x_parts)
    n_sub = ROUTE_SUBTILES
    tq = TOKEN_TILE
    tm = n_sub * tq
    per_seq = seq_len // tm
    const = lambda i: (0, 0)
    return pl.pallas_call(
        functools.partial(_mla_pre_kernel, n_x=len(x_parts), n0=x_parts[0].shape[0] // tm),
        out_shape=(
            jax.ShapeDtypeStruct((N_HEADS, n_tok // tq, HEAD_PAD, tq), BF16),
            jax.ShapeDtypeStruct((N_HEADS, n_tok, HEAD_PAD), BF16),
            jax.ShapeDtypeStruct((N_HEADS, n_tok // seq_len, V_ROWS, seq_len), BF16),
        ),
        grid=(n_tok // tm,),
        in_specs=_part_specs(x_parts, tm) + [
            pl.BlockSpec((1, 1, 6 * D_MODEL), lambda i: (i // per_seq, 0, 0)),
            pl.BlockSpec(w["w_in"].shape, const),
            pl.BlockSpec(w["q_norm"].shape, const),
            pl.BlockSpec(w["kv_norm"].shape, const),
            pl.BlockSpec(w["w_uq_t"].shape, const),
            pl.BlockSpec(w["w_uk"].shape, const),
            pl.BlockSpec(w["w_uv_t"].shape, const),
            pl.BlockSpec((ROPE_HALF, tm), lambda i: (0, i % per_seq)),
            pl.BlockSpec((ROPE_HALF, tm), lambda i: (0, i % per_seq)),
            pl.BlockSpec((tm, LANES), lambda i: (i % per_seq, 0)),
            pl.BlockSpec((tm, LANES), lambda i: (i % per_seq, 0)),
            pl.BlockSpec((tm, LANES), lambda i: (i % per_seq, 0)),
        ],
        out_specs=(
            pl.BlockSpec((N_HEADS, n_sub, HEAD_PAD, tq), lambda i: (0, i, 0, 0)),
            pl.BlockSpec((N_HEADS, tm, HEAD_PAD), lambda i: (0, i, 0)),
            pl.BlockSpec((N_HEADS, 1, V_ROWS, tm), lambda i: (0, i // per_seq, 0, i % per_seq)),
        ),
        compiler_params=pltpu.CompilerParams(
            dimension_semantics=("arbitrary",), vmem_limit_bytes=VMEM_LIMIT),
        name="mla_pre",
    )(*x_parts, mod, w["w_in"], w["q_norm"], w["kv_norm"], w["w_uq_t"], w["w_uk"], w["w_uv_t"],
      w["cos_t"], w["sin_t"], w["rope_c"], w["rope_s1"], w["rope_s2"])


def _attention_kernel(qt_ref, k_ref, vt_ref, ot_ref, s0_ref, s1_ref):
    n_heads, n_q = qt_ref.shape[:2]
    n_tiles = n_heads * n_q
    n_keys = k_ref.shape[1]
    kc = n_keys // ATTN_KEY_CHUNKS

    def stage(t_next, s_next_ref, t_cur, s_cur_ref, m_cur):
        m_next, o = None, None
        if t_next is not None:
            h_next, j_next = t_next // n_q, t_next % n_q
        if t_cur is not None:
            h_cur, j_cur = t_cur // n_q, t_cur % n_q
        for c in range(ATTN_KEY_CHUNKS):
            rows = slice(c * kc, (c + 1) * kc)
            if t_next is not None:
                s = jnp.dot(k_ref[h_next, rows, :], qt_ref[h_next, j_next],
                            preferred_element_type=F32)
                s_next_ref[rows, :] = s
                cm = jnp.max(s, axis=0, keepdims=True)
                m_next = cm if m_next is None else jnp.maximum(m_next, cm)
            if t_cur is not None:
                p = jnp.exp2(s_cur_ref[rows, :] - m_cur).astype(BF16)
                part = jnp.dot(vt_ref[h_cur, 0, :, rows], p,
                               preferred_element_type=F32)
                o = part if o is None else o + part
        if t_cur is not None:
            denom = o[D_V:D_V + 1, :]
            ot_ref[h_cur, j_cur] = (o[:D_V, :] * (1.0 / denom)).astype(BF16)
        return m_next

    def body(i, m0):
        t = 2 * i
        m1 = stage(t + 1, s1_ref, t, s0_ref, m0)
        return stage(t + 2, s0_ref, t + 1, s1_ref, m1)

    m0 = lax.fori_loop(0, n_tiles // 2 - 1, body, stage(0, s0_ref, None, None, None))
    m1 = stage(n_tiles - 1, s1_ref, n_tiles - 2, s0_ref, m0)
    stage(None, None, n_tiles - 1, s1_ref, m1)


def _attention(qt, k, vt, seq_len):
    n_heads, n_tiles, _, tq = qt.shape
    per_seq = seq_len // tq
    n_seq = n_tiles // per_seq
    hb = ATTN_HEADS_PER_STEP
    return pl.pallas_call(
        _attention_kernel,
        out_shape=jax.ShapeDtypeStruct((n_heads, n_tiles, D_V, tq), BF16),
        grid=(n_seq, n_heads // hb),
        in_specs=[
            pl.BlockSpec((hb, per_seq, HEAD_PAD, tq), lambda b, h: (h, b, 0, 0)),
            pl.BlockSpec((hb, seq_len, HEAD_PAD), lambda b, h: (h, b, 0)),
            pl.BlockSpec((hb, 1, V_ROWS, seq_len), lambda b, h: (h, b, 0, 0)),
        ],
        out_specs=pl.BlockSpec((hb, per_seq, D_V, tq), lambda b, h: (h, b, 0, 0)),
        scratch_shapes=[pltpu.VMEM((seq_len, tq), F32), pltpu.VMEM((seq_len, tq), F32)],
        compiler_params=pltpu.CompilerParams(
            dimension_semantics=("arbitrary", "arbitrary"), vmem_limit_bytes=VMEM_LIMIT),
        name="attention",
    )(qt, k, vt)


def _route_init(cnt_ref):
    @pl.when(pl.program_id(0) == 0)
    def _():
        cnt_ref[...] = jnp.zeros_like(cnt_ref)


def _interleave(chains):
    chains = list(chains)
    done = object()
    while chains:
        chains = [ch for ch in chains if next(ch, done) is not done]


def _route_tail(x1_chains, mod, rwt_ref, rb_ref, out_refs):
    cnt_ref = out_refs[-1]
    state = {"running": cnt_ref[...][:, 0:1]}

    def chain(sub, x1_chain):
        x1 = yield from x1_chain
        yield from _route_chain(sub, x1, mod, rwt_ref, rb_ref, out_refs, state)

    _interleave(chain(sub, ch) for sub, ch in enumerate(x1_chains))
    cnt_ref[...] = jnp.broadcast_to(state["running"], cnt_ref.shape)


def _route_chain(sub, x1, mod, rwt_ref, rb_ref, out_refs, state):
    x1_ref, hp_ref, topi_ref, rank_ref, wtm_ref, _ = out_refs
    tm = x1.shape[0]
    rows = pl.ds(sub * tm, tm)
    sh2 = mod[:, 3 * D_MODEL:4 * D_MODEL]
    sc2 = mod[:, 4 * D_MODEL:5 * D_MODEL]
    x1_ref[rows, :] = x1
    h2 = _rms(x1) * (1.0 + sc2) + sh2
    _pack_rows(h2, hp_ref, sub * tm)
    logits = lax.dot_general(rwt_ref[...], h2.astype(BF16), (((1,), (1,)), ((), ())),
                             preferred_element_type=F32) + rb_ref[...]
    yield
    e_iota = lax.broadcasted_iota(I32, (N_EXPERTS, tm), 0)
    vals, idxs = [], []
    work = logits
    for _ in range(TOP_K):
        m = jnp.max(work, axis=0, keepdims=True)
        idx = jnp.min(jnp.where(work == m, e_iota, N_EXPERTS), axis=0, keepdims=True)
        vals.append(m)
        idxs.append(idx)
        work = jnp.where(e_iota == idx, -jnp.inf, work)
    ex = [jnp.exp(v - vals[0]) for v in vals]
    inv = 1.0 / (ex[0] + ex[1] + ex[2] + ex[3])
    topw = jnp.concatenate([e * inv for e in ex], axis=0)
    topi_ref[sub] = jnp.concatenate(idxs, axis=0)
    yield

    row = lax.broadcasted_iota(I32, (tm, tm), 0)
    col = lax.broadcasted_iota(I32, (tm, tm), 1)
    earlier = (row < col).astype(BF16)
    running = state["running"]
    ranks = []
    for kk in range(TOP_K):
        onehot = (e_iota == idxs[kk]).astype(F32)
        before = jnp.dot(onehot.astype(BF16), earlier, preferred_element_type=F32)
        ranks.append(jnp.sum(onehot * (running + before), axis=0, keepdims=True))
        running = running + jnp.sum(onehot, axis=1, keepdims=True)
    rank_ref[sub] = jnp.concatenate(ranks, axis=0).astype(I32)
    state["running"] = running
    yield

    wpad = jnp.concatenate(
        [jnp.broadcast_to(topw[kk:kk + 1], (SC_LANES, tm)) for kk in range(TOP_K)]
        + [jnp.zeros((LANES - TOP_K * SC_LANES, tm), F32)], axis=0)
    wtm_ref[rows, :] = wpad.T


def _route_out_shapes(n_tok):
    n_tiles = n_tok // TOKEN_TILE
    return (
        jax.ShapeDtypeStruct((n_tok, D_MODEL), F32),
        jax.ShapeDtypeStruct((n_tok * ROW_CHUNKS, LANES), I32),
        jax.ShapeDtypeStruct((n_tiles, TOP_K, TOKEN_TILE), I32),
        jax.ShapeDtypeStruct((n_tiles, TOP_K, TOKEN_TILE), I32),
        jax.ShapeDtypeStruct((n_tok, LANES), F32),
        jax.ShapeDtypeStruct((N_EXPERTS, LANES), F32),
    )


def _route_out_specs(n_sub):
    tm = n_sub * TOKEN_TILE
    return (
        pl.BlockSpec((tm, D_MODEL), lambda i: (i, 0)),
        pl.BlockSpec((tm * ROW_CHUNKS, LANES), lambda i: (i, 0)),
        pl.BlockSpec((n_sub, TOP_K, TOKEN_TILE), lambda i: (i, 0, 0)),
        pl.BlockSpec((n_sub, TOP_K, TOKEN_TILE), lambda i: (i, 0, 0)),
        pl.BlockSpec((tm, LANES), lambda i: (i, 0)),
        pl.BlockSpec((N_EXPERTS, LANES), lambda i: (0, 0)),
    )


def _post_mix_kernel(ot_ref, *refs, n_x, n0):
    x_refs = refs[:n_x]
    mod_ref, wo_ref, rwt_ref, rb_ref = refs[n_x:n_x + 4]
    out_refs = refs[n_x + 4:]
    _route_init(out_refs[-1])
    n_sub, ts = ot_ref.shape[1], ot_ref.shape[3]
    mod = mod_ref[0]
    g1 = mod[:, 2 * D_MODEL:3 * D_MODEL]
    x = _part_load(x_refs, n0)

    def mixer(sub):
        ot = ot_ref[:, sub].reshape(N_HEADS * D_V, ts)
        mix = lax.dot_general(ot, wo_ref[...], (((0,), (0,)), ((), ())),
                              preferred_element_type=F32)
        x1 = x[sub * ts:(sub + 1) * ts, :] + g1 * mix
        yield
        return x1

    _route_tail([mixer(sub) for sub in range(n_sub)], mod, rwt_ref, rb_ref, out_refs)


def _post_mix(ot, x_parts, mod, w_o, rw_t, rb, seq_len):
    n_tok = sum(p.shape[0] for p in x_parts)
    n_sub = ROUTE_SUBTILES
    tm = n_sub * ot.shape[3]
    per_seq = seq_len // tm
    const = lambda i: (0, 0)
    return pl.pallas_call(
        functools.partial(_post_mix_kernel, n_x=len(x_parts), n0=x_parts[0].shape[0] // tm),
        out_shape=_route_out_shapes(n_tok),
        grid=(n_tok // tm,),
        in_specs=[pl.BlockSpec((N_HEADS, n_sub, D_V, ot.shape[3]), lambda i: (0, i, 0, 0))]
        + _part_specs(x_parts, tm) + [
            pl.BlockSpec((1, 1, 6 * D_MODEL), lambda i: (i // per_seq, 0, 0)),
            pl.BlockSpec(w_o.shape, const),
            pl.BlockSpec(rw_t.shape, const),
            pl.BlockSpec(rb.shape, const),
        ],
        out_specs=_route_out_specs(n_sub),
        compiler_params=pltpu.CompilerParams(
            dimension_semantics=("arbitrary",), vmem_limit_bytes=VMEM_LIMIT),
        name="post_mix",
    )(ot, *x_parts, mod, w_o, rw_t, rb)


def _pool_mix_kernel(*refs, seq_len, pending):
    n_x = 7 if pending else 3
    x_refs = refs[:n_x]
    mod_ref, pw_ref, ps_ref, rwt_ref, rb_ref = refs[n_x:n_x + 5]
    out_refs = refs[n_x + 5:]
    _route_init(out_refs[-1])
    tm = x_refs[0].shape[0]
    per_seq = seq_len // tm
    mod = mod_ref[0]
    sh1 = mod[:, 0:D_MODEL]
    sc1 = mod[:, D_MODEL:2 * D_MODEL]
    g1 = mod[:, 2 * D_MODEL:3 * D_MODEL]
    pos0 = (pl.program_id(0) % per_seq) * tm

    if pending:
        g2_prev = x_refs[6][0][:, 5 * D_MODEL:6 * D_MODEL]

        def layer_input(k):
            x1_ref, moe_ref = x_refs[k], x_refs[3 + k]
            return x1_ref[...] + g2_prev * _unpack_rows(moe_ref, 0, x1_ref.shape[0])
    else:
        def layer_input(k):
            return x_refs[k][...]

    def normed(v):
        return _rms(v) * (1.0 + sc1) + sh1

    x = layer_input(0)
    h = normed(x)
    h_prev = jnp.where(pos0 > 0, normed(layer_input(1)), 0.0)
    h_next = jnp.where(pos0 + tm < seq_len, normed(layer_input(2)), 0.0)
    hext = jnp.concatenate([h_prev, h, h_next], axis=0)
    n_ext = tm + 2 * POOL_HALO
    pos = pos0 + lax.broadcasted_iota(I32, (tm, 1), 0)

    diffs = []
    for g, win in enumerate(POOL_WINDOWS):
        left = win // 2
        right = win - 1 - left
        cols = slice(g * POOL_GROUP, (g + 1) * POOL_GROUP)
        s = hext[:, cols]
        span = 1
        while span < win:
            s = s + pltpu.roll(s, span, 0)
            span *= 2
        if right:
            s = pltpu.roll(s, n_ext - right, 0)
        num = s[POOL_HALO:POOL_HALO + tm, :]
        count = (jnp.minimum(pos + right + 1, seq_len) - jnp.maximum(pos - left, 0)).astype(F32)
        diffs.append((num / count - h[:, cols]).astype(BF16))

    ts = TOKEN_TILE

    def mixer(sub):
        rows = slice(sub * ts, (sub + 1) * ts)
        outs = [jnp.dot(diffs[g][rows, :], pw_ref[g], preferred_element_type=F32)
                for g in range(len(POOL_WINDOWS))]
        x1 = x[rows, :] + g1 * (jnp.concatenate(outs, axis=1) * ps_ref[...])
        yield
        return x1

    _route_tail([mixer(sub) for sub in range(tm // ts)], mod, rwt_ref, rb_ref, out_refs)


def _pool_mix(x, mod, pool_w, pool_scale, rw_t, rb, seq_len):
    pending = isinstance(x, tuple)
    n_tok = (x[0] if pending else x).shape[0]
    tm = ROUTE_SUBTILES * TOKEN_TILE
    per_seq = seq_len // tm
    halo_per_tile = tm // POOL_HALO
    n_halo = n_tok // POOL_HALO
    const = lambda i: (0, 0)
    seq_block = lambda i: (i // per_seq, 0, 0)
    halo_prev = lambda i: (jnp.maximum(i * halo_per_tile - 1, 0), 0)
    halo_next = lambda i: (jnp.minimum((i + 1) * halo_per_tile, n_halo - 1), 0)

    def tile_and_halos(lines_per_row, width):
        return [pl.BlockSpec((tm * lines_per_row, width), lambda i: (i, 0)),
                pl.BlockSpec((POOL_HALO * lines_per_row, width), halo_prev),
                pl.BlockSpec((POOL_HALO * lines_per_row, width), halo_next)]

    if pending:
        x1, moe, mod_prev = x
        x_args = [x1, x1, x1, moe, moe, moe, mod_prev]
        x_specs = (tile_and_halos(1, D_MODEL) + tile_and_halos(ROW_CHUNKS, LANES)
                   + [pl.BlockSpec((1, 1, 6 * D_MODEL), seq_block)])
    else:
        x_args = [x, x, x]
        x_specs = tile_and_halos(1, D_MODEL)
    return pl.pallas_call(
        functools.partial(_pool_mix_kernel, seq_len=seq_len, pending=pending),
        out_shape=_route_out_shapes(n_tok),
        grid=(n_tok // tm,),
        in_specs=x_specs + [
            pl.BlockSpec((1, 1, 6 * D_MODEL), seq_block),
            pl.BlockSpec(pool_w.shape, lambda i: (0, 0, 0)),
            pl.BlockSpec(pool_scale.shape, const),
            pl.BlockSpec(rw_t.shape, const),
            pl.BlockSpec(rb.shape, const),
        ],
        out_specs=_route_out_specs(ROUTE_SUBTILES),
        compiler_params=pltpu.CompilerParams(
            dimension_semantics=("arbitrary",), vmem_limit_bytes=VMEM_LIMIT),
        name="pool_mix",
    )(*x_args, mod, pool_w, pool_scale, rw_t, rb)


def _slots_kernel(pstart_ref, topi_ref, rank_ref, dest_ref):
    topi = topi_ref[...]
    start = jnp.zeros_like(topi)
    for e in range(N_EXPERTS):
        start = jnp.where(topi == e, pstart_ref[e], start)
    dest_ref[...] = start + rank_ref[...]


def _slots(pstart, topi, rank):
    n_tiles, _, tm = topi.shape
    tb = math.gcd(n_tiles, 32)
    spec = pl.BlockSpec((tb, TOP_K, tm), lambda i, ps: (i, 0, 0))
    return pl.pallas_call(
        _slots_kernel,
        out_shape=jax.ShapeDtypeStruct(topi.shape, I32),
        grid_spec=pltpu.PrefetchScalarGridSpec(
            num_scalar_prefetch=1, grid=(n_tiles // tb,), in_specs=[spec, spec], out_specs=spec),
        compiler_params=pltpu.CompilerParams(dimension_semantics=("arbitrary",)),
        name="moe_slots",
    )(pstart, topi, rank)


def _sc_chunk_rows(c, tm, width=SC_ROWS):
    per_tile = tm // width
    tile = c // per_tile
    part = c % per_tile
    return [(tile * TOP_K + kk) * per_tile + part for kk in range(TOP_K)]


def _sc_dispatch(hp, dest, n_slots, tm):
    hp = hp.reshape((-1,) + ROW_SHAPE)
    n_tok = hp.shape[0]
    rows_per_w = dest.shape[0] // SC_WORKERS
    chunks_per_w = n_tok // SC_WORKERS // SC_ROWS
    mesh = plsc.VectorSubcoreMesh(core_axis_name="c", subcore_axis_name="s")

    @functools.partial(
        pl.kernel, mesh=mesh,
        out_type=jax.ShapeDtypeStruct((n_slots,) + ROW_SHAPE, I32),
        scratch_types=[
            pltpu.VMEM((rows_per_w, SC_ROWS), I32),
            pltpu.VMEM((SC_ROWS,) + ROW_SHAPE, I32),
            pltpu.SemaphoreType.DMA,
        ],
        name="sc_dispatch",
    )
    def run(hp_hbm, dest_hbm, xs_hbm, dest_v, rows_v, sem):
        wid = lax.axis_index("s") * SC_CORES + lax.axis_index("c")
        pltpu.sync_copy(dest_hbm.at[pl.ds(wid * rows_per_w, rows_per_w)], dest_v)

        @pl.loop(0, chunks_per_w)
        def _(c):
            tok0 = (wid * chunks_per_w + c) * SC_ROWS
            pltpu.sync_copy(hp_hbm.at[pl.ds(tok0, SC_ROWS)], rows_v)
            copies = [pltpu.async_copy(rows_v, xs_hbm.at[dest_v.at[row]], sem)
                      for row in _sc_chunk_rows(c, tm)]
            for cp in copies:
                cp.wait()

    return run(hp, dest).reshape(n_slots * ROW_CHUNKS, LANES)


def _sc_gather_sum(ys, dest, wtm, tm):
    ys = ys.reshape((-1,) + ROW_SHAPE)
    n_tok = wtm.shape[0]
    g = SC_SUM_ROWS
    rows_per_w = dest.shape[0] // SC_WORKERS
    chunks_per_w = n_tok // SC_WORKERS // g
    mesh = plsc.VectorSubcoreMesh(core_axis_name="c", subcore_axis_name="s")

    @functools.partial(
        pl.kernel, mesh=mesh,
        out_type=jax.ShapeDtypeStruct((n_tok,) + ROW_SHAPE, I32),
        scratch_types=[
            pltpu.VMEM((rows_per_w, g), I32),
            pltpu.VMEM((TOP_K, g) + ROW_SHAPE, I32),
            pltpu.VMEM((g, LANES), F32),
            pltpu.VMEM((g,) + ROW_SHAPE, I32),
            pltpu.SemaphoreType.DMA,
        ],
        compiler_params=pltpu.CompilerParams(needs_layout_passes=False),
        name="sc_gather_sum",
    )
    def run(ys_hbm, dest_hbm, w_hbm, out_hbm, dest_v, rows_v, w_v, out_v, sem):
        wid = lax.axis_index("s") * SC_CORES + lax.axis_index("c")
        pltpu.sync_copy(dest_hbm.at[pl.ds(wid * rows_per_w, rows_per_w)], dest_v)

        @pl.loop(0, chunks_per_w)
        def _(c):
            tok0 = (wid * chunks_per_w + c) * g
            copies = [pltpu.async_copy(ys_hbm.at[dest_v.at[row]], rows_v.at[kk], sem)
                      for kk, row in enumerate(_sc_chunk_rows(c, tm, g))]
            pltpu.sync_copy(w_hbm.at[pl.ds(tok0, g)], w_v)
            for cp in copies:
                cp.wait()

            @pl.loop(0, g)
            def _(t):
                wk = [w_v[t, pl.ds(SC_LANES * kk, SC_LANES)] for kk in range(TOP_K)]
                for j in range(ROW_CHUNKS):
                    for i in range(LANES // SC_LANES):
                        lanes = pl.ds(SC_LANES * i, SC_LANES)
                        lo = hi = None
                        for kk in range(TOP_K):
                            word = rows_v[kk, t, j, lanes]
                            a = lax.bitcast_convert_type(lax.shift_left(word, 16), F32) * wk[kk]
                            b = lax.bitcast_convert_type(word & jnp.int32(-65536), F32) * wk[kk]
                            lo = a if lo is None else lo + a
                            hi = b if hi is None else hi + b
                        packed = plsc.pack(lo, hi, format=plsc.PackFormat.INTERLEAVED)
                        out_v[t, j, lanes] = plsc.bitcast(packed, I32)

            pltpu.sync_copy(out_v, out_hbm.at[pl.ds(tok0, g)])

    return run(ys, dest, wtm).reshape(n_tok * ROW_CHUNKS, LANES)


def _expert_kernel(e_ref, first_ref, slot_ref, next_ref, nused_ref,
                   xs_ref, w1_hbm, b1_ref, w2_hbm, b2_ref, ys_ref,
                   w1f_ref, w2f_ref, w1b_ref, w2b_ref, sem, *, layer):
    i = pl.program_id(0)

    def weight_copies(expert, slot):
        return (pltpu.make_async_copy(w1_hbm.at[layer, expert], w1f_ref.at[slot], sem.at[slot, 0]),
                pltpu.make_async_copy(w2_hbm.at[layer, expert], w2f_ref.at[slot], sem.at[slot, 1]))

    @pl.when(first_ref[i] == 1)
    def _():
        slot = slot_ref[i]

        @pl.when(i == 0)
        def _():
            for cp in weight_copies(e_ref[i], slot):
                cp.start()

        for cp in weight_copies(e_ref[i], slot):
            cp.wait()
        w1b_ref[...] = w1f_ref[slot].astype(BF16)
        w2b_ref[...] = w2f_ref[slot].astype(BF16)

        @pl.when(next_ref[i] >= 0)
        def _():
            for cp in weight_copies(next_ref[i], 1 - slot):
                cp.start()

    @pl.when(i < nused_ref[0])
    def _():
        for r in range(0, xs_ref.shape[0] // ROW_CHUNKS, EXPERT_SUB):
            xb = _unpack_rows(xs_ref, r, EXPERT_SUB).astype(BF16)
            gu = jnp.dot(xb, w1b_ref[...], preferred_element_type=F32) + b1_ref[0, 0]
            gate = jnp.minimum(gu[:, :D_FF], SWIGLU_LIMIT)
            up = jnp.clip(gu[:, D_FF:], -SWIGLU_LIMIT, SWIGLU_LIMIT)
            act = (up + 1.0) * (gate * jax.nn.sigmoid(SWIGLU_ALPHA * gate))
            y = jnp.dot(act.astype(BF16), w2b_ref[...], preferred_element_type=F32) + b2_ref[0, 0]
            _pack_rows(y, ys_ref, r)


def _expert_plan(pend, n_blk, bm):
    blk = jnp.arange(n_blk, dtype=I32)
    n_used = (pend[-1] // bm).astype(I32)
    expert = jnp.minimum(jnp.sum(blk[:, None] * bm >= pend[None, :], axis=1), N_EXPERTS - 1)
    expert = expert.astype(I32)
    prev = jnp.concatenate([jnp.full((1,), -1, I32), expert[:-1]])
    first = (blk < n_used) & (expert != prev)
    slot = (jnp.cumsum(first.astype(I32)) - 1) % 2
    later_first = first[None, :] & (blk[None, :] > blk[:, None])
    nxt = jnp.where(jnp.any(later_first, axis=1), expert[jnp.argmax(later_first, axis=1)], -1)
    return (expert, first.astype(I32), slot.astype(I32), nxt.astype(I32), n_used.reshape(1))


def _expert_ffn(plan, xs, layer, w1, b1, w2, b2):
    n_slots = xs.shape[0] // ROW_CHUNKS
    bm = EXPERT_TILE
    used_block = lambda i, e, f, s, nx, nu: (jnp.minimum(i, nu[0] - 1), 0)
    bias_block = lambda i, e, f, s, nx, nu: (layer, e[i], 0, 0)
    return pl.pallas_call(
        functools.partial(_expert_kernel, layer=layer),
        out_shape=jax.ShapeDtypeStruct(xs.shape, I32),
        grid_spec=pltpu.PrefetchScalarGridSpec(
            num_scalar_prefetch=5,
            grid=(n_slots // bm,),
            in_specs=[
                pl.BlockSpec((bm * ROW_CHUNKS, LANES), used_block),
                pl.BlockSpec(memory_space=pl.ANY),
                pl.BlockSpec((1, 1, 1, 2 * D_FF), bias_block),
                pl.BlockSpec(memory_space=pl.ANY),
                pl.BlockSpec((1, 1, 1, D_MODEL), bias_block),
            ],
            out_specs=pl.BlockSpec((bm * ROW_CHUNKS, LANES), used_block),
            scratch_shapes=[
                pltpu.VMEM((2, D_MODEL, 2 * D_FF), F32), pltpu.VMEM((2, D_FF, D_MODEL), F32),
                pltpu.VMEM((D_MODEL, 2 * D_FF), BF16), pltpu.VMEM((D_FF, D_MODEL), BF16),
                pltpu.SemaphoreType.DMA((2, 2)),
            ],
        ),
        compiler_params=pltpu.CompilerParams(
            dimension_semantics=("arbitrary",), vmem_limit_bytes=VMEM_LIMIT),
        name="expert_ffn",
    )(*plan, xs, w1, b1, w2, b2)


def _combine_kernel(moe_ref, x1_ref, mod_ref, fn_ref, o_ref, *, final):
    tm = x1_ref.shape[0]
    g2 = mod_ref[0][:, 5 * D_MODEL:6 * D_MODEL]
    out = x1_ref[...] + g2 * _unpack_rows(moe_ref, 0, tm)
    if final:
        out = _rms(out) * fn_ref[...]
    o_ref[...] = out


def _combine(moe, x1, mod, final_norm, seq_len, final):
    n_tok = x1.shape[0]
    tm = COMBINE_TILE
    per_seq = seq_len // tm
    return pl.pallas_call(
        functools.partial(_combine_kernel, final=final),
        out_shape=jax.ShapeDtypeStruct((n_tok, D_MODEL), F32),
        grid=(n_tok // tm,),
        in_specs=[
            pl.BlockSpec((tm * ROW_CHUNKS, LANES), lambda i: (i, 0)),
            pl.BlockSpec((tm, D_MODEL), lambda i: (i, 0)),
            pl.BlockSpec((1, 1, 6 * D_MODEL), lambda i: (i // per_seq, 0, 0)),
            pl.BlockSpec((1, D_MODEL), lambda i: (0, 0)),
        ],
        out_specs=pl.BlockSpec((tm, D_MODEL), lambda i: (i, 0)),
        compiler_params=pltpu.CompilerParams(
            dimension_semantics=("arbitrary",), vmem_limit_bytes=VMEM_LIMIT),
        name="moe_combine",
    )(moe, x1, mod, final_norm)


def _moe_layers(routes, mods, layer, ffn_w, final_norm, seq_len, final, defer_combine):
    bm = EXPERT_TILE
    plans = []
    for x1, hp, topi, rank, wtm, counts in routes:
        n_slots = x1.shape[0] * TOP_K + N_EXPERTS * bm
        n_blk = n_slots // bm
        cnt = counts[:, 0].astype(I32)
        padded = (cnt + bm - 1) // bm * bm
        pend = jnp.cumsum(padded)
        pstart = (pend - padded).astype(I32)
        dest = _slots(pstart, topi, rank).reshape(-1, SC_ROWS)
        plans.append((n_slots, _expert_plan(pend, n_blk, bm), dest, topi.shape[2]))
    xs = [_sc_dispatch(r[1], dest, n_slots, tm)
          for r, (n_slots, _, dest, tm) in zip(routes, plans)]
    ys = [_expert_ffn(plan, x, layer, *ffn_w) for x, (_, plan, _, _) in zip(xs, plans)]
    moe = [_sc_gather_sum(y, dest.reshape(-1, SC_SUM_ROWS), r[4], tm)
           for y, r, (_, _, dest, tm) in zip(ys, routes, plans)]
    if defer_combine:
        return [(r[0], m, mod) for m, r, mod in zip(moe, routes, mods)]
    return [_combine(m, r[0], mod, final_norm, seq_len, final)
            for m, r, mod in zip(moe, routes, mods)]


def _rope_tables(seq_len):
    inv_freq = 1.0 / (ROPE_THETA ** (jnp.arange(0, D_ROPE, 2, dtype=F32) / D_ROPE))
    ang = jnp.arange(seq_len, dtype=F32)[:, None] * inv_freq[None, :]
    cos, sin = jnp.cos(ang), jnp.sin(ang)
    ones = jnp.ones((seq_len, ROPE_LO), F32)
    zeros_lo = jnp.zeros((seq_len, ROPE_LO), F32)
    zeros_hi = jnp.zeros((seq_len, HEAD_PAD - ROPE_LO - D_ROPE), F32)
    zeros_h = jnp.zeros((seq_len, ROPE_HALF), F32)
    rope_c = jnp.concatenate([ones, cos, cos, zeros_hi], axis=1)
    rope_s1 = jnp.concatenate([zeros_lo, -sin, zeros_h, zeros_hi], axis=1)
    rope_s2 = jnp.concatenate([zeros_lo, zeros_h, sin, zeros_hi], axis=1)
    return cos.T, sin.T, rope_c, rope_s1, rope_s2


def _mla_weights(w_in, q_norm, kv_norm, w_uq, w_ukv, w_o, seq_len):
    d_qk = D_NOPE + D_ROPE
    q_scale = d_qk ** -0.5 * math.log2(math.e)
    pad_pe = jnp.zeros((D_MODEL, HEAD_PAD), F32).at[:, ROPE_LO:ROPE_LO + D_ROPE].set(
        w_in[:, Q_RANK + KV_RANK:])
    w_in_p = jnp.concatenate([w_in[:, :Q_RANK + KV_RANK], pad_pe], axis=1)
    w_uq_s = w_uq * q_scale
    w_kv = w_ukv.reshape(KV_RANK, N_HEADS, D_NOPE + D_V)
    w_uk_p = jnp.pad(w_kv[:, :, :D_NOPE], ((0, 0), (0, 0), (0, HEAD_PAD - D_NOPE)))
    cos_t, sin_t, rope_c, rope_s1, rope_s2 = _rope_tables(seq_len)
    return {
        "w_in": w_in_p.astype(BF16),
        "q_norm": q_norm.reshape(1, Q_RANK),
        "kv_norm": kv_norm.reshape(1, KV_RANK),
        "w_uq_t": w_uq_s.T.astype(BF16),
        "w_uk": w_uk_p.reshape(KV_RANK, N_HEADS * HEAD_PAD).astype(BF16),
        "w_uv_t": w_kv[:, :, D_NOPE:].reshape(KV_RANK, N_HEADS * D_V).T.astype(BF16),
        "w_o": w_o.astype(BF16),
        "cos_t": cos_t, "sin_t": sin_t, "rope_c": rope_c, "rope_s1": rope_s1, "rope_s2": rope_s2,
    }


def _router_weights(router_w, router_b):
    return router_w.T.astype(BF16), router_b.reshape(N_EXPERTS, 1)


def kernel(x_prompt, x_sample, c_prompt, c_sample, ada_w, ada_b, mla_w_in, mla_q_norm,
           mla_kv_norm, mla_w_uq, mla_w_ukv, mla_w_o, pool_w, pool_scale, router_w, router_b,
           moe_w1, moe_b1, moe_w2, moe_b2, final_norm):
    n_prompt, seq_len, _ = x_prompt.shape
    assert x_sample.shape[1] == seq_len and seq_len % (ROUTE_SUBTILES * TOKEN_TILE) == 0
    depth = ada_w.shape[0]
    n_sample = x_sample.shape[0]
    xs = [x_prompt.reshape(-1, D_MODEL), x_sample.reshape(-1, D_MODEL)]
    n_seqs = [n_prompt, n_sample]
    mods = _ada_mod(jnp.concatenate([c_prompt, c_sample], axis=0), ada_w, ada_b)
    fnorm = final_norm.reshape(1, D_MODEL)
    ffn_w = (moe_w1, moe_b1.reshape(depth, N_EXPERTS, 1, 2 * D_FF),
             moe_w2, moe_b2.reshape(depth, N_EXPERTS, 1, D_MODEL))

    for i in range(depth):
        rw_t, rb = _router_weights(router_w[i], router_b[i])
        j = i // 2
        group_mods = [mods[i, :n_prompt].reshape(n_prompt, 1, 6 * D_MODEL),
                      mods[i, n_prompt:].reshape(n_sample, 1, 6 * D_MODEL)]
        if i % 2 == 0:
            w = _mla_weights(mla_w_in[j], mla_q_norm[j], mla_kv_norm[j], mla_w_uq[j],
                             mla_w_ukv[j], mla_w_o[j], seq_len)
        routes = []
        for x, mod in zip(xs, group_mods):
            if i % 2 == 0:
                qt, k, vt = _mla_pre([x], mod, w, seq_len)
                ot = _attention(qt, k, vt, seq_len)
                routes.append(_post_mix(ot, [x], mod, w["w_o"], rw_t, rb, seq_len))
            else:
                routes.append(_pool_mix(x, mod, pool_w[j].astype(BF16),
                                        pool_scale[j].reshape(1, D_MODEL), rw_t, rb, seq_len))
        last = i == depth - 1
        next_is_pool = not last and (i + 1) % 2 == 1
        xs = _moe_layers(routes, group_mods, i, ffn_w, fnorm, seq_len, last, next_is_pool)

    return (xs[0].reshape(n_prompt, seq_len, D_MODEL), xs[1].reshape(n_sample, seq_len, D_MODEL))
```

```python
import functools
import math

import jax
import jax.numpy as jnp
from jax import lax
from jax.experimental import pallas as pl
from jax.experimental.pallas import tpu as pltpu
from jax.experimental.pallas import tpu_sc as plsc

F32 = jnp.float32
BF16 = jnp.bfloat16
I32 = jnp.int32

D_MODEL = 1024
N_HEADS = 16
Q_RANK = 384
KV_RANK = 256
D_NOPE = 64
D_ROPE = 32
D_V = 64
V_ROWS = D_V + 16
ROPE_THETA = 10000.0
POOL_WINDOWS = (2, 4, 8, 16)
POOL_GROUP = D_MODEL // len(POOL_WINDOWS)
N_EXPERTS = 32
TOP_K = 4
D_FF = D_MODEL
SWIGLU_LIMIT = 7.0
SWIGLU_ALPHA = 1.702
EPS = 1e-6

LANES = 128
SUBLANES = 8
HEAD_PAD = 128
ROPE_LO = D_NOPE
ROPE_HALF = D_ROPE // 2
VMEM_LIMIT = 56 * 1024 * 1024

TOKEN_TILE = 256
COMBINE_TILE = 512
ROUTE_SUBTILES = 4
EXPERT_TILE = 512
EXPERT_SUB = 256
ATTN_KEY_CHUNKS = 2
ATTN_HEADS_PER_STEP = 8
POOL_HALO = 8
PACKED = D_MODEL // 2
ROW_CHUNKS = PACKED // LANES
ROW_SHAPE = (ROW_CHUNKS, LANES)

SC_CORES = 2
SC_SUBCORES = 16
SC_WORKERS = SC_CORES * SC_SUBCORES
SC_LANES = 16
SC_ROWS = 64
SC_SUM_ROWS = 32


def _rms(x):
    return x * lax.rsqrt(jnp.mean(x * x, axis=-1, keepdims=True) + EPS)


def _pack_rows(y, out_ref, row0=0):
    n = y.shape[0]
    lo = lax.bitcast_convert_type(y[:, :PACKED].astype(BF16).astype(F32), I32)
    hi = lax.bitcast_convert_type(y[:, PACKED:].astype(BF16).astype(F32), I32)
    words = lax.shift_right_logical(lo, 16) | (hi & jnp.int32(-65536))
    for j in range(ROW_CHUNKS):
        out_ref[pl.ds(row0 * ROW_CHUNKS + j, n, stride=ROW_CHUNKS), :] = (
            words[:, j * LANES:(j + 1) * LANES])


def _unpack_rows(in_ref, row0, n):
    words = jnp.concatenate(
        [in_ref[pl.ds(row0 * ROW_CHUNKS + j, n, stride=ROW_CHUNKS), :] for j in range(ROW_CHUNKS)],
        axis=1)
    lo = lax.bitcast_convert_type(lax.shift_left(words, 16), F32)
    hi = lax.bitcast_convert_type(words & jnp.int32(-65536), F32)
    return jnp.concatenate([lo, hi], axis=1)


def _part_specs(parts, tm):
    if len(parts) == 1:
        return [pl.BlockSpec((tm, D_MODEL), lambda i: (i, 0))]
    n0 = parts[0].shape[0] // tm
    return [pl.BlockSpec((tm, D_MODEL), lambda i: (jnp.minimum(i, n0 - 1), 0)),
            pl.BlockSpec((tm, D_MODEL), lambda i: (jnp.maximum(i - n0, 0), 0))]


def _part_load(refs, n0):
    if len(refs) == 1:
        return refs[0][...]
    return jnp.where(pl.program_id(0) < n0, refs[0][...], refs[1][...])


def _ada_kernel(c_ref, w_ref, b_ref, o_ref):
    c = c_ref[...]
    act = (c * jax.nn.sigmoid(c)).astype(BF16)
    o_ref[0] = jnp.dot(act, w_ref[0].astype(BF16), preferred_element_type=F32) + b_ref[0]


def _ada_mod(c, ada_w, ada_b):
    depth, _, n_out = ada_w.shape
    n_seq = c.shape[0]
    tn = 1536
    return pl.pallas_call(
        _ada_kernel,
        out_shape=jax.ShapeDtypeStruct((depth, n_seq, n_out), F32),
        grid=(depth, n_out // tn),
        in_specs=[
            pl.BlockSpec((n_seq, D_MODEL), lambda l, j: (0, 0)),
            pl.BlockSpec((1, D_MODEL, tn), lambda l, j: (l, 0, j)),
            pl.BlockSpec((1, 1, tn), lambda l, j: (l, 0, j)),
        ],
        out_specs=pl.BlockSpec((1, n_seq, tn), lambda l, j: (l, 0, j)),
        compiler_params=pltpu.CompilerParams(
            dimension_semantics=("arbitrary", "arbitrary"), vmem_limit_bytes=VMEM_LIMIT),
        name="ada_mod",
    )(c, ada_w, ada_b.reshape(depth, 1, n_out))


def _mla_pre_kernel(*refs, n_x, n0):
    x_refs = refs[:n_x]
    (mod_ref, win_ref, qn_ref, kvn_ref, wuqt_ref, wuk_ref, wuvt_ref,
     cost_ref, sint_ref, ck_ref, s1k_ref, s2k_ref, qt_ref, k_ref, vt_ref) = refs[n_x:]
    n_sub, ts = qt_ref.shape[1], qt_ref.shape[3]
    mod = mod_ref[0]
    sh1 = mod[:, 0:D_MODEL]
    sc1 = mod[:, D_MODEL:2 * D_MODEL]
    x = _part_load(x_refs, n0)
    d_qk = D_NOPE + D_ROPE

    def chain(sub):
        rows = slice(sub * ts, (sub + 1) * ts)
        h = (_rms(x[rows, :]) * (1.0 + sc1) + sh1).astype(BF16)
        a = jnp.dot(h, win_ref[...], preferred_element_type=F32)
        yield
        cq = (_rms(a[:, :Q_RANK]) * qn_ref[...]).astype(BF16)
        ckv = (_rms(a[:, Q_RANK:Q_RANK + KV_RANK]) * kvn_ref[...]).astype(BF16)
        kpe = a[:, Q_RANK + KV_RANK:]
        kpe = (kpe * ck_ref[rows, :]
               + pltpu.roll(kpe, LANES - ROPE_HALF, 1) * s1k_ref[rows, :]
               + pltpu.roll(kpe, ROPE_HALF, 1) * s2k_ref[rows, :])
        yield
        qt = lax.dot_general(wuqt_ref[...], cq, (((1,), (1,)), ((), ())),
                             preferred_element_type=F32)
        q3 = qt.reshape(N_HEADS, d_qk, ts)
        x1 = q3[:, ROPE_LO:ROPE_LO + ROPE_HALF, :]
        x2 = q3[:, ROPE_LO + ROPE_HALF:, :]
        cos = cost_ref[:, rows][None]
        sin = sint_ref[:, rows][None]
        q3 = jnp.concatenate(
            [q3[:, :ROPE_LO, :], x1 * cos - x2 * sin, x2 * cos + x1 * sin], axis=1)
        qt_ref[:, sub, :d_qk, :] = q3.astype(BF16)
        qt_ref[:, sub, d_qk:, :] = jnp.zeros((N_HEADS, HEAD_PAD - d_qk, ts), BF16)
        yield
        kn = jnp.dot(ckv, wuk_ref[...], preferred_element_type=F32)
        for hd in range(N_HEADS):
            k_ref[hd, rows, :] = (kn[:, hd * HEAD_PAD:(hd + 1) * HEAD_PAD] + kpe).astype(BF16)
        yield
        vt = lax.dot_general(wuvt_ref[...], ckv, (((1,), (1,)), ((), ())),
                             preferred_element_type=F32)
        vt_ref[:, 0, :D_V, rows] = vt.reshape(N_HEADS, D_V, ts).astype(BF16)
        vt_ref[:, 0, D_V:, rows] = jnp.ones((N_HEADS, V_ROWS - D_V, ts), BF16)

    _interleave(chain(sub) for sub in range(n_sub))


def _mla_pre(x_parts, mod, w, seq_len):
    n_tok = sum(p.shape[0] for p in x_parts)
    n_sub = ROUTE_SUBTILES
    tq = TOKEN_TILE
    tm = n_sub * tq
    per_seq = seq_len // tm
    const = lambda i: (0, 0)
    return pl.pallas_call(
        functools.partial(_mla_pre_kernel, n_x=len(x_parts), n0=x_parts[0].shape[0] // tm),
        out_shape=(
            jax.ShapeDtypeStruct((N_HEADS, n_tok // tq, HEAD_PAD, tq), BF16),
            jax.ShapeDtypeStruct((N_HEADS, n_tok, HEAD_PAD), BF16),
            jax.ShapeDtypeStruct((N_HEADS, n_tok // seq_len, V_ROWS, seq_len), BF16),
        ),
        grid=(n_tok // tm,),
        in_specs=_part_specs(x_parts, tm) + [
            pl.BlockSpec((1, 1, 6 * D_MODEL), lambda i: (i // per_seq, 0, 0)),
            pl.BlockSpec(w["w_in"].shape, const),
            pl.BlockSpec(w["q_norm"].shape, const),
            pl.BlockSpec(w["kv_norm"].shape, const),
            pl.BlockSpec(w["w_uq_t"].shape, const),
            pl.BlockSpec(w["w_uk"].shape, const),
            pl.BlockSpec(w["w_uv_t"].shape, const),
            pl.BlockSpec((ROPE_HALF, tm), lambda i: (0, i % per_seq)),
            pl.BlockSpec((ROPE_HALF, tm), lambda i: (0, i % per_seq)),
            pl.BlockSpec((tm, LANES), lambda i: (i % per_seq, 0)),
            pl.BlockSpec((tm, LANES), lambda i: (i % per_seq, 0)),
            pl.BlockSpec((tm, LANES), lambda i: (i % per_seq, 0)),
        ],
        out_specs=(
            pl.BlockSpec((N_HEADS, n_sub, HEAD_PAD, tq), lambda i: (0, i, 0, 0)),
            pl.BlockSpec((N_HEADS, tm, HEAD_PAD), lambda i: (0, i, 0)),
            pl.BlockSpec((N_HEADS, 1, V_ROWS, tm), lambda i: (0, i // per_seq, 0, i % per_seq)),
        ),
        compiler_params=pltpu.CompilerParams(
            dimension_semantics=("arbitrary",), vmem_limit_bytes=VMEM_LIMIT),
        name="mla_pre",
    )(*x_parts, mod, w["w_in"], w["q_norm"], w["kv_norm"], w["w_uq_t"], w["w_uk"], w["w_uv_t"],
      w["cos_t"], w["sin_t"], w["rope_c"], w["rope_s1"], w["rope_s2"])


def _attention_kernel(qt_ref, k_ref, vt_ref, ot_ref, s0_ref, s1_ref):
    n_heads, n_q = qt_ref.shape[:2]
    n_tiles = n_heads * n_q
    n_keys = k_ref.shape[1]
    kc = n_keys // ATTN_KEY_CHUNKS

    def stage(t_next, s_next_ref, t_cur, s_cur_ref, m_cur):
        m_next, o = None, None
        if t_next is not None:
            h_next, j_next = t_next // n_q, t_next % n_q
        if t_cur is not None:
            h_cur, j_cur = t_cur // n_q, t_cur % n_q
        for c in range(ATTN_KEY_CHUNKS):
            rows = slice(c * kc, (c + 1) * kc)
            if t_next is not None:
                s = jnp.dot(k_ref[h_next, rows, :], qt_ref[h_next, j_next],
                            preferred_element_type=F32)
                s_next_ref[rows, :] = s
                cm = jnp.max(s, axis=0, keepdims=True)
                m_next = cm if m_next is None else jnp.maximum(m_next, cm)
            if t_cur is not None:
                p = jnp.exp2(s_cur_ref[rows, :] - m_cur).astype(BF16)
                part = jnp.dot(vt_ref[h_cur, 0, :, rows], p,
                               preferred_element_type=F32)
                o = part if o is None else o + part
        if t_cur is not None:
            denom = o[D_V:D_V + 1, :]
            ot_ref[h_cur, j_cur] = (o[:D_V, :] * (1.0 / denom)).astype(BF16)
        return m_next

    def body(i, m0):
        t = 2 * i
        m1 = stage(t + 1, s1_ref, t, s0_ref, m0)
        return stage(t + 2, s0_ref, t + 1, s1_ref, m1)

    m0 = lax.fori_loop(0, n_tiles // 2 - 1, body, stage(0, s0_ref, None, None, None))
    m1 = stage(n_tiles - 1, s1_ref, n_tiles - 2, s0_ref, m0)
    stage(None, None, n_tiles - 1, s1_ref, m1)


def _attention(qt, k, vt, seq_len):
    n_heads, n_tiles, _, tq = qt.shape
    per_seq = seq_len // tq
    n_seq = n_tiles // per_seq
    hb = ATTN_HEADS_PER_STEP
    return pl.pallas_call(
        _attention_kernel,
        out_shape=jax.ShapeDtypeStruct((n_heads, n_tiles, D_V, tq), BF16),
        grid=(n_seq, n_heads // hb),
        in_specs=[
            pl.BlockSpec((hb, per_seq, HEAD_PAD, tq), lambda b, h: (h, b, 0, 0)),
            pl.BlockSpec((hb, seq_len, HEAD_PAD), lambda b, h: (h, b, 0)),
            pl.BlockSpec((hb, 1, V_ROWS, seq_len), lambda b, h: (h, b, 0, 0)),
        ],
        out_specs=pl.BlockSpec((hb, per_seq, D_V, tq), lambda b, h: (h, b, 0, 0)),
        scratch_shapes=[pltpu.VMEM((seq_len, tq), F32), pltpu.VMEM((seq_len, tq), F32)],
        compiler_params=pltpu.CompilerParams(
            dimension_semantics=("arbitrary", "arbitrary"), vmem_limit_bytes=VMEM_LIMIT),
        name="attention",
    )(qt, k, vt)


def _route_init(cnt_ref):
    @pl.when(pl.program_id(0) == 0)
    def _():
        cnt_ref[...] = jnp.zeros_like(cnt_ref)


def _interleave(chains):
    chains = list(chains)
    done = object()
    while chains:
        chains = [ch for ch in chains if next(ch, done) is not done]


def _route_tail(x1_chains, mod, rwt_ref, rb_ref, out_refs):
    cnt_ref = out_refs[-1]
    state = {"running": cnt_ref[...][:, 0:1]}

    def chain(sub, x1_chain):
        x1 = yield from x1_chain
        yield from _route_chain(sub, x1, mod, rwt_ref, rb_ref, out_refs, state)

    _interleave(chain(sub, ch) for sub, ch in enumerate(x1_chains))
    cnt_ref[...] = jnp.broadcast_to(state["running"], cnt_ref.shape)


def _route_chain(sub, x1, mod, rwt_ref, rb_ref, out_refs, state):
    x1_ref, hp_ref, topi_ref, rank_ref, wtm_ref, _ = out_refs
    tm = x1.shape[0]
    rows = pl.ds(sub * tm, tm)
    sh2 = mod[:, 3 * D_MODEL:4 * D_MODEL]
    sc2 = mod[:, 4 * D_MODEL:5 * D_MODEL]
    x1_ref[rows, :] = x1
    h2 = _rms(x1) * (1.0 + sc2) + sh2
    _pack_rows(h2, hp_ref, sub * tm)
    logits = lax.dot_general(rwt_ref[...], h2.astype(BF16), (((1,), (1,)), ((), ())),
                             preferred_element_type=F32) + rb_ref[...]
    yield
    e_iota = lax.broadcasted_iota(I32, (N_EXPERTS, tm), 0)
    vals, idxs = [], []
    work = logits
    for _ in range(TOP_K):
        m = jnp.max(work, axis=0, keepdims=True)
        idx = jnp.min(jnp.where(work == m, e_iota, N_EXPERTS), axis=0, keepdims=True)
        vals.append(m)
        idxs.append(idx)
        work = jnp.where(e_iota == idx, -jnp.inf, work)
    ex = [jnp.exp(v - vals[0]) for v in vals]
    inv = 1.0 / (ex[0] + ex[1] + ex[2] + ex[3])
    topw = jnp.concatenate([e * inv for e in ex], axis=0)
    topi_ref[sub] = jnp.concatenate(idxs, axis=0)
    yield

    row = lax.broadcasted_iota(I32, (tm, tm), 0)
    col = lax.broadcasted_iota(I32, (tm, tm), 1)
    earlier = (row < col).astype(BF16)
    running = state["running"]
    ranks = []
    for kk in range(TOP_K):
        onehot = (e_iota == idxs[kk]).astype(F32)
        before = jnp.dot(onehot.astype(BF16), earlier, preferred_element_type=F32)
        ranks.append(jnp.sum(onehot * (running + before), axis=0, keepdims=True))
        running = running + jnp.sum(onehot, axis=1, keepdims=True)
    rank_ref[sub] = jnp.concatenate(ranks, axis=0).astype(I32)
    state["running"] = running
    yield

    wpad = jnp.concatenate(
        [jnp.broadcast_to(topw[kk:kk + 1], (SC_LANES, tm)) for kk in range(TOP_K)]
        + [jnp.zeros((LANES - TOP_K * SC_LANES, tm), F32)], axis=0)
    wtm_ref[rows, :] = wpad.T


def _route_out_shapes(n_tok):
    n_tiles = n_tok // TOKEN_TILE
    return (
        jax.ShapeDtypeStruct((n_tok, D_MODEL), F32),
        jax.ShapeDtypeStruct((n_tok * ROW_CHUNKS, LANES), I32),
        jax.ShapeDtypeStruct((n_tiles, TOP_K, TOKEN_TILE), I32),
        jax.ShapeDtypeStruct((n_tiles, TOP_K, TOKEN_TILE), I32),
        jax.ShapeDtypeStruct((n_tok, LANES), F32),
        jax.ShapeDtypeStruct((N_EXPERTS, LANES), F32),
    )


def _route_out_specs(n_sub):
    tm = n_sub * TOKEN_TILE
    return (
        pl.BlockSpec((tm, D_MODEL), lambda i: (i, 0)),
        pl.BlockSpec((tm * ROW_CHUNKS, LANES), lambda i: (i, 0)),
        pl.BlockSpec((n_sub, TOP_K, TOKEN_TILE), lambda i: (i, 0, 0)),
        pl.BlockSpec((n_sub, TOP_K, TOKEN_TILE), lambda i: (i, 0, 0)),
        pl.BlockSpec((tm, LANES), lambda i: (i, 0)),
        pl.BlockSpec((N_EXPERTS, LANES), lambda i: (0, 0)),
    )


def _post_mix_kernel(ot_ref, *refs, n_x, n0):
    x_refs = refs[:n_x]
    mod_ref, wo_ref, rwt_ref, rb_ref = refs[n_x:n_x + 4]
    out_refs = refs[n_x + 4:]
    _route_init(out_refs[-1])
    n_sub, ts = ot_ref.shape[1], ot_ref.shape[3]
    mod = mod_ref[0]
    g1 = mod[:, 2 * D_MODEL:3 * D_MODEL]
    x = _part_load(x_refs, n0)

    def mixer(sub):
        ot = ot_ref[:, sub].reshape(N_HEADS * D_V, ts)
        mix = lax.dot_general(ot, wo_ref[...], (((0,), (0,)), ((), ())),
                              preferred_element_type=F32)
        x1 = x[sub * ts:(sub + 1) * ts, :] + g1 * mix
        yield
        return x1

    _route_tail([mixer(sub) for sub in range(n_sub)], mod, rwt_ref, rb_ref, out_refs)


def _post_mix(ot, x_parts, mod, w_o, rw_t, rb, seq_len):
    n_tok = sum(p.shape[0] for p in x_parts)
    n_sub = ROUTE_SUBTILES
    tm = n_sub * ot.shape[3]
    per_seq = seq_len // tm
    const = lambda i: (0, 0)
    return pl.pallas_call(
        functools.partial(_post_mix_kernel, n_x=len(x_parts), n0=x_parts[0].shape[0] // tm),
        out_shape=_route_out_shapes(n_tok),
        grid=(n_tok // tm,),
        in_specs=[pl.BlockSpec((N_HEADS, n_sub, D_V, ot.shape[3]), lambda i: (0, i, 0, 0))]
        + _part_specs(x_parts, tm) + [
            pl.BlockSpec((1, 1, 6 * D_MODEL), lambda i: (i // per_seq, 0, 0)),
            pl.BlockSpec(w_o.shape, const),
            pl.BlockSpec(rw_t.shape, const),
            pl.BlockSpec(rb.shape, const),
        ],
        out_specs=_route_out_specs(n_sub),
        compiler_params=pltpu.CompilerParams(
            dimension_semantics=("arbitrary",), vmem_limit_bytes=VMEM_LIMIT),
        name="post_mix",
    )(ot, *x_parts, mod, w_o, rw_t, rb)


def _pool_mix_kernel(*refs, seq_len, pending):
    n_x = 7 if pending else 3
    x_refs = refs[:n_x]
    mod_ref, pw_ref, ps_ref, rwt_ref, rb_ref = refs[n_x:n_x + 5]
    out_refs = refs[n_x + 5:]
    _route_init(out_refs[-1])
    tm = x_refs[0].shape[0]
    per_seq = seq_len // tm
    mod = mod_ref[0]
    sh1 = mod[:, 0:D_MODEL]
    sc1 = mod[:, D_MODEL:2 * D_MODEL]
    g1 = mod[:, 2 * D_MODEL:3 * D_MODEL]
    pos0 = (pl.program_id(0) % per_seq) * tm

    if pending:
        g2_prev = x_refs[6][0][:, 5 * D_MODEL:6 * D_MODEL]

        def layer_input(k):
            x1_ref, moe_ref = x_refs[k], x_refs[3 + k]
            return x1_ref[...] + g2_prev * _unpack_rows(moe_ref, 0, x1_ref.shape[0])
    else:
        def layer_input(k):
            return x_refs[k][...]

    def normed(v):
        return _rms(v) * (1.0 + sc1) + sh1

    x = layer_input(0)
    h = normed(x)
    h_prev = jnp.where(pos0 > 0, normed(layer_input(1)), 0.0)
    h_next = jnp.where(pos0 + tm < seq_len, normed(layer_input(2)), 0.0)
    hext = jnp.concatenate([h_prev, h, h_next], axis=0)
    n_ext = tm + 2 * POOL_HALO
    pos = pos0 + lax.broadcasted_iota(I32, (tm, 1), 0)

    diffs = []
    for g, win in enumerate(POOL_WINDOWS):
        left = win // 2
        right = win - 1 - left
        cols = slice(g * POOL_GROUP, (g + 1) * POOL_GROUP)
        s = hext[:, cols]
        span = 1
        while span < win:
            s = s + pltpu.roll(s, span, 0)
            span *= 2
        if right:
            s = pltpu.roll(s, n_ext - right, 0)
        num = s[POOL_HALO:POOL_HALO + tm, :]
        count = (jnp.minimum(pos + right + 1, seq_len) - jnp.maximum(pos - left, 0)).astype(F32)
        diffs.append((num / count - h[:, cols]).astype(BF16))

    ts = TOKEN_TILE

    def mixer(sub):
        rows = slice(sub * ts, (sub + 1) * ts)
        outs = [jnp.dot(diffs[g][rows, :], pw_ref[g], preferred_element_type=F32)
                for g in range(len(POOL_WINDOWS))]
        x1 = x[rows, :] + g1 * (jnp.concatenate(outs, axis=1) * ps_ref[...])
        yield
        return x1

    _route_tail([mixer(sub) for sub in range(tm // ts)], mod, rwt_ref, rb_ref, out_refs)


def _pool_mix(x, mod, pool_w, pool_scale, rw_t, rb, seq_len):
    pending = isinstance(x, tuple)
    n_tok = (x[0] if pending else x).shape[0]
    tm = ROUTE_SUBTILES * TOKEN_TILE
    per_seq = seq_len // tm
    halo_per_tile = tm // POOL_HALO
    n_halo = n_tok // POOL_HALO
    const = lambda i: (0, 0)
    seq_block = lambda i: (i // per_seq, 0, 0)
    halo_prev = lambda i: (jnp.maximum(i * halo_per_tile - 1, 0), 0)
    halo_next = lambda i: (jnp.minimum((i + 1) * halo_per_tile, n_halo - 1), 0)

    def tile_and_halos(lines_per_row, width):
        return [pl.BlockSpec((tm * lines_per_row, width), lambda i: (i, 0)),
                pl.BlockSpec((POOL_HALO * lines_per_row, width), halo_prev),
                pl.BlockSpec((POOL_HALO * lines_per_row, width), halo_next)]

    if pending:
        x1, moe, mod_prev = x
        x_args = [x1, x1, x1, moe, moe, moe, mod_prev]
        x_specs = (tile_and_halos(1, D_MODEL) + tile_and_halos(ROW_CHUNKS, LANES)
                   + [pl.BlockSpec((1, 1, 6 * D_MODEL), seq_block)])
    else:
        x_args = [x, x, x]
        x_specs = tile_and_halos(1, D_MODEL)
    return pl.pallas_call(
        functools.partial(_pool_mix_kernel, seq_len=seq_len, pending=pending),
        out_shape=_route_out_shapes(n_tok),
        grid=(n_tok // tm,),
        in_specs=x_specs + [
            pl.BlockSpec((1, 1, 6 * D_MODEL), seq_block),
            pl.BlockSpec(pool_w.shape, lambda i: (0, 0, 0)),
            pl.BlockSpec(pool_scale.shape, const),
            pl.BlockSpec(rw_t.shape, const),
            pl.BlockSpec(rb.shape, const),
        ],
        out_specs=_route_out_specs(ROUTE_SUBTILES),
        compiler_params=pltpu.CompilerParams(
            dimension_semantics=("arbitrary",), vmem_limit_bytes=VMEM_LIMIT),
        name="pool_mix",
    )(*x_args, mod, pool_w, pool_scale, rw_t, rb)


def _slots_kernel(pstart_ref, topi_ref, rank_ref, dest_ref):
    topi = topi_ref[...]
    start = jnp.zeros_like(topi)
    for e in range(N_EXPERTS):
        start = jnp.where(topi == e, pstart_ref[e], start)
    dest_ref[...] = start + rank_ref[...]


def _slots(pstart, topi, rank):
    n_tiles, _, tm = topi.shape
    tb = math.gcd(n_tiles, 32)
    spec = pl.BlockSpec((tb, TOP_K, tm), lambda i, ps: (i, 0, 0))
    return pl.pallas_call(
        _slots_kernel,
        out_shape=jax.ShapeDtypeStruct(topi.shape, I32),
        grid_spec=pltpu.PrefetchScalarGridSpec(
            num_scalar_prefetch=1, grid=(n_tiles // tb,), in_specs=[spec, spec], out_specs=spec),
        compiler_params=pltpu.CompilerParams(dimension_semantics=("arbitrary",)),
        name="moe_slots",
    )(pstart, topi, rank)


def _sc_chunk_rows(c, tm, width=SC_ROWS):
    per_tile = tm // width
    tile = c // per_tile
    part = c % per_tile
    return [(tile * TOP_K + kk) * per_tile + part for kk in range(TOP_K)]


def _sc_dispatch(hp, dest, n_slots, tm):
    hp = hp.reshape((-1,) + ROW_SHAPE)
    n_tok = hp.shape[0]
    rows_per_w = dest.shape[0] // SC_WORKERS
    chunks_per_w = n_tok // SC_WORKERS // SC_ROWS
    mesh = plsc.VectorSubcoreMesh(core_axis_name="c", subcore_axis_name="s")

    @functools.partial(
        pl.kernel, mesh=mesh,
        out_type=jax.ShapeDtypeStruct((n_slots,) + ROW_SHAPE, I32),
        scratch_types=[
            pltpu.VMEM((rows_per_w, SC_ROWS), I32),
            pltpu.VMEM((SC_ROWS,) + ROW_SHAPE, I32),
            pltpu.SemaphoreType.DMA,
        ],
        name="sc_dispatch",
    )
    def run(hp_hbm, dest_hbm, xs_hbm, dest_v, rows_v, sem):
        wid = lax.axis_index("s") * SC_CORES + lax.axis_index("c")
        pltpu.sync_copy(dest_hbm.at[pl.ds(wid * rows_per_w, rows_per_w)], dest_v)

        @pl.loop(0, chunks_per_w)
        def _(c):
            tok0 = (wid * chunks_per_w + c) * SC_ROWS
            pltpu.sync_copy(hp_hbm.at[pl.ds(tok0, SC_ROWS)], rows_v)
            copies = [pltpu.async_copy(rows_v, xs_hbm.at[dest_v.at[row]], sem)
                      for row in _sc_chunk_rows(c, tm)]
            for cp in copies:
                cp.wait()

    return run(hp, dest).reshape(n_slots * ROW_CHUNKS, LANES)


def _sc_gather_sum(ys, dest, wtm, tm):
    ys = ys.reshape((-1,) + ROW_SHAPE)
    n_tok = wtm.shape[0]
    g = SC_SUM_ROWS
    rows_per_w = dest.shape[0] // SC_WORKERS
    chunks_per_w = n_tok // SC_WORKERS // g
    mesh = plsc.VectorSubcoreMesh(core_axis_name="c", subcore_axis_name="s")

    @functools.partial(
        pl.kernel, mesh=mesh,
        out_type=jax.ShapeDtypeStruct((n_tok,) + ROW_SHAPE, I32),
        scratch_types=[
            pltpu.VMEM((rows_per_w, g), I32),
            pltpu.VMEM((TOP_K, g) + ROW_SHAPE, I32),
            pltpu.VMEM((g, LANES), F32),
            pltpu.VMEM((g,) + ROW_SHAPE, I32),
            pltpu.SemaphoreType.DMA,
        ],
        compiler_params=pltpu.CompilerParams(needs_layout_passes=False),
        name="sc_gather_sum",
    )
    def run(ys_hbm, dest_hbm, w_hbm, out_hbm, dest_v, rows_v, w_v, out_v, sem):
        wid = lax.axis_index("s") * SC_CORES + lax.axis_index("c")
        pltpu.sync_copy(dest_hbm.at[pl.ds(wid * rows_per_w, rows_per_w)], dest_v)

        @pl.loop(0, chunks_per_w)
        def _(c):
            tok0 = (wid * chunks_per_w + c) * g
            copies = [pltpu.async_copy(ys_hbm.at[dest_v.at[row]], rows_v.at[kk], sem)
                      for kk, row in enumerate(_sc_chunk_rows(c, tm, g))]
            pltpu.sync_copy(w_hbm.at[pl.ds(tok0, g)], w_v)
            for cp in copies:
                cp.wait()

            @pl.loop(0, g)
            def _(t):
                wk = [w_v[t, pl.ds(SC_LANES * kk, SC_LANES)] for kk in range(TOP_K)]
                for j in range(ROW_CHUNKS):
                    for i in range(LANES // SC_LANES):
                        lanes = pl.ds(SC_LANES * i, SC_LANES)
                        lo = hi = None
                        for kk in range(TOP_K):
                            word = rows_v[kk, t, j, lanes]
                            a = lax.bitcast_convert_type(lax.shift_left(word, 16), F32) * wk[kk]
                            b = lax.bitcast_convert_type(word & jnp.int32(-65536), F32) * wk[kk]
                            lo = a if lo is None else lo + a
                            hi = b if hi is None else hi + b
                        packed = plsc.pack(lo, hi, format=plsc.PackFormat.INTERLEAVED)
                        out_v[t, j, lanes] = plsc.bitcast(packed, I32)

            pltpu.sync_copy(out_v, out_hbm.at[pl.ds(tok0, g)])

    return run(ys, dest, wtm).reshape(n_tok * ROW_CHUNKS, LANES)


def _expert_kernel(e_ref, first_ref, slot_ref, next_ref, nused_ref,
                   xs_ref, w1_hbm, b1_ref, w2_hbm, b2_ref, ys_ref,
                   w1f_ref, w2f_ref, w1b_ref, w2b_ref, sem, *, layer):
    i = pl.program_id(0)

    def weight_copies(expert, slot):
        return (pltpu.make_async_copy(w1_hbm.at[layer, expert], w1f_ref.at[slot], sem.at[slot, 0]),
                pltpu.make_async_copy(w2_hbm.at[layer, expert], w2f_ref.at[slot], sem.at[slot, 1]))

    @pl.when(first_ref[i] == 1)
    def _():
        slot = slot_ref[i]

        @pl.when(i == 0)
        def _():
            for cp in weight_copies(e_ref[i], slot):
                cp.start()

        for cp in weight_copies(e_ref[i], slot):
            cp.wait()
        w1b_ref[...] = w1f_ref[slot].astype(BF16)
        w2b_ref[...] = w2f_ref[slot].astype(BF16)

        @pl.when(next_ref[i] >= 0)
        def _():
            for cp in weight_copies(next_ref[i], 1 - slot):
                cp.start()

    @pl.when(i < nused_ref[0])
    def _():
        for r in range(0, xs_ref.shape[0] // ROW_CHUNKS, EXPERT_SUB):
            xb = _unpack_rows(xs_ref, r, EXPERT_SUB).astype(BF16)
            gu = jnp.dot(xb, w1b_ref[...], preferred_element_type=F32) + b1_ref[0, 0]
            gate = jnp.minimum(gu[:, :D_FF], SWIGLU_LIMIT)
            up = jnp.clip(gu[:, D_FF:], -SWIGLU_LIMIT, SWIGLU_LIMIT)
            act = (up + 1.0) * (gate * jax.nn.sigmoid(SWIGLU_ALPHA * gate))
            y = jnp.dot(act.astype(BF16), w2b_ref[...], preferred_element_type=F32) + b2_ref[0, 0]
            _pack_rows(y, ys_ref, r)


def _expert_plan(pend, n_blk, bm):
    blk = jnp.arange(n_blk, dtype=I32)
    n_used = (pend[-1] // bm).astype(I32)
    expert = jnp.minimum(jnp.sum(blk[:, None] * bm >= pend[None, :], axis=1), N_EXPERTS - 1)
    expert = expert.astype(I32)
    prev = jnp.concatenate([jnp.full((1,), -1, I32), expert[:-1]])
    first = (blk < n_used) & (expert != prev)
    slot = (jnp.cumsum(first.astype(I32)) - 1) % 2
    later_first = first[None, :] & (blk[None, :] > blk[:, None])
    nxt = jnp.where(jnp.any(later_first, axis=1), expert[jnp.argmax(later_first, axis=1)], -1)
    return (expert, first.astype(I32), slot.astype(I32), nxt.astype(I32), n_used.reshape(1))


def _expert_ffn(plan, xs, layer, w1, b1, w2, b2):
    n_slots = xs.shape[0] // ROW_CHUNKS
    bm = EXPERT_TILE
    used_block = lambda i, e, f, s, nx, nu: (jnp.minimum(i, nu[0] - 1), 0)
    bias_block = lambda i, e, f, s, nx, nu: (layer, e[i], 0, 0)
    return pl.pallas_call(
        functools.partial(_expert_kernel, layer=layer),
        out_shape=jax.ShapeDtypeStruct(xs.shape, I32),
        grid_spec=pltpu.PrefetchScalarGridSpec(
            num_scalar_prefetch=5,
            grid=(n_slots // bm,),
            in_specs=[
                pl.BlockSpec((bm * ROW_CHUNKS, LANES), used_block),
                pl.BlockSpec(memory_space=pl.ANY),
                pl.BlockSpec((1, 1, 1, 2 * D_FF), bias_block),
                pl.BlockSpec(memory_space=pl.ANY),
                pl.BlockSpec((1, 1, 1, D_MODEL), bias_block),
            ],
            out_specs=pl.BlockSpec((bm * ROW_CHUNKS, LANES), used_block),
            scratch_shapes=[
                pltpu.VMEM((2, D_MODEL, 2 * D_FF), F32), pltpu.VMEM((2, D_FF, D_MODEL), F32),
                pltpu.VMEM((D_MODEL, 2 * D_FF), BF16), pltpu.VMEM((D_FF, D_MODEL), BF16),
                pltpu.SemaphoreType.DMA((2, 2)),
            ],
        ),
        compiler_params=pltpu.CompilerParams(
            dimension_semantics=("arbitrary",), vmem_limit_bytes=VMEM_LIMIT),
        name="expert_ffn",
    )(*plan, xs, w1, b1, w2, b2)


def _combine_kernel(moe_ref, x1_ref, mod_ref, fn_ref, o_ref, *, final):
    tm = x1_ref.shape[0]
    g2 = mod_ref[0][:, 5 * D_MODEL:6 * D_MODEL]
    out = x1_ref[...] + g2 * _unpack_rows(moe_ref, 0, tm)
    if final:
        out = _rms(out) * fn_ref[...]
    o_ref[...] = out


def _combine(moe, x1, mod, final_norm, seq_len, final):
    n_tok = x1.shape[0]
    tm = COMBINE_TILE
    per_seq = seq_len // tm
    return pl.pallas_call(
        functools.partial(_combine_kernel, final=final),
        out_shape=jax.ShapeDtypeStruct((n_tok, D_MODEL), F32),
        grid=(n_tok // tm,),
        in_specs=[
            pl.BlockSpec((tm * ROW_CHUNKS, LANES), lambda i: (i, 0)),
            pl.BlockSpec((tm, D_MODEL), lambda i: (i, 0)),
            pl.BlockSpec((1, 1, 6 * D_MODEL), lambda i: (i // per_seq, 0, 0)),
            pl.BlockSpec((1, D_MODEL), lambda i: (0, 0)),
        ],
        out_specs=pl.BlockSpec((tm, D_MODEL), lambda i: (i, 0)),
        compiler_params=pltpu.CompilerParams(
            dimension_semantics=("arbitrary",), vmem_limit_bytes=VMEM_LIMIT),
        name="moe_combine",
    )(moe, x1, mod, final_norm)


def _moe_layers(routes, mods, layer, ffn_w, final_norm, seq_len, final, defer_combine):
    bm = EXPERT_TILE
    plans = []
    for x1, hp, topi, rank, wtm, counts in routes:
        n_slots = x1.shape[0] * TOP_K + N_EXPERTS * bm
        n_blk = n_slots // bm
        cnt = counts[:, 0].astype(I32)
        padded = (cnt + bm - 1) // bm * bm
        pend = jnp.cumsum(padded)
        pstart = (pend - padded).astype(I32)
        dest = _slots(pstart, topi, rank).reshape(-1, SC_ROWS)
        plans.append((n_slots, _expert_plan(pend, n_blk, bm), dest, topi.shape[2]))
    xs = [_sc_dispatch(r[1], dest, n_slots, tm)
          for r, (n_slots, _, dest, tm) in zip(routes, plans)]
    ys = [_expert_ffn(plan, x, layer, *ffn_w) for x, (_, plan, _, _) in zip(xs, plans)]
    moe = [_sc_gather_sum(y, dest.reshape(-1, SC_SUM_ROWS), r[4], tm)
           for y, r, (_, _, dest, tm) in zip(ys, routes, plans)]
    if defer_combine:
        return [(r[0], m, mod) for m, r, mod in zip(moe, routes, mods)]
    return [_combine(m, r[0], mod, final_norm, seq_len, final)
            for m, r, mod in zip(moe, routes, mods)]


def _rope_tables(seq_len):
    inv_freq = 1.0 / (ROPE_THETA ** (jnp.arange(0, D_ROPE, 2, dtype=F32) / D_ROPE))
    ang = jnp.arange(seq_len, dtype=F32)[:, None] * inv_freq[None, :]
    cos, sin = jnp.cos(ang), jnp.sin(ang)
    ones = jnp.ones((seq_len, ROPE_LO), F32)
    zeros_lo = jnp.zeros((seq_len, ROPE_LO), F32)
    zeros_hi = jnp.zeros((seq_len, HEAD_PAD - ROPE_LO - D_ROPE), F32)
    zeros_h = jnp.zeros((seq_len, ROPE_HALF), F32)
    rope_c = jnp.concatenate([ones, cos, cos, zeros_hi], axis=1)
    rope_s1 = jnp.concatenate([zeros_lo, -sin, zeros_h, zeros_hi], axis=1)
    rope_s2 = jnp.concatenate([zeros_lo, zeros_h, sin, zeros_hi], axis=1)
    return cos.T, sin.T, rope_c, rope_s1, rope_s2


def _mla_weights(w_in, q_norm, kv_norm, w_uq, w_ukv, w_o, seq_len):
    d_qk = D_NOPE + D_ROPE
    q_scale = d_qk ** -0.5 * math.log2(math.e)
    pad_pe = jnp.zeros((D_MODEL, HEAD_PAD), F32).at[:, ROPE_LO:ROPE_LO + D_ROPE].set(
        w_in[:, Q_RANK + KV_RANK:])
    w_in_p = jnp.concatenate([w_in[:, :Q_RANK + KV_RANK], pad_pe], axis=1)
    w_uq_s = w_uq * q_scale
    w_kv = w_ukv.reshape(KV_RANK, N_HEADS, D_NOPE + D_V)
    w_uk_p = jnp.pad(w_kv[:, :, :D_NOPE], ((0, 0), (0, 0), (0, HEAD_PAD - D_NOPE)))
    cos_t, sin_t, rope_c, rope_s1, rope_s2 = _rope_tables(seq_len)
    return {
        "w_in": w_in_p.astype(BF16),
        "q_norm": q_norm.reshape(1, Q_RANK),
        "kv_norm": kv_norm.reshape(1, KV_RANK),
        "w_uq_t": w_uq_s.T.astype(BF16),
        "w_uk": w_uk_p.reshape(KV_RANK, N_HEADS * HEAD_PAD).astype(BF16),
        "w_uv_t": w_kv[:, :, D_NOPE:].reshape(KV_RANK, N_HEADS * D_V).T.astype(BF16),
        "w_o": w_o.astype(BF16),
        "cos_t": cos_t, "sin_t": sin_t, "rope_c": rope_c, "rope_s1": rope_s1, "rope_s2": rope_s2,
    }


def _router_weights(router_w, router_b):
    return router_w.T.astype(BF16), router_b.reshape(N_EXPERTS, 1)


def kernel(x_prompt, x_sample, c_prompt, c_sample, ada_w, ada_b, mla_w_in, mla_q_norm,
           mla_kv_norm, mla_w_uq, mla_w_ukv, mla_w_o, pool_w, pool_scale, router_w, router_b,
           moe_w1, moe_b1, moe_w2, moe_b2, final_norm):
    n_prompt, seq_len, _ = x_prompt.shape
    assert x_sample.shape[1] == seq_len and seq_len % (ROUTE_SUBTILES * TOKEN_TILE) == 0
    depth = ada_w.shape[0]
    n_sample = x_sample.shape[0]
    xs = [x_prompt.reshape(-1, D_MODEL), x_sample.reshape(-1, D_MODEL)]
    n_seqs = [n_prompt, n_sample]
    mods = _ada_mod(jnp.concatenate([c_prompt, c_sample], axis=0), ada_w, ada_b)
    fnorm = final_norm.reshape(1, D_MODEL)
    ffn_w = (moe_w1, moe_b1.reshape(depth, N_EXPERTS, 1, 2 * D_FF),
             moe_w2, moe_b2.reshape(depth, N_EXPERTS, 1, D_MODEL))

    for i in range(depth):
        rw_t, rb = _router_weights(router_w[i], router_b[i])
        j = i // 2
        group_mods = [mods[i, :n_prompt].reshape(n_prompt, 1, 6 * D_MODEL),
                      mods[i, n_prompt:].reshape(n_sample, 1, 6 * D_MODEL)]
        if i % 2 == 0:
            w = _mla_weights(mla_w_in[j], mla_q_norm[j], mla_kv_norm[j], mla_w_uq[j],
                             mla_w_ukv[j], mla_w_o[j], seq_len)
        routes = []
        for x, mod in zip(xs, group_mods):
            if i % 2 == 0:
                qt, k, vt = _mla_pre([x], mod, w, seq_len)
                ot = _attention(qt, k, vt, seq_len)
                routes.append(_post_mix(ot, [x], mod, w["w_o"], rw_t, rb, seq_len))
            else:
                routes.append(_pool_mix(x, mod, pool_w[j].astype(BF16),
                                        pool_scale[j].reshape(1, D_MODEL), rw_t, rb, seq_len))
        last = i == depth - 1
        next_is_pool = not last and (i + 1) % 2 == 1
        xs = _moe_layers(routes, group_mods, i, ffn_w, fnorm, seq_len, last, next_is_pool)

    return (xs[0].reshape(n_prompt, seq_len, D_MODEL), xs[1].reshape(n_sample, seq_len, D_MODEL))
```

```python
import functools
import math

import jax
import jax.numpy as jnp
from jax import lax
from jax.experimental import pallas as pl
from jax.experimental.pallas import tpu as pltpu
from jax.experimental.pallas import tpu_sc as plsc

F32 = jnp.float32
BF16 = jnp.bfloat16
I32 = jnp.int32

D_MODEL = 1024
N_HEADS = 16
Q_RANK = 384
KV_RANK = 256
D_NOPE = 64
D_ROPE = 32
D_V = 64
V_ROWS = D_V + 16
ROPE_THETA = 10000.0
POOL_WINDOWS = (2, 4, 8, 16)
POOL_GROUP = D_MODEL // len(POOL_WINDOWS)
N_EXPERTS = 32
TOP_K = 4
D_FF = D_MODEL
SWIGLU_LIMIT = 7.0
SWIGLU_ALPHA = 1.702
EPS = 1e-6

LANES = 128
SUBLANES = 8
HEAD_PAD = 128
ROPE_LO = D_NOPE
ROPE_HALF = D_ROPE // 2
VMEM_LIMIT = 56 * 1024 * 1024

TOKEN_TILE = 256
COMBINE_TILE = 1024
ROUTE_SUBTILES = 4
EXPERT_TILE = 512
EXPERT_SUBS = (256, 256)
ATTN_KEY_CHUNKS = 2
ATTN_HEADS_PER_STEP = 8
POOL_HALO = 8
PACKED = D_MODEL // 2
ROW_CHUNKS = PACKED // LANES
ROW_SHAPE = (ROW_CHUNKS, LANES)

SC_CORES = 2
SC_SUBCORES = 16
SC_WORKERS = SC_CORES * SC_SUBCORES
SC_LANES = 16
SC_ROWS = 64
SC_SUM_ROWS = 32


def _rms(x):
    return x * lax.rsqrt(jnp.mean(x * x, axis=-1, keepdims=True) + EPS)


def _pack_rows(y, out_ref, row0=0):
    n = y.shape[0]
    lo = lax.bitcast_convert_type(y[:, :PACKED].astype(BF16).astype(F32), I32)
    hi = lax.bitcast_convert_type(y[:, PACKED:].astype(BF16).astype(F32), I32)
    words = lax.shift_right_logical(lo, 16) | (hi & jnp.int32(-65536))
    for j in range(ROW_CHUNKS):
        out_ref[pl.ds(row0 * ROW_CHUNKS + j, n, stride=ROW_CHUNKS), :] = (
            words[:, j * LANES:(j + 1) * LANES])


def _unpack_rows(in_ref, row0, n):
    words = jnp.concatenate(
        [in_ref[pl.ds(row0 * ROW_CHUNKS + j, n, stride=ROW_CHUNKS), :] for j in range(ROW_CHUNKS)],
        axis=1)
    lo = lax.bitcast_convert_type(lax.shift_left(words, 16), F32)
    hi = lax.bitcast_convert_type(words & jnp.int32(-65536), F32)
    return jnp.concatenate([lo, hi], axis=1)


def _part_specs(parts, tm):
    if len(parts) == 1:
        return [pl.BlockSpec((tm, D_MODEL), lambda i: (i, 0))]
    n0 = parts[0].shape[0] // tm
    return [pl.BlockSpec((tm, D_MODEL), lambda i: (jnp.minimum(i, n0 - 1), 0)),
            pl.BlockSpec((tm, D_MODEL), lambda i: (jnp.maximum(i - n0, 0), 0))]


def _part_load(refs, n0):
    if len(refs) == 1:
        return refs[0][...]
    return jnp.where(pl.program_id(0) < n0, refs[0][...], refs[1][...])


def _ada_kernel(c_ref, w_ref, b_ref, o_ref):
    c = c_ref[...]
    act = (c * jax.nn.sigmoid(c)).astype(BF16)
    o_ref[0] = jnp.dot(act, w_ref[0].astype(BF16), preferred_element_type=F32) + b_ref[0]


def _ada_mod(c, ada_w, ada_b):
    depth, _, n_out = ada_w.shape
    n_seq = c.shape[0]
    tn = 1536
    return pl.pallas_call(
        _ada_kernel,
        out_shape=jax.ShapeDtypeStruct((depth, n_seq, n_out), F32),
        grid=(depth, n_out // tn),
        in_specs=[
            pl.BlockSpec((n_seq, D_MODEL), lambda l, j: (0, 0)),
            pl.BlockSpec((1, D_MODEL, tn), lambda l, j: (l, 0, j)),
            pl.BlockSpec((1, 1, tn), lambda l, j: (l, 0, j)),
        ],
        out_specs=pl.BlockSpec((1, n_seq, tn), lambda l, j: (l, 0, j)),
        compiler_params=pltpu.CompilerParams(
            dimension_semantics=("arbitrary", "arbitrary"), vmem_limit_bytes=VMEM_LIMIT),
        name="ada_mod",
    )(c, ada_w, ada_b.reshape(depth, 1, n_out))


def _mla_pre_kernel(*refs, n_x, n0):
    x_refs = refs[:n_x]
    (mod_ref, win_ref, qn_ref, kvn_ref, wuqt_ref, wuk_ref, wuvt_ref,
     cost_ref, sint_ref, ck_ref, s1k_ref, s2k_ref, qt_ref, k_ref, vt_ref) = refs[n_x:]
    n_sub, ts = qt_ref.shape[1], qt_ref.shape[3]
    mod = mod_ref[0]
    sh1 = mod[:, 0:D_MODEL]
    sc1 = mod[:, D_MODEL:2 * D_MODEL]
    x = _part_load(x_refs, n0)
    d_qk = D_NOPE + D_ROPE

    def chain(sub):
        rows = slice(sub * ts, (sub + 1) * ts)
        h = (_rms(x[rows, :]) * (1.0 + sc1) + sh1).astype(BF16)
        a = jnp.dot(h, win_ref[...], preferred_element_type=F32)
        yield
        cq = (_rms(a[:, :Q_RANK]) * qn_ref[...]).astype(BF16)
        ckv = (_rms(a[:, Q_RANK:Q_RANK + KV_RANK]) * kvn_ref[...]).astype(BF16)
        kpe = a[:, Q_RANK + KV_RANK:]
        kpe = (kpe * ck_ref[rows, :]
               + pltpu.roll(kpe, LANES - ROPE_HALF, 1) * s1k_ref[rows, :]
               + pltpu.roll(kpe, ROPE_HALF, 1) * s2k_ref[rows, :])
        yield
        qt = lax.dot_general(wuqt_ref[...], cq, (((1,), (1,)), ((), ())),
                             preferred_element_type=F32)
        q3 = qt.reshape(N_HEADS, d_qk, ts)
        x1 = q3[:, ROPE_LO:ROPE_LO + ROPE_HALF, :]
        x2 = q3[:, ROPE_LO + ROPE_HALF:, :]
        cos = cost_ref[:, rows][None]
        sin = sint_ref[:, rows][None]
        q3 = jnp.concatenate(
            [q3[:, :ROPE_LO, :], x1 * cos - x2 * sin, x2 * cos + x1 * sin], axis=1)
        qt_ref[:, sub, :d_qk, :] = q3.astype(BF16)
        qt_ref[:, sub, d_qk:, :] = jnp.zeros((N_HEADS, HEAD_PAD - d_qk, ts), BF16)
        yield
        kn = jnp.dot(ckv, wuk_ref[...], preferred_element_type=F32)
        for hd in range(N_HEADS):
            k_ref[hd, rows, :] = (kn[:, hd * HEAD_PAD:(hd + 1) * HEAD_PAD] + kpe).astype(BF16)
        yield
        vt = lax.dot_general(wuvt_ref[...], ckv, (((1,), (1,)), ((), ())),
                             preferred_element_type=F32)
        vt_ref[:, 0, :D_V, rows] = vt.reshape(N_HEADS, D_V, ts).astype(BF16)
        vt_ref[:, 0, D_V:, rows] = jnp.ones((N_HEADS, V_ROWS - D_V, ts), BF16)

    _interleave(chain(sub) for sub in range(n_sub))


def _mla_pre(x_parts, mod, w, seq_len):
    n_tok = sum(p.shape[0] for p in x_parts)
    n_sub = ROUTE_SUBTILES
    tq = TOKEN_TILE
    tm = n_sub * tq
    per_seq = seq_len // tm
    const = lambda i: (0, 0)
    return pl.pallas_call(
        functools.partial(_mla_pre_kernel, n_x=len(x_parts), n0=x_parts[0].shape[0] // tm),
        out_shape=(
            jax.ShapeDtypeStruct((N_HEADS, n_tok // tq, HEAD_PAD, tq), BF16),
            jax.ShapeDtypeStruct((N_HEADS, n_tok, HEAD_PAD), BF16),
            jax.ShapeDtypeStruct((N_HEADS, n_tok // seq_len, V_ROWS, seq_len), BF16),
        ),
        grid=(n_tok // tm,),
        in_specs=_part_specs(x_parts, tm) + [
            pl.BlockSpec((1, 1, 6 * D_MODEL), lambda i: (i // per_seq, 0, 0)),
            pl.BlockSpec(w["w_in"].shape, const),
            pl.BlockSpec(w["q_norm"].shape, const),
            pl.BlockSpec(w["kv_norm"].shape, const),
            pl.BlockSpec(w["w_uq_t"].shape, const),
            pl.BlockSpec(w["w_uk"].shape, const),
            pl.BlockSpec(w["w_uv_t"].shape, const),
            pl.BlockSpec((ROPE_HALF, tm), lambda i: (0, i % per_seq)),
            pl.BlockSpec((ROPE_HALF, tm), lambda i: (0, i % per_seq)),
            pl.BlockSpec((tm, LANES), lambda i: (i % per_seq, 0)),
            pl.BlockSpec((tm, LANES), lambda i: (i % per_seq, 0)),
            pl.BlockSpec((tm, LANES), lambda i: (i % per_seq, 0)),
        ],
        out_specs=(
            pl.BlockSpec((N_HEADS, n_sub, HEAD_PAD, tq), lambda i: (0, i, 0, 0)),
            pl.BlockSpec((N_HEADS, tm, HEAD_PAD), lambda i: (0, i, 0)),
            pl.BlockSpec((N_HEADS, 1, V_ROWS, tm), lambda i: (0, i // per_seq, 0, i % per_seq)),
        ),
        compiler_params=pltpu.CompilerParams(
            dimension_semantics=("arbitrary",), vmem_limit_bytes=VMEM_LIMIT),
        name="mla_pre",
    )(*x_parts, mod, w["w_in"], w["q_norm"], w["kv_norm"], w["w_uq_t"], w["w_uk"], w["w_uv_t"],
      w["cos_t"], w["sin_t"], w["rope_c"], w["rope_s1"], w["rope_s2"])


def _attention_kernel(qt_ref, k_ref, vt_ref, ot_ref, s0_ref, s1_ref):
    n_heads, n_q = qt_ref.shape[:2]
    n_tiles = n_heads * n_q
    n_keys = k_ref.shape[1]
    kc = n_keys // ATTN_KEY_CHUNKS

    def stage(t_next, s_next_ref, t_cur, s_cur_ref, m_cur):
        m_next, o = None, None
        if t_next is not None:
            h_next, j_next = t_next // n_q, t_next % n_q
        if t_cur is not None:
            h_cur, j_cur = t_cur // n_q, t_cur % n_q
        for c in range(ATTN_KEY_CHUNKS):
            rows = slice(c * kc, (c + 1) * kc)
            if t_next is not None:
                s = jnp.dot(k_ref[h_next, rows, :], qt_ref[h_next, j_next],
                            preferred_element_type=F32)
                s_next_ref[rows, :] = s
                cm = jnp.max(s, axis=0, keepdims=True)
                m_next = cm if m_next is None else jnp.maximum(m_next, cm)
            if t_cur is not None:
                p = jnp.exp2(s_cur_ref[rows, :] - m_cur).astype(BF16)
                part = jnp.dot(vt_ref[h_cur, 0, :, rows], p,
                               preferred_element_type=F32)
                o = part if o is None else o + part
        if t_cur is not None:
            denom = o[D_V:D_V + 1, :]
            ot_ref[h_cur, j_cur] = (o[:D_V, :] * (1.0 / denom)).astype(BF16)
        return m_next

    def body(i, m0):
        t = 2 * i
        m1 = stage(t + 1, s1_ref, t, s0_ref, m0)
        return stage(t + 2, s0_ref, t + 1, s1_ref, m1)

    m0 = lax.fori_loop(0, n_tiles // 2 - 1, body, stage(0, s0_ref, None, None, None))
    m1 = stage(n_tiles - 1, s1_ref, n_tiles - 2, s0_ref, m0)
    stage(None, None, n_tiles - 1, s1_ref, m1)


def _attention(qt, k, vt, seq_len):
    n_heads, n_tiles, _, tq = qt.shape
    per_seq = seq_len // tq
    n_seq = n_tiles // per_seq
    hb = ATTN_HEADS_PER_STEP
    return pl.pallas_call(
        _attention_kernel,
        out_shape=jax.ShapeDtypeStruct((n_heads, n_tiles, D_V, tq), BF16),
        grid=(n_seq, n_heads // hb),
        in_specs=[
            pl.BlockSpec((hb, per_seq, HEAD_PAD, tq), lambda b, h: (h, b, 0, 0)),
            pl.BlockSpec((hb, seq_len, HEAD_PAD), lambda b, h: (h, b, 0)),
            pl.BlockSpec((hb, 1, V_ROWS, seq_len), lambda b, h: (h, b, 0, 0)),
        ],
        out_specs=pl.BlockSpec((hb, per_seq, D_V, tq), lambda b, h: (h, b, 0, 0)),
        scratch_shapes=[pltpu.VMEM((seq_len, tq), F32), pltpu.VMEM((seq_len, tq), F32)],
        compiler_params=pltpu.CompilerParams(
            dimension_semantics=("arbitrary", "arbitrary"), vmem_limit_bytes=VMEM_LIMIT),
        name="attention",
    )(qt, k, vt)


def _route_init(cnt_ref):
    @pl.when(pl.program_id(0) == 0)
    def _():
        cnt_ref[...] = jnp.zeros_like(cnt_ref)


def _interleave(chains):
    chains = list(chains)
    done = object()
    while chains:
        chains = [ch for ch in chains if next(ch, done) is not done]


def _route_tail(x1_chains, mod, rwt_ref, rb_ref, out_refs):
    cnt_ref = out_refs[-1]
    state = {"running": cnt_ref[...][:, 0:1]}

    def chain(sub, x1_chain):
        x1 = yield from x1_chain
        yield from _route_chain(sub, x1, mod, rwt_ref, rb_ref, out_refs, state)

    _interleave(chain(sub, ch) for sub, ch in enumerate(x1_chains))
    cnt_ref[...] = jnp.broadcast_to(state["running"], cnt_ref.shape)


def _route_chain(sub, x1, mod, rwt_ref, rb_ref, out_refs, state):
    x1_ref, hp_ref, topi_ref, rank_ref, wtm_ref, _ = out_refs
    tm = x1.shape[0]
    rows = pl.ds(sub * tm, tm)
    sh2 = mod[:, 3 * D_MODEL:4 * D_MODEL]
    sc2 = mod[:, 4 * D_MODEL:5 * D_MODEL]
    x1_ref[rows, :] = x1
    h2 = _rms(x1) * (1.0 + sc2) + sh2
    _pack_rows(h2, hp_ref, sub * tm)
    logits = lax.dot_general(rwt_ref[...], h2.astype(BF16), (((1,), (1,)), ((), ())),
                             preferred_element_type=F32) + rb_ref[...]
    yield
    e_iota = lax.broadcasted_iota(I32, (N_EXPERTS, tm), 0)
    vals, idxs = [], []
    work = logits
    for _ in range(TOP_K):
        m = jnp.max(work, axis=0, keepdims=True)
        idx = jnp.min(jnp.where(work == m, e_iota, N_EXPERTS), axis=0, keepdims=True)
        vals.append(m)
        idxs.append(idx)
        work = jnp.where(e_iota == idx, -jnp.inf, work)
    ex = [jnp.exp(v - vals[0]) for v in vals]
    inv = 1.0 / (ex[0] + ex[1] + ex[2] + ex[3])
    topw = jnp.concatenate([e * inv for e in ex], axis=0)
    topi_ref[sub] = jnp.concatenate(idxs, axis=0)
    yield

    row = lax.broadcasted_iota(I32, (tm, tm), 0)
    col = lax.broadcasted_iota(I32, (tm, tm), 1)
    earlier = (row < col).astype(BF16)
    running = state["running"]
    ranks = []
    for kk in range(TOP_K):
        onehot = (e_iota == idxs[kk]).astype(F32)
        before = jnp.dot(onehot.astype(BF16), earlier, preferred_element_type=F32)
        ranks.append(jnp.sum(onehot * (running + before), axis=0, keepdims=True))
        running = running + jnp.sum(onehot, axis=1, keepdims=True)
    rank_ref[sub] = jnp.concatenate(ranks, axis=0).astype(I32)
    state["running"] = running
    yield

    wpad = jnp.concatenate(
        [jnp.broadcast_to(topw[kk:kk + 1], (SC_LANES, tm)) for kk in range(TOP_K)]
        + [jnp.zeros((LANES - TOP_K * SC_LANES, tm), F32)], axis=0)
    wtm_ref[rows, :] = wpad.T


def _route_out_shapes(n_tok):
    n_tiles = n_tok // TOKEN_TILE
    return (
        jax.ShapeDtypeStruct((n_tok, D_MODEL), F32),
        jax.ShapeDtypeStruct((n_tok * ROW_CHUNKS, LANES), I32),
        jax.ShapeDtypeStruct((n_tiles, TOP_K, TOKEN_TILE), I32),
        jax.ShapeDtypeStruct((n_tiles, TOP_K, TOKEN_TILE), I32),
        jax.ShapeDtypeStruct((n_tok, LANES), F32),
        jax.ShapeDtypeStruct((N_EXPERTS, LANES), F32),
    )


def _route_out_specs(n_sub):
    tm = n_sub * TOKEN_TILE
    return (
        pl.BlockSpec((tm, D_MODEL), lambda i: (i, 0)),
        pl.BlockSpec((tm * ROW_CHUNKS, LANES), lambda i: (i, 0)),
        pl.BlockSpec((n_sub, TOP_K, TOKEN_TILE), lambda i: (i, 0, 0)),
        pl.BlockSpec((n_sub, TOP_K, TOKEN_TILE), lambda i: (i, 0, 0)),
        pl.BlockSpec((tm, LANES), lambda i: (i, 0)),
        pl.BlockSpec((N_EXPERTS, LANES), lambda i: (0, 0)),
    )


def _post_mix_kernel(ot_ref, *refs, n_x, n0):
    x_refs = refs[:n_x]
    mod_ref, wo_ref, rwt_ref, rb_ref = refs[n_x:n_x + 4]
    out_refs = refs[n_x + 4:]
    _route_init(out_refs[-1])
    n_sub, ts = ot_ref.shape[1], ot_ref.shape[3]
    mod = mod_ref[0]
    g1 = mod[:, 2 * D_MODEL:3 * D_MODEL]
    x = _part_load(x_refs, n0)

    def mixer(sub):
        ot = ot_ref[:, sub].reshape(N_HEADS * D_V, ts)
        mix = lax.dot_general(ot, wo_ref[...], (((0,), (0,)), ((), ())),
                              preferred_element_type=F32)
        x1 = x[sub * ts:(sub + 1) * ts, :] + g1 * mix
        yield
        return x1

    _route_tail([mixer(sub) for sub in range(n_sub)], mod, rwt_ref, rb_ref, out_refs)


def _post_mix(ot, x_parts, mod, w_o, rw_t, rb, seq_len):
    n_tok = sum(p.shape[0] for p in x_parts)
    n_sub = ROUTE_SUBTILES
    tm = n_sub * ot.shape[3]
    per_seq = seq_len // tm
    const = lambda i: (0, 0)
    return pl.pallas_call(
        functools.partial(_post_mix_kernel, n_x=len(x_parts), n0=x_parts[0].shape[0] // tm),
        out_shape=_route_out_shapes(n_tok),
        grid=(n_tok // tm,),
        in_specs=[pl.BlockSpec((N_HEADS, n_sub, D_V, ot.shape[3]), lambda i: (0, i, 0, 0))]
        + _part_specs(x_parts, tm) + [
            pl.BlockSpec((1, 1, 6 * D_MODEL), lambda i: (i // per_seq, 0, 0)),
            pl.BlockSpec(w_o.shape, const),
            pl.BlockSpec(rw_t.shape, const),
            pl.BlockSpec(rb.shape, const),
        ],
        out_specs=_route_out_specs(n_sub),
        compiler_params=pltpu.CompilerParams(
            dimension_semantics=("arbitrary",), vmem_limit_bytes=VMEM_LIMIT),
        name="post_mix",
    )(ot, *x_parts, mod, w_o, rw_t, rb)


def _pool_mix_kernel(*refs, seq_len, pending):
    n_x = 7 if pending else 3
    x_refs = refs[:n_x]
    mod_ref, pw_ref, ps_ref, rwt_ref, rb_ref = refs[n_x:n_x + 5]
    out_refs = refs[n_x + 5:]
    _route_init(out_refs[-1])
    tm = x_refs[0].shape[0]
    per_seq = seq_len // tm
    mod = mod_ref[0]
    sh1 = mod[:, 0:D_MODEL]
    sc1 = mod[:, D_MODEL:2 * D_MODEL]
    g1 = mod[:, 2 * D_MODEL:3 * D_MODEL]
    pos0 = (pl.program_id(0) % per_seq) * tm

    if pending:
        g2_prev = x_refs[6][0][:, 5 * D_MODEL:6 * D_MODEL]

        def layer_input(k):
            x1_ref, moe_ref = x_refs[k], x_refs[3 + k]
            return x1_ref[...] + g2_prev * _unpack_rows(moe_ref, 0, x1_ref.shape[0])
    else:
        def layer_input(k):
            return x_refs[k][...]

    def normed(v):
        return _rms(v) * (1.0 + sc1) + sh1

    x = layer_input(0)
    h = normed(x)
    h_prev = jnp.where(pos0 > 0, normed(layer_input(1)), 0.0)
    h_next = jnp.where(pos0 + tm < seq_len, normed(layer_input(2)), 0.0)
    hext = jnp.concatenate([h_prev, h, h_next], axis=0)
    n_ext = tm + 2 * POOL_HALO
    pos = pos0 + lax.broadcasted_iota(I32, (tm, 1), 0)

    diffs = []
    for g, win in enumerate(POOL_WINDOWS):
        left = win // 2
        right = win - 1 - left
        cols = slice(g * POOL_GROUP, (g + 1) * POOL_GROUP)
        s = hext[:, cols]
        span = 1
        while span < win:
            s = s + pltpu.roll(s, span, 0)
            span *= 2
        if right:
            s = pltpu.roll(s, n_ext - right, 0)
        num = s[POOL_HALO:POOL_HALO + tm, :]
        count = (jnp.minimum(pos + right + 1, seq_len) - jnp.maximum(pos - left, 0)).astype(F32)
        diffs.append((num / count - h[:, cols]).astype(BF16))

    ts = TOKEN_TILE

    def mixer(sub):
        rows = slice(sub * ts, (sub + 1) * ts)
        outs = [jnp.dot(diffs[g][rows, :], pw_ref[g], preferred_element_type=F32)
                for g in range(len(POOL_WINDOWS))]
        x1 = x[rows, :] + g1 * (jnp.concatenate(outs, axis=1) * ps_ref[...])
        yield
        return x1

    _route_tail([mixer(sub) for sub in range(tm // ts)], mod, rwt_ref, rb_ref, out_refs)


def _pool_mix(x, mod, pool_w, pool_scale, rw_t, rb, seq_len):
    pending = isinstance(x, tuple)
    n_tok = (x[0] if pending else x).shape[0]
    tm = ROUTE_SUBTILES * TOKEN_TILE
    per_seq = seq_len // tm
    halo_per_tile = tm // POOL_HALO
    n_halo = n_tok // POOL_HALO
    const = lambda i: (0, 0)
    seq_block = lambda i: (i // per_seq, 0, 0)
    halo_prev = lambda i: (jnp.maximum(i * halo_per_tile - 1, 0), 0)
    halo_next = lambda i: (jnp.minimum((i + 1) * halo_per_tile, n_halo - 1), 0)

    def tile_and_halos(lines_per_row, width):
        return [pl.BlockSpec((tm * lines_per_row, width), lambda i: (i, 0)),
                pl.BlockSpec((POOL_HALO * lines_per_row, width), halo_prev),
                pl.BlockSpec((POOL_HALO * lines_per_row, width), halo_next)]

    if pending:
        x1, moe, mod_prev = x
        x_args = [x1, x1, x1, moe, moe, moe, mod_prev]
        x_specs = (tile_and_halos(1, D_MODEL) + tile_and_halos(ROW_CHUNKS, LANES)
                   + [pl.BlockSpec((1, 1, 6 * D_MODEL), seq_block)])
    else:
        x_args = [x, x, x]
        x_specs = tile_and_halos(1, D_MODEL)
    return pl.pallas_call(
        functools.partial(_pool_mix_kernel, seq_len=seq_len, pending=pending),
        out_shape=_route_out_shapes(n_tok),
        grid=(n_tok // tm,),
        in_specs=x_specs + [
            pl.BlockSpec((1, 1, 6 * D_MODEL), seq_block),
            pl.BlockSpec(pool_w.shape, lambda i: (0, 0, 0)),
            pl.BlockSpec(pool_scale.shape, const),
            pl.BlockSpec(rw_t.shape, const),
            pl.BlockSpec(rb.shape, const),
        ],
        out_specs=_route_out_specs(ROUTE_SUBTILES),
        compiler_params=pltpu.CompilerParams(
            dimension_semantics=("arbitrary",), vmem_limit_bytes=VMEM_LIMIT),
        name="pool_mix",
    )(*x_args, mod, pool_w, pool_scale, rw_t, rb)


def _slots_kernel(pstart_ref, topi_ref, rank_ref, dest_ref):
    topi = topi_ref[...]
    start = jnp.zeros_like(topi)
    for e in range(N_EXPERTS):
        start = jnp.where(topi == e, pstart_ref[e], start)
    dest_ref[...] = start + rank_ref[...]


def _slots(pstart, topi, rank):
    n_tiles, _, tm = topi.shape
    tb = math.gcd(n_tiles, 32)
    spec = pl.BlockSpec((tb, TOP_K, tm), lambda i, ps: (i, 0, 0))
    return pl.pallas_call(
        _slots_kernel,
        out_shape=jax.ShapeDtypeStruct(topi.shape, I32),
        grid_spec=pltpu.PrefetchScalarGridSpec(
            num_scalar_prefetch=1, grid=(n_tiles // tb,), in_specs=[spec, spec], out_specs=spec),
        compiler_params=pltpu.CompilerParams(dimension_semantics=("arbitrary",)),
        name="moe_slots",
    )(pstart, topi, rank)


def _sc_chunk_rows(c, tm, width=SC_ROWS):
    per_tile = tm // width
    tile = c // per_tile
    part = c % per_tile
    return [(tile * TOP_K + kk) * per_tile + part for kk in range(TOP_K)]


def _sc_dispatch(hp, dest, n_slots, tm):
    hp = hp.reshape((-1,) + ROW_SHAPE)
    n_tok = hp.shape[0]
    assert n_tok % (SC_WORKERS * tm) == 0
    rows_per_w = dest.shape[0] // SC_WORKERS
    chunks_per_w = n_tok // SC_WORKERS // SC_ROWS
    mesh = plsc.VectorSubcoreMesh(core_axis_name="c", subcore_axis_name="s")

    @functools.partial(
        pl.kernel, mesh=mesh,
        out_type=jax.ShapeDtypeStruct((n_slots,) + ROW_SHAPE, I32),
        scratch_types=[
            pltpu.VMEM((rows_per_w, SC_ROWS), I32),
            pltpu.VMEM((SC_ROWS,) + ROW_SHAPE, I32),
            pltpu.SemaphoreType.DMA,
        ],
        name="sc_dispatch",
    )
    def run(hp_hbm, dest_hbm, xs_hbm, dest_v, rows_v, sem):
        wid = lax.axis_index("s") * SC_CORES + lax.axis_index("c")
        pltpu.sync_copy(dest_hbm.at[pl.ds(wid * rows_per_w, rows_per_w)], dest_v)

        @pl.loop(0, chunks_per_w)
        def _(c):
            tok0 = (wid * chunks_per_w + c) * SC_ROWS
            pltpu.sync_copy(hp_hbm.at[pl.ds(tok0, SC_ROWS)], rows_v)
            copies = [pltpu.async_copy(rows_v, xs_hbm.at[dest_v.at[row]], sem)
                      for row in _sc_chunk_rows(c, tm)]
            for cp in copies:
                cp.wait()

    return run(hp, dest).reshape(n_slots * ROW_CHUNKS, LANES)


def _sc_gather_sum(ys, dest, wtm, tm):
    ys = ys.reshape((-1,) + ROW_SHAPE)
    n_tok = wtm.shape[0]
    assert n_tok % (SC_WORKERS * tm) == 0
    g = SC_SUM_ROWS
    rows_per_w = dest.shape[0] // SC_WORKERS
    chunks_per_w = n_tok // SC_WORKERS // g
    mesh = plsc.VectorSubcoreMesh(core_axis_name="c", subcore_axis_name="s")

    @functools.partial(
        pl.kernel, mesh=mesh,
        out_type=jax.ShapeDtypeStruct((n_tok,) + ROW_SHAPE, I32),
        scratch_types=[
            pltpu.VMEM((rows_per_w, g), I32),
            pltpu.VMEM((TOP_K, g) + ROW_SHAPE, I32),
            pltpu.VMEM((g, LANES), F32),
            pltpu.VMEM((g,) + ROW_SHAPE, I32),
            pltpu.SemaphoreType.DMA,
        ],
        compiler_params=pltpu.CompilerParams(needs_layout_passes=False),
        name="sc_gather_sum",
    )
    def run(ys_hbm, dest_hbm, w_hbm, out_hbm, dest_v, rows_v, w_v, out_v, sem):
        wid = lax.axis_index("s") * SC_CORES + lax.axis_index("c")
        pltpu.sync_copy(dest_hbm.at[pl.ds(wid * rows_per_w, rows_per_w)], dest_v)

        @pl.loop(0, chunks_per_w)
        def _(c):
            tok0 = (wid * chunks_per_w + c) * g
            copies = [pltpu.async_copy(ys_hbm.at[dest_v.at[row]], rows_v.at[kk], sem)
                      for kk, row in enumerate(_sc_chunk_rows(c, tm, g))]
            pltpu.sync_copy(w_hbm.at[pl.ds(tok0, g)], w_v)
            for cp in copies:
                cp.wait()

            @pl.loop(0, g)
            def _(t):
                wk = [w_v[t, pl.ds(SC_LANES * kk, SC_LANES)] for kk in range(TOP_K)]
                for j in range(ROW_CHUNKS):
                    for i in range(LANES // SC_LANES):
                        lanes = pl.ds(SC_LANES * i, SC_LANES)
                        lo = hi = None
                        for kk in range(TOP_K):
                            word = rows_v[kk, t, j, lanes]
                            a = lax.bitcast_convert_type(lax.shift_left(word, 16), F32) * wk[kk]
                            b = lax.bitcast_convert_type(word & jnp.int32(-65536), F32) * wk[kk]
                            lo = a if lo is None else lo + a
                            hi = b if hi is None else hi + b
                        packed = plsc.pack(lo, hi, format=plsc.PackFormat.INTERLEAVED)
                        out_v[t, j, lanes] = plsc.bitcast(packed, I32)

            pltpu.sync_copy(out_v, out_hbm.at[pl.ds(tok0, g)])

    return run(ys, dest, wtm).reshape(n_tok * ROW_CHUNKS, LANES)


def _expert_kernel(e_ref, first_ref, slot_ref, next_ref, nused_ref,
                   xs_ref, w1_hbm, b1_ref, w2_hbm, b2_ref, ys_ref,
                   w1f_ref, w2f_ref, w1b_ref, w2b_ref, sem, *, layer):
    i = pl.program_id(0)

    def weight_copies(expert, slot):
        return (pltpu.make_async_copy(w1_hbm.at[layer, expert], w1f_ref.at[slot], sem.at[slot, 0]),
                pltpu.make_async_copy(w2_hbm.at[layer, expert], w2f_ref.at[slot], sem.at[slot, 1]))

    @pl.when(first_ref[i] == 1)
    def _():
        slot = slot_ref[i]

        @pl.when(i == 0)
        def _():
            for cp in weight_copies(e_ref[i], slot):
                cp.start()

        for cp in weight_copies(e_ref[i], slot):
            cp.wait()
        w1b_ref[...] = w1f_ref[slot].astype(BF16)
        w2b_ref[...] = w2f_ref[slot].astype(BF16)

        @pl.when(next_ref[i] >= 0)
        def _():
            for cp in weight_copies(next_ref[i], 1 - slot):
                cp.start()

    @pl.when(i < nused_ref[0])
    def _():
        r = 0
        for n in EXPERT_SUBS:
            xb = _unpack_rows(xs_ref, r, n).astype(BF16)
            gu = jnp.dot(xb, w1b_ref[...], preferred_element_type=F32) + b1_ref[0, 0]
            gate = jnp.minimum(gu[:, :D_FF], SWIGLU_LIMIT)
            up = jnp.clip(gu[:, D_FF:], -SWIGLU_LIMIT, SWIGLU_LIMIT)
            act = (up + 1.0) * (gate * jax.nn.sigmoid(SWIGLU_ALPHA * gate))
            y = jnp.dot(act.astype(BF16), w2b_ref[...], preferred_element_type=F32) + b2_ref[0, 0]
            _pack_rows(y, ys_ref, r)
            r += n


def _expert_plan(pend, n_blk, bm):
    blk = jnp.arange(n_blk, dtype=I32)
    n_used = (pend[-1] // bm).astype(I32)
    expert = jnp.minimum(jnp.sum(blk[:, None] * bm >= pend[None, :], axis=1), N_EXPERTS - 1)
    expert = expert.astype(I32)
    prev = jnp.concatenate([jnp.full((1,), -1, I32), expert[:-1]])
    first = (blk < n_used) & (expert != prev)
    slot = (jnp.cumsum(first.astype(I32)) - 1) % 2
    later_first = first[None, :] & (blk[None, :] > blk[:, None])
    nxt = jnp.where(jnp.any(later_first, axis=1), expert[jnp.argmax(later_first, axis=1)], -1)
    return (expert, first.astype(I32), slot.astype(I32), nxt.astype(I32), n_used.reshape(1))


def _expert_ffn(plan, xs, layer, w1, b1, w2, b2):
    n_slots = xs.shape[0] // ROW_CHUNKS
    bm = EXPERT_TILE
    used_block = lambda i, e, f, s, nx, nu: (jnp.minimum(i, nu[0] - 1), 0)
    bias_block = lambda i, e, f, s, nx, nu: (layer, e[i], 0, 0)
    return pl.pallas_call(
        functools.partial(_expert_kernel, layer=layer),
        out_shape=jax.ShapeDtypeStruct(xs.shape, I32),
        grid_spec=pltpu.PrefetchScalarGridSpec(
            num_scalar_prefetch=5,
            grid=(n_slots // bm,),
            in_specs=[
                pl.BlockSpec((bm * ROW_CHUNKS, LANES), used_block),
                pl.BlockSpec(memory_space=pl.ANY),
                pl.BlockSpec((1, 1, 1, 2 * D_FF), bias_block),
                pl.BlockSpec(memory_space=pl.ANY),
                pl.BlockSpec((1, 1, 1, D_MODEL), bias_block),
            ],
            out_specs=pl.BlockSpec((bm * ROW_CHUNKS, LANES), used_block),
            scratch_shapes=[
                pltpu.VMEM((2, D_MODEL, 2 * D_FF), F32), pltpu.VMEM((2, D_FF, D_MODEL), F32),
                pltpu.VMEM((D_MODEL, 2 * D_FF), BF16), pltpu.VMEM((D_FF, D_MODEL), BF16),
                pltpu.SemaphoreType.DMA((2, 2)),
            ],
        ),
        compiler_params=pltpu.CompilerParams(
            dimension_semantics=("arbitrary",), vmem_limit_bytes=VMEM_LIMIT),
        name="expert_ffn",
    )(*plan, xs, w1, b1, w2, b2)


def _combine_kernel(moe_ref, x1_ref, mod_ref, fn_ref, o_ref, *, final):
    tm = x1_ref.shape[0]
    g2 = mod_ref[0][:, 5 * D_MODEL:6 * D_MODEL]
    out = x1_ref[...] + g2 * _unpack_rows(moe_ref, 0, tm)
    if final:
        out = _rms(out) * fn_ref[...]
    o_ref[...] = out


def _combine(moe, x1, mod, final_norm, seq_len, final):
    n_tok = x1.shape[0]
    tm = COMBINE_TILE
    per_seq = seq_len // tm
    return pl.pallas_call(
        functools.partial(_combine_kernel, final=final),
        out_shape=jax.ShapeDtypeStruct((n_tok, D_MODEL), F32),
        grid=(n_tok // tm,),
        in_specs=[
            pl.BlockSpec((tm * ROW_CHUNKS, LANES), lambda i: (i, 0)),
            pl.BlockSpec((tm, D_MODEL), lambda i: (i, 0)),
            pl.BlockSpec((1, 1, 6 * D_MODEL), lambda i: (i // per_seq, 0, 0)),
            pl.BlockSpec((1, D_MODEL), lambda i: (0, 0)),
        ],
        out_specs=pl.BlockSpec((tm, D_MODEL), lambda i: (i, 0)),
        compiler_params=pltpu.CompilerParams(
            dimension_semantics=("arbitrary",), vmem_limit_bytes=VMEM_LIMIT),
        name="moe_combine",
    )(moe, x1, mod, final_norm)


def _moe_layers(routes, mods, layer, ffn_w, final_norm, seq_len, final, defer_combine):
    bm = EXPERT_TILE
    plans = []
    for x1, hp, topi, rank, wtm, counts in routes:
        n_slots = x1.shape[0] * TOP_K + N_EXPERTS * bm
        n_blk = n_slots // bm
        cnt = counts[:, 0].astype(I32)
        padded = (cnt + bm - 1) // bm * bm
        pend = jnp.cumsum(padded)
        pstart = (pend - padded).astype(I32)
        dest = _slots(pstart, topi, rank).reshape(-1, SC_ROWS)
        plans.append((n_slots, _expert_plan(pend, n_blk, bm), dest, topi.shape[2]))
    xs = [_sc_dispatch(r[1], dest, n_slots, tm)
          for r, (n_slots, _, dest, tm) in zip(routes, plans)]
    ys = [_expert_ffn(plan, x, layer, *ffn_w) for x, (_, plan, _, _) in zip(xs, plans)]
    moe = [_sc_gather_sum(y, dest.reshape(-1, SC_SUM_ROWS), r[4], tm)
           for y, r, (_, _, dest, tm) in zip(ys, routes, plans)]
    if defer_combine:
        return [(r[0], m, mod) for m, r, mod in zip(moe, routes, mods)]
    return [_combine(m, r[0], mod, final_norm, seq_len, final)
            for m, r, mod in zip(moe, routes, mods)]


def _rope_tables(seq_len):
    inv_freq = 1.0 / (ROPE_THETA ** (jnp.arange(0, D_ROPE, 2, dtype=F32) / D_ROPE))
    ang = jnp.arange(seq_len, dtype=F32)[:, None] * inv_freq[None, :]
    cos, sin = jnp.cos(ang), jnp.sin(ang)
    ones = jnp.ones((seq_len, ROPE_LO), F32)
    zeros_lo = jnp.zeros((seq_len, ROPE_LO), F32)
    zeros_hi = jnp.zeros((seq_len, HEAD_PAD - ROPE_LO - D_ROPE), F32)
    zeros_h = jnp.zeros((seq_len, ROPE_HALF), F32)
    rope_c = jnp.concatenate([ones, cos, cos, zeros_hi], axis=1)
    rope_s1 = jnp.concatenate([zeros_lo, -sin, zeros_h, zeros_hi], axis=1)
    rope_s2 = jnp.concatenate([zeros_lo, zeros_h, sin, zeros_hi], axis=1)
    return cos.T, sin.T, rope_c, rope_s1, rope_s2


def _mla_weights(w_in, q_norm, kv_norm, w_uq, w_ukv, w_o, seq_len):
    d_qk = D_NOPE + D_ROPE
    q_scale = d_qk ** -0.5 * math.log2(math.e)
    pad_pe = jnp.zeros((D_MODEL, HEAD_PAD), F32).at[:, ROPE_LO:ROPE_LO + D_ROPE].set(
        w_in[:, Q_RANK + KV_RANK:])
    w_in_p = jnp.concatenate([w_in[:, :Q_RANK + KV_RANK], pad_pe], axis=1)
    w_uq_s = w_uq * q_scale
    w_kv = w_ukv.reshape(KV_RANK, N_HEADS, D_NOPE + D_V)
    w_uk_p = jnp.pad(w_kv[:, :, :D_NOPE], ((0, 0), (0, 0), (0, HEAD_PAD - D_NOPE)))
    cos_t, sin_t, rope_c, rope_s1, rope_s2 = _rope_tables(seq_len)
    return {
        "w_in": w_in_p.astype(BF16),
        "q_norm": q_norm.reshape(1, Q_RANK),
        "kv_norm": kv_norm.reshape(1, KV_RANK),
        "w_uq_t": w_uq_s.T.astype(BF16),
        "w_uk": w_uk_p.reshape(KV_RANK, N_HEADS * HEAD_PAD).astype(BF16),
        "w_uv_t": w_kv[:, :, D_NOPE:].reshape(KV_RANK, N_HEADS * D_V).T.astype(BF16),
        "w_o": w_o.astype(BF16),
        "cos_t": cos_t, "sin_t": sin_t, "rope_c": rope_c, "rope_s1": rope_s1, "rope_s2": rope_s2,
    }


def _router_weights(router_w, router_b):
    return router_w.T.astype(BF16), router_b.reshape(N_EXPERTS, 1)


def kernel(x_prompt, x_sample, c_prompt, c_sample, ada_w, ada_b, mla_w_in, mla_q_norm,
           mla_kv_norm, mla_w_uq, mla_w_ukv, mla_w_o, pool_w, pool_scale, router_w, router_b,
           moe_w1, moe_b1, moe_w2, moe_b2, final_norm):
    n_prompt, seq_len, _ = x_prompt.shape
    assert x_sample.shape[1] == seq_len and seq_len % (ROUTE_SUBTILES * TOKEN_TILE) == 0
    depth = ada_w.shape[0]
    n_sample = x_sample.shape[0]
    xs = [x_prompt.reshape(-1, D_MODEL), x_sample.reshape(-1, D_MODEL)]
    n_seqs = [n_prompt, n_sample]
    mods = _ada_mod(jnp.concatenate([c_prompt, c_sample], axis=0), ada_w, ada_b)
    fnorm = final_norm.reshape(1, D_MODEL)
    ffn_w = (moe_w1, moe_b1.reshape(depth, N_EXPERTS, 1, 2 * D_FF),
             moe_w2, moe_b2.reshape(depth, N_EXPERTS, 1, D_MODEL))

    for i in range(depth):
        rw_t, rb = _router_weights(router_w[i], router_b[i])
        j = i // 2
        group_mods = [mods[i, :n_prompt].reshape(n_prompt, 1, 6 * D_MODEL),
                      mods[i, n_prompt:].reshape(n_sample, 1, 6 * D_MODEL)]
        if i % 2 == 0:
            w = _mla_weights(mla_w_in[j], mla_q_norm[j], mla_kv_norm[j], mla_w_uq[j],
                             mla_w_ukv[j], mla_w_o[j], seq_len)
        routes = []
        for x, mod in zip(xs, group_mods):
            if i % 2 == 0:
                qt, k, vt = _mla_pre([x], mod, w, seq_len)
                ot = _attention(qt, k, vt, seq_len)
                routes.append(_post_mix(ot, [x], mod, w["w_o"], rw_t, rb, seq_len))
            else:
                routes.append(_pool_mix(x, mod, pool_w[j].astype(BF16),
                                        pool_scale[j].reshape(1, D_MODEL), rw_t, rb, seq_len))
        last = i == depth - 1
        next_is_pool = not last and (i + 1) % 2 == 1
        xs = _moe_layers(routes, group_mods, i, ffn_w, fnorm, seq_len, last, next_is_pool)

    return (xs[0].reshape(n_prompt, seq_len, D_MODEL), xs[1].reshape(n_sample, seq_len, D_MODEL))
```

```python
import functools
import math

import jax
import jax.numpy as jnp
from jax import lax
from jax.experimental import pallas as pl
from jax.experimental.pallas import tpu as pltpu
from jax.experimental.pallas import tpu_sc as plsc

F32 = jnp.float32
BF16 = jnp.bfloat16
I32 = jnp.int32

D_MODEL = 1024
N_HEADS = 16
Q_RANK = 384
KV_RANK = 256
D_NOPE = 64
D_ROPE = 32
D_V = 64
V_ROWS = D_V + 16
ROPE_THETA = 10000.0
POOL_WINDOWS = (2, 4, 8, 16)
POOL_GROUP = D_MODEL // len(POOL_WINDOWS)
N_EXPERTS = 32
TOP_K = 4
D_FF = D_MODEL
SWIGLU_LIMIT = 7.0
SWIGLU_ALPHA = 1.702
EPS = 1e-6

LANES = 128
HEAD_PAD = 128
ROPE_LO = D_NOPE
ROPE_HALF = D_ROPE // 2
VMEM_LIMIT = 56 * 1024 * 1024

TOKEN_TILE = 256
COMBINE_TILE = 1024
ROUTE_SUBTILES = 4
EXPERT_TILE = 512
EXPERT_SUBS = (256, 256)
ATTN_KEY_CHUNKS = 2
ATTN_HEADS_PER_STEP = 8
POOL_HALO = 8
PACKED = D_MODEL // 2
ROW_CHUNKS = PACKED // LANES
ROW_SHAPE = (ROW_CHUNKS, LANES)
HIGH_HALF = -65536
ADA_COLS = 1536

SC_CORES = 2
SC_SUBCORES = 16
SC_WORKERS = SC_CORES * SC_SUBCORES
SC_LANES = 16
SC_ROWS = 64
SC_SUM_ROWS = 32


def _rms(x):
    return x * lax.rsqrt(jnp.mean(x * x, axis=-1, keepdims=True) + EPS)


def _pack_rows(y, out_ref, row0=0):
    n = y.shape[0]
    lo = lax.bitcast_convert_type(y[:, :PACKED].astype(BF16).astype(F32), I32)
    hi = lax.bitcast_convert_type(y[:, PACKED:].astype(BF16).astype(F32), I32)
    words = lax.shift_right_logical(lo, 16) | (hi & jnp.int32(HIGH_HALF))
    for j in range(ROW_CHUNKS):
        out_ref[pl.ds(row0 * ROW_CHUNKS + j, n, stride=ROW_CHUNKS), :] = (
            words[:, j * LANES:(j + 1) * LANES])


def _unpack_rows(in_ref, row0, n):
    words = jnp.concatenate(
        [in_ref[pl.ds(row0 * ROW_CHUNKS + j, n, stride=ROW_CHUNKS), :] for j in range(ROW_CHUNKS)],
        axis=1)
    lo = lax.bitcast_convert_type(lax.shift_left(words, 16), F32)
    hi = lax.bitcast_convert_type(words & jnp.int32(HIGH_HALF), F32)
    return jnp.concatenate([lo, hi], axis=1)


def _interleave(chains):
    chains = list(chains)
    done = object()
    while chains:
        chains = [ch for ch in chains if next(ch, done) is not done]


def _ada_kernel(c_ref, w_ref, b_ref, o_ref):
    c = c_ref[...]
    act = (c * jax.nn.sigmoid(c)).astype(BF16)
    o_ref[0] = jnp.dot(act, w_ref[0].astype(BF16), preferred_element_type=F32) + b_ref[0]


def _ada_mod(c, ada_w, ada_b):
    depth, _, n_out = ada_w.shape
    n_seq = c.shape[0]
    tn = ADA_COLS
    return pl.pallas_call(
        _ada_kernel,
        out_shape=jax.ShapeDtypeStruct((depth, n_seq, n_out), F32),
        grid=(depth, n_out // tn),
        in_specs=[
            pl.BlockSpec((n_seq, D_MODEL), lambda l, j: (0, 0)),
            pl.BlockSpec((1, D_MODEL, tn), lambda l, j: (l, 0, j)),
            pl.BlockSpec((1, 1, tn), lambda l, j: (l, 0, j)),
        ],
        out_specs=pl.BlockSpec((1, n_seq, tn), lambda l, j: (l, 0, j)),
        compiler_params=pltpu.CompilerParams(
            dimension_semantics=("arbitrary", "arbitrary"), vmem_limit_bytes=VMEM_LIMIT),
        name="ada_mod",
    )(c, ada_w, ada_b.reshape(depth, 1, n_out))


def _mla_pre_kernel(x_ref, mod_ref, win_ref, qn_ref, kvn_ref, wuqt_ref, wuk_ref, wuvt_ref,
                    cost_ref, sint_ref, ck_ref, s1k_ref, s2k_ref, qt_ref, k_ref, vt_ref):
    n_sub, ts = qt_ref.shape[1], qt_ref.shape[3]
    mod = mod_ref[0]
    sh1 = mod[:, 0:D_MODEL]
    sc1 = mod[:, D_MODEL:2 * D_MODEL]
    x = x_ref[...]
    d_qk = D_NOPE + D_ROPE

    def chain(sub):
        rows = slice(sub * ts, (sub + 1) * ts)
        h = (_rms(x[rows, :]) * (1.0 + sc1) + sh1).astype(BF16)
        a = jnp.dot(h, win_ref[...], preferred_element_type=F32)
        yield
        cq = (_rms(a[:, :Q_RANK]) * qn_ref[...]).astype(BF16)
        ckv = (_rms(a[:, Q_RANK:Q_RANK + KV_RANK]) * kvn_ref[...]).astype(BF16)
        kpe = a[:, Q_RANK + KV_RANK:]
        kpe = (kpe * ck_ref[rows, :]
               + pltpu.roll(kpe, LANES - ROPE_HALF, 1) * s1k_ref[rows, :]
               + pltpu.roll(kpe, ROPE_HALF, 1) * s2k_ref[rows, :])
        yield
        qt = lax.dot_general(wuqt_ref[...], cq, (((1,), (1,)), ((), ())),
                             preferred_element_type=F32)
        q3 = qt.reshape(N_HEADS, d_qk, ts)
        x1 = q3[:, ROPE_LO:ROPE_LO + ROPE_HALF, :]
        x2 = q3[:, ROPE_LO + ROPE_HALF:, :]
        cos = cost_ref[:, rows][None]
        sin = sint_ref[:, rows][None]
        q3 = jnp.concatenate(
            [q3[:, :ROPE_LO, :], x1 * cos - x2 * sin, x2 * cos + x1 * sin], axis=1)
        qt_ref[:, sub, :d_qk, :] = q3.astype(BF16)
        qt_ref[:, sub, d_qk:, :] = jnp.zeros((N_HEADS, HEAD_PAD - d_qk, ts), BF16)
        yield
        kn = jnp.dot(ckv, wuk_ref[...], preferred_element_type=F32)
        for hd in range(N_HEADS):
            k_ref[hd, rows, :] = (kn[:, hd * HEAD_PAD:(hd + 1) * HEAD_PAD] + kpe).astype(BF16)
        yield
        vt = lax.dot_general(wuvt_ref[...], ckv, (((1,), (1,)), ((), ())),
                             preferred_element_type=F32)
        vt_ref[:, 0, :D_V, rows] = vt.reshape(N_HEADS, D_V, ts).astype(BF16)
        vt_ref[:, 0, D_V:, rows] = jnp.ones((N_HEADS, V_ROWS - D_V, ts), BF16)

    _interleave(chain(sub) for sub in range(n_sub))


def _mla_pre(x, mod, w, seq_len):
    n_tok = x.shape[0]
    n_sub = ROUTE_SUBTILES
    tq = TOKEN_TILE
    tm = n_sub * tq
    per_seq = seq_len // tm
    const = lambda i: (0, 0)
    return pl.pallas_call(
        _mla_pre_kernel,
        out_shape=(
            jax.ShapeDtypeStruct((N_HEADS, n_tok // tq, HEAD_PAD, tq), BF16),
            jax.ShapeDtypeStruct((N_HEADS, n_tok, HEAD_PAD), BF16),
            jax.ShapeDtypeStruct((N_HEADS, n_tok // seq_len, V_ROWS, seq_len), BF16),
        ),
        grid=(n_tok // tm,),
        in_specs=[
            pl.BlockSpec((tm, D_MODEL), lambda i: (i, 0)),
            pl.BlockSpec((1, 1, 6 * D_MODEL), lambda i: (i // per_seq, 0, 0)),
            pl.BlockSpec(w["w_in"].shape, const),
            pl.BlockSpec(w["q_norm"].shape, const),
            pl.BlockSpec(w["kv_norm"].shape, const),
            pl.BlockSpec(w["w_uq_t"].shape, const),
            pl.BlockSpec(w["w_uk"].shape, const),
            pl.BlockSpec(w["w_uv_t"].shape, const),
            pl.BlockSpec((ROPE_HALF, tm), lambda i: (0, i % per_seq)),
            pl.BlockSpec((ROPE_HALF, tm), lambda i: (0, i % per_seq)),
            pl.BlockSpec((tm, LANES), lambda i: (i % per_seq, 0)),
            pl.BlockSpec((tm, LANES), lambda i: (i % per_seq, 0)),
            pl.BlockSpec((tm, LANES), lambda i: (i % per_seq, 0)),
        ],
        out_specs=(
            pl.BlockSpec((N_HEADS, n_sub, HEAD_PAD, tq), lambda i: (0, i, 0, 0)),
            pl.BlockSpec((N_HEADS, tm, HEAD_PAD), lambda i: (0, i, 0)),
            pl.BlockSpec((N_HEADS, 1, V_ROWS, tm), lambda i: (0, i // per_seq, 0, i % per_seq)),
        ),
        compiler_params=pltpu.CompilerParams(
            dimension_semantics=("arbitrary",), vmem_limit_bytes=VMEM_LIMIT),
        name="mla_pre",
    )(x, mod, w["w_in"], w["q_norm"], w["kv_norm"], w["w_uq_t"], w["w_uk"], w["w_uv_t"],
      w["cos_t"], w["sin_t"], w["rope_c"], w["rope_s1"], w["rope_s2"])


def _attention_kernel(qt_ref, k_ref, vt_ref, ot_ref, s0_ref, s1_ref):
    n_heads, n_q = qt_ref.shape[:2]
    n_tiles = n_heads * n_q
    n_keys = k_ref.shape[1]
    kc = n_keys // ATTN_KEY_CHUNKS

    def stage(t_next, s_next_ref, t_cur, s_cur_ref, m_cur):
        m_next, o = None, None
        if t_next is not None:
            h_next, j_next = t_next // n_q, t_next % n_q
        if t_cur is not None:
            h_cur, j_cur = t_cur // n_q, t_cur % n_q
        for c in range(ATTN_KEY_CHUNKS):
            rows = slice(c * kc, (c + 1) * kc)
            if t_next is not None:
                s = jnp.dot(k_ref[h_next, rows, :], qt_ref[h_next, j_next],
                            preferred_element_type=F32)
                s_next_ref[rows, :] = s
                cm = jnp.max(s, axis=0, keepdims=True)
                m_next = cm if m_next is None else jnp.maximum(m_next, cm)
            if t_cur is not None:
                p = jnp.exp2(s_cur_ref[rows, :] - m_cur).astype(BF16)
                part = jnp.dot(vt_ref[h_cur, 0, :, rows], p,
                               preferred_element_type=F32)
                o = part if o is None else o + part
        if t_cur is not None:
            denom = o[D_V:D_V + 1, :]
            ot_ref[h_cur, j_cur] = (o[:D_V, :] * (1.0 / denom)).astype(BF16)
        return m_next

    def body(i, m0):
        t = 2 * i
        m1 = stage(t + 1, s1_ref, t, s0_ref, m0)
        return stage(t + 2, s0_ref, t + 1, s1_ref, m1)

    m0 = lax.fori_loop(0, n_tiles // 2 - 1, body, stage(0, s0_ref, None, None, None))
    m1 = stage(n_tiles - 1, s1_ref, n_tiles - 2, s0_ref, m0)
    stage(None, None, n_tiles - 1, s1_ref, m1)


def _attention(qt, k, vt, seq_len):
    n_heads, n_tiles, _, tq = qt.shape
    per_seq = seq_len // tq
    n_seq = n_tiles // per_seq
    hb = ATTN_HEADS_PER_STEP
    return pl.pallas_call(
        _attention_kernel,
        out_shape=jax.ShapeDtypeStruct((n_heads, n_tiles, D_V, tq), BF16),
        grid=(n_seq, n_heads // hb),
        in_specs=[
            pl.BlockSpec((hb, per_seq, HEAD_PAD, tq), lambda b, h: (h, b, 0, 0)),
            pl.BlockSpec((hb, seq_len, HEAD_PAD), lambda b, h: (h, b, 0)),
            pl.BlockSpec((hb, 1, V_ROWS, seq_len), lambda b, h: (h, b, 0, 0)),
        ],
        out_specs=pl.BlockSpec((hb, per_seq, D_V, tq), lambda b, h: (h, b, 0, 0)),
        scratch_shapes=[pltpu.VMEM((seq_len, tq), F32), pltpu.VMEM((seq_len, tq), F32)],
        compiler_params=pltpu.CompilerParams(
            dimension_semantics=("arbitrary", "arbitrary"), vmem_limit_bytes=VMEM_LIMIT),
        name="attention",
    )(qt, k, vt)


def _route_init(cnt_ref):
    @pl.when(pl.program_id(0) == 0)
    def _():
        cnt_ref[...] = jnp.zeros_like(cnt_ref)


def _route_tail(x1_chains, mod, rwt_ref, rb_ref, out_refs):
    cnt_ref = out_refs[-1]
    ts = TOKEN_TILE
    row = lax.broadcasted_iota(I32, (ts, ts), 0)
    col = lax.broadcasted_iota(I32, (ts, ts), 1)
    state = {
        "running": cnt_ref[...][:, 0:1],
        "earlier": (row < col).astype(BF16),
        "e_iota": lax.broadcasted_iota(I32, (N_EXPERTS, ts), 0),
    }

    def chain(sub, x1_chain):
        x1 = yield from x1_chain
        yield from _route_chain(sub, x1, mod, rwt_ref, rb_ref, out_refs, state)

    _interleave(chain(sub, ch) for sub, ch in enumerate(x1_chains))
    cnt_ref[...] = jnp.broadcast_to(state["running"], cnt_ref.shape)


def _route_chain(sub, x1, mod, rwt_ref, rb_ref, out_refs, state):
    x1_ref, hp_ref, topi_ref, rank_ref, wtm_ref, _ = out_refs
    tm = x1.shape[0]
    rows = pl.ds(sub * tm, tm)
    sh2 = mod[:, 3 * D_MODEL:4 * D_MODEL]
    sc2 = mod[:, 4 * D_MODEL:5 * D_MODEL]
    x1_ref[rows, :] = x1
    h2 = _rms(x1) * (1.0 + sc2) + sh2
    _pack_rows(h2, hp_ref, sub * tm)
    logits = lax.dot_general(rwt_ref[...], h2.astype(BF16), (((1,), (1,)), ((), ())),
                             preferred_element_type=F32) + rb_ref[...]
    yield
    e_iota = state["e_iota"]
    vals, idxs = [], []
    work = logits
    for _ in range(TOP_K):
        m = jnp.max(work, axis=0, keepdims=True)
        idx = jnp.min(jnp.where(work == m, e_iota, N_EXPERTS), axis=0, keepdims=True)
        vals.append(m)
        idxs.append(idx)
        work = jnp.where(e_iota == idx, -jnp.inf, work)
    ex = [jnp.exp(v - vals[0]) for v in vals]
    inv = 1.0 / (ex[0] + ex[1] + ex[2] + ex[3])
    topw = jnp.concatenate([e * inv for e in ex], axis=0)
    topi_ref[sub] = jnp.concatenate(idxs, axis=0)
    yield

    earlier = state["earlier"]
    running = state["running"]
    ranks = []
    for kk in range(TOP_K):
        onehot = (e_iota == idxs[kk]).astype(F32)
        before = jnp.dot(onehot.astype(BF16), earlier, preferred_element_type=F32)
        ranks.append(jnp.sum(onehot * (running + before), axis=0, keepdims=True))
        running = running + jnp.sum(onehot, axis=1, keepdims=True)
    rank_ref[sub] = jnp.concatenate(ranks, axis=0).astype(I32)
    state["running"] = running
    yield

    wpad = jnp.concatenate(
        [jnp.broadcast_to(topw[kk:kk + 1], (SC_LANES, tm)) for kk in range(TOP_K)]
        + [jnp.zeros((LANES - TOP_K * SC_LANES, tm), F32)], axis=0)
    wtm_ref[rows, :] = wpad.T


def _route_out_shapes(n_tok):
    n_tiles = n_tok // TOKEN_TILE
    return (
        jax.ShapeDtypeStruct((n_tok, D_MODEL), F32),
        jax.ShapeDtypeStruct((n_tok * ROW_CHUNKS, LANES), I32),
        jax.ShapeDtypeStruct((n_tiles, TOP_K, TOKEN_TILE), I32),
        jax.ShapeDtypeStruct((n_tiles, TOP_K, TOKEN_TILE), I32),
        jax.ShapeDtypeStruct((n_tok, LANES), F32),
        jax.ShapeDtypeStruct((N_EXPERTS, LANES), F32),
    )


def _route_out_specs(n_sub):
    tm = n_sub * TOKEN_TILE
    return (
        pl.BlockSpec((tm, D_MODEL), lambda i: (i, 0)),
        pl.BlockSpec((tm * ROW_CHUNKS, LANES), lambda i: (i, 0)),
        pl.BlockSpec((n_sub, TOP_K, TOKEN_TILE), lambda i: (i, 0, 0)),
        pl.BlockSpec((n_sub, TOP_K, TOKEN_TILE), lambda i: (i, 0, 0)),
        pl.BlockSpec((tm, LANES), lambda i: (i, 0)),
        pl.BlockSpec((N_EXPERTS, LANES), lambda i: (0, 0)),
    )


def _post_mix_kernel(ot_ref, x_ref, mod_ref, wo_ref, rwt_ref, rb_ref, *out_refs):
    _route_init(out_refs[-1])
    n_sub, ts = ot_ref.shape[1], ot_ref.shape[3]
    mod = mod_ref[0]
    g1 = mod[:, 2 * D_MODEL:3 * D_MODEL]
    x = x_ref[...]

    def mixer(sub):
        ot = ot_ref[:, sub].reshape(N_HEADS * D_V, ts)
        mix = lax.dot_general(ot, wo_ref[...], (((0,), (0,)), ((), ())),
                              preferred_element_type=F32)
        x1 = x[sub * ts:(sub + 1) * ts, :] + g1 * mix
        yield
        return x1

    _route_tail([mixer(sub) for sub in range(n_sub)], mod, rwt_ref, rb_ref, out_refs)


def _post_mix(ot, x, mod, w_o, rw_t, rb, seq_len):
    n_tok = x.shape[0]
    n_sub = ROUTE_SUBTILES
    tm = n_sub * ot.shape[3]
    per_seq = seq_len // tm
    const = lambda i: (0, 0)
    return pl.pallas_call(
        _post_mix_kernel,
        out_shape=_route_out_shapes(n_tok),
        grid=(n_tok // tm,),
        in_specs=[
            pl.BlockSpec((N_HEADS, n_sub, D_V, ot.shape[3]), lambda i: (0, i, 0, 0)),
            pl.BlockSpec((tm, D_MODEL), lambda i: (i, 0)),
            pl.BlockSpec((1, 1, 6 * D_MODEL), lambda i: (i // per_seq, 0, 0)),
            pl.BlockSpec(w_o.shape, const),
            pl.BlockSpec(rw_t.shape, const),
            pl.BlockSpec(rb.shape, const),
        ],
        out_specs=_route_out_specs(n_sub),
        compiler_params=pltpu.CompilerParams(
            dimension_semantics=("arbitrary",), vmem_limit_bytes=VMEM_LIMIT),
        name="post_mix",
    )(ot, x, mod, w_o, rw_t, rb)


def _pool_mix_kernel(*refs, seq_len, pending):
    n_x = 7 if pending else 3
    x_refs = refs[:n_x]
    mod_ref, pw_ref, ps_ref, rwt_ref, rb_ref = refs[n_x:n_x + 5]
    out_refs = refs[n_x + 5:]
    _route_init(out_refs[-1])
    tm = x_refs[0].shape[0]
    per_seq = seq_len // tm
    mod = mod_ref[0]
    sh1 = mod[:, 0:D_MODEL]
    sc1 = mod[:, D_MODEL:2 * D_MODEL]
    g1 = mod[:, 2 * D_MODEL:3 * D_MODEL]
    pos0 = (pl.program_id(0) % per_seq) * tm

    if pending:
        g2_prev = x_refs[6][0][:, 5 * D_MODEL:6 * D_MODEL]

        def layer_input(k):
            x1_ref, moe_ref = x_refs[k], x_refs[3 + k]
            return x1_ref[...] + g2_prev * _unpack_rows(moe_ref, 0, x1_ref.shape[0])
    else:
        def layer_input(k):
            return x_refs[k][...]

    def normed(v):
        return _rms(v) * (1.0 + sc1) + sh1

    x = layer_input(0)
    h = normed(x)
    h_prev = jnp.where(pos0 > 0, normed(layer_input(1)), 0.0)
    h_next = jnp.where(pos0 + tm < seq_len, normed(layer_input(2)), 0.0)
    hext = jnp.concatenate([h_prev, h, h_next], axis=0)
    n_ext = tm + 2 * POOL_HALO
    pos = pos0 + lax.broadcasted_iota(I32, (tm, 1), 0)

    diffs = []
    for g, win in enumerate(POOL_WINDOWS):
        left = win // 2
        right = win - 1 - left
        cols = slice(g * POOL_GROUP, (g + 1) * POOL_GROUP)
        s = hext[:, cols]
        span = 1
        while span < win:
            s = s + pltpu.roll(s, span, 0)
            span *= 2
        if right:
            s = pltpu.roll(s, n_ext - right, 0)
        num = s[POOL_HALO:POOL_HALO + tm, :]
        count = (jnp.minimum(pos + right + 1, seq_len) - jnp.maximum(pos - left, 0)).astype(F32)
        diffs.append((num / count - h[:, cols]).astype(BF16))

    ts = TOKEN_TILE

    def mixer(sub):
        rows = slice(sub * ts, (sub + 1) * ts)
        outs = [jnp.dot(diffs[g][rows, :], pw_ref[g], preferred_element_type=F32)
                for g in range(len(POOL_WINDOWS))]
        x1 = x[rows, :] + g1 * (jnp.concatenate(outs, axis=1) * ps_ref[...])
        yield
        return x1

    _route_tail([mixer(sub) for sub in range(tm // ts)], mod, rwt_ref, rb_ref, out_refs)


def _pool_mix(x, mod, pool_w, pool_scale, rw_t, rb, seq_len):
    pending = isinstance(x, tuple)
    n_tok = (x[0] if pending else x).shape[0]
    tm = ROUTE_SUBTILES * TOKEN_TILE
    per_seq = seq_len // tm
    halo_per_tile = tm // POOL_HALO
    n_halo = n_tok // POOL_HALO
    const = lambda i: (0, 0)
    seq_block = lambda i: (i // per_seq, 0, 0)
    halo_prev = lambda i: (jnp.maximum(i * halo_per_tile - 1, 0), 0)
    halo_next = lambda i: (jnp.minimum((i + 1) * halo_per_tile, n_halo - 1), 0)

    def tile_and_halos(lines_per_row, width):
        return [pl.BlockSpec((tm * lines_per_row, width), lambda i: (i, 0)),
                pl.BlockSpec((POOL_HALO * lines_per_row, width), halo_prev),
                pl.BlockSpec((POOL_HALO * lines_per_row, width), halo_next)]

    if pending:
        x1, moe, mod_prev = x
        x_args = [x1, x1, x1, moe, moe, moe, mod_prev]
        x_specs = (tile_and_halos(1, D_MODEL) + tile_and_halos(ROW_CHUNKS, LANES)
                   + [pl.BlockSpec((1, 1, 6 * D_MODEL), seq_block)])
    else:
        x_args = [x, x, x]
        x_specs = tile_and_halos(1, D_MODEL)
    return pl.pallas_call(
        functools.partial(_pool_mix_kernel, seq_len=seq_len, pending=pending),
        out_shape=_route_out_shapes(n_tok),
        grid=(n_tok // tm,),
        in_specs=x_specs + [
            pl.BlockSpec((1, 1, 6 * D_MODEL), seq_block),
            pl.BlockSpec(pool_w.shape, lambda i: (0, 0, 0)),
            pl.BlockSpec(pool_scale.shape, const),
            pl.BlockSpec(rw_t.shape, const),
            pl.BlockSpec(rb.shape, const),
        ],
        out_specs=_route_out_specs(ROUTE_SUBTILES),
        compiler_params=pltpu.CompilerParams(
            dimension_semantics=("arbitrary",), vmem_limit_bytes=VMEM_LIMIT),
        name="pool_mix",
    )(*x_args, mod, pool_w, pool_scale, rw_t, rb)


def _slots_kernel(pstart_ref, topi_ref, rank_ref, dest_ref):
    topi = topi_ref[...]
    start = jnp.zeros_like(topi)
    for e in range(N_EXPERTS):
        start = jnp.where(topi == e, pstart_ref[e], start)
    dest_ref[...] = start + rank_ref[...]


def _slots(pstart, topi, rank):
    n_tiles, _, tm = topi.shape
    tb = math.gcd(n_tiles, 32)
    spec = pl.BlockSpec((tb, TOP_K, tm), lambda i, ps: (i, 0, 0))
    return pl.pallas_call(
        _slots_kernel,
        out_shape=jax.ShapeDtypeStruct(topi.shape, I32),
        grid_spec=pltpu.PrefetchScalarGridSpec(
            num_scalar_prefetch=1, grid=(n_tiles // tb,), in_specs=[spec, spec], out_specs=spec),
        compiler_params=pltpu.CompilerParams(dimension_semantics=("arbitrary",)),
        name="moe_slots",
    )(pstart, topi, rank)


def _sc_chunk_rows(c, tm, width=SC_ROWS):
    per_tile = tm // width
    tile = c // per_tile
    part = c % per_tile
    return [(tile * TOP_K + kk) * per_tile + part for kk in range(TOP_K)]


def _sc_dispatch(hp, dest, n_slots, tm):
    hp = hp.reshape((-1,) + ROW_SHAPE)
    n_tok = hp.shape[0]
    assert n_tok % (SC_WORKERS * tm) == 0
    rows_per_w = dest.shape[0] // SC_WORKERS
    chunks_per_w = n_tok // SC_WORKERS // SC_ROWS
    mesh = plsc.VectorSubcoreMesh(core_axis_name="c", subcore_axis_name="s")

    @functools.partial(
        pl.kernel, mesh=mesh,
        out_type=jax.ShapeDtypeStruct((n_slots,) + ROW_SHAPE, I32),
        scratch_types=[
            pltpu.VMEM((rows_per_w, SC_ROWS), I32),
            pltpu.VMEM((SC_ROWS,) + ROW_SHAPE, I32),
            pltpu.SemaphoreType.DMA,
        ],
        name="sc_dispatch",
    )
    def run(hp_hbm, dest_hbm, xs_hbm, dest_v, rows_v, sem):
        wid = lax.axis_index("s") * SC_CORES + lax.axis_index("c")
        pltpu.sync_copy(dest_hbm.at[pl.ds(wid * rows_per_w, rows_per_w)], dest_v)

        @pl.loop(0, chunks_per_w)
        def _(c):
            tok0 = (wid * chunks_per_w + c) * SC_ROWS
            pltpu.sync_copy(hp_hbm.at[pl.ds(tok0, SC_ROWS)], rows_v)
            copies = [pltpu.async_copy(rows_v, xs_hbm.at[dest_v.at[row]], sem)
                      for row in _sc_chunk_rows(c, tm)]
            for cp in copies:
                cp.wait()

    return run(hp, dest).reshape(n_slots * ROW_CHUNKS, LANES)


def _sc_gather_sum(ys, dest, wtm, tm):
    ys = ys.reshape((-1,) + ROW_SHAPE)
    n_tok = wtm.shape[0]
    assert n_tok % (SC_WORKERS * tm) == 0
    g = SC_SUM_ROWS
    rows_per_w = dest.shape[0] // SC_WORKERS
    chunks_per_w = n_tok // SC_WORKERS // g
    mesh = plsc.VectorSubcoreMesh(core_axis_name="c", subcore_axis_name="s")

    @functools.partial(
        pl.kernel, mesh=mesh,
        out_type=jax.ShapeDtypeStruct((n_tok,) + ROW_SHAPE, I32),
        scratch_types=[
            pltpu.VMEM((rows_per_w, g), I32),
            pltpu.VMEM((TOP_K, g) + ROW_SHAPE, I32),
            pltpu.VMEM((g, LANES), F32),
            pltpu.VMEM((g,) + ROW_SHAPE, I32),
            pltpu.SemaphoreType.DMA,
        ],
        compiler_params=pltpu.CompilerParams(needs_layout_passes=False),
        name="sc_gather_sum",
    )
    def run(ys_hbm, dest_hbm, w_hbm, out_hbm, dest_v, rows_v, w_v, out_v, sem):
        wid = lax.axis_index("s") * SC_CORES + lax.axis_index("c")
        pltpu.sync_copy(dest_hbm.at[pl.ds(wid * rows_per_w, rows_per_w)], dest_v)

        @pl.loop(0, chunks_per_w)
        def _(c):
            tok0 = (wid * chunks_per_w + c) * g
            copies = [pltpu.async_copy(ys_hbm.at[dest_v.at[row]], rows_v.at[kk], sem)
                      for kk, row in enumerate(_sc_chunk_rows(c, tm, g))]
            pltpu.sync_copy(w_hbm.at[pl.ds(tok0, g)], w_v)
            for cp in copies:
                cp.wait()

            @pl.loop(0, g)
            def _(t):
                wk = [w_v[t, pl.ds(SC_LANES * kk, SC_LANES)] for kk in range(TOP_K)]
                for j in range(ROW_CHUNKS):
                    for i in range(LANES // SC_LANES):
                        lanes = pl.ds(SC_LANES * i, SC_LANES)
                        lo = hi = None
                        for kk in range(TOP_K):
                            word = rows_v[kk, t, j, lanes]
                            a = lax.bitcast_convert_type(lax.shift_left(word, 16), F32) * wk[kk]
                            b = lax.bitcast_convert_type(word & jnp.int32(HIGH_HALF), F32) * wk[kk]
                            lo = a if lo is None else lo + a
                            hi = b if hi is None else hi + b
                        packed = plsc.pack(lo, hi, format=plsc.PackFormat.INTERLEAVED)
                        out_v[t, j, lanes] = plsc.bitcast(packed, I32)

            pltpu.sync_copy(out_v, out_hbm.at[pl.ds(tok0, g)])

    return run(ys, dest, wtm).reshape(n_tok * ROW_CHUNKS, LANES)


def _expert_kernel(e_ref, first_ref, slot_ref, next_ref, nused_ref,
                   xs_ref, w1_hbm, b1_ref, w2_hbm, b2_ref, ys_ref,
                   w1f_ref, w2f_ref, w1b_ref, w2b_ref, sem, *, layer):
    i = pl.program_id(0)

    def weight_copies(expert, slot):
        return (pltpu.make_async_copy(w1_hbm.at[layer, expert], w1f_ref.at[slot], sem.at[slot, 0]),
                pltpu.make_async_copy(w2_hbm.at[layer, expert], w2f_ref.at[slot], sem.at[slot, 1]))

    @pl.when(first_ref[i] == 1)
    def _():
        slot = slot_ref[i]

        @pl.when(i == 0)
        def _():
            for cp in weight_copies(e_ref[i], slot):
                cp.start()

        for cp in weight_copies(e_ref[i], slot):
            cp.wait()
        w1b_ref[...] = w1f_ref[slot].astype(BF16)
        w2b_ref[...] = w2f_ref[slot].astype(BF16)

        @pl.when(next_ref[i] >= 0)
        def _():
            for cp in weight_copies(next_ref[i], 1 - slot):
                cp.start()

    @pl.when(i < nused_ref[0])
    def _():
        r = 0
        for n in EXPERT_SUBS:
            xb = _unpack_rows(xs_ref, r, n).astype(BF16)
            gu = jnp.dot(xb, w1b_ref[...], preferred_element_type=F32) + b1_ref[0, 0]
            gate = jnp.minimum(gu[:, :D_FF], SWIGLU_LIMIT)
            up = jnp.clip(gu[:, D_FF:], -SWIGLU_LIMIT, SWIGLU_LIMIT)
            act = (up + 1.0) * (gate * jax.nn.sigmoid(SWIGLU_ALPHA * gate))
            y = jnp.dot(act.astype(BF16), w2b_ref[...], preferred_element_type=F32) + b2_ref[0, 0]
            _pack_rows(y, ys_ref, r)
            r += n


def _expert_plan(pend, n_blk, bm):
    blk = jnp.arange(n_blk, dtype=I32)
    n_used = (pend[-1] // bm).astype(I32)
    expert = jnp.minimum(jnp.sum(blk[:, None] * bm >= pend[None, :], axis=1), N_EXPERTS - 1)
    expert = expert.astype(I32)
    prev = jnp.concatenate([jnp.full((1,), -1, I32), expert[:-1]])
    first = (blk < n_used) & (expert != prev)
    slot = (jnp.cumsum(first.astype(I32)) - 1) % 2
    later_first = first[None, :] & (blk[None, :] > blk[:, None])
    nxt = jnp.where(jnp.any(later_first, axis=1), expert[jnp.argmax(later_first, axis=1)], -1)
    return (expert, first.astype(I32), slot.astype(I32), nxt.astype(I32), n_used.reshape(1))


def _expert_ffn(plan, xs, layer, w1, b1, w2, b2):
    n_slots = xs.shape[0] // ROW_CHUNKS
    bm = EXPERT_TILE
    used_block = lambda i, e, f, s, nx, nu: (jnp.minimum(i, nu[0] - 1), 0)
    bias_block = lambda i, e, f, s, nx, nu: (layer, e[i], 0, 0)
    return pl.pallas_call(
        functools.partial(_expert_kernel, layer=layer),
        out_shape=jax.ShapeDtypeStruct(xs.shape, I32),
        grid_spec=pltpu.PrefetchScalarGridSpec(
            num_scalar_prefetch=5,
            grid=(n_slots // bm,),
            in_specs=[
                pl.BlockSpec((bm * ROW_CHUNKS, LANES), used_block),
                pl.BlockSpec(memory_space=pl.ANY),
                pl.BlockSpec((1, 1, 1, 2 * D_FF), bias_block),
                pl.BlockSpec(memory_space=pl.ANY),
                pl.BlockSpec((1, 1, 1, D_MODEL), bias_block),
            ],
            out_specs=pl.BlockSpec((bm * ROW_CHUNKS, LANES), used_block),
            scratch_shapes=[
                pltpu.VMEM((2, D_MODEL, 2 * D_FF), F32), pltpu.VMEM((2, D_FF, D_MODEL), F32),
                pltpu.VMEM((D_MODEL, 2 * D_FF), BF16), pltpu.VMEM((D_FF, D_MODEL), BF16),
                pltpu.SemaphoreType.DMA((2, 2)),
            ],
        ),
        compiler_params=pltpu.CompilerParams(
            dimension_semantics=("arbitrary",), vmem_limit_bytes=VMEM_LIMIT),
        name="expert_ffn",
    )(*plan, xs, w1, b1, w2, b2)


def _combine_kernel(moe_ref, x1_ref, mod_ref, fn_ref, o_ref, *, final):
    tm = x1_ref.shape[0]
    g2 = mod_ref[0][:, 5 * D_MODEL:6 * D_MODEL]
    out = x1_ref[...] + g2 * _unpack_rows(moe_ref, 0, tm)
    if final:
        out = _rms(out) * fn_ref[...]
    o_ref[...] = out


def _combine(moe, x1, mod, final_norm, seq_len, final):
    n_tok = x1.shape[0]
    tm = COMBINE_TILE
    per_seq = seq_len // tm
    return pl.pallas_call(
        functools.partial(_combine_kernel, final=final),
        out_shape=jax.ShapeDtypeStruct((n_tok, D_MODEL), F32),
        grid=(n_tok // tm,),
        in_specs=[
            pl.BlockSpec((tm * ROW_CHUNKS, LANES), lambda i: (i, 0)),
            pl.BlockSpec((tm, D_MODEL), lambda i: (i, 0)),
            pl.BlockSpec((1, 1, 6 * D_MODEL), lambda i: (i // per_seq, 0, 0)),
            pl.BlockSpec((1, D_MODEL), lambda i: (0, 0)),
        ],
        out_specs=pl.BlockSpec((tm, D_MODEL), lambda i: (i, 0)),
        compiler_params=pltpu.CompilerParams(
            dimension_semantics=("arbitrary",), vmem_limit_bytes=VMEM_LIMIT),
        name="moe_combine",
    )(moe, x1, mod, final_norm)


def _moe_layers(routes, mods, layer, ffn_w, final_norm, seq_len, final, defer_combine):
    bm = EXPERT_TILE
    plans = []
    for x1, hp, topi, rank, wtm, counts in routes:
        n_slots = x1.shape[0] * TOP_K + N_EXPERTS * bm
        n_blk = n_slots // bm
        cnt = counts[:, 0].astype(I32)
        padded = (cnt + bm - 1) // bm * bm
        pend = jnp.cumsum(padded)
        pstart = (pend - padded).astype(I32)
        dest = _slots(pstart, topi, rank).reshape(-1, SC_ROWS)
        plans.append((n_slots, _expert_plan(pend, n_blk, bm), dest, topi.shape[2]))
    xs = [_sc_dispatch(r[1], dest, n_slots, tm)
          for r, (n_slots, _, dest, tm) in zip(routes, plans)]
    ys = [_expert_ffn(plan, x, layer, *ffn_w) for x, (_, plan, _, _) in zip(xs, plans)]
    moe = [_sc_gather_sum(y, dest.reshape(-1, SC_SUM_ROWS), r[4], tm)
           for y, r, (_, _, dest, tm) in zip(ys, routes, plans)]
    if defer_combine:
        return [(r[0], m, mod) for m, r, mod in zip(moe, routes, mods)]
    return [_combine(m, r[0], mod, final_norm, seq_len, final)
            for m, r, mod in zip(moe, routes, mods)]


def _rope_tables(seq_len):
    inv_freq = 1.0 / (ROPE_THETA ** (jnp.arange(0, D_ROPE, 2, dtype=F32) / D_ROPE))
    ang = jnp.arange(seq_len, dtype=F32)[:, None] * inv_freq[None, :]
    cos, sin = jnp.cos(ang), jnp.sin(ang)
    ones = jnp.ones((seq_len, ROPE_LO), F32)
    zeros_lo = jnp.zeros((seq_len, ROPE_LO), F32)
    zeros_hi = jnp.zeros((seq_len, HEAD_PAD - ROPE_LO - D_ROPE), F32)
    zeros_h = jnp.zeros((seq_len, ROPE_HALF), F32)
    rope_c = jnp.concatenate([ones, cos, cos, zeros_hi], axis=1)
    rope_s1 = jnp.concatenate([zeros_lo, -sin, zeros_h, zeros_hi], axis=1)
    rope_s2 = jnp.concatenate([zeros_lo, zeros_h, sin, zeros_hi], axis=1)
    return cos.T, sin.T, rope_c, rope_s1, rope_s2


def _mla_weights(w_in, q_norm, kv_norm, w_uq, w_ukv, w_o, seq_len):
    d_qk = D_NOPE + D_ROPE
    q_scale = d_qk ** -0.5 * math.log2(math.e)
    pad_pe = jnp.zeros((D_MODEL, HEAD_PAD), F32).at[:, ROPE_LO:ROPE_LO + D_ROPE].set(
        w_in[:, Q_RANK + KV_RANK:])
    w_in_p = jnp.concatenate([w_in[:, :Q_RANK + KV_RANK], pad_pe], axis=1)
    w_uq_s = w_uq * q_scale
    w_kv = w_ukv.reshape(KV_RANK, N_HEADS, D_NOPE + D_V)
    w_uk_p = jnp.pad(w_kv[:, :, :D_NOPE], ((0, 0), (0, 0), (0, HEAD_PAD - D_NOPE)))
    cos_t, sin_t, rope_c, rope_s1, rope_s2 = _rope_tables(seq_len)
    return {
        "w_in": w_in_p.astype(BF16),
        "q_norm": q_norm.reshape(1, Q_RANK),
        "kv_norm": kv_norm.reshape(1, KV_RANK),
        "w_uq_t": w_uq_s.T.astype(BF16),
        "w_uk": w_uk_p.reshape(KV_RANK, N_HEADS * HEAD_PAD).astype(BF16),
        "w_uv_t": w_kv[:, :, D_NOPE:].reshape(KV_RANK, N_HEADS * D_V).T.astype(BF16),
        "w_o": w_o.astype(BF16),
        "cos_t": cos_t, "sin_t": sin_t, "rope_c": rope_c, "rope_s1": rope_s1, "rope_s2": rope_s2,
    }


def _router_weights(router_w, router_b):
    return router_w.T.astype(BF16), router_b.reshape(N_EXPERTS, 1)


def kernel(x_prompt, x_sample, c_prompt, c_sample, ada_w, ada_b, mla_w_in, mla_q_norm,
           mla_kv_norm, mla_w_uq, mla_w_ukv, mla_w_o, pool_w, pool_scale, router_w, router_b,
           moe_w1, moe_b1, moe_w2, moe_b2, final_norm):
    n_prompt, seq_len, _ = x_prompt.shape
    assert x_sample.shape[1] == seq_len and seq_len % (ROUTE_SUBTILES * TOKEN_TILE) == 0
    depth = ada_w.shape[0]
    n_sample = x_sample.shape[0]
    xs = [x_prompt.reshape(-1, D_MODEL), x_sample.reshape(-1, D_MODEL)]
    n_seqs = [n_prompt, n_sample]
    mods = _ada_mod(jnp.concatenate([c_prompt, c_sample], axis=0), ada_w, ada_b)
    fnorm = final_norm.reshape(1, D_MODEL)
    ffn_w = (moe_w1, moe_b1.reshape(depth, N_EXPERTS, 1, 2 * D_FF),
             moe_w2, moe_b2.reshape(depth, N_EXPERTS, 1, D_MODEL))

    for i in range(depth):
        rw_t, rb = _router_weights(router_w[i], router_b[i])
        j = i // 2
        group_mods = [mods[i, :n_prompt].reshape(n_prompt, 1, 6 * D_MODEL),
                      mods[i, n_prompt:].reshape(n_sample, 1, 6 * D_MODEL)]
        if i % 2 == 0:
            w = _mla_weights(mla_w_in[j], mla_q_norm[j], mla_kv_norm[j], mla_w_uq[j],
                             mla_w_ukv[j], mla_w_o[j], seq_len)
        routes = []
        for x, mod in zip(xs, group_mods):
            if i % 2 == 0:
                qt, k, vt = _mla_pre(x, mod, w, seq_len)
                ot = _attention(qt, k, vt, seq_len)
                routes.append(_post_mix(ot, x, mod, w["w_o"], rw_t, rb, seq_len))
            else:
                routes.append(_pool_mix(x, mod, pool_w[j].astype(BF16),
                                        pool_scale[j].reshape(1, D_MODEL), rw_t, rb, seq_len))
        last = i == depth - 1
        next_is_pool = not last and (i + 1) % 2 == 1
        xs = _moe_layers(routes, group_mods, i, ffn_w, fnorm, seq_len, last, next_is_pool)

    return (xs[0].reshape(n_prompt, seq_len, D_MODEL), xs[1].reshape(n_sample, seq_len, D_MODEL))
```

```python
import functools
import math

import jax
import jax.numpy as jnp
from jax import lax
from jax.experimental import pallas as pl
from jax.experimental.pallas import tpu as pltpu
from jax.experimental.pallas import tpu_sc as plsc

F32 = jnp.float32
BF16 = jnp.bfloat16
I32 = jnp.int32

D_MODEL = 1024
N_HEADS = 16
Q_RANK = 384
KV_RANK = 256
D_NOPE = 64
D_ROPE = 32
D_V = 64
V_ROWS = D_V + 16
ROPE_THETA = 10000.0
POOL_WINDOWS = (2, 4, 8, 16)
POOL_GROUP = D_MODEL // len(POOL_WINDOWS)
N_EXPERTS = 32
TOP_K = 4
D_FF = D_MODEL
SWIGLU_LIMIT = 7.0
SWIGLU_ALPHA = 1.702
EPS = 1e-6

LANES = 128
HEAD_PAD = 128
ROPE_LO = D_NOPE
ROPE_HALF = D_ROPE // 2
VMEM_LIMIT = 56 * 1024 * 1024

TOKEN_TILE = 256
COMBINE_TILE = 1024
ROUTE_SUBTILES = 4
EXPERT_TILE = 512
EXPERT_SUBS = (256, 256)
ATTN_KEY_CHUNKS = 2
ATTN_HEADS_PER_STEP = 8
POOL_HALO = 8
PACKED = D_MODEL // 2
ROW_CHUNKS = PACKED // LANES
ROW_SHAPE = (ROW_CHUNKS, LANES)
HIGH_HALF = -65536
ADA_COLS = 1536

SC_CORES = 2
SC_SUBCORES = 16
SC_WORKERS = SC_CORES * SC_SUBCORES
SC_LANES = 16
SC_ROWS = 64
SC_SUM_ROWS = 32


def _rms(x):
    return x * lax.rsqrt(jnp.mean(x * x, axis=-1, keepdims=True) + EPS)


def _pack_rows(y, out_ref, row0=0):
    n = y.shape[0]
    lo = lax.bitcast_convert_type(y[:, :PACKED].astype(BF16).astype(F32), I32)
    hi = lax.bitcast_convert_type(y[:, PACKED:].astype(BF16).astype(F32), I32)
    words = lax.shift_right_logical(lo, 16) | (hi & jnp.int32(HIGH_HALF))
    for j in range(ROW_CHUNKS):
        out_ref[pl.ds(row0 * ROW_CHUNKS + j, n, stride=ROW_CHUNKS), :] = (
            words[:, j * LANES:(j + 1) * LANES])


def _unpack_rows(in_ref, row0, n):
    words = jnp.concatenate(
        [in_ref[pl.ds(row0 * ROW_CHUNKS + j, n, stride=ROW_CHUNKS), :] for j in range(ROW_CHUNKS)],
        axis=1)
    lo = lax.bitcast_convert_type(lax.shift_left(words, 16), F32)
    hi = lax.bitcast_convert_type(words & jnp.int32(HIGH_HALF), F32)
    return jnp.concatenate([lo, hi], axis=1)


def _interleave(chains):
    chains = list(chains)
    done = object()
    while chains:
        chains = [ch for ch in chains if next(ch, done) is not done]


def _ada_kernel(c_ref, w_ref, b_ref, o_ref):
    c = c_ref[...]
    act = (c * jax.nn.sigmoid(c)).astype(BF16)
    o_ref[0] = jnp.dot(act, w_ref[0].astype(BF16), preferred_element_type=F32) + b_ref[0]


def _ada_mod(c, ada_w, ada_b):
    depth, _, n_out = ada_w.shape
    n_seq = c.shape[0]
    tn = ADA_COLS
    return pl.pallas_call(
        _ada_kernel,
        out_shape=jax.ShapeDtypeStruct((depth, n_seq, n_out), F32),
        grid=(depth, n_out // tn),
        in_specs=[
            pl.BlockSpec((n_seq, D_MODEL), lambda l, j: (0, 0)),
            pl.BlockSpec((1, D_MODEL, tn), lambda l, j: (l, 0, j)),
            pl.BlockSpec((1, 1, tn), lambda l, j: (l, 0, j)),
        ],
        out_specs=pl.BlockSpec((1, n_seq, tn), lambda l, j: (l, 0, j)),
        compiler_params=pltpu.CompilerParams(
            dimension_semantics=("arbitrary", "arbitrary"), vmem_limit_bytes=VMEM_LIMIT),
        name="ada_mod",
    )(c, ada_w, ada_b.reshape(depth, 1, n_out))


def _mla_pre_kernel(x_ref, mod_ref, win_ref, qn_ref, kvn_ref, wuqt_ref, wuk_ref, wuvt_ref,
                    cost_ref, sint_ref, ck_ref, s1k_ref, s2k_ref, qt_ref, k_ref, vt_ref):
    n_sub, ts = qt_ref.shape[1], qt_ref.shape[3]
    mod = mod_ref[0]
    sh1 = mod[:, 0:D_MODEL]
    sc1 = mod[:, D_MODEL:2 * D_MODEL]
    x = x_ref[...]
    d_qk = D_NOPE + D_ROPE

    def chain(sub):
        rows = slice(sub * ts, (sub + 1) * ts)
        h = (_rms(x[rows, :]) * (1.0 + sc1) + sh1).astype(BF16)
        a = jnp.dot(h, win_ref[...], preferred_element_type=F32)
        yield
        cq = (_rms(a[:, :Q_RANK]) * qn_ref[...]).astype(BF16)
        ckv = (_rms(a[:, Q_RANK:Q_RANK + KV_RANK]) * kvn_ref[...]).astype(BF16)
        kpe = a[:, Q_RANK + KV_RANK:]
        kpe = (kpe * ck_ref[rows, :]
               + pltpu.roll(kpe, LANES - ROPE_HALF, 1) * s1k_ref[rows, :]
               + pltpu.roll(kpe, ROPE_HALF, 1) * s2k_ref[rows, :])
        yield
        qt = lax.dot_general(wuqt_ref[...], cq, (((1,), (1,)), ((), ())),
                             preferred_element_type=F32)
        q3 = qt.reshape(N_HEADS, d_qk, ts)
        x1 = q3[:, ROPE_LO:ROPE_LO + ROPE_HALF, :]
        x2 = q3[:, ROPE_LO + ROPE_HALF:, :]
        cos = cost_ref[:, rows][None]
        sin = sint_ref[:, rows][None]
        q3 = jnp.concatenate(
            [q3[:, :ROPE_LO, :], x1 * cos - x2 * sin, x2 * cos + x1 * sin], axis=1)
        qt_ref[:, sub, :d_qk, :] = q3.astype(BF16)
        qt_ref[:, sub, d_qk:, :] = jnp.zeros((N_HEADS, HEAD_PAD - d_qk, ts), BF16)
        yield
        kn = jnp.dot(ckv, wuk_ref[...], preferred_element_type=F32)
        for hd in range(N_HEADS):
            k_ref[hd, rows, :] = (kn[:, hd * HEAD_PAD:(hd + 1) * HEAD_PAD] + kpe).astype(BF16)
        yield
        vt = lax.dot_general(wuvt_ref[...], ckv, (((1,), (1,)), ((), ())),
                             preferred_element_type=F32)
        vt_ref[:, 0, :D_V, rows] = vt.reshape(N_HEADS, D_V, ts).astype(BF16)
        vt_ref[:, 0, D_V:, rows] = jnp.ones((N_HEADS, V_ROWS - D_V, ts), BF16)

    _interleave(chain(sub) for sub in range(n_sub))


def _mla_pre(x, mod, w, seq_len):
    n_tok = x.shape[0]
    n_sub = ROUTE_SUBTILES
    tq = TOKEN_TILE
    tm = n_sub * tq
    per_seq = seq_len // tm
    const = lambda i: (0, 0)
    return pl.pallas_call(
        _mla_pre_kernel,
        out_shape=(
            jax.ShapeDtypeStruct((N_HEADS, n_tok // tq, HEAD_PAD, tq), BF16),
            jax.ShapeDtypeStruct((N_HEADS, n_tok, HEAD_PAD), BF16),
            jax.ShapeDtypeStruct((N_HEADS, n_tok // seq_len, V_ROWS, seq_len), BF16),
        ),
        grid=(n_tok // tm,),
        in_specs=[
            pl.BlockSpec((tm, D_MODEL), lambda i: (i, 0)),
            pl.BlockSpec((1, 1, 6 * D_MODEL), lambda i: (i // per_seq, 0, 0)),
            pl.BlockSpec(w["w_in"].shape, const),
            pl.BlockSpec(w["q_norm"].shape, const),
            pl.BlockSpec(w["kv_norm"].shape, const),
            pl.BlockSpec(w["w_uq_t"].shape, const),
            pl.BlockSpec(w["w_uk"].shape, const),
            pl.BlockSpec(w["w_uv_t"].shape, const),
            pl.BlockSpec((ROPE_HALF, tm), lambda i: (0, i % per_seq)),
            pl.BlockSpec((ROPE_HALF, tm), lambda i: (0, i % per_seq)),
            pl.BlockSpec((tm, LANES), lambda i: (i % per_seq, 0)),
            pl.BlockSpec((tm, LANES), lambda i: (i % per_seq, 0)),
            pl.BlockSpec((tm, LANES), lambda i: (i % per_seq, 0)),
        ],
        out_specs=(
            pl.BlockSpec((N_HEADS, n_sub, HEAD_PAD, tq), lambda i: (0, i, 0, 0)),
            pl.BlockSpec((N_HEADS, tm, HEAD_PAD), lambda i: (0, i, 0)),
            pl.BlockSpec((N_HEADS, 1, V_ROWS, tm), lambda i: (0, i // per_seq, 0, i % per_seq)),
        ),
        compiler_params=pltpu.CompilerParams(
            dimension_semantics=("arbitrary",), vmem_limit_bytes=VMEM_LIMIT),
        name="mla_pre",
    )(x, mod, w["w_in"], w["q_norm"], w["kv_norm"], w["w_uq_t"], w["w_uk"], w["w_uv_t"],
      w["cos_t"], w["sin_t"], w["rope_c"], w["rope_s1"], w["rope_s2"])


def _attention_kernel(qt_ref, k_ref, vt_ref, ot_ref, s0_ref, s1_ref):
    n_heads, n_q = qt_ref.shape[:2]
    n_tiles = n_heads * n_q
    n_keys = k_ref.shape[1]
    kc = n_keys // ATTN_KEY_CHUNKS

    def stage(t_next, s_next_ref, t_cur, s_cur_ref, m_cur):
        m_next, o = None, None
        if t_next is not None:
            h_next, j_next = t_next // n_q, t_next % n_q
        if t_cur is not None:
            h_cur, j_cur = t_cur // n_q, t_cur % n_q
        for c in range(ATTN_KEY_CHUNKS):
            rows = slice(c * kc, (c + 1) * kc)
            if t_next is not None:
                s = jnp.dot(k_ref[h_next, rows, :], qt_ref[h_next, j_next],
                            preferred_element_type=F32)
                s_next_ref[rows, :] = s
                cm = jnp.max(s, axis=0, keepdims=True)
                m_next = cm if m_next is None else jnp.maximum(m_next, cm)
            if t_cur is not None:
                p = jnp.exp2(s_cur_ref[rows, :] - m_cur).astype(BF16)
                part = jnp.dot(vt_ref[h_cur, 0, :, rows], p,
                               preferred_element_type=F32)
                o = part if o is None else o + part
        if t_cur is not None:
            denom = o[D_V:D_V + 1, :]
            ot_ref[h_cur, j_cur] = (o[:D_V, :] * (1.0 / denom)).astype(BF16)
        return m_next

    def body(i, m0):
        t = 2 * i
        m1 = stage(t + 1, s1_ref, t, s0_ref, m0)
        return stage(t + 2, s0_ref, t + 1, s1_ref, m1)

    m0 = lax.fori_loop(0, n_tiles // 2 - 1, body, stage(0, s0_ref, None, None, None))
    m1 = stage(n_tiles - 1, s1_ref, n_tiles - 2, s0_ref, m0)
    stage(None, None, n_tiles - 1, s1_ref, m1)


def _attention(qt, k, vt, seq_len):
    n_heads, n_tiles, _, tq = qt.shape
    per_seq = seq_len // tq
    n_seq = n_tiles // per_seq
    hb = ATTN_HEADS_PER_STEP
    return pl.pallas_call(
        _attention_kernel,
        out_shape=jax.ShapeDtypeStruct((n_heads, n_tiles, D_V, tq), BF16),
        grid=(n_seq, n_heads // hb),
        in_specs=[
            pl.BlockSpec((hb, per_seq, HEAD_PAD, tq), lambda b, h: (h, b, 0, 0)),
            pl.BlockSpec((hb, seq_len, HEAD_PAD), lambda b, h: (h, b, 0)),
            pl.BlockSpec((hb, 1, V_ROWS, seq_len), lambda b, h: (h, b, 0, 0)),
        ],
        out_specs=pl.BlockSpec((hb, per_seq, D_V, tq), lambda b, h: (h, b, 0, 0)),
        scratch_shapes=[pltpu.VMEM((seq_len, tq), F32), pltpu.VMEM((seq_len, tq), F32)],
        compiler_params=pltpu.CompilerParams(
            dimension_semantics=("arbitrary", "arbitrary"), vmem_limit_bytes=VMEM_LIMIT),
        name="attention",
    )(qt, k, vt)


def _route_init(cnt_ref):
    @pl.when(pl.program_id(0) == 0)
    def _():
        cnt_ref[...] = jnp.zeros_like(cnt_ref)


def _route_tail(x1_chains, mod, rwt_ref, rb_ref, out_refs):
    cnt_ref = out_refs[-1]
    ts = TOKEN_TILE
    row = lax.broadcasted_iota(I32, (ts, ts), 0)
    col = lax.broadcasted_iota(I32, (ts, ts), 1)
    state = {
        "running": cnt_ref[...][:, 0:1],
        "earlier": (row < col).astype(BF16),
        "e_iota": lax.broadcasted_iota(I32, (N_EXPERTS, ts), 0),
    }

    def chain(sub, x1_chain):
        x1 = yield from x1_chain
        yield from _route_chain(sub, x1, mod, rwt_ref, rb_ref, out_refs, state)

    _interleave(chain(sub, ch) for sub, ch in enumerate(x1_chains))
    cnt_ref[...] = jnp.broadcast_to(state["running"], cnt_ref.shape)


def _route_chain(sub, x1, mod, rwt_ref, rb_ref, out_refs, state):
    x1_ref, hp_ref, topi_ref, rank_ref, wtm_ref, _ = out_refs
    tm = x1.shape[0]
    rows = pl.ds(sub * tm, tm)
    sh2 = mod[:, 3 * D_MODEL:4 * D_MODEL]
    sc2 = mod[:, 4 * D_MODEL:5 * D_MODEL]
    x1_ref[rows, :] = x1
    h2 = _rms(x1) * (1.0 + sc2) + sh2
    _pack_rows(h2, hp_ref, sub * tm)
    logits = lax.dot_general(rwt_ref[...], h2.astype(BF16), (((1,), (1,)), ((), ())),
                             preferred_element_type=F32) + rb_ref[...]
    yield
    e_iota = state["e_iota"]
    vals, idxs = [], []
    work = logits
    for _ in range(TOP_K):
        m = jnp.max(work, axis=0, keepdims=True)
        idx = jnp.min(jnp.where(work == m, e_iota, N_EXPERTS), axis=0, keepdims=True)
        vals.append(m)
        idxs.append(idx)
        work = jnp.where(e_iota == idx, -jnp.inf, work)
    ex = [jnp.exp(v - vals[0]) for v in vals]
    inv = 1.0 / (ex[0] + ex[1] + ex[2] + ex[3])
    topw = jnp.concatenate([e * inv for e in ex], axis=0)
    topi_ref[sub] = jnp.concatenate(idxs, axis=0)
    yield

    earlier = state["earlier"]
    running = state["running"]
    ranks = []
    for kk in range(TOP_K):
        onehot = (e_iota == idxs[kk]).astype(F32)
        before = jnp.dot(onehot.astype(BF16), earlier, preferred_element_type=F32)
        ranks.append(jnp.sum(onehot * (running + before), axis=0, keepdims=True))
        running = running + jnp.sum(onehot, axis=1, keepdims=True)
    rank_ref[sub] = jnp.concatenate(ranks, axis=0).astype(I32)
    state["running"] = running
    yield

    wpad = jnp.concatenate(
        [jnp.broadcast_to(topw[kk:kk + 1], (SC_LANES, tm)) for kk in range(TOP_K)]
        + [jnp.zeros((LANES - TOP_K * SC_LANES, tm), F32)], axis=0)
    wtm_ref[rows, :] = wpad.T


def _route_out_shapes(n_tok):
    n_tiles = n_tok // TOKEN_TILE
    return (
        jax.ShapeDtypeStruct((n_tok, D_MODEL), F32),
        jax.ShapeDtypeStruct((n_tok * ROW_CHUNKS, LANES), I32),
        jax.ShapeDtypeStruct((n_tiles, TOP_K, TOKEN_TILE), I32),
        jax.ShapeDtypeStruct((n_tiles, TOP_K, TOKEN_TILE), I32),
        jax.ShapeDtypeStruct((n_tok, LANES), F32),
        jax.ShapeDtypeStruct((N_EXPERTS, LANES), F32),
    )


def _route_out_specs(n_sub):
    tm = n_sub * TOKEN_TILE
    return (
        pl.BlockSpec((tm, D_MODEL), lambda i: (i, 0)),
        pl.BlockSpec((tm * ROW_CHUNKS, LANES), lambda i: (i, 0)),
        pl.BlockSpec((n_sub, TOP_K, TOKEN_TILE), lambda i: (i, 0, 0)),
        pl.BlockSpec((n_sub, TOP_K, TOKEN_TILE), lambda i: (i, 0, 0)),
        pl.BlockSpec((tm, LANES), lambda i: (i, 0)),
        pl.BlockSpec((N_EXPERTS, LANES), lambda i: (0, 0)),
    )


def _post_mix_kernel(ot_ref, x_ref, mod_ref, wo_ref, rwt_ref, rb_ref, *out_refs):
    _route_init(out_refs[-1])
    n_sub, ts = ot_ref.shape[1], ot_ref.shape[3]
    mod = mod_ref[0]
    g1 = mod[:, 2 * D_MODEL:3 * D_MODEL]
    x = x_ref[...]

    def mixer(sub):
        ot = ot_ref[:, sub].reshape(N_HEADS * D_V, ts)
        mix = lax.dot_general(ot, wo_ref[...], (((0,), (0,)), ((), ())),
                              preferred_element_type=F32)
        x1 = x[sub * ts:(sub + 1) * ts, :] + g1 * mix
        yield
        return x1

    _route_tail([mixer(sub) for sub in range(n_sub)], mod, rwt_ref, rb_ref, out_refs)


def _post_mix(ot, x, mod, w_o, rw_t, rb, seq_len):
    n_tok = x.shape[0]
    n_sub = ROUTE_SUBTILES
    tm = n_sub * ot.shape[3]
    per_seq = seq_len // tm
    const = lambda i: (0, 0)
    return pl.pallas_call(
        _post_mix_kernel,
        out_shape=_route_out_shapes(n_tok),
        grid=(n_tok // tm,),
        in_specs=[
            pl.BlockSpec((N_HEADS, n_sub, D_V, ot.shape[3]), lambda i: (0, i, 0, 0)),
            pl.BlockSpec((tm, D_MODEL), lambda i: (i, 0)),
            pl.BlockSpec((1, 1, 6 * D_MODEL), lambda i: (i // per_seq, 0, 0)),
            pl.BlockSpec(w_o.shape, const),
            pl.BlockSpec(rw_t.shape, const),
            pl.BlockSpec(rb.shape, const),
        ],
        out_specs=_route_out_specs(n_sub),
        compiler_params=pltpu.CompilerParams(
            dimension_semantics=("arbitrary",), vmem_limit_bytes=VMEM_LIMIT),
        name="post_mix",
    )(ot, x, mod, w_o, rw_t, rb)


def _pool_mix_kernel(*refs, seq_len, pending):
    n_x = 7 if pending else 3
    x_refs = refs[:n_x]
    mod_ref, pw_ref, ps_ref, rwt_ref, rb_ref = refs[n_x:n_x + 5]
    out_refs = refs[n_x + 5:]
    _route_init(out_refs[-1])
    tm = x_refs[0].shape[0]
    per_seq = seq_len // tm
    mod = mod_ref[0]
    sh1 = mod[:, 0:D_MODEL]
    sc1 = mod[:, D_MODEL:2 * D_MODEL]
    g1 = mod[:, 2 * D_MODEL:3 * D_MODEL]
    pos0 = (pl.program_id(0) % per_seq) * tm

    if pending:
        g2_prev = x_refs[6][0][:, 5 * D_MODEL:6 * D_MODEL]

        def layer_input(k):
            x1_ref, moe_ref = x_refs[k], x_refs[3 + k]
            return x1_ref[...] + g2_prev * _unpack_rows(moe_ref, 0, x1_ref.shape[0])
    else:
        def layer_input(k):
            return x_refs[k][...]

    def normed(v):
        return _rms(v) * (1.0 + sc1) + sh1

    x = layer_input(0)
    h = normed(x)
    h_prev = jnp.where(pos0 > 0, normed(layer_input(1)), 0.0)
    h_next = jnp.where(pos0 + tm < seq_len, normed(layer_input(2)), 0.0)
    hext = jnp.concatenate([h_prev, h, h_next], axis=0)
    n_ext = tm + 2 * POOL_HALO
    pos = pos0 + lax.broadcasted_iota(I32, (tm, 1), 0)

    diffs = []
    for g, win in enumerate(POOL_WINDOWS):
        left = win // 2
        right = win - 1 - left
        cols = slice(g * POOL_GROUP, (g + 1) * POOL_GROUP)
        s = hext[:, cols]
        span = 1
        while span < win:
            s = s + pltpu.roll(s, span, 0)
            span *= 2
        if right:
            s = pltpu.roll(s, n_ext - right, 0)
        num = s[POOL_HALO:POOL_HALO + tm, :]
        count = (jnp.minimum(pos + right + 1, seq_len) - jnp.maximum(pos - left, 0)).astype(F32)
        diffs.append((num / count - h[:, cols]).astype(BF16))

    ts = TOKEN_TILE

    def mixer(sub):
        rows = slice(sub * ts, (sub + 1) * ts)
        outs = [jnp.dot(diffs[g][rows, :], pw_ref[g], preferred_element_type=F32)
                for g in range(len(POOL_WINDOWS))]
        x1 = x[rows, :] + g1 * (jnp.concatenate(outs, axis=1) * ps_ref[...])
        yield
        return x1

    _route_tail([mixer(sub) for sub in range(tm // ts)], mod, rwt_ref, rb_ref, out_refs)


def _pool_mix(x, mod, pool_w, pool_scale, rw_t, rb, seq_len):
    pending = isinstance(x, tuple)
    n_tok = (x[0] if pending else x).shape[0]
    tm = ROUTE_SUBTILES * TOKEN_TILE
    per_seq = seq_len // tm
    halo_per_tile = tm // POOL_HALO
    n_halo = n_tok // POOL_HALO
    const = lambda i: (0, 0)
    seq_block = lambda i: (i // per_seq, 0, 0)
    halo_prev = lambda i: (jnp.maximum(i * halo_per_tile - 1, 0), 0)
    halo_next = lambda i: (jnp.minimum((i + 1) * halo_per_tile, n_halo - 1), 0)

    def tile_and_halos(lines_per_row, width):
        return [pl.BlockSpec((tm * lines_per_row, width), lambda i: (i, 0)),
                pl.BlockSpec((POOL_HALO * lines_per_row, width), halo_prev),
                pl.BlockSpec((POOL_HALO * lines_per_row, width), halo_next)]

    if pending:
        x1, moe, mod_prev = x
        x_args = [x1, x1, x1, moe, moe, moe, mod_prev]
        x_specs = (tile_and_halos(1, D_MODEL) + tile_and_halos(ROW_CHUNKS, LANES)
                   + [pl.BlockSpec((1, 1, 6 * D_MODEL), seq_block)])
    else:
        x_args = [x, x, x]
        x_specs = tile_and_halos(1, D_MODEL)
    return pl.pallas_call(
        functools.partial(_pool_mix_kernel, seq_len=seq_len, pending=pending),
        out_shape=_route_out_shapes(n_tok),
        grid=(n_tok // tm,),
        in_specs=x_specs + [
            pl.BlockSpec((1, 1, 6 * D_MODEL), seq_block),
            pl.BlockSpec(pool_w.shape, lambda i: (0, 0, 0)),
            pl.BlockSpec(pool_scale.shape, const),
            pl.BlockSpec(rw_t.shape, const),
            pl.BlockSpec(rb.shape, const),
        ],
        out_specs=_route_out_specs(ROUTE_SUBTILES),
        compiler_params=pltpu.CompilerParams(
            dimension_semantics=("arbitrary",), vmem_limit_bytes=VMEM_LIMIT),
        name="pool_mix",
    )(*x_args, mod, pool_w, pool_scale, rw_t, rb)


def _slots_kernel(pstart_ref, topi_ref, rank_ref, dest_ref):
    topi = topi_ref[...]
    start = jnp.zeros_like(topi)
    for e in range(N_EXPERTS):
        start = jnp.where(topi == e, pstart_ref[e], start)
    dest_ref[...] = start + rank_ref[...]


def _slots(pstart, topi, rank):
    n_tiles, _, tm = topi.shape
    tb = math.gcd(n_tiles, 32)
    spec = pl.BlockSpec((tb, TOP_K, tm), lambda i, ps: (i, 0, 0))
    return pl.pallas_call(
        _slots_kernel,
        out_shape=jax.ShapeDtypeStruct(topi.shape, I32),
        grid_spec=pltpu.PrefetchScalarGridSpec(
            num_scalar_prefetch=1, grid=(n_tiles // tb,), in_specs=[spec, spec], out_specs=spec),
        compiler_params=pltpu.CompilerParams(dimension_semantics=("arbitrary",)),
        name="moe_slots",
    )(pstart, topi, rank)


def _sc_chunk_rows(c, tm, width=SC_ROWS):
    per_tile = tm // width
    tile = c // per_tile
    part = c % per_tile
    return [(tile * TOP_K + kk) * per_tile + part for kk in range(TOP_K)]


def _sc_dispatch(hp, dest, n_slots, tm):
    hp = hp.reshape((-1,) + ROW_SHAPE)
    n_tok = hp.shape[0]
    assert n_tok % (SC_WORKERS * tm) == 0
    rows_per_w = dest.shape[0] // SC_WORKERS
    chunks_per_w = n_tok // SC_WORKERS // SC_ROWS
    assert chunks_per_w % 2 == 0
    mesh = plsc.VectorSubcoreMesh(core_axis_name="c", subcore_axis_name="s")

    @functools.partial(
        pl.kernel, mesh=mesh,
        out_type=jax.ShapeDtypeStruct((n_slots,) + ROW_SHAPE, I32),
        scratch_types=[
            pltpu.VMEM((rows_per_w, SC_ROWS), I32),
            pltpu.VMEM((2, SC_ROWS) + ROW_SHAPE, I32),
            pltpu.SemaphoreType.DMA((2,)),
            pltpu.SemaphoreType.DMA,
        ],
        name="sc_dispatch",
    )
    def run(hp_hbm, dest_hbm, xs_hbm, dest_v, rows_v, load_sem, scatter_sem):
        wid = lax.axis_index("s") * SC_CORES + lax.axis_index("c")
        pltpu.sync_copy(dest_hbm.at[pl.ds(wid * rows_per_w, rows_per_w)], dest_v)

        def load(c, slot):
            tok0 = (wid * chunks_per_w + c) * SC_ROWS
            return pltpu.make_async_copy(hp_hbm.at[pl.ds(tok0, SC_ROWS)], rows_v.at[slot],
                                         load_sem.at[slot])

        load(0, 0).start()

        @pl.loop(0, chunks_per_w, step=2)
        def _(c0):
            for slot in range(2):
                c = c0 + slot
                load(c, slot).wait()

                @pl.when(c + 1 < chunks_per_w)
                def _():
                    load(c + 1, 1 - slot).start()

                copies = [pltpu.async_copy(rows_v.at[slot], xs_hbm.at[dest_v.at[row]],
                                           scatter_sem)
                          for row in _sc_chunk_rows(c, tm)]
                for cp in copies:
                    cp.wait()

    return run(hp, dest).reshape(n_slots * ROW_CHUNKS, LANES)


def _sc_gather_sum(ys, dest, wtm, tm):
    ys = ys.reshape((-1,) + ROW_SHAPE)
    n_tok = wtm.shape[0]
    assert n_tok % (SC_WORKERS * tm) == 0
    g = SC_SUM_ROWS
    rows_per_w = dest.shape[0] // SC_WORKERS
    chunks_per_w = n_tok // SC_WORKERS // g
    mesh = plsc.VectorSubcoreMesh(core_axis_name="c", subcore_axis_name="s")

    @functools.partial(
        pl.kernel, mesh=mesh,
        out_type=jax.ShapeDtypeStruct((n_tok,) + ROW_SHAPE, I32),
        scratch_types=[
            pltpu.VMEM((rows_per_w, g), I32),
            pltpu.VMEM((TOP_K, g) + ROW_SHAPE, I32),
            pltpu.VMEM((g, LANES), F32),
            pltpu.VMEM((g,) + ROW_SHAPE, I32),
            pltpu.SemaphoreType.DMA,
        ],
        compiler_params=pltpu.CompilerParams(needs_layout_passes=False),
        name="sc_gather_sum",
    )
    def run(ys_hbm, dest_hbm, w_hbm, out_hbm, dest_v, rows_v, w_v, out_v, sem):
        wid = lax.axis_index("s") * SC_CORES + lax.axis_index("c")
        pltpu.sync_copy(dest_hbm.at[pl.ds(wid * rows_per_w, rows_per_w)], dest_v)

        @pl.loop(0, chunks_per_w)
        def _(c):
            tok0 = (wid * chunks_per_w + c) * g
            copies = [pltpu.async_copy(ys_hbm.at[dest_v.at[row]], rows_v.at[kk], sem)
                      for kk, row in enumerate(_sc_chunk_rows(c, tm, g))]
            pltpu.sync_copy(w_hbm.at[pl.ds(tok0, g)], w_v)
            for cp in copies:
                cp.wait()

            @pl.loop(0, g)
            def _(t):
                wk = [w_v[t, pl.ds(SC_LANES * kk, SC_LANES)] for kk in range(TOP_K)]
                for j in range(ROW_CHUNKS):
                    for i in range(LANES // SC_LANES):
                        lanes = pl.ds(SC_LANES * i, SC_LANES)
                        lo = hi = None
                        for kk in range(TOP_K):
                            word = rows_v[kk, t, j, lanes]
                            a = lax.bitcast_convert_type(lax.shift_left(word, 16), F32) * wk[kk]
                            b = lax.bitcast_convert_type(word & jnp.int32(HIGH_HALF), F32) * wk[kk]
                            lo = a if lo is None else lo + a
                            hi = b if hi is None else hi + b
                        packed = plsc.pack(lo, hi, format=plsc.PackFormat.INTERLEAVED)
                        out_v[t, j, lanes] = plsc.bitcast(packed, I32)

            pltpu.sync_copy(out_v, out_hbm.at[pl.ds(tok0, g)])

    return run(ys, dest, wtm).reshape(n_tok * ROW_CHUNKS, LANES)


def _expert_kernel(e_ref, first_ref, slot_ref, next_ref, nused_ref,
                   xs_ref, w1_hbm, b1_ref, w2_hbm, b2_ref, ys_ref,
                   w1f_ref, w2f_ref, w1b_ref, w2b_ref, sem, *, layer):
    i = pl.program_id(0)

    def weight_copies(expert, slot):
        return (pltpu.make_async_copy(w1_hbm.at[layer, expert], w1f_ref.at[slot], sem.at[slot, 0]),
                pltpu.make_async_copy(w2_hbm.at[layer, expert], w2f_ref.at[slot], sem.at[slot, 1]))

    @pl.when(first_ref[i] == 1)
    def _():
        slot = slot_ref[i]

        @pl.when(i == 0)
        def _():
            for cp in weight_copies(e_ref[i], slot):
                cp.start()

        for cp in weight_copies(e_ref[i], slot):
            cp.wait()
        w1b_ref[...] = w1f_ref[slot].astype(BF16)
        w2b_ref[...] = w2f_ref[slot].astype(BF16)

        @pl.when(next_ref[i] >= 0)
        def _():
            for cp in weight_copies(next_ref[i], 1 - slot):
                cp.start()

    @pl.when(i < nused_ref[0])
    def _():
        r = 0
        for n in EXPERT_SUBS:
            xb = _unpack_rows(xs_ref, r, n).astype(BF16)
            gu = jnp.dot(xb, w1b_ref[...], preferred_element_type=F32) + b1_ref[0, 0]
            gate = jnp.minimum(gu[:, :D_FF], SWIGLU_LIMIT)
            up = jnp.clip(gu[:, D_FF:], -SWIGLU_LIMIT, SWIGLU_LIMIT)
            act = (up + 1.0) * (gate * jax.nn.sigmoid(SWIGLU_ALPHA * gate))
            y = jnp.dot(act.astype(BF16), w2b_ref[...], preferred_element_type=F32) + b2_ref[0, 0]
            _pack_rows(y, ys_ref, r)
            r += n


def _expert_plan(pend, n_blk, bm):
    blk = jnp.arange(n_blk, dtype=I32)
    n_used = (pend[-1] // bm).astype(I32)
    expert = jnp.minimum(jnp.sum(blk[:, None] * bm >= pend[None, :], axis=1), N_EXPERTS - 1)
    expert = expert.astype(I32)
    prev = jnp.concatenate([jnp.full((1,), -1, I32), expert[:-1]])
    first = (blk < n_used) & (expert != prev)
    slot = (jnp.cumsum(first.astype(I32)) - 1) % 2
    later_first = first[None, :] & (blk[None, :] > blk[:, None])
    nxt = jnp.where(jnp.any(later_first, axis=1), expert[jnp.argmax(later_first, axis=1)], -1)
    return (expert, first.astype(I32), slot.astype(I32), nxt.astype(I32), n_used.reshape(1))


def _expert_ffn(plan, xs, layer, w1, b1, w2, b2):
    n_slots = xs.shape[0] // ROW_CHUNKS
    bm = EXPERT_TILE
    used_block = lambda i, e, f, s, nx, nu: (jnp.minimum(i, nu[0] - 1), 0)
    bias_block = lambda i, e, f, s, nx, nu: (layer, e[i], 0, 0)
    return pl.pallas_call(
        functools.partial(_expert_kernel, layer=layer),
        out_shape=jax.ShapeDtypeStruct(xs.shape, I32),
        grid_spec=pltpu.PrefetchScalarGridSpec(
            num_scalar_prefetch=5,
            grid=(n_slots // bm,),
            in_specs=[
                pl.BlockSpec((bm * ROW_CHUNKS, LANES), used_block),
                pl.BlockSpec(memory_space=pl.ANY),
                pl.BlockSpec((1, 1, 1, 2 * D_FF), bias_block),
                pl.BlockSpec(memory_space=pl.ANY),
                pl.BlockSpec((1, 1, 1, D_MODEL), bias_block),
            ],
            out_specs=pl.BlockSpec((bm * ROW_CHUNKS, LANES), used_block),
            scratch_shapes=[
                pltpu.VMEM((2, D_MODEL, 2 * D_FF), F32), pltpu.VMEM((2, D_FF, D_MODEL), F32),
                pltpu.VMEM((D_MODEL, 2 * D_FF), BF16), pltpu.VMEM((D_FF, D_MODEL), BF16),
                pltpu.SemaphoreType.DMA((2, 2)),
            ],
        ),
        compiler_params=pltpu.CompilerParams(
            dimension_semantics=("arbitrary",), vmem_limit_bytes=VMEM_LIMIT),
        name="expert_ffn",
    )(*plan, xs, w1, b1, w2, b2)


def _combine_kernel(moe_ref, x1_ref, mod_ref, fn_ref, o_ref, *, final):
    tm = x1_ref.shape[0]
    g2 = mod_ref[0][:, 5 * D_MODEL:6 * D_MODEL]
    out = x1_ref[...] + g2 * _unpack_rows(moe_ref, 0, tm)
    if final:
        out = _rms(out) * fn_ref[...]
    o_ref[...] = out


def _combine(moe, x1, mod, final_norm, seq_len, final):
    n_tok = x1.shape[0]
    tm = COMBINE_TILE
    per_seq = seq_len // tm
    return pl.pallas_call(
        functools.partial(_combine_kernel, final=final),
        out_shape=jax.ShapeDtypeStruct((n_tok, D_MODEL), F32),
        grid=(n_tok // tm,),
        in_specs=[
            pl.BlockSpec((tm * ROW_CHUNKS, LANES), lambda i: (i, 0)),
            pl.BlockSpec((tm, D_MODEL), lambda i: (i, 0)),
            pl.BlockSpec((1, 1, 6 * D_MODEL), lambda i: (i // per_seq, 0, 0)),
            pl.BlockSpec((1, D_MODEL), lambda i: (0, 0)),
        ],
        out_specs=pl.BlockSpec((tm, D_MODEL), lambda i: (i, 0)),
        compiler_params=pltpu.CompilerParams(
            dimension_semantics=("arbitrary",), vmem_limit_bytes=VMEM_LIMIT),
        name="moe_combine",
    )(moe, x1, mod, final_norm)


def _moe_layers(routes, mods, layer, ffn_w, final_norm, seq_len, final, defer_combine):
    bm = EXPERT_TILE
    plans = []
    for x1, hp, topi, rank, wtm, counts in routes:
        n_slots = x1.shape[0] * TOP_K + N_EXPERTS * bm
        n_blk = n_slots // bm
        cnt = counts[:, 0].astype(I32)
        padded = (cnt + bm - 1) // bm * bm
        pend = jnp.cumsum(padded)
        pstart = (pend - padded).astype(I32)
        dest = _slots(pstart, topi, rank).reshape(-1, SC_ROWS)
        plans.append((n_slots, _expert_plan(pend, n_blk, bm), dest, topi.shape[2]))
    xs = [_sc_dispatch(r[1], dest, n_slots, tm)
          for r, (n_slots, _, dest, tm) in zip(routes, plans)]
    ys = [_expert_ffn(plan, x, layer, *ffn_w) for x, (_, plan, _, _) in zip(xs, plans)]
    moe = [_sc_gather_sum(y, dest.reshape(-1, SC_SUM_ROWS), r[4], tm)
           for y, r, (_, _, dest, tm) in zip(ys, routes, plans)]
    if defer_combine:
        return [(r[0], m, mod) for m, r, mod in zip(moe, routes, mods)]
    return [_combine(m, r[0], mod, final_norm, seq_len, final)
            for m, r, mod in zip(moe, routes, mods)]


def _rope_tables(seq_len):
    inv_freq = 1.0 / (ROPE_THETA ** (jnp.arange(0, D_ROPE, 2, dtype=F32) / D_ROPE))
    ang = jnp.arange(seq_len, dtype=F32)[:, None] * inv_freq[None, :]
    cos, sin = jnp.cos(ang), jnp.sin(ang)
    ones = jnp.ones((seq_len, ROPE_LO), F32)
    zeros_lo = jnp.zeros((seq_len, ROPE_LO), F32)
    zeros_hi = jnp.zeros((seq_len, HEAD_PAD - ROPE_LO - D_ROPE), F32)
    zeros_h = jnp.zeros((seq_len, ROPE_HALF), F32)
    rope_c = jnp.concatenate([ones, cos, cos, zeros_hi], axis=1)
    rope_s1 = jnp.concatenate([zeros_lo, -sin, zeros_h, zeros_hi], axis=1)
    rope_s2 = jnp.concatenate([zeros_lo, zeros_h, sin, zeros_hi], axis=1)
    return cos.T, sin.T, rope_c, rope_s1, rope_s2


def _mla_weights(w_in, q_norm, kv_norm, w_uq, w_ukv, w_o, seq_len):
    d_qk = D_NOPE + D_ROPE
    q_scale = d_qk ** -0.5 * math.log2(math.e)
    pad_pe = jnp.zeros((D_MODEL, HEAD_PAD), F32).at[:, ROPE_LO:ROPE_LO + D_ROPE].set(
        w_in[:, Q_RANK + KV_RANK:])
    w_in_p = jnp.concatenate([w_in[:, :Q_RANK + KV_RANK], pad_pe], axis=1)
    w_uq_s = w_uq * q_scale
    w_kv = w_ukv.reshape(KV_RANK, N_HEADS, D_NOPE + D_V)
    w_uk_p = jnp.pad(w_kv[:, :, :D_NOPE], ((0, 0), (0, 0), (0, HEAD_PAD - D_NOPE)))
    cos_t, sin_t, rope_c, rope_s1, rope_s2 = _rope_tables(seq_len)
    return {
        "w_in": w_in_p.astype(BF16),
        "q_norm": q_norm.reshape(1, Q_RANK),
        "kv_norm": kv_norm.reshape(1, KV_RANK),
        "w_uq_t": w_uq_s.T.astype(BF16),
        "w_uk": w_uk_p.reshape(KV_RANK, N_HEADS * HEAD_PAD).astype(BF16),
        "w_uv_t": w_kv[:, :, D_NOPE:].reshape(KV_RANK, N_HEADS * D_V).T.astype(BF16),
        "w_o": w_o.astype(BF16),
        "cos_t": cos_t, "sin_t": sin_t, "rope_c": rope_c, "rope_s1": rope_s1, "rope_s2": rope_s2,
    }


def _router_weights(router_w, router_b):
    return router_w.T.astype(BF16), router_b.reshape(N_EXPERTS, 1)


def kernel(x_prompt, x_sample, c_prompt, c_sample, ada_w, ada_b, mla_w_in, mla_q_norm,
           mla_kv_norm, mla_w_uq, mla_w_ukv, mla_w_o, pool_w, pool_scale, router_w, router_b,
           moe_w1, moe_b1, moe_w2, moe_b2, final_norm):
    n_prompt, seq_len, _ = x_prompt.shape
    assert x_sample.shape[1] == seq_len and seq_len % (ROUTE_SUBTILES * TOKEN_TILE) == 0
    depth = ada_w.shape[0]
    n_sample = x_sample.shape[0]
    xs = [x_prompt.reshape(-1, D_MODEL), x_sample.reshape(-1, D_MODEL)]
    n_seqs = [n_prompt, n_sample]
    mods = _ada_mod(jnp.concatenate([c_prompt, c_sample], axis=0), ada_w, ada_b)
    fnorm = final_norm.reshape(1, D_MODEL)
    ffn_w = (moe_w1, moe_b1.reshape(depth, N_EXPERTS, 1, 2 * D_FF),
             moe_w2, moe_b2.reshape(depth, N_EXPERTS, 1, D_MODEL))

    for i in range(depth):
        rw_t, rb = _router_weights(router_w[i], router_b[i])
        j = i // 2
        group_mods = [mods[i, :n_prompt].reshape(n_prompt, 1, 6 * D_MODEL),
                      mods[i, n_prompt:].reshape(n_sample, 1, 6 * D_MODEL)]
        if i % 2 == 0:
            w = _mla_weights(mla_w_in[j], mla_q_norm[j], mla_kv_norm[j], mla_w_uq[j],
                             mla_w_ukv[j], mla_w_o[j], seq_len)
        routes = []
        for x, mod in zip(xs, group_mods):
            if i % 2 == 0:
                qt, k, vt = _mla_pre(x, mod, w, seq_len)
                ot = _attention(qt, k, vt, seq_len)
                routes.append(_post_mix(ot, x, mod, w["w_o"], rw_t, rb, seq_len))
            else:
                routes.append(_pool_mix(x, mod, pool_w[j].astype(BF16),
                                        pool_scale[j].reshape(1, D_MODEL), rw_t, rb, seq_len))
        last = i == depth - 1
        next_is_pool = not last and (i + 1) % 2 == 1
        xs = _moe_layers(routes, group_mods, i, ffn_w, fnorm, seq_len, last, next_is_pool)

    return (xs[0].reshape(n_prompt, seq_len, D_MODEL), xs[1].reshape(n_sample, seq_len, D_MODEL))
```

```python
import functools
import math

import jax
import jax.numpy as jnp
from jax import lax
from jax.experimental import pallas as pl
from jax.experimental.pallas import tpu as pltpu
from jax.experimental.pallas import tpu_sc as plsc

F32 = jnp.float32
BF16 = jnp.bfloat16
I32 = jnp.int32

D_MODEL = 1024
N_HEADS = 16
Q_RANK = 384
KV_RANK = 256
D_NOPE = 64
D_ROPE = 32
D_V = 64
V_ROWS = D_V + 16
ROPE_THETA = 10000.0
POOL_WINDOWS = (2, 4, 8, 16)
POOL_GROUP = D_MODEL // len(POOL_WINDOWS)
N_EXPERTS = 32
TOP_K = 4
D_FF = D_MODEL
SWIGLU_LIMIT = 7.0
SWIGLU_ALPHA = 1.702
EPS = 1e-6

LANES = 128
HEAD_PAD = 128
ROPE_LO = D_NOPE
ROPE_HALF = D_ROPE // 2
VMEM_LIMIT = 56 * 1024 * 1024

TOKEN_TILE = 256
COMBINE_TILE = 1024
ROUTE_SUBTILES = 4
EXPERT_TILE = 512
EXPERT_SUBS = (256, 256)
ATTN_KEY_CHUNKS = 2
ATTN_HEADS_PER_STEP = 8
POOL_HALO = 8
PACKED = D_MODEL // 2
ROW_CHUNKS = PACKED // LANES
ROW_SHAPE = (ROW_CHUNKS, LANES)
HIGH_HALF = -65536
ADA_COLS = 1536

SC_CORES = 2
SC_SUBCORES = 16
SC_WORKERS = SC_CORES * SC_SUBCORES
SC_LANES = 16
SC_ROWS = 64
SC_SUM_ROWS = 32


def _rms(x):
    return x * lax.rsqrt(jnp.mean(x * x, axis=-1, keepdims=True) + EPS)


def _pack_rows(y, out_ref, row0=0):
    n = y.shape[0]
    lo = lax.bitcast_convert_type(y[:, :PACKED].astype(BF16).astype(F32), I32)
    hi = lax.bitcast_convert_type(y[:, PACKED:].astype(BF16).astype(F32), I32)
    words = lax.shift_right_logical(lo, 16) | (hi & jnp.int32(HIGH_HALF))
    for j in range(ROW_CHUNKS):
        out_ref[pl.ds(row0 * ROW_CHUNKS + j, n, stride=ROW_CHUNKS), :] = (
            words[:, j * LANES:(j + 1) * LANES])


def _unpack_rows(in_ref, row0, n):
    words = jnp.concatenate(
        [in_ref[pl.ds(row0 * ROW_CHUNKS + j, n, stride=ROW_CHUNKS), :] for j in range(ROW_CHUNKS)],
        axis=1)
    lo = lax.bitcast_convert_type(lax.shift_left(words, 16), F32)
    hi = lax.bitcast_convert_type(words & jnp.int32(HIGH_HALF), F32)
    return jnp.concatenate([lo, hi], axis=1)


def _interleave(chains):
    chains = list(chains)
    done = object()
    while chains:
        chains = [ch for ch in chains if next(ch, done) is not done]


def _ada_kernel(c_ref, w_ref, b_ref, o_ref):
    c = c_ref[...]
    act = (c * jax.nn.sigmoid(c)).astype(BF16)
    o_ref[0] = jnp.dot(act, w_ref[0].astype(BF16), preferred_element_type=F32) + b_ref[0]


def _ada_mod(c, ada_w, ada_b):
    depth, _, n_out = ada_w.shape
    n_seq = c.shape[0]
    tn = ADA_COLS
    return pl.pallas_call(
        _ada_kernel,
        out_shape=jax.ShapeDtypeStruct((depth, n_seq, n_out), F32),
        grid=(depth, n_out // tn),
        in_specs=[
            pl.BlockSpec((n_seq, D_MODEL), lambda l, j: (0, 0)),
            pl.BlockSpec((1, D_MODEL, tn), lambda l, j: (l, 0, j)),
            pl.BlockSpec((1, 1, tn), lambda l, j: (l, 0, j)),
        ],
        out_specs=pl.BlockSpec((1, n_seq, tn), lambda l, j: (l, 0, j)),
        compiler_params=pltpu.CompilerParams(
            dimension_semantics=("arbitrary", "arbitrary"), vmem_limit_bytes=VMEM_LIMIT),
        name="ada_mod",
    )(c, ada_w, ada_b.reshape(depth, 1, n_out))


def _mla_pre_kernel(x_ref, mod_ref, win_ref, qn_ref, kvn_ref, wuqt_ref, wuk_ref, wuvt_ref,
                    cost_ref, sint_ref, ck_ref, s1k_ref, s2k_ref, qt_ref, k_ref, vt_ref):
    n_sub, ts = qt_ref.shape[1], qt_ref.shape[3]
    mod = mod_ref[0]
    sh1 = mod[:, 0:D_MODEL]
    sc1 = mod[:, D_MODEL:2 * D_MODEL]
    x = x_ref[...]
    d_qk = D_NOPE + D_ROPE

    def chain(sub):
        rows = slice(sub * ts, (sub + 1) * ts)
        h = (_rms(x[rows, :]) * (1.0 + sc1) + sh1).astype(BF16)
        a = jnp.dot(h, win_ref[...], preferred_element_type=F32)
        yield
        cq = (_rms(a[:, :Q_RANK]) * qn_ref[...]).astype(BF16)
        ckv = (_rms(a[:, Q_RANK:Q_RANK + KV_RANK]) * kvn_ref[...]).astype(BF16)
        kpe = a[:, Q_RANK + KV_RANK:]
        kpe = (kpe * ck_ref[rows, :]
               + pltpu.roll(kpe, LANES - ROPE_HALF, 1) * s1k_ref[rows, :]
               + pltpu.roll(kpe, ROPE_HALF, 1) * s2k_ref[rows, :])
        yield
        qt = lax.dot_general(wuqt_ref[...], cq, (((1,), (1,)), ((), ())),
                             preferred_element_type=F32)
        q3 = qt.reshape(N_HEADS, d_qk, ts)
        x1 = q3[:, ROPE_LO:ROPE_LO + ROPE_HALF, :]
        x2 = q3[:, ROPE_LO + ROPE_HALF:, :]
        cos = cost_ref[:, rows][None]
        sin = sint_ref[:, rows][None]
        q3 = jnp.concatenate(
            [q3[:, :ROPE_LO, :], x1 * cos - x2 * sin, x2 * cos + x1 * sin], axis=1)
        qb = q3.astype(BF16)
        pad = jnp.zeros((HEAD_PAD - d_qk, ts), BF16)
        for hd in range(N_HEADS):
            if hd % 2 == 0:
                qt_ref[hd, sub, :d_qk, :] = qb[hd]
                qt_ref[hd, sub, d_qk:, :] = pad
            else:
                qt_ref[hd, sub, :D_ROPE, :] = qb[hd, ROPE_LO:, :]
                qt_ref[hd, sub, D_ROPE:2 * D_ROPE, :] = pad
                qt_ref[hd, sub, 2 * D_ROPE:, :] = qb[hd, :ROPE_LO, :]
        yield
        kn = jnp.dot(ckv, wuk_ref[...], preferred_element_type=F32)
        lane = lax.broadcasted_iota(I32, (1, LANES), 1)
        kpe_odd = pltpu.roll(kpe, LANES - ROPE_LO, 1)
        for pair in range(N_HEADS // 2):
            col = kn[:, pair * LANES:(pair + 1) * LANES]
            k_ref[2 * pair, rows, :] = jnp.where(lane < ROPE_LO, col, kpe).astype(BF16)
            k_ref[2 * pair + 1, rows, :] = jnp.where(lane >= ROPE_LO, col, kpe_odd).astype(BF16)
        yield
        vt = lax.dot_general(wuvt_ref[...], ckv, (((1,), (1,)), ((), ())),
                             preferred_element_type=F32)
        vt_ref[:, 0, :D_V, rows] = vt.reshape(N_HEADS, D_V, ts).astype(BF16)
        vt_ref[:, 0, D_V:, rows] = jnp.ones((N_HEADS, V_ROWS - D_V, ts), BF16)

    _interleave(chain(sub) for sub in range(n_sub))


def _mla_pre(x, mod, w, seq_len):
    n_tok = x.shape[0]
    n_sub = ROUTE_SUBTILES
    tq = TOKEN_TILE
    tm = n_sub * tq
    per_seq = seq_len // tm
    const = lambda i: (0, 0)
    return pl.pallas_call(
        _mla_pre_kernel,
        out_shape=(
            jax.ShapeDtypeStruct((N_HEADS, n_tok // tq, HEAD_PAD, tq), BF16),
            jax.ShapeDtypeStruct((N_HEADS, n_tok, HEAD_PAD), BF16),
            jax.ShapeDtypeStruct((N_HEADS, n_tok // seq_len, V_ROWS, seq_len), BF16),
        ),
        grid=(n_tok // tm,),
        in_specs=[
            pl.BlockSpec((tm, D_MODEL), lambda i: (i, 0)),
            pl.BlockSpec((1, 1, 6 * D_MODEL), lambda i: (i // per_seq, 0, 0)),
            pl.BlockSpec(w["w_in"].shape, const),
            pl.BlockSpec(w["q_norm"].shape, const),
            pl.BlockSpec(w["kv_norm"].shape, const),
            pl.BlockSpec(w["w_uq_t"].shape, const),
            pl.BlockSpec(w["w_uk"].shape, const),
            pl.BlockSpec(w["w_uv_t"].shape, const),
            pl.BlockSpec((ROPE_HALF, tm), lambda i: (0, i % per_seq)),
            pl.BlockSpec((ROPE_HALF, tm), lambda i: (0, i % per_seq)),
            pl.BlockSpec((tm, LANES), lambda i: (i % per_seq, 0)),
            pl.BlockSpec((tm, LANES), lambda i: (i % per_seq, 0)),
            pl.BlockSpec((tm, LANES), lambda i: (i % per_seq, 0)),
        ],
        out_specs=(
            pl.BlockSpec((N_HEADS, n_sub, HEAD_PAD, tq), lambda i: (0, i, 0, 0)),
            pl.BlockSpec((N_HEADS, tm, HEAD_PAD), lambda i: (0, i, 0)),
            pl.BlockSpec((N_HEADS, 1, V_ROWS, tm), lambda i: (0, i // per_seq, 0, i % per_seq)),
        ),
        compiler_params=pltpu.CompilerParams(
            dimension_semantics=("arbitrary",), vmem_limit_bytes=VMEM_LIMIT),
        name="mla_pre",
    )(x, mod, w["w_in"], w["q_norm"], w["kv_norm"], w["w_uq_t"], w["w_uk"], w["w_uv_t"],
      w["cos_t"], w["sin_t"], w["rope_c"], w["rope_s1"], w["rope_s2"])


def _attention_kernel(qt_ref, k_ref, vt_ref, ot_ref, s0_ref, s1_ref):
    n_heads, n_q = qt_ref.shape[:2]
    n_tiles = n_heads * n_q
    n_keys = k_ref.shape[1]
    kc = n_keys // ATTN_KEY_CHUNKS

    def stage(t_next, s_next_ref, t_cur, s_cur_ref, m_cur):
        m_next, o = None, None
        if t_next is not None:
            h_next, j_next = t_next // n_q, t_next % n_q
        if t_cur is not None:
            h_cur, j_cur = t_cur // n_q, t_cur % n_q
        for c in range(ATTN_KEY_CHUNKS):
            rows = slice(c * kc, (c + 1) * kc)
            if t_next is not None:
                s = jnp.dot(k_ref[h_next, rows, :], qt_ref[h_next, j_next],
                            preferred_element_type=F32)
                s_next_ref[rows, :] = s
                cm = jnp.max(s, axis=0, keepdims=True)
                m_next = cm if m_next is None else jnp.maximum(m_next, cm)
            if t_cur is not None:
                p = jnp.exp2(s_cur_ref[rows, :] - m_cur).astype(BF16)
                part = jnp.dot(vt_ref[h_cur, 0, :, rows], p,
                               preferred_element_type=F32)
                o = part if o is None else o + part
        if t_cur is not None:
            denom = o[D_V:D_V + 1, :]
            ot_ref[h_cur, j_cur] = (o[:D_V, :] * (1.0 / denom)).astype(BF16)
        return m_next

    def body(i, m0):
        t = 2 * i
        m1 = stage(t + 1, s1_ref, t, s0_ref, m0)
        return stage(t + 2, s0_ref, t + 1, s1_ref, m1)

    m0 = lax.fori_loop(0, n_tiles // 2 - 1, body, stage(0, s0_ref, None, None, None))
    m1 = stage(n_tiles - 1, s1_ref, n_tiles - 2, s0_ref, m0)
    stage(None, None, n_tiles - 1, s1_ref, m1)


def _attention(qt, k, vt, seq_len):
    n_heads, n_tiles, _, tq = qt.shape
    per_seq = seq_len // tq
    n_seq = n_tiles // per_seq
    hb = ATTN_HEADS_PER_STEP
    return pl.pallas_call(
        _attention_kernel,
        out_shape=jax.ShapeDtypeStruct((n_heads, n_tiles, D_V, tq), BF16),
        grid=(n_seq, n_heads // hb),
        in_specs=[
            pl.BlockSpec((hb, per_seq, HEAD_PAD, tq), lambda b, h: (h, b, 0, 0)),
            pl.BlockSpec((hb, seq_len, HEAD_PAD), lambda b, h: (h, b, 0)),
            pl.BlockSpec((hb, 1, V_ROWS, seq_len), lambda b, h: (h, b, 0, 0)),
        ],
        out_specs=pl.BlockSpec((hb, per_seq, D_V, tq), lambda b, h: (h, b, 0, 0)),
        scratch_shapes=[pltpu.VMEM((seq_len, tq), F32), pltpu.VMEM((seq_len, tq), F32)],
        compiler_params=pltpu.CompilerParams(
            dimension_semantics=("arbitrary", "arbitrary"), vmem_limit_bytes=VMEM_LIMIT),
        name="attention",
    )(qt, k, vt)


def _route_init(cnt_ref):
    @pl.when(pl.program_id(0) == 0)
    def _():
        cnt_ref[...] = jnp.zeros_like(cnt_ref)


def _route_tail(x1_chains, mod, rwt_ref, rb_ref, out_refs):
    cnt_ref = out_refs[-1]
    ts = TOKEN_TILE
    row = lax.broadcasted_iota(I32, (ts, ts), 0)
    col = lax.broadcasted_iota(I32, (ts, ts), 1)
    state = {
        "running": cnt_ref[...][:, 0:1],
        "earlier": (row < col).astype(BF16),
        "e_iota": lax.broadcasted_iota(I32, (N_EXPERTS, ts), 0),
    }

    def chain(sub, x1_chain):
        x1 = yield from x1_chain
        yield from _route_chain(sub, x1, mod, rwt_ref, rb_ref, out_refs, state)

    _interleave(chain(sub, ch) for sub, ch in enumerate(x1_chains))
    cnt_ref[...] = jnp.broadcast_to(state["running"], cnt_ref.shape)


def _route_chain(sub, x1, mod, rwt_ref, rb_ref, out_refs, state):
    x1_ref, hp_ref, topi_ref, rank_ref, wtm_ref, _ = out_refs
    tm = x1.shape[0]
    rows = pl.ds(sub * tm, tm)
    sh2 = mod[:, 3 * D_MODEL:4 * D_MODEL]
    sc2 = mod[:, 4 * D_MODEL:5 * D_MODEL]
    x1_ref[rows, :] = x1
    h2 = _rms(x1) * (1.0 + sc2) + sh2
    _pack_rows(h2, hp_ref, sub * tm)
    logits = lax.dot_general(rwt_ref[...], h2.astype(BF16), (((1,), (1,)), ((), ())),
                             preferred_element_type=F32) + rb_ref[...]
    yield
    e_iota = state["e_iota"]
    vals, idxs = [], []
    work = logits
    for _ in range(TOP_K):
        m = jnp.max(work, axis=0, keepdims=True)
        idx = jnp.min(jnp.where(work == m, e_iota, N_EXPERTS), axis=0, keepdims=True)
        vals.append(m)
        idxs.append(idx)
        work = jnp.where(e_iota == idx, -jnp.inf, work)
    ex = [jnp.exp(v - vals[0]) for v in vals]
    inv = 1.0 / (ex[0] + ex[1] + ex[2] + ex[3])
    topw = jnp.concatenate([e * inv for e in ex], axis=0)
    topi_ref[sub] = jnp.concatenate(idxs, axis=0)
    yield

    earlier = state["earlier"]
    running = state["running"]
    ranks = []
    for kk in range(TOP_K):
        onehot = (e_iota == idxs[kk]).astype(F32)
        before = jnp.dot(onehot.astype(BF16), earlier, preferred_element_type=F32)
        ranks.append(jnp.sum(onehot * (running + before), axis=0, keepdims=True))
        running = running + jnp.sum(onehot, axis=1, keepdims=True)
    rank_ref[sub] = jnp.concatenate(ranks, axis=0).astype(I32)
    state["running"] = running
    yield

    wpad = jnp.concatenate(
        [jnp.broadcast_to(topw[kk:kk + 1], (SC_LANES, tm)) for kk in range(TOP_K)]
        + [jnp.zeros((LANES - TOP_K * SC_LANES, tm), F32)], axis=0)
    wtm_ref[rows, :] = wpad.T


def _route_out_shapes(n_tok):
    n_tiles = n_tok // TOKEN_TILE
    return (
        jax.ShapeDtypeStruct((n_tok, D_MODEL), F32),
        jax.ShapeDtypeStruct((n_tok * ROW_CHUNKS, LANES), I32),
        jax.ShapeDtypeStruct((n_tiles, TOP_K, TOKEN_TILE), I32),
        jax.ShapeDtypeStruct((n_tiles, TOP_K, TOKEN_TILE), I32),
        jax.ShapeDtypeStruct((n_tok, LANES), F32),
        jax.ShapeDtypeStruct((N_EXPERTS, LANES), F32),
    )


def _route_out_specs(n_sub):
    tm = n_sub * TOKEN_TILE
    return (
        pl.BlockSpec((tm, D_MODEL), lambda i: (i, 0)),
        pl.BlockSpec((tm * ROW_CHUNKS, LANES), lambda i: (i, 0)),
        pl.BlockSpec((n_sub, TOP_K, TOKEN_TILE), lambda i: (i, 0, 0)),
        pl.BlockSpec((n_sub, TOP_K, TOKEN_TILE), lambda i: (i, 0, 0)),
        pl.BlockSpec((tm, LANES), lambda i: (i, 0)),
        pl.BlockSpec((N_EXPERTS, LANES), lambda i: (0, 0)),
    )


def _post_mix_kernel(ot_ref, x_ref, mod_ref, wo_ref, rwt_ref, rb_ref, *out_refs):
    _route_init(out_refs[-1])
    n_sub, ts = ot_ref.shape[1], ot_ref.shape[3]
    mod = mod_ref[0]
    g1 = mod[:, 2 * D_MODEL:3 * D_MODEL]
    x = x_ref[...]

    def mixer(sub):
        ot = ot_ref[:, sub].reshape(N_HEADS * D_V, ts)
        mix = lax.dot_general(ot, wo_ref[...], (((0,), (0,)), ((), ())),
                              preferred_element_type=F32)
        x1 = x[sub * ts:(sub + 1) * ts, :] + g1 * mix
        yield
        return x1

    _route_tail([mixer(sub) for sub in range(n_sub)], mod, rwt_ref, rb_ref, out_refs)


def _post_mix(ot, x, mod, w_o, rw_t, rb, seq_len):
    n_tok = x.shape[0]
    n_sub = ROUTE_SUBTILES
    tm = n_sub * ot.shape[3]
    per_seq = seq_len // tm
    const = lambda i: (0, 0)
    return pl.pallas_call(
        _post_mix_kernel,
        out_shape=_route_out_shapes(n_tok),
        grid=(n_tok // tm,),
        in_specs=[
            pl.BlockSpec((N_HEADS, n_sub, D_V, ot.shape[3]), lambda i: (0, i, 0, 0)),
            pl.BlockSpec((tm, D_MODEL), lambda i: (i, 0)),
            pl.BlockSpec((1, 1, 6 * D_MODEL), lambda i: (i // per_seq, 0, 0)),
            pl.BlockSpec(w_o.shape, const),
            pl.BlockSpec(rw_t.shape, const),
            pl.BlockSpec(rb.shape, const),
        ],
        out_specs=_route_out_specs(n_sub),
        compiler_params=pltpu.CompilerParams(
            dimension_semantics=("arbitrary",), vmem_limit_bytes=VMEM_LIMIT),
        name="post_mix",
    )(ot, x, mod, w_o, rw_t, rb)


def _pool_mix_kernel(*refs, seq_len, pending):
    n_x = 7 if pending else 3
    x_refs = refs[:n_x]
    mod_ref, pw_ref, ps_ref, rwt_ref, rb_ref = refs[n_x:n_x + 5]
    out_refs = refs[n_x + 5:]
    _route_init(out_refs[-1])
    tm = x_refs[0].shape[0]
    per_seq = seq_len // tm
    mod = mod_ref[0]
    sh1 = mod[:, 0:D_MODEL]
    sc1 = mod[:, D_MODEL:2 * D_MODEL]
    g1 = mod[:, 2 * D_MODEL:3 * D_MODEL]
    pos0 = (pl.program_id(0) % per_seq) * tm

    if pending:
        g2_prev = x_refs[6][0][:, 5 * D_MODEL:6 * D_MODEL]

        def layer_input(k):
            x1_ref, moe_ref = x_refs[k], x_refs[3 + k]
            return x1_ref[...] + g2_prev * _unpack_rows(moe_ref, 0, x1_ref.shape[0])
    else:
        def layer_input(k):
            return x_refs[k][...]

    def normed(v):
        return _rms(v) * (1.0 + sc1) + sh1

    x = layer_input(0)
    h = normed(x)
    h_prev = jnp.where(pos0 > 0, normed(layer_input(1)), 0.0)
    h_next = jnp.where(pos0 + tm < seq_len, normed(layer_input(2)), 0.0)
    hext = jnp.concatenate([h_prev, h, h_next], axis=0)
    n_ext = tm + 2 * POOL_HALO
    pos = pos0 + lax.broadcasted_iota(I32, (tm, 1), 0)

    diffs = []
    for g, win in enumerate(POOL_WINDOWS):
        left = win // 2
        right = win - 1 - left
        cols = slice(g * POOL_GROUP, (g + 1) * POOL_GROUP)
        s = hext[:, cols]
        span = 1
        while span < win:
            s = s + pltpu.roll(s, span, 0)
            span *= 2
        if right:
            s = pltpu.roll(s, n_ext - right, 0)
        num = s[POOL_HALO:POOL_HALO + tm, :]
        count = (jnp.minimum(pos + right + 1, seq_len) - jnp.maximum(pos - left, 0)).astype(F32)
        diffs.append((num / count - h[:, cols]).astype(BF16))

    ts = TOKEN_TILE

    def mixer(sub):
        rows = slice(sub * ts, (sub + 1) * ts)
        outs = [jnp.dot(diffs[g][rows, :], pw_ref[g], preferred_element_type=F32)
                for g in range(len(POOL_WINDOWS))]
        x1 = x[rows, :] + g1 * (jnp.concatenate(outs, axis=1) * ps_ref[...])
        yield
        return x1

    _route_tail([mixer(sub) for sub in range(tm // ts)], mod, rwt_ref, rb_ref, out_refs)


def _pool_mix(x, mod, pool_w, pool_scale, rw_t, rb, seq_len):
    pending = isinstance(x, tuple)
    n_tok = (x[0] if pending else x).shape[0]
    tm = ROUTE_SUBTILES * TOKEN_TILE
    per_seq = seq_len // tm
    halo_per_tile = tm // POOL_HALO
    n_halo = n_tok // POOL_HALO
    const = lambda i: (0, 0)
    seq_block = lambda i: (i // per_seq, 0, 0)
    halo_prev = lambda i: (jnp.maximum(i * halo_per_tile - 1, 0), 0)
    halo_next = lambda i: (jnp.minimum((i + 1) * halo_per_tile, n_halo - 1), 0)

    def tile_and_halos(lines_per_row, width):
        return [pl.BlockSpec((tm * lines_per_row, width), lambda i: (i, 0)),
                pl.BlockSpec((POOL_HALO * lines_per_row, width), halo_prev),
                pl.BlockSpec((POOL_HALO * lines_per_row, width), halo_next)]

    if pending:
        x1, moe, mod_prev = x
        x_args = [x1, x1, x1, moe, moe, moe, mod_prev]
        x_specs = (tile_and_halos(1, D_MODEL) + tile_and_halos(ROW_CHUNKS, LANES)
                   + [pl.BlockSpec((1, 1, 6 * D_MODEL), seq_block)])
    else:
        x_args = [x, x, x]
        x_specs = tile_and_halos(1, D_MODEL)
    return pl.pallas_call(
        functools.partial(_pool_mix_kernel, seq_len=seq_len, pending=pending),
        out_shape=_route_out_shapes(n_tok),
        grid=(n_tok // tm,),
        in_specs=x_specs + [
            pl.BlockSpec((1, 1, 6 * D_MODEL), seq_block),
            pl.BlockSpec(pool_w.shape, lambda i: (0, 0, 0)),
            pl.BlockSpec(pool_scale.shape, const),
            pl.BlockSpec(rw_t.shape, const),
            pl.BlockSpec(rb.shape, const),
        ],
        out_specs=_route_out_specs(ROUTE_SUBTILES),
        compiler_params=pltpu.CompilerParams(
            dimension_semantics=("arbitrary",), vmem_limit_bytes=VMEM_LIMIT),
        name="pool_mix",
    )(*x_args, mod, pool_w, pool_scale, rw_t, rb)


def _slots_kernel(pstart_ref, topi_ref, rank_ref, dest_ref):
    topi = topi_ref[...]
    start = jnp.zeros_like(topi)
    for e in range(N_EXPERTS):
        start = jnp.where(topi == e, pstart_ref[e], start)
    dest_ref[...] = start + rank_ref[...]


def _slots(pstart, topi, rank):
    n_tiles, _, tm = topi.shape
    tb = math.gcd(n_tiles, 32)
    spec = pl.BlockSpec((tb, TOP_K, tm), lambda i, ps: (i, 0, 0))
    return pl.pallas_call(
        _slots_kernel,
        out_shape=jax.ShapeDtypeStruct(topi.shape, I32),
        grid_spec=pltpu.PrefetchScalarGridSpec(
            num_scalar_prefetch=1, grid=(n_tiles // tb,), in_specs=[spec, spec], out_specs=spec),
        compiler_params=pltpu.CompilerParams(dimension_semantics=("arbitrary",)),
        name="moe_slots",
    )(pstart, topi, rank)


def _sc_chunk_rows(c, tm, width=SC_ROWS):
    per_tile = tm // width
    tile = c // per_tile
    part = c % per_tile
    return [(tile * TOP_K + kk) * per_tile + part for kk in range(TOP_K)]


def _sc_dispatch(hp, dest, n_slots, tm):
    hp = hp.reshape((-1,) + ROW_SHAPE)
    n_tok = hp.shape[0]
    assert n_tok % (SC_WORKERS * tm) == 0
    rows_per_w = dest.shape[0] // SC_WORKERS
    chunks_per_w = n_tok // SC_WORKERS // SC_ROWS
    mesh = plsc.VectorSubcoreMesh(core_axis_name="c", subcore_axis_name="s")

    @functools.partial(
        pl.kernel, mesh=mesh,
        out_type=jax.ShapeDtypeStruct((n_slots,) + ROW_SHAPE, I32),
        scratch_types=[
            pltpu.VMEM((rows_per_w, SC_ROWS), I32),
            pltpu.VMEM((SC_ROWS,) + ROW_SHAPE, I32),
            pltpu.SemaphoreType.DMA,
        ],
        name="sc_dispatch",
    )
    def run(hp_hbm, dest_hbm, xs_hbm, dest_v, rows_v, sem):
        wid = lax.axis_index("s") * SC_CORES + lax.axis_index("c")
        pltpu.sync_copy(dest_hbm.at[pl.ds(wid * rows_per_w, rows_per_w)], dest_v)

        @pl.loop(0, chunks_per_w)
        def _(c):
            tok0 = (wid * chunks_per_w + c) * SC_ROWS
            pltpu.sync_copy(hp_hbm.at[pl.ds(tok0, SC_ROWS)], rows_v)
            copies = [pltpu.async_copy(rows_v, xs_hbm.at[dest_v.at[row]], sem)
                      for row in _sc_chunk_rows(c, tm)]
            for cp in copies:
                cp.wait()

    return run(hp, dest).reshape(n_slots * ROW_CHUNKS, LANES)


def _sc_gather_sum(ys, dest, wtm, tm):
    ys = ys.reshape((-1,) + ROW_SHAPE)
    n_tok = wtm.shape[0]
    assert n_tok % (SC_WORKERS * tm) == 0
    g = SC_SUM_ROWS
    rows_per_w = dest.shape[0] // SC_WORKERS
    chunks_per_w = n_tok // SC_WORKERS // g
    mesh = plsc.VectorSubcoreMesh(core_axis_name="c", subcore_axis_name="s")

    @functools.partial(
        pl.kernel, mesh=mesh,
        out_type=jax.ShapeDtypeStruct((n_tok,) + ROW_SHAPE, I32),
        scratch_types=[
            pltpu.VMEM((rows_per_w, g), I32),
            pltpu.VMEM((TOP_K, g) + ROW_SHAPE, I32),
            pltpu.VMEM((g, LANES), F32),
            pltpu.VMEM((g,) + ROW_SHAPE, I32),
            pltpu.SemaphoreType.DMA,
        ],
        compiler_params=pltpu.CompilerParams(needs_layout_passes=False),
        name="sc_gather_sum",
    )
    def run(ys_hbm, dest_hbm, w_hbm, out_hbm, dest_v, rows_v, w_v, out_v, sem):
        wid = lax.axis_index("s") * SC_CORES + lax.axis_index("c")
        pltpu.sync_copy(dest_hbm.at[pl.ds(wid * rows_per_w, rows_per_w)], dest_v)

        @pl.loop(0, chunks_per_w)
        def _(c):
            tok0 = (wid * chunks_per_w + c) * g
            copies = [pltpu.async_copy(ys_hbm.at[dest_v.at[row]], rows_v.at[kk], sem)
                      for kk, row in enumerate(_sc_chunk_rows(c, tm, g))]
            pltpu.sync_copy(w_hbm.at[pl.ds(tok0, g)], w_v)
            for cp in copies:
                cp.wait()

            @pl.loop(0, g)
            def _(t):
                wk = [w_v[t, pl.ds(SC_LANES * kk, SC_LANES)] for kk in range(TOP_K)]
                for j in range(ROW_CHUNKS):
                    for i in range(LANES // SC_LANES):
                        lanes = pl.ds(SC_LANES * i, SC_LANES)
                        lo = hi = None
                        for kk in range(TOP_K):
                            word = rows_v[kk, t, j, lanes]
                            a = lax.bitcast_convert_type(lax.shift_left(word, 16), F32) * wk[kk]
                            b = lax.bitcast_convert_type(word & jnp.int32(HIGH_HALF), F32) * wk[kk]
                            lo = a if lo is None else lo + a
                            hi = b if hi is None else hi + b
                        packed = plsc.pack(lo, hi, format=plsc.PackFormat.INTERLEAVED)
                        out_v[t, j, lanes] = plsc.bitcast(packed, I32)

            pltpu.sync_copy(out_v, out_hbm.at[pl.ds(tok0, g)])

    return run(ys, dest, wtm).reshape(n_tok * ROW_CHUNKS, LANES)


def _expert_kernel(e_ref, first_ref, slot_ref, next_ref, nused_ref,
                   xs_ref, w1_hbm, b1_ref, w2_hbm, b2_ref, ys_ref,
                   w1f_ref, w2f_ref, w1b_ref, w2b_ref, sem, *, layer):
    i = pl.program_id(0)

    def weight_copies(expert, slot):
        return (pltpu.make_async_copy(w1_hbm.at[layer, expert], w1f_ref.at[slot], sem.at[slot, 0]),
                pltpu.make_async_copy(w2_hbm.at[layer, expert], w2f_ref.at[slot], sem.at[slot, 1]))

    @pl.when(first_ref[i] == 1)
    def _():
        slot = slot_ref[i]

        @pl.when(i == 0)
        def _():
            for cp in weight_copies(e_ref[i], slot):
                cp.start()

        for cp in weight_copies(e_ref[i], slot):
            cp.wait()
        w1b_ref[...] = w1f_ref[slot].astype(BF16)
        w2b_ref[...] = w2f_ref[slot].astype(BF16)

        @pl.when(next_ref[i] >= 0)
        def _():
            for cp in weight_copies(next_ref[i], 1 - slot):
                cp.start()

    @pl.when(i < nused_ref[0])
    def _():
        r = 0
        for n in EXPERT_SUBS:
            xb = _unpack_rows(xs_ref, r, n).astype(BF16)
            gu = jnp.dot(xb, w1b_ref[...], preferred_element_type=F32) + b1_ref[0, 0]
            gate = jnp.minimum(gu[:, :D_FF], SWIGLU_LIMIT)
            up = jnp.clip(gu[:, D_FF:], -SWIGLU_LIMIT, SWIGLU_LIMIT)
            act = (up + 1.0) * (gate * jax.nn.sigmoid(SWIGLU_ALPHA * gate))
            y = jnp.dot(act.astype(BF16), w2b_ref[...], preferred_element_type=F32) + b2_ref[0, 0]
            _pack_rows(y, ys_ref, r)
            r += n


def _expert_plan(pend, n_blk, bm):
    blk = jnp.arange(n_blk, dtype=I32)
    n_used = (pend[-1] // bm).astype(I32)
    expert = jnp.minimum(jnp.sum(blk[:, None] * bm >= pend[None, :], axis=1), N_EXPERTS - 1)
    expert = expert.astype(I32)
    prev = jnp.concatenate([jnp.full((1,), -1, I32), expert[:-1]])
    first = (blk < n_used) & (expert != prev)
    slot = (jnp.cumsum(first.astype(I32)) - 1) % 2
    later_first = first[None, :] & (blk[None, :] > blk[:, None])
    nxt = jnp.where(jnp.any(later_first, axis=1), expert[jnp.argmax(later_first, axis=1)], -1)
    return (expert, first.astype(I32), slot.astype(I32), nxt.astype(I32), n_used.reshape(1))


def _expert_ffn(plan, xs, layer, w1, b1, w2, b2):
    n_slots = xs.shape[0] // ROW_CHUNKS
    bm = EXPERT_TILE
    used_block = lambda i, e, f, s, nx, nu: (jnp.minimum(i, nu[0] - 1), 0)
    bias_block = lambda i, e, f, s, nx, nu: (layer, e[i], 0, 0)
    return pl.pallas_call(
        functools.partial(_expert_kernel, layer=layer),
        out_shape=jax.ShapeDtypeStruct(xs.shape, I32),
        grid_spec=pltpu.PrefetchScalarGridSpec(
            num_scalar_prefetch=5,
            grid=(n_slots // bm,),
            in_specs=[
                pl.BlockSpec((bm * ROW_CHUNKS, LANES), used_block),
                pl.BlockSpec(memory_space=pl.ANY),
                pl.BlockSpec((1, 1, 1, 2 * D_FF), bias_block),
                pl.BlockSpec(memory_space=pl.ANY),
                pl.BlockSpec((1, 1, 1, D_MODEL), bias_block),
            ],
            out_specs=pl.BlockSpec((bm * ROW_CHUNKS, LANES), used_block),
            scratch_shapes=[
                pltpu.VMEM((2, D_MODEL, 2 * D_FF), F32), pltpu.VMEM((2, D_FF, D_MODEL), F32),
                pltpu.VMEM((D_MODEL, 2 * D_FF), BF16), pltpu.VMEM((D_FF, D_MODEL), BF16),
                pltpu.SemaphoreType.DMA((2, 2)),
            ],
        ),
        compiler_params=pltpu.CompilerParams(
            dimension_semantics=("arbitrary",), vmem_limit_bytes=VMEM_LIMIT),
        name="expert_ffn",
    )(*plan, xs, w1, b1, w2, b2)


def _combine_kernel(moe_ref, x1_ref, mod_ref, fn_ref, o_ref, *, final):
    tm = x1_ref.shape[0]
    g2 = mod_ref[0][:, 5 * D_MODEL:6 * D_MODEL]
    out = x1_ref[...] + g2 * _unpack_rows(moe_ref, 0, tm)
    if final:
        out = _rms(out) * fn_ref[...]
    o_ref[...] = out


def _combine(moe, x1, mod, final_norm, seq_len, final):
    n_tok = x1.shape[0]
    tm = COMBINE_TILE
    per_seq = seq_len // tm
    return pl.pallas_call(
        functools.partial(_combine_kernel, final=final),
        out_shape=jax.ShapeDtypeStruct((n_tok, D_MODEL), F32),
        grid=(n_tok // tm,),
        in_specs=[
            pl.BlockSpec((tm * ROW_CHUNKS, LANES), lambda i: (i, 0)),
            pl.BlockSpec((tm, D_MODEL), lambda i: (i, 0)),
            pl.BlockSpec((1, 1, 6 * D_MODEL), lambda i: (i // per_seq, 0, 0)),
            pl.BlockSpec((1, D_MODEL), lambda i: (0, 0)),
        ],
        out_specs=pl.BlockSpec((tm, D_MODEL), lambda i: (i, 0)),
        compiler_params=pltpu.CompilerParams(
            dimension_semantics=("arbitrary",), vmem_limit_bytes=VMEM_LIMIT),
        name="moe_combine",
    )(moe, x1, mod, final_norm)


def _moe_layers(routes, mods, layer, ffn_w, final_norm, seq_len, final, defer_combine):
    bm = EXPERT_TILE
    plans = []
    for x1, hp, topi, rank, wtm, counts in routes:
        n_slots = x1.shape[0] * TOP_K + N_EXPERTS * bm
        n_blk = n_slots // bm
        cnt = counts[:, 0].astype(I32)
        padded = (cnt + bm - 1) // bm * bm
        pend = jnp.cumsum(padded)
        pstart = (pend - padded).astype(I32)
        dest = _slots(pstart, topi, rank).reshape(-1, SC_ROWS)
        plans.append((n_slots, _expert_plan(pend, n_blk, bm), dest, topi.shape[2]))
    xs = [_sc_dispatch(r[1], dest, n_slots, tm)
          for r, (n_slots, _, dest, tm) in zip(routes, plans)]
    ys = [_expert_ffn(plan, x, layer, *ffn_w) for x, (_, plan, _, _) in zip(xs, plans)]
    moe = [_sc_gather_sum(y, dest.reshape(-1, SC_SUM_ROWS), r[4], tm)
           for y, r, (_, _, dest, tm) in zip(ys, routes, plans)]
    if defer_combine:
        return [(r[0], m, mod) for m, r, mod in zip(moe, routes, mods)]
    return [_combine(m, r[0], mod, final_norm, seq_len, final)
            for m, r, mod in zip(moe, routes, mods)]


def _rope_tables(seq_len):
    inv_freq = 1.0 / (ROPE_THETA ** (jnp.arange(0, D_ROPE, 2, dtype=F32) / D_ROPE))
    ang = jnp.arange(seq_len, dtype=F32)[:, None] * inv_freq[None, :]
    cos, sin = jnp.cos(ang), jnp.sin(ang)
    ones = jnp.ones((seq_len, ROPE_LO), F32)
    zeros_lo = jnp.zeros((seq_len, ROPE_LO), F32)
    zeros_hi = jnp.zeros((seq_len, HEAD_PAD - ROPE_LO - D_ROPE), F32)
    zeros_h = jnp.zeros((seq_len, ROPE_HALF), F32)
    rope_c = jnp.concatenate([ones, cos, cos, zeros_hi], axis=1)
    rope_s1 = jnp.concatenate([zeros_lo, -sin, zeros_h, zeros_hi], axis=1)
    rope_s2 = jnp.concatenate([zeros_lo, zeros_h, sin, zeros_hi], axis=1)
    return cos.T, sin.T, rope_c, rope_s1, rope_s2


def _mla_weights(w_in, q_norm, kv_norm, w_uq, w_ukv, w_o, seq_len):
    d_qk = D_NOPE + D_ROPE
    q_scale = d_qk ** -0.5 * math.log2(math.e)
    pad_pe = jnp.zeros((D_MODEL, HEAD_PAD), F32).at[:, ROPE_LO:ROPE_LO + D_ROPE].set(
        w_in[:, Q_RANK + KV_RANK:])
    w_in_p = jnp.concatenate([w_in[:, :Q_RANK + KV_RANK], pad_pe], axis=1)
    w_uq_s = w_uq * q_scale
    w_kv = w_ukv.reshape(KV_RANK, N_HEADS, D_NOPE + D_V)
    cos_t, sin_t, rope_c, rope_s1, rope_s2 = _rope_tables(seq_len)
    return {
        "w_in": w_in_p.astype(BF16),
        "q_norm": q_norm.reshape(1, Q_RANK),
        "kv_norm": kv_norm.reshape(1, KV_RANK),
        "w_uq_t": w_uq_s.T.astype(BF16),
        "w_uk": w_kv[:, :, :D_NOPE].reshape(KV_RANK, N_HEADS * D_NOPE).astype(BF16),
        "w_uv_t": w_kv[:, :, D_NOPE:].reshape(KV_RANK, N_HEADS * D_V).T.astype(BF16),
        "w_o": w_o.astype(BF16),
        "cos_t": cos_t, "sin_t": sin_t, "rope_c": rope_c, "rope_s1": rope_s1, "rope_s2": rope_s2,
    }


def _router_weights(router_w, router_b):
    return router_w.T.astype(BF16), router_b.reshape(N_EXPERTS, 1)


def kernel(x_prompt, x_sample, c_prompt, c_sample, ada_w, ada_b, mla_w_in, mla_q_norm,
           mla_kv_norm, mla_w_uq, mla_w_ukv, mla_w_o, pool_w, pool_scale, router_w, router_b,
           moe_w1, moe_b1, moe_w2, moe_b2, final_norm):
    n_prompt, seq_len, _ = x_prompt.shape
    assert x_sample.shape[1] == seq_len and seq_len % (ROUTE_SUBTILES * TOKEN_TILE) == 0
    depth = ada_w.shape[0]
    n_sample = x_sample.shape[0]
    xs = [x_prompt.reshape(-1, D_MODEL), x_sample.reshape(-1, D_MODEL)]
    n_seqs = [n_prompt, n_sample]
    mods = _ada_mod(jnp.concatenate([c_prompt, c_sample], axis=0), ada_w, ada_b)
    fnorm = final_norm.reshape(1, D_MODEL)
    ffn_w = (moe_w1, moe_b1.reshape(depth, N_EXPERTS, 1, 2 * D_FF),
             moe_w2, moe_b2.reshape(depth, N_EXPERTS, 1, D_MODEL))

    for i in range(depth):
        rw_t, rb = _router_weights(router_w[i], router_b[i])
        j = i // 2
        group_mods = [mods[i, :n_prompt].reshape(n_prompt, 1, 6 * D_MODEL),
                      mods[i, n_prompt:].reshape(n_sample, 1, 6 * D_MODEL)]
        if i % 2 == 0:
            w = _mla_weights(mla_w_in[j], mla_q_norm[j], mla_kv_norm[j], mla_w_uq[j],
                             mla_w_ukv[j], mla_w_o[j], seq_len)
        routes = []
        for x, mod in zip(xs, group_mods):
            if i % 2 == 0:
                qt, k, vt = _mla_pre(x, mod, w, seq_len)
                ot = _attention(qt, k, vt, seq_len)
                routes.append(_post_mix(ot, x, mod, w["w_o"], rw_t, rb, seq_len))
            else:
                routes.append(_pool_mix(x, mod, pool_w[j].astype(BF16),
                                        pool_scale[j].reshape(1, D_MODEL), rw_t, rb, seq_len))
        last = i == depth - 1
        next_is_pool = not last and (i + 1) % 2 == 1
        xs = _moe_layers(routes, group_mods, i, ffn_w, fnorm, seq_len, last, next_is_pool)

    return (xs[0].reshape(n_prompt, seq_len, D_MODEL), xs[1].reshape(n_sample, seq_len, D_MODEL))
```

```python
import functools
import math

import jax
import jax.numpy as jnp
from jax import lax
from jax.experimental import pallas as pl
from jax.experimental.pallas import tpu as pltpu
from jax.experimental.pallas import tpu_sc as plsc

F32 = jnp.float32
BF16 = jnp.bfloat16
I32 = jnp.int32

D_MODEL = 1024
N_HEADS = 16
Q_RANK = 384
KV_RANK = 256
D_NOPE = 64
D_ROPE = 32
D_V = 64
V_ROWS = D_V + 16
ROPE_THETA = 10000.0
POOL_WINDOWS = (2, 4, 8, 16)
POOL_GROUP = D_MODEL // len(POOL_WINDOWS)
N_EXPERTS = 32
TOP_K = 4
D_FF = D_MODEL
SWIGLU_LIMIT = 7.0
SWIGLU_ALPHA = 1.702
EPS = 1e-6

LANES = 128
HEAD_PAD = 128
ROPE_LO = D_NOPE
ROPE_HALF = D_ROPE // 2
VMEM_LIMIT = 56 * 1024 * 1024

TOKEN_TILE = 256
COMBINE_TILE = 1024
ROUTE_SUBTILES = 4
EXPERT_TILE = 512
EXPERT_SUBS = (256, 256)
ATTN_KEY_CHUNKS = 2
ATTN_HEADS_PER_STEP = 16
POOL_HALO = 8
PACKED = D_MODEL // 2
ROW_CHUNKS = PACKED // LANES
ROW_SHAPE = (ROW_CHUNKS, LANES)
HIGH_HALF = -65536
ADA_COLS = 1536

SC_CORES = 2
SC_SUBCORES = 16
SC_WORKERS = SC_CORES * SC_SUBCORES
SC_LANES = 16
SC_ROWS = 64
SC_SUM_ROWS = 32


def _rms(x):
    return x * lax.rsqrt(jnp.mean(x * x, axis=-1, keepdims=True) + EPS)


def _pack_rows(y, out_ref, row0=0):
    n = y.shape[0]
    lo = lax.bitcast_convert_type(y[:, :PACKED].astype(BF16).astype(F32), I32)
    hi = lax.bitcast_convert_type(y[:, PACKED:].astype(BF16).astype(F32), I32)
    words = lax.shift_right_logical(lo, 16) | (hi & jnp.int32(HIGH_HALF))
    for j in range(ROW_CHUNKS):
        out_ref[pl.ds(row0 * ROW_CHUNKS + j, n, stride=ROW_CHUNKS), :] = (
            words[:, j * LANES:(j + 1) * LANES])


def _unpack_rows(in_ref, row0, n):
    words = jnp.concatenate(
        [in_ref[pl.ds(row0 * ROW_CHUNKS + j, n, stride=ROW_CHUNKS), :] for j in range(ROW_CHUNKS)],
        axis=1)
    lo = lax.bitcast_convert_type(lax.shift_left(words, 16), F32)
    hi = lax.bitcast_convert_type(words & jnp.int32(HIGH_HALF), F32)
    return jnp.concatenate([lo, hi], axis=1)


def _interleave(chains):
    chains = list(chains)
    done = object()
    while chains:
        chains = [ch for ch in chains if next(ch, done) is not done]


def _ada_kernel(c_ref, w_ref, b_ref, o_ref):
    c = c_ref[...]
    act = (c * jax.nn.sigmoid(c)).astype(BF16)
    o_ref[0] = jnp.dot(act, w_ref[0].astype(BF16), preferred_element_type=F32) + b_ref[0]


def _ada_mod(c, ada_w, ada_b):
    depth, _, n_out = ada_w.shape
    n_seq = c.shape[0]
    tn = ADA_COLS
    return pl.pallas_call(
        _ada_kernel,
        out_shape=jax.ShapeDtypeStruct((depth, n_seq, n_out), F32),
        grid=(depth, n_out // tn),
        in_specs=[
            pl.BlockSpec((n_seq, D_MODEL), lambda l, j: (0, 0)),
            pl.BlockSpec((1, D_MODEL, tn), lambda l, j: (l, 0, j)),
            pl.BlockSpec((1, 1, tn), lambda l, j: (l, 0, j)),
        ],
        out_specs=pl.BlockSpec((1, n_seq, tn), lambda l, j: (l, 0, j)),
        compiler_params=pltpu.CompilerParams(
            dimension_semantics=("arbitrary", "arbitrary"), vmem_limit_bytes=VMEM_LIMIT),
        name="ada_mod",
    )(c, ada_w, ada_b.reshape(depth, 1, n_out))


def _mla_pre_kernel(x_ref, mod_ref, win_ref, qn_ref, kvn_ref, wuqt_ref, wuk_ref, wuvt_ref,
                    cost_ref, sint_ref, ck_ref, s1k_ref, s2k_ref, qt_ref, k_ref, vt_ref):
    n_sub, ts = qt_ref.shape[1], qt_ref.shape[3]
    mod = mod_ref[0]
    sh1 = mod[:, 0:D_MODEL]
    sc1 = mod[:, D_MODEL:2 * D_MODEL]
    x = x_ref[...]
    d_qk = D_NOPE + D_ROPE

    def chain(sub):
        rows = slice(sub * ts, (sub + 1) * ts)
        h = (_rms(x[rows, :]) * (1.0 + sc1) + sh1).astype(BF16)
        a = jnp.dot(h, win_ref[...], preferred_element_type=F32)
        yield
        cq = (_rms(a[:, :Q_RANK]) * qn_ref[...]).astype(BF16)
        ckv = (_rms(a[:, Q_RANK:Q_RANK + KV_RANK]) * kvn_ref[...]).astype(BF16)
        kpe = a[:, Q_RANK + KV_RANK:]
        kpe = (kpe * ck_ref[rows, :]
               + pltpu.roll(kpe, LANES - ROPE_HALF, 1) * s1k_ref[rows, :]
               + pltpu.roll(kpe, ROPE_HALF, 1) * s2k_ref[rows, :])
        yield
        qt = lax.dot_general(wuqt_ref[...], cq, (((1,), (1,)), ((), ())),
                             preferred_element_type=F32)
        q3 = qt.reshape(N_HEADS, d_qk, ts)
        x1 = q3[:, ROPE_LO:ROPE_LO + ROPE_HALF, :]
        x2 = q3[:, ROPE_LO + ROPE_HALF:, :]
        cos = cost_ref[:, rows][None]
        sin = sint_ref[:, rows][None]
        q3 = jnp.concatenate(
            [q3[:, :ROPE_LO, :], x1 * cos - x2 * sin, x2 * cos + x1 * sin], axis=1)
        qt_ref[:, sub, :d_qk, :] = q3.astype(BF16)
        qt_ref[:, sub, d_qk:, :] = jnp.zeros((N_HEADS, HEAD_PAD - d_qk, ts), BF16)
        yield
        kn = jnp.dot(ckv, wuk_ref[...], preferred_element_type=F32)
        for hd in range(N_HEADS):
            k_ref[hd, rows, :] = (kn[:, hd * HEAD_PAD:(hd + 1) * HEAD_PAD] + kpe).astype(BF16)
        yield
        vt = lax.dot_general(wuvt_ref[...], ckv, (((1,), (1,)), ((), ())),
                             preferred_element_type=F32)
        vt_ref[:, 0, :D_V, rows] = vt.reshape(N_HEADS, D_V, ts).astype(BF16)
        vt_ref[:, 0, D_V:, rows] = jnp.ones((N_HEADS, V_ROWS - D_V, ts), BF16)

    _interleave(chain(sub) for sub in range(n_sub))


def _mla_pre(x, mod, w, seq_len):
    n_tok = x.shape[0]
    n_sub = ROUTE_SUBTILES
    tq = TOKEN_TILE
    tm = n_sub * tq
    per_seq = seq_len // tm
    const = lambda i: (0, 0)
    return pl.pallas_call(
        _mla_pre_kernel,
        out_shape=(
            jax.ShapeDtypeStruct((N_HEADS, n_tok // tq, HEAD_PAD, tq), BF16),
            jax.ShapeDtypeStruct((N_HEADS, n_tok, HEAD_PAD), BF16),
            jax.ShapeDtypeStruct((N_HEADS, n_tok // seq_len, V_ROWS, seq_len), BF16),
        ),
        grid=(n_tok // tm,),
        in_specs=[
            pl.BlockSpec((tm, D_MODEL), lambda i: (i, 0)),
            pl.BlockSpec((1, 1, 6 * D_MODEL), lambda i: (i // per_seq, 0, 0)),
            pl.BlockSpec(w["w_in"].shape, const),
            pl.BlockSpec(w["q_norm"].shape, const),
            pl.BlockSpec(w["kv_norm"].shape, const),
            pl.BlockSpec(w["w_uq_t"].shape, const),
            pl.BlockSpec(w["w_uk"].shape, const),
            pl.BlockSpec(w["w_uv_t"].shape, const),
            pl.BlockSpec((ROPE_HALF, tm), lambda i: (0, i % per_seq)),
            pl.BlockSpec((ROPE_HALF, tm), lambda i: (0, i % per_seq)),
            pl.BlockSpec((tm, LANES), lambda i: (i % per_seq, 0)),
            pl.BlockSpec((tm, LANES), lambda i: (i % per_seq, 0)),
            pl.BlockSpec((tm, LANES), lambda i: (i % per_seq, 0)),
        ],
        out_specs=(
            pl.BlockSpec((N_HEADS, n_sub, HEAD_PAD, tq), lambda i: (0, i, 0, 0)),
            pl.BlockSpec((N_HEADS, tm, HEAD_PAD), lambda i: (0, i, 0)),
            pl.BlockSpec((N_HEADS, 1, V_ROWS, tm), lambda i: (0, i // per_seq, 0, i % per_seq)),
        ),
        compiler_params=pltpu.CompilerParams(
            dimension_semantics=("arbitrary",), vmem_limit_bytes=VMEM_LIMIT),
        name="mla_pre",
    )(x, mod, w["w_in"], w["q_norm"], w["kv_norm"], w["w_uq_t"], w["w_uk"], w["w_uv_t"],
      w["cos_t"], w["sin_t"], w["rope_c"], w["rope_s1"], w["rope_s2"])


def _attention_kernel(qt_ref, k_ref, vt_ref, ot_ref, s0_ref, s1_ref):
    n_heads, n_q = qt_ref.shape[:2]
    n_tiles = n_heads * n_q
    n_keys = k_ref.shape[1]
    kc = n_keys // ATTN_KEY_CHUNKS

    def stage(t_next, s_next_ref, t_cur, s_cur_ref, m_cur):
        m_next, o = None, None
        if t_next is not None:
            h_next, j_next = t_next // n_q, t_next % n_q
        if t_cur is not None:
            h_cur, j_cur = t_cur // n_q, t_cur % n_q
        for c in range(ATTN_KEY_CHUNKS):
            rows = slice(c * kc, (c + 1) * kc)
            if t_next is not None:
                s = jnp.dot(k_ref[h_next, rows, :], qt_ref[h_next, j_next],
                            preferred_element_type=F32)
                s_next_ref[rows, :] = s
                cm = jnp.max(s, axis=0, keepdims=True)
                m_next = cm if m_next is None else jnp.maximum(m_next, cm)
            if t_cur is not None:
                p = jnp.exp2(s_cur_ref[rows, :] - m_cur).astype(BF16)
                part = jnp.dot(vt_ref[h_cur, 0, :, rows], p,
                               preferred_element_type=F32)
                o = part if o is None else o + part
        if t_cur is not None:
            denom = o[D_V:D_V + 1, :]
            ot_ref[h_cur, j_cur] = (o[:D_V, :] * (1.0 / denom)).astype(BF16)
        return m_next

    def body(i, m0):
        t = 2 * i
        m1 = stage(t + 1, s1_ref, t, s0_ref, m0)
        return stage(t + 2, s0_ref, t + 1, s1_ref, m1)

    m0 = lax.fori_loop(0, n_tiles // 2 - 1, body, stage(0, s0_ref, None, None, None))
    m1 = stage(n_tiles - 1, s1_ref, n_tiles - 2, s0_ref, m0)
    stage(None, None, n_tiles - 1, s1_ref, m1)


def _attention(qt, k, vt, seq_len):
    n_heads, n_tiles, _, tq = qt.shape
    per_seq = seq_len // tq
    n_seq = n_tiles // per_seq
    hb = ATTN_HEADS_PER_STEP
    return pl.pallas_call(
        _attention_kernel,
        out_shape=jax.ShapeDtypeStruct((n_heads, n_tiles, D_V, tq), BF16),
        grid=(n_seq, n_heads // hb),
        in_specs=[
            pl.BlockSpec((hb, per_seq, HEAD_PAD, tq), lambda b, h: (h, b, 0, 0)),
            pl.BlockSpec((hb, seq_len, HEAD_PAD), lambda b, h: (h, b, 0)),
            pl.BlockSpec((hb, 1, V_ROWS, seq_len), lambda b, h: (h, b, 0, 0)),
        ],
        out_specs=pl.BlockSpec((hb, per_seq, D_V, tq), lambda b, h: (h, b, 0, 0)),
        scratch_shapes=[pltpu.VMEM((seq_len, tq), F32), pltpu.VMEM((seq_len, tq), F32)],
        compiler_params=pltpu.CompilerParams(
            dimension_semantics=("arbitrary", "arbitrary"), vmem_limit_bytes=VMEM_LIMIT),
        name="attention",
    )(qt, k, vt)


def _route_init(cnt_ref):
    @pl.when(pl.program_id(0) == 0)
    def _():
        cnt_ref[...] = jnp.zeros_like(cnt_ref)


def _route_tail(x1_chains, mod, rwt_ref, rb_ref, out_refs):
    cnt_ref = out_refs[-1]
    ts = TOKEN_TILE
    row = lax.broadcasted_iota(I32, (ts, ts), 0)
    col = lax.broadcasted_iota(I32, (ts, ts), 1)
    state = {
        "running": cnt_ref[...][:, 0:1],
        "earlier": (row < col).astype(BF16),
        "e_iota": lax.broadcasted_iota(I32, (N_EXPERTS, ts), 0),
    }

    def chain(sub, x1_chain):
        x1 = yield from x1_chain
        yield from _route_chain(sub, x1, mod, rwt_ref, rb_ref, out_refs, state)

    _interleave(chain(sub, ch) for sub, ch in enumerate(x1_chains))
    cnt_ref[...] = jnp.broadcast_to(state["running"], cnt_ref.shape)


def _route_chain(sub, x1, mod, rwt_ref, rb_ref, out_refs, state):
    x1_ref, hp_ref, topi_ref, rank_ref, wtm_ref, _ = out_refs
    tm = x1.shape[0]
    rows = pl.ds(sub * tm, tm)
    sh2 = mod[:, 3 * D_MODEL:4 * D_MODEL]
    sc2 = mod[:, 4 * D_MODEL:5 * D_MODEL]
    x1_ref[rows, :] = x1
    h2 = _rms(x1) * (1.0 + sc2) + sh2
    _pack_rows(h2, hp_ref, sub * tm)
    logits = lax.dot_general(rwt_ref[...], h2.astype(BF16), (((1,), (1,)), ((), ())),
                             preferred_element_type=F32) + rb_ref[...]
    yield
    e_iota = state["e_iota"]
    vals, idxs = [], []
    work = logits
    for _ in range(TOP_K):
        m = jnp.max(work, axis=0, keepdims=True)
        idx = jnp.min(jnp.where(work == m, e_iota, N_EXPERTS), axis=0, keepdims=True)
        vals.append(m)
        idxs.append(idx)
        work = jnp.where(e_iota == idx, -jnp.inf, work)
    ex = [jnp.exp(v - vals[0]) for v in vals]
    inv = 1.0 / (ex[0] + ex[1] + ex[2] + ex[3])
    topw = jnp.concatenate([e * inv for e in ex], axis=0)
    topi_ref[sub] = jnp.concatenate(idxs, axis=0)
    yield

    earlier = state["earlier"]
    running = state["running"]
    ranks = []
    for kk in range(TOP_K):
        onehot = (e_iota == idxs[kk]).astype(F32)
        before = jnp.dot(onehot.astype(BF16), earlier, preferred_element_type=F32)
        ranks.append(jnp.sum(onehot * (running + before), axis=0, keepdims=True))
        running = running + jnp.sum(onehot, axis=1, keepdims=True)
    rank_ref[sub] = jnp.concatenate(ranks, axis=0).astype(I32)
    state["running"] = running
    yield

    wpad = jnp.concatenate(
        [jnp.broadcast_to(topw[kk:kk + 1], (SC_LANES, tm)) for kk in range(TOP_K)]
        + [jnp.zeros((LANES - TOP_K * SC_LANES, tm), F32)], axis=0)
    wtm_ref[rows, :] = wpad.T


def _route_out_shapes(n_tok):
    n_tiles = n_tok // TOKEN_TILE
    return (
        jax.ShapeDtypeStruct((n_tok, D_MODEL), F32),
        jax.ShapeDtypeStruct((n_tok * ROW_CHUNKS, LANES), I32),
        jax.ShapeDtypeStruct((n_tiles, TOP_K, TOKEN_TILE), I32),
        jax.ShapeDtypeStruct((n_tiles, TOP_K, TOKEN_TILE), I32),
        jax.ShapeDtypeStruct((n_tok, LANES), F32),
        jax.ShapeDtypeStruct((N_EXPERTS, LANES), F32),
    )


def _route_out_specs(n_sub):
    tm = n_sub * TOKEN_TILE
    return (
        pl.BlockSpec((tm, D_MODEL), lambda i: (i, 0)),
        pl.BlockSpec((tm * ROW_CHUNKS, LANES), lambda i: (i, 0)),
        pl.BlockSpec((n_sub, TOP_K, TOKEN_TILE), lambda i: (i, 0, 0)),
        pl.BlockSpec((n_sub, TOP_K, TOKEN_TILE), lambda i: (i, 0, 0)),
        pl.BlockSpec((tm, LANES), lambda i: (i, 0)),
        pl.BlockSpec((N_EXPERTS, LANES), lambda i: (0, 0)),
    )


def _post_mix_kernel(ot_ref, x_ref, mod_ref, wo_ref, rwt_ref, rb_ref, *out_refs):
    _route_init(out_refs[-1])
    n_sub, ts = ot_ref.shape[1], ot_ref.shape[3]
    mod = mod_ref[0]
    g1 = mod[:, 2 * D_MODEL:3 * D_MODEL]
    x = x_ref[...]

    def mixer(sub):
        ot = ot_ref[:, sub].reshape(N_HEADS * D_V, ts)
        mix = lax.dot_general(ot, wo_ref[...], (((0,), (0,)), ((), ())),
                              preferred_element_type=F32)
        x1 = x[sub * ts:(sub + 1) * ts, :] + g1 * mix
        yield
        return x1

    _route_tail([mixer(sub) for sub in range(n_sub)], mod, rwt_ref, rb_ref, out_refs)


def _post_mix(ot, x, mod, w_o, rw_t, rb, seq_len):
    n_tok = x.shape[0]
    n_sub = ROUTE_SUBTILES
    tm = n_sub * ot.shape[3]
    per_seq = seq_len // tm
    const = lambda i: (0, 0)
    return pl.pallas_call(
        _post_mix_kernel,
        out_shape=_route_out_shapes(n_tok),
        grid=(n_tok // tm,),
        in_specs=[
            pl.BlockSpec((N_HEADS, n_sub, D_V, ot.shape[3]), lambda i: (0, i, 0, 0)),
            pl.BlockSpec((tm, D_MODEL), lambda i: (i, 0)),
            pl.BlockSpec((1, 1, 6 * D_MODEL), lambda i: (i // per_seq, 0, 0)),
            pl.BlockSpec(w_o.shape, const),
            pl.BlockSpec(rw_t.shape, const),
            pl.BlockSpec(rb.shape, const),
        ],
        out_specs=_route_out_specs(n_sub),
        compiler_params=pltpu.CompilerParams(
            dimension_semantics=("arbitrary",), vmem_limit_bytes=VMEM_LIMIT),
        name="post_mix",
    )(ot, x, mod, w_o, rw_t, rb)


def _pool_mix_kernel(*refs, seq_len, pending):
    n_x = 7 if pending else 3
    x_refs = refs[:n_x]
    mod_ref, pw_ref, ps_ref, rwt_ref, rb_ref = refs[n_x:n_x + 5]
    out_refs = refs[n_x + 5:]
    _route_init(out_refs[-1])
    tm = x_refs[0].shape[0]
    per_seq = seq_len // tm
    mod = mod_ref[0]
    sh1 = mod[:, 0:D_MODEL]
    sc1 = mod[:, D_MODEL:2 * D_MODEL]
    g1 = mod[:, 2 * D_MODEL:3 * D_MODEL]
    pos0 = (pl.program_id(0) % per_seq) * tm

    if pending:
        g2_prev = x_refs[6][0][:, 5 * D_MODEL:6 * D_MODEL]

        def layer_input(k):
            x1_ref, moe_ref = x_refs[k], x_refs[3 + k]
            return x1_ref[...] + g2_prev * _unpack_rows(moe_ref, 0, x1_ref.shape[0])
    else:
        def layer_input(k):
            return x_refs[k][...]

    def normed(v):
        return _rms(v) * (1.0 + sc1) + sh1

    x = layer_input(0)
    h = normed(x)
    h_prev = jnp.where(pos0 > 0, normed(layer_input(1)), 0.0)
    h_next = jnp.where(pos0 + tm < seq_len, normed(layer_input(2)), 0.0)
    hext = jnp.concatenate([h_prev, h, h_next], axis=0)
    n_ext = tm + 2 * POOL_HALO
    pos = pos0 + lax.broadcasted_iota(I32, (tm, 1), 0)

    diffs = []
    for g, win in enumerate(POOL_WINDOWS):
        left = win // 2
        right = win - 1 - left
        cols = slice(g * POOL_GROUP, (g + 1) * POOL_GROUP)
        s = hext[:, cols]
        span = 1
        while span < win:
            s = s + pltpu.roll(s, span, 0)
            span *= 2
        if right:
            s = pltpu.roll(s, n_ext - right, 0)
        num = s[POOL_HALO:POOL_HALO + tm, :]
        count = (jnp.minimum(pos + right + 1, seq_len) - jnp.maximum(pos - left, 0)).astype(F32)
        diffs.append((num / count - h[:, cols]).astype(BF16))

    ts = TOKEN_TILE

    def mixer(sub):
        rows = slice(sub * ts, (sub + 1) * ts)
        outs = [jnp.dot(diffs[g][rows, :], pw_ref[g], preferred_element_type=F32)
                for g in range(len(POOL_WINDOWS))]
        x1 = x[rows, :] + g1 * (jnp.concatenate(outs, axis=1) * ps_ref[...])
        yield
        return x1

    _route_tail([mixer(sub) for sub in range(tm // ts)], mod, rwt_ref, rb_ref, out_refs)


def _pool_mix(x, mod, pool_w, pool_scale, rw_t, rb, seq_len):
    pending = isinstance(x, tuple)
    n_tok = (x[0] if pending else x).shape[0]
    tm = ROUTE_SUBTILES * TOKEN_TILE
    per_seq = seq_len // tm
    halo_per_tile = tm // POOL_HALO
    n_halo = n_tok // POOL_HALO
    const = lambda i: (0, 0)
    seq_block = lambda i: (i // per_seq, 0, 0)
    halo_prev = lambda i: (jnp.maximum(i * halo_per_tile - 1, 0), 0)
    halo_next = lambda i: (jnp.minimum((i + 1) * halo_per_tile, n_halo - 1), 0)

    def tile_and_halos(lines_per_row, width):
        return [pl.BlockSpec((tm * lines_per_row, width), lambda i: (i, 0)),
                pl.BlockSpec((POOL_HALO * lines_per_row, width), halo_prev),
                pl.BlockSpec((POOL_HALO * lines_per_row, width), halo_next)]

    if pending:
        x1, moe, mod_prev = x
        x_args = [x1, x1, x1, moe, moe, moe, mod_prev]
        x_specs = (tile_and_halos(1, D_MODEL) + tile_and_halos(ROW_CHUNKS, LANES)
                   + [pl.BlockSpec((1, 1, 6 * D_MODEL), seq_block)])
    else:
        x_args = [x, x, x]
        x_specs = tile_and_halos(1, D_MODEL)
    return pl.pallas_call(
        functools.partial(_pool_mix_kernel, seq_len=seq_len, pending=pending),
        out_shape=_route_out_shapes(n_tok),
        grid=(n_tok // tm,),
        in_specs=x_specs + [
            pl.BlockSpec((1, 1, 6 * D_MODEL), seq_block),
            pl.BlockSpec(pool_w.shape, lambda i: (0, 0, 0)),
            pl.BlockSpec(pool_scale.shape, const),
            pl.BlockSpec(rw_t.shape, const),
            pl.BlockSpec(rb.shape, const),
        ],
        out_specs=_route_out_specs(ROUTE_SUBTILES),
        compiler_params=pltpu.CompilerParams(
            dimension_semantics=("arbitrary",), vmem_limit_bytes=VMEM_LIMIT),
        name="pool_mix",
    )(*x_args, mod, pool_w, pool_scale, rw_t, rb)


def _slots_kernel(pstart_ref, topi_ref, rank_ref, dest_ref):
    topi = topi_ref[...]
    start = jnp.zeros_like(topi)
    for e in range(N_EXPERTS):
        start = jnp.where(topi == e, pstart_ref[e], start)
    dest_ref[...] = start + rank_ref[...]


def _slots(pstart, topi, rank):
    n_tiles, _, tm = topi.shape
    tb = math.gcd(n_tiles, 32)
    spec = pl.BlockSpec((tb, TOP_K, tm), lambda i, ps: (i, 0, 0))
    return pl.pallas_call(
        _slots_kernel,
        out_shape=jax.ShapeDtypeStruct(topi.shape, I32),
        grid_spec=pltpu.PrefetchScalarGridSpec(
            num_scalar_prefetch=1, grid=(n_tiles // tb,), in_specs=[spec, spec], out_specs=spec),
        compiler_params=pltpu.CompilerParams(dimension_semantics=("arbitrary",)),
        name="moe_slots",
    )(pstart, topi, rank)


def _sc_chunk_rows(c, tm, width=SC_ROWS):
    per_tile = tm // width
    tile = c // per_tile
    part = c % per_tile
    return [(tile * TOP_K + kk) * per_tile + part for kk in range(TOP_K)]


def _sc_dispatch(hp, dest, n_slots, tm):
    hp = hp.reshape((-1,) + ROW_SHAPE)
    n_tok = hp.shape[0]
    assert n_tok % (SC_WORKERS * tm) == 0
    rows_per_w = dest.shape[0] // SC_WORKERS
    chunks_per_w = n_tok // SC_WORKERS // SC_ROWS
    mesh = plsc.VectorSubcoreMesh(core_axis_name="c", subcore_axis_name="s")

    @functools.partial(
        pl.kernel, mesh=mesh,
        out_type=jax.ShapeDtypeStruct((n_slots,) + ROW_SHAPE, I32),
        scratch_types=[
            pltpu.VMEM((rows_per_w, SC_ROWS), I32),
            pltpu.VMEM((SC_ROWS,) + ROW_SHAPE, I32),
            pltpu.SemaphoreType.DMA,
        ],
        name="sc_dispatch",
    )
    def run(hp_hbm, dest_hbm, xs_hbm, dest_v, rows_v, sem):
        wid = lax.axis_index("s") * SC_CORES + lax.axis_index("c")
        pltpu.sync_copy(dest_hbm.at[pl.ds(wid * rows_per_w, rows_per_w)], dest_v)

        @pl.loop(0, chunks_per_w)
        def _(c):
            tok0 = (wid * chunks_per_w + c) * SC_ROWS
            pltpu.sync_copy(hp_hbm.at[pl.ds(tok0, SC_ROWS)], rows_v)
            copies = [pltpu.async_copy(rows_v, xs_hbm.at[dest_v.at[row]], sem)
                      for row in _sc_chunk_rows(c, tm)]
            for cp in copies:
                cp.wait()

    return run(hp, dest).reshape(n_slots * ROW_CHUNKS, LANES)


def _sc_gather_sum(ys, dest, wtm, tm):
    ys = ys.reshape((-1,) + ROW_SHAPE)
    n_tok = wtm.shape[0]
    assert n_tok % (SC_WORKERS * tm) == 0
    g = SC_SUM_ROWS
    rows_per_w = dest.shape[0] // SC_WORKERS
    chunks_per_w = n_tok // SC_WORKERS // g
    mesh = plsc.VectorSubcoreMesh(core_axis_name="c", subcore_axis_name="s")

    @functools.partial(
        pl.kernel, mesh=mesh,
        out_type=jax.ShapeDtypeStruct((n_tok,) + ROW_SHAPE, I32),
        scratch_types=[
            pltpu.VMEM((rows_per_w, g), I32),
            pltpu.VMEM((TOP_K, g) + ROW_SHAPE, I32),
            pltpu.VMEM((g, LANES), F32),
            pltpu.VMEM((g,) + ROW_SHAPE, I32),
            pltpu.SemaphoreType.DMA,
        ],
        compiler_params=pltpu.CompilerParams(needs_layout_passes=False),
        name="sc_gather_sum",
    )
    def run(ys_hbm, dest_hbm, w_hbm, out_hbm, dest_v, rows_v, w_v, out_v, sem):
        wid = lax.axis_index("s") * SC_CORES + lax.axis_index("c")
        pltpu.sync_copy(dest_hbm.at[pl.ds(wid * rows_per_w, rows_per_w)], dest_v)

        @pl.loop(0, chunks_per_w)
        def _(c):
            tok0 = (wid * chunks_per_w + c) * g
            copies = [pltpu.async_copy(ys_hbm.at[dest_v.at[row]], rows_v.at[kk], sem)
                      for kk, row in enumerate(_sc_chunk_rows(c, tm, g))]
            pltpu.sync_copy(w_hbm.at[pl.ds(tok0, g)], w_v)
            for cp in copies:
                cp.wait()

            @pl.loop(0, g)
            def _(t):
                wk = [w_v[t, pl.ds(SC_LANES * kk, SC_LANES)] for kk in range(TOP_K)]
                for j in range(ROW_CHUNKS):
                    for i in range(LANES // SC_LANES):
                        lanes = pl.ds(SC_LANES * i, SC_LANES)
                        lo = hi = None
                        for kk in range(TOP_K):
                            word = rows_v[kk, t, j, lanes]
                            a = lax.bitcast_convert_type(lax.shift_left(word, 16), F32) * wk[kk]
                            b = lax.bitcast_convert_type(word & jnp.int32(HIGH_HALF), F32) * wk[kk]
                            lo = a if lo is None else lo + a
                            hi = b if hi is None else hi + b
                        packed = plsc.pack(lo, hi, format=plsc.PackFormat.INTERLEAVED)
                        out_v[t, j, lanes] = plsc.bitcast(packed, I32)

            pltpu.sync_copy(out_v, out_hbm.at[pl.ds(tok0, g)])

    return run(ys, dest, wtm).reshape(n_tok * ROW_CHUNKS, LANES)


def _expert_kernel(e_ref, first_ref, slot_ref, next_ref, nused_ref,
                   xs_ref, w1_hbm, b1_ref, w2_hbm, b2_ref, ys_ref,
                   w1f_ref, w2f_ref, w1b_ref, w2b_ref, sem, *, layer):
    i = pl.program_id(0)

    def weight_copies(expert, slot):
        return (pltpu.make_async_copy(w1_hbm.at[layer, expert], w1f_ref.at[slot], sem.at[slot, 0]),
                pltpu.make_async_copy(w2_hbm.at[layer, expert], w2f_ref.at[slot], sem.at[slot, 1]))

    @pl.when(first_ref[i] == 1)
    def _():
        slot = slot_ref[i]

        @pl.when(i == 0)
        def _():
            for cp in weight_copies(e_ref[i], slot):
                cp.start()

        for cp in weight_copies(e_ref[i], slot):
            cp.wait()
        w1b_ref[...] = w1f_ref[slot].astype(BF16)
        w2b_ref[...] = w2f_ref[slot].astype(BF16)

        @pl.when(next_ref[i] >= 0)
        def _():
            for cp in weight_copies(next_ref[i], 1 - slot):
                cp.start()

    @pl.when(i < nused_ref[0])
    def _():
        r = 0
        for n in EXPERT_SUBS:
            xb = _unpack_rows(xs_ref, r, n).astype(BF16)
            gu = jnp.dot(xb, w1b_ref[...], preferred_element_type=F32) + b1_ref[0, 0]
            gate = jnp.minimum(gu[:, :D_FF], SWIGLU_LIMIT)
            up = jnp.clip(gu[:, D_FF:], -SWIGLU_LIMIT, SWIGLU_LIMIT)
            act = (up + 1.0) * (gate * jax.nn.sigmoid(SWIGLU_ALPHA * gate))
            y = jnp.dot(act.astype(BF16), w2b_ref[...], preferred_element_type=F32) + b2_ref[0, 0]
            _pack_rows(y, ys_ref, r)
            r += n


def _expert_plan(pend, n_blk, bm):
    blk = jnp.arange(n_blk, dtype=I32)
    n_used = (pend[-1] // bm).astype(I32)
    expert = jnp.minimum(jnp.sum(blk[:, None] * bm >= pend[None, :], axis=1), N_EXPERTS - 1)
    expert = expert.astype(I32)
    prev = jnp.concatenate([jnp.full((1,), -1, I32), expert[:-1]])
    first = (blk < n_used) & (expert != prev)
    slot = (jnp.cumsum(first.astype(I32)) - 1) % 2
    later_first = first[None, :] & (blk[None, :] > blk[:, None])
    nxt = jnp.where(jnp.any(later_first, axis=1), expert[jnp.argmax(later_first, axis=1)], -1)
    return (expert, first.astype(I32), slot.astype(I32), nxt.astype(I32), n_used.reshape(1))


def _expert_ffn(plan, xs, layer, w1, b1, w2, b2):
    n_slots = xs.shape[0] // ROW_CHUNKS
    bm = EXPERT_TILE
    used_block = lambda i, e, f, s, nx, nu: (jnp.minimum(i, nu[0] - 1), 0)
    bias_block = lambda i, e, f, s, nx, nu: (layer, e[i], 0, 0)
    return pl.pallas_call(
        functools.partial(_expert_kernel, layer=layer),
        out_shape=jax.ShapeDtypeStruct(xs.shape, I32),
        grid_spec=pltpu.PrefetchScalarGridSpec(
            num_scalar_prefetch=5,
            grid=(n_slots // bm,),
            in_specs=[
                pl.BlockSpec((bm * ROW_CHUNKS, LANES), used_block),
                pl.BlockSpec(memory_space=pl.ANY),
                pl.BlockSpec((1, 1, 1, 2 * D_FF), bias_block),
                pl.BlockSpec(memory_space=pl.ANY),
                pl.BlockSpec((1, 1, 1, D_MODEL), bias_block),
            ],
            out_specs=pl.BlockSpec((bm * ROW_CHUNKS, LANES), used_block),
            scratch_shapes=[
                pltpu.VMEM((2, D_MODEL, 2 * D_FF), F32), pltpu.VMEM((2, D_FF, D_MODEL), F32),
                pltpu.VMEM((D_MODEL, 2 * D_FF), BF16), pltpu.VMEM((D_FF, D_MODEL), BF16),
                pltpu.SemaphoreType.DMA((2, 2)),
            ],
        ),
        compiler_params=pltpu.CompilerParams(
            dimension_semantics=("arbitrary",), vmem_limit_bytes=VMEM_LIMIT),
        name="expert_ffn",
    )(*plan, xs, w1, b1, w2, b2)


def _combine_kernel(moe_ref, x1_ref, mod_ref, fn_ref, o_ref, *, final):
    tm = x1_ref.shape[0]
    g2 = mod_ref[0][:, 5 * D_MODEL:6 * D_MODEL]
    out = x1_ref[...] + g2 * _unpack_rows(moe_ref, 0, tm)
    if final:
        out = _rms(out) * fn_ref[...]
    o_ref[...] = out


def _combine(moe, x1, mod, final_norm, seq_len, final):
    n_tok = x1.shape[0]
    tm = COMBINE_TILE
    per_seq = seq_len // tm
    return pl.pallas_call(
        functools.partial(_combine_kernel, final=final),
        out_shape=jax.ShapeDtypeStruct((n_tok, D_MODEL), F32),
        grid=(n_tok // tm,),
        in_specs=[
            pl.BlockSpec((tm * ROW_CHUNKS, LANES), lambda i: (i, 0)),
            pl.BlockSpec((tm, D_MODEL), lambda i: (i, 0)),
            pl.BlockSpec((1, 1, 6 * D_MODEL), lambda i: (i // per_seq, 0, 0)),
            pl.BlockSpec((1, D_MODEL), lambda i: (0, 0)),
        ],
        out_specs=pl.BlockSpec((tm, D_MODEL), lambda i: (i, 0)),
        compiler_params=pltpu.CompilerParams(
            dimension_semantics=("arbitrary",), vmem_limit_bytes=VMEM_LIMIT),
        name="moe_combine",
    )(moe, x1, mod, final_norm)


def _moe_layers(routes, mods, layer, ffn_w, final_norm, seq_len, final, defer_combine):
    bm = EXPERT_TILE
    plans = []
    for x1, hp, topi, rank, wtm, counts in routes:
        n_slots = x1.shape[0] * TOP_K + N_EXPERTS * bm
        n_blk = n_slots // bm
        cnt = counts[:, 0].astype(I32)
        padded = (cnt + bm - 1) // bm * bm
        pend = jnp.cumsum(padded)
        pstart = (pend - padded).astype(I32)
        dest = _slots(pstart, topi, rank).reshape(-1, SC_ROWS)
        plans.append((n_slots, _expert_plan(pend, n_blk, bm), dest, topi.shape[2]))
    xs = [_sc_dispatch(r[1], dest, n_slots, tm)
          for r, (n_slots, _, dest, tm) in zip(routes, plans)]
    ys = [_expert_ffn(plan, x, layer, *ffn_w) for x, (_, plan, _, _) in zip(xs, plans)]
    moe = [_sc_gather_sum(y, dest.reshape(-1, SC_SUM_ROWS), r[4], tm)
           for y, r, (_, _, dest, tm) in zip(ys, routes, plans)]
    if defer_combine:
        return [(r[0], m, mod) for m, r, mod in zip(moe, routes, mods)]
    return [_combine(m, r[0], mod, final_norm, seq_len, final)
            for m, r, mod in zip(moe, routes, mods)]


def _rope_tables(seq_len):
    inv_freq = 1.0 / (ROPE_THETA ** (jnp.arange(0, D_ROPE, 2, dtype=F32) / D_ROPE))
    ang = jnp.arange(seq_len, dtype=F32)[:, None] * inv_freq[None, :]
    cos, sin = jnp.cos(ang), jnp.sin(ang)
    ones = jnp.ones((seq_len, ROPE_LO), F32)
    zeros_lo = jnp.zeros((seq_len, ROPE_LO), F32)
    zeros_hi = jnp.zeros((seq_len, HEAD_PAD - ROPE_LO - D_ROPE), F32)
    zeros_h = jnp.zeros((seq_len, ROPE_HALF), F32)
    rope_c = jnp.concatenate([ones, cos, cos, zeros_hi], axis=1)
    rope_s1 = jnp.concatenate([zeros_lo, -sin, zeros_h, zeros_hi], axis=1)
    rope_s2 = jnp.concatenate([zeros_lo, zeros_h, sin, zeros_hi], axis=1)
    return cos.T, sin.T, rope_c, rope_s1, rope_s2


def _mla_weights(w_in, q_norm, kv_norm, w_uq, w_ukv, w_o, seq_len):
    d_qk = D_NOPE + D_ROPE
    q_scale = d_qk ** -0.5 * math.log2(math.e)
    pad_pe = jnp.zeros((D_MODEL, HEAD_PAD), F32).at[:, ROPE_LO:ROPE_LO + D_ROPE].set(
        w_in[:, Q_RANK + KV_RANK:])
    w_in_p = jnp.concatenate([w_in[:, :Q_RANK + KV_RANK], pad_pe], axis=1)
    w_uq_s = w_uq * q_scale
    w_kv = w_ukv.reshape(KV_RANK, N_HEADS, D_NOPE + D_V)
    w_uk_p = jnp.pad(w_kv[:, :, :D_NOPE], ((0, 0), (0, 0), (0, HEAD_PAD - D_NOPE)))
    cos_t, sin_t, rope_c, rope_s1, rope_s2 = _rope_tables(seq_len)
    return {
        "w_in": w_in_p.astype(BF16),
        "q_norm": q_norm.reshape(1, Q_RANK),
        "kv_norm": kv_norm.reshape(1, KV_RANK),
        "w_uq_t": w_uq_s.T.astype(BF16),
        "w_uk": w_uk_p.reshape(KV_RANK, N_HEADS * HEAD_PAD).astype(BF16),
        "w_uv_t": w_kv[:, :, D_NOPE:].reshape(KV_RANK, N_HEADS * D_V).T.astype(BF16),
        "w_o": w_o.astype(BF16),
        "cos_t": cos_t, "sin_t": sin_t, "rope_c": rope_c, "rope_s1": rope_s1, "rope_s2": rope_s2,
    }


def _router_weights(router_w, router_b):
    return router_w.T.astype(BF16), router_b.reshape(N_EXPERTS, 1)


def kernel(x_prompt, x_sample, c_prompt, c_sample, ada_w, ada_b, mla_w_in, mla_q_norm,
           mla_kv_norm, mla_w_uq, mla_w_ukv, mla_w_o, pool_w, pool_scale, router_w, router_b,
           moe_w1, moe_b1, moe_w2, moe_b2, final_norm):
    n_prompt, seq_len, _ = x_prompt.shape
    assert x_sample.shape[1] == seq_len and seq_len % (ROUTE_SUBTILES * TOKEN_TILE) == 0
    depth = ada_w.shape[0]
    n_sample = x_sample.shape[0]
    xs = [x_prompt.reshape(-1, D_MODEL), x_sample.reshape(-1, D_MODEL)]
    n_seqs = [n_prompt, n_sample]
    mods = _ada_mod(jnp.concatenate([c_prompt, c_sample], axis=0), ada_w, ada_b)
    fnorm = final_norm.reshape(1, D_MODEL)
    ffn_w = (moe_w1, moe_b1.reshape(depth, N_EXPERTS, 1, 2 * D_FF),
             moe_w2, moe_b2.reshape(depth, N_EXPERTS, 1, D_MODEL))

    for i in range(depth):
        rw_t, rb = _router_weights(router_w[i], router_b[i])
        j = i // 2
        group_mods = [mods[i, :n_prompt].reshape(n_prompt, 1, 6 * D_MODEL),
                      mods[i, n_prompt:].reshape(n_sample, 1, 6 * D_MODEL)]
        if i % 2 == 0:
            w = _mla_weights(mla_w_in[j], mla_q_norm[j], mla_kv_norm[j], mla_w_uq[j],
                             mla_w_ukv[j], mla_w_o[j], seq_len)
        routes = []
        for x, mod in zip(xs, group_mods):
            if i % 2 == 0:
                qt, k, vt = _mla_pre(x, mod, w, seq_len)
                ot = _attention(qt, k, vt, seq_len)
                routes.append(_post_mix(ot, x, mod, w["w_o"], rw_t, rb, seq_len))
            else:
                routes.append(_pool_mix(x, mod, pool_w[j].astype(BF16),
                                        pool_scale[j].reshape(1, D_MODEL), rw_t, rb, seq_len))
        last = i == depth - 1
        next_is_pool = not last and (i + 1) % 2 == 1
        xs = _moe_layers(routes, group_mods, i, ffn_w, fnorm, seq_len, last, next_is_pool)

    return (xs[0].reshape(n_prompt, seq_len, D_MODEL), xs[1].reshape(n_sample, seq_len, D_MODEL))
```
